```python
import jax
import jax.numpy as jnp
from jax import lax
import numpy as np

D_MODEL = 1024
BATCH = 16
SEQ = 2048
DEPTH = 1

SWA_HEAD_DIM = 64
SWA_Q_HEADS = 8
SWA_KV_HEADS = 2
SWA_GROUP = SWA_Q_HEADS // SWA_KV_HEADS
WINDOW = 128
BAND_BLOCK = 128
RET_HEADS = 4
RET_QK_DIM = 128
RET_V_DIM = 256
RET_CHUNK = 128
ROPE_BASE = 10000.0
MEM_LEN = 256
MEM_HEADS = 4
MEM_HEAD_DIM = 128
N_BRANCH = 3
N_EXPERTS = 32
TOP_K = 4
D_FF = 1024
SWIGLU_LIMIT = 7.0
SWIGLU_ALPHA = 1.702
EPS = 1e-6
NEG_INF = -1e30

SWA_Q_W = SWA_Q_HEADS * SWA_HEAD_DIM
SWA_KV_W = SWA_KV_HEADS * SWA_HEAD_DIM
RET_QK_W = RET_HEADS * RET_QK_DIM
RET_V_W = RET_HEADS * RET_V_DIM
MEM_W = MEM_HEADS * MEM_HEAD_DIM
IN_SIZES = (SWA_Q_W, SWA_KV_W, SWA_KV_W, RET_QK_W, RET_QK_W, RET_V_W, RET_V_W, MEM_W, N_BRANCH * D_MODEL)
IN_WIDTH = sum(IN_SIZES)

kernel_name = "hybrid_swa_retention_memory_moe_block"


def rms_norm(x, w):
    xf = x.astype(jnp.float32)
    y = xf * lax.rsqrt(jnp.mean(xf * xf, axis=-1, keepdims=True) + EPS)
    return (y * w.astype(jnp.float32)).astype(x.dtype)


def head_layer_norm(t, w):
    mu = jnp.mean(t, axis=-1, keepdims=True)
    tc = t - mu
    var = jnp.mean(tc * tc, axis=-1, keepdims=True)
    return tc * lax.rsqrt(var + EPS) * w.astype(jnp.float32)


def split_cols(z, sizes):
    outs = []
    off = 0
    for s in sizes:
        outs.append(z[..., off:off + s])
        off += s
    return outs


def rotary(t):
    S, d = t.shape[1], t.shape[-1]
    half = d // 2
    inv = 1.0 / (ROPE_BASE ** (jnp.arange(0, half, dtype=jnp.float32) / half))
    ang = jnp.arange(S, dtype=jnp.float32)[:, None] * inv[None, :]
    cos = jnp.cos(ang)[None, :, None, :]
    sin = jnp.sin(ang)[None, :, None, :]
    tf = t.astype(jnp.float32)
    t1, t2 = tf[..., :half], tf[..., half:]
    return jnp.concatenate([t1 * cos - t2 * sin, t1 * sin + t2 * cos], axis=-1).astype(t.dtype)


def sliding_window_attention(q, k, v, sinks):
    B, S = q.shape[0], q.shape[1]
    nb = S // BAND_BLOCK
    qb = q.reshape(B, nb, BAND_BLOCK, SWA_KV_HEADS, SWA_GROUP, SWA_HEAD_DIM)

    def band(t):
        tp = jnp.pad(t, ((0, 0), (BAND_BLOCK, 0), (0, 0), (0, 0)))
        tb = tp.reshape(B, nb + 1, BAND_BLOCK, SWA_KV_HEADS, SWA_HEAD_DIM)
        return jnp.concatenate([tb[:, :-1], tb[:, 1:]], axis=2)

    kb, vb = band(k), band(v)
    scores = jnp.einsum('bnqhgd,bnkhd->bnhgqk', qb, kb,
                        preferred_element_type=jnp.float32) * (SWA_HEAD_DIM ** -0.5)
    qi = jnp.arange(BAND_BLOCK)[:, None]
    kj = jnp.arange(2 * BAND_BLOCK)[None, :]
    diff = BAND_BLOCK + qi - kj
    in_win = (diff >= 0) & (diff < WINDOW)
    not_pad = (jnp.arange(nb) > 0)[:, None, None] | (kj >= BAND_BLOCK)[None]
    mask = in_win[None] & not_pad
    scores = jnp.where(mask[None, :, None, None], scores, NEG_INF)
    sink = jnp.broadcast_to(sinks.astype(jnp.float32).reshape(1, 1, SWA_KV_HEADS, SWA_GROUP, 1, 1),
                            scores.shape[:-1] + (1,))
    probs = jax.nn.softmax(jnp.concatenate([scores, sink], axis=-1), axis=-1)[..., :-1]
    out = jnp.einsum('bnhgqk,bnkhd->bnqhgd', probs.astype(v.dtype), vb)
    return out.reshape(B, S, SWA_Q_HEADS * SWA_HEAD_DIM)


def retention(q, k, v):
    B, S = q.shape[0], q.shape[1]
    L = RET_CHUNK
    nc = S // L
    log_g = jnp.log1p(-(2.0 ** (-5.0 - jnp.arange(RET_HEADS, dtype=jnp.float32))))
    pos = jnp.arange(L, dtype=jnp.float32)
    diff = pos[:, None] - pos[None, :]
    decay_mask = jnp.where(diff[None] >= 0,
                           jnp.exp(jnp.maximum(diff, 0.0)[None] * log_g[:, None, None]), 0.0)
    q_decay = jnp.exp((pos[None, :] + 1.0) * log_g[:, None])[..., None]
    k_decay = jnp.exp((L - 1.0 - pos[None, :]) * log_g[:, None])[..., None]
    chunk_decay = jnp.exp(L * log_g)[:, None, None]

    def chunks(t):
        d = t.shape[-1]
        return t.astype(jnp.float32).reshape(B, nc, L, RET_HEADS, d).transpose(1, 0, 3, 2, 4)

    qc, kc, vc = chunks(q), chunks(k), chunks(v)

    def step(state, inp):
        qi, ki, vi = inp
        inner = jnp.einsum('bhld,bhmd->bhlm', qi, ki) * decay_mask[None]
        o = jnp.einsum('bhlm,bhme->bhle', inner, vi) + \
            jnp.einsum('bhld,bhde->bhle', qi * q_decay[None], state)
        state = state * chunk_decay[None] + jnp.einsum('bhld,bhle->bhde', ki * k_decay[None], vi)
        return state, o

    state0 = jnp.zeros((B, RET_HEADS, RET_QK_DIM, RET_V_DIM), jnp.float32)
    _, o = lax.scan(step, state0, (qc, kc, vc))
    return o.transpose(1, 0, 3, 2, 4).reshape(B, S, RET_HEADS, RET_V_DIM)


def memory_attention(q, mk, mv):
    B, S = q.shape[0], q.shape[1]
    s = jnp.einsum('bshd,bmhd->bhsm', q, mk, preferred_element_type=jnp.float32) * (MEM_HEAD_DIM ** -0.5)
    p = jax.nn.softmax(s, axis=-1)
    o = jnp.einsum('bhsm,bmhd->bshd', p.astype(mv.dtype), mv)
    return o.reshape(B, S, MEM_W)


def routed_experts(h, w_router, b_router, w_up, b_up, w_down, b_down):
    B, S, D = h.shape
    T = B * S
    ht = h.reshape(T, D)
    logits = (ht @ w_router + b_router).astype(jnp.float32)
    top_vals, top_idx = lax.top_k(logits, TOP_K)
    gate = jax.nn.softmax(top_vals, axis=-1)
    flat_e = top_idx.reshape(-1)
    order = jnp.argsort(flat_e)
    e_sorted = flat_e[order]
    tok = order // TOP_K
    xs = ht[tok]
    group_sizes = jnp.bincount(flat_e, length=N_EXPERTS).astype(jnp.int32)
    hid = lax.ragged_dot(xs, w_up, group_sizes) + b_up[e_sorted]
    x_glu = jnp.minimum(hid[:, 0::2], SWIGLU_LIMIT)
    x_lin = jnp.clip(hid[:, 1::2], -SWIGLU_LIMIT, SWIGLU_LIMIT)
    act = x_glu * jax.nn.sigmoid(SWIGLU_ALPHA * x_glu) * (x_lin + 1.0)
    out = lax.ragged_dot(act, w_down, group_sizes) + b_down[e_sorted]
    out = out * gate.reshape(-1)[order][:, None].astype(out.dtype)
    y = jnp.zeros_like(ht).at[tok].add(out)
    return y.reshape(B, S, D)


def setup_inputs(seed: int = 0) -> dict:
    key = jax.random.key(seed)
    ks = jax.random.split(key, 23)
    f32 = jnp.float32
    L, D = DEPTH, D_MODEL

    def nrm(k, shape, scale):
        return jax.random.normal(k, shape, f32) * scale

    def gain(k, shape):
        return 1.0 + 0.05 * jax.random.normal(k, shape, f32)

    return {
        'x': nrm(ks[0], (BATCH, SEQ, D), 1.0),
        'mem': nrm(ks[1], (BATCH, MEM_LEN, D), 1.0),
        'mix_norm_w': gain(ks[2], (L, D)),
        'mem_norm_w': gain(ks[3], (L, D)),
        'w_in': nrm(ks[4], (L, D, IN_WIDTH), D ** -0.5),
        'swa_q_norm_w': gain(ks[5], (L, SWA_HEAD_DIM)),
        'swa_k_norm_w': gain(ks[6], (L, SWA_HEAD_DIM)),
        'swa_sinks': nrm(ks[7], (L, SWA_Q_HEADS), 0.5),
        'ret_norm_w': gain(ks[8], (L, RET_V_W)),
        'w_mem_kv': nrm(ks[9], (L, D, 2 * MEM_W), D ** -0.5),
        'mem_q_norm_w': gain(ks[10], (L, MEM_HEAD_DIM)),
        'mem_k_norm_w': gain(ks[11], (L, MEM_HEAD_DIM)),
        'w_br_swa': nrm(ks[12], (L, SWA_Q_W, D), SWA_Q_W ** -0.5),
        'w_br_ret': nrm(ks[13], (L, RET_V_W, D), RET_V_W ** -0.5),
        'w_br_mem': nrm(ks[14], (L, MEM_W, D), MEM_W ** -0.5),
        'w_out': nrm(ks[15], (L, D, D), D ** -0.5),
        'ffn_norm_w': gain(ks[16], (L, D)),
        'w_router': nrm(ks[17], (L, D, N_EXPERTS), D ** -0.5),
        'b_router': nrm(ks[18], (L, N_EXPERTS), 0.01),
        'w_up': nrm(ks[19], (L, N_EXPERTS, D, 2 * D_FF), D ** -0.5),
        'b_up': nrm(ks[20], (L, N_EXPERTS, 2 * D_FF), 0.01),
        'w_down': nrm(ks[21], (L, N_EXPERTS, D_FF, D), D_FF ** -0.5),
        'b_down': nrm(ks[22], (L, N_EXPERTS, D), 0.01),
    }


def reference(x, mem, mix_norm_w, mem_norm_w, w_in, swa_q_norm_w, swa_k_norm_w, swa_sinks,
              ret_norm_w, w_mem_kv, mem_q_norm_w, mem_k_norm_w, w_br_swa, w_br_ret, w_br_mem,
              w_out, ffn_norm_w, w_router, b_router, w_up, b_up, w_down, b_down):
    B, S, D = x.shape
    for l in range(DEPTH):
        h = rms_norm(x, mix_norm_w[l])
        z = h @ w_in[l]
        sq, sk, sv, rq, rk, rv, rg, mq, zg = split_cols(z, IN_SIZES)

        sq = rms_norm(sq.reshape(B, S, SWA_Q_HEADS, SWA_HEAD_DIM), swa_q_norm_w[l])
        sk = rms_norm(sk.reshape(B, S, SWA_KV_HEADS, SWA_HEAD_DIM), swa_k_norm_w[l])
        sv = sv.reshape(B, S, SWA_KV_HEADS, SWA_HEAD_DIM)
        o_swa = sliding_window_attention(sq, sk, sv, swa_sinks[l])

        rq = rotary(rq.reshape(B, S, RET_HEADS, RET_QK_DIM))
        rk = rotary(rk.reshape(B, S, RET_HEADS, RET_QK_DIM)) * (RET_QK_DIM ** -0.5)
        rv = rv.reshape(B, S, RET_HEADS, RET_V_DIM)
        o_ret = head_layer_norm(retention(rq, rk, rv), ret_norm_w[l].reshape(RET_HEADS, RET_V_DIM))
        o_ret = jax.nn.silu(rg) * o_ret.reshape(B, S, RET_V_W).astype(x.dtype)

        m = rms_norm(mem, mem_norm_w[l])
        mk, mv = split_cols(m @ w_mem_kv[l], (MEM_W, MEM_W))
        M = mem.shape[1]
        mq = rms_norm(mq.reshape(B, S, MEM_HEADS, MEM_HEAD_DIM), mem_q_norm_w[l])
        mk = rms_norm(mk.reshape(B, M, MEM_HEADS, MEM_HEAD_DIM), mem_k_norm_w[l])
        mv = mv.reshape(B, M, MEM_HEADS, MEM_HEAD_DIM)
        o_mem = memory_attention(mq, mk, mv)

        gates = jax.nn.sigmoid(zg).reshape(B, S, N_BRANCH, D)
        merged = (gates[:, :, 0] * (o_swa @ w_br_swa[l])
                  + gates[:, :, 1] * (o_ret @ w_br_ret[l])
                  + gates[:, :, 2] * (o_mem @ w_br_mem[l]))
        x = x + merged @ w_out[l]

        h2 = rms_norm(x, ffn_norm_w[l])
        x = x + routed_experts(h2, w_router[l], b_router[l], w_up[l], b_up[l], w_down[l], b_down[l])
    return x
```

```python
import functools

import jax
import jax.numpy as jnp
from jax import lax
from jax.experimental import pallas as pl
from jax.experimental.pallas import tpu as pltpu

F32 = jnp.float32
BF16 = jnp.bfloat16
I32 = jnp.int32

D_MODEL = 1024
SWA_HEAD_DIM = 64
SWA_Q_HEADS = 8
SWA_KV_HEADS = 2
SWA_GROUP = SWA_Q_HEADS // SWA_KV_HEADS
WINDOW = 128
BAND_BLOCK = 128
RET_HEADS = 4
RET_QK_DIM = 128
RET_V_DIM = 256
RET_CHUNK = 128
ROPE_BASE = 10000.0
MEM_HEADS = 4
MEM_HEAD_DIM = 128
N_BRANCH = 3
N_EXPERTS = 32
TOP_K = 4
D_FF = 1024
SWIGLU_LIMIT = 7.0
SWIGLU_ALPHA = 1.702
EPS = 1e-6
NEG_INF = -1e30

SWA_Q_W = SWA_Q_HEADS * SWA_HEAD_DIM
SWA_KV_W = SWA_KV_HEADS * SWA_HEAD_DIM
RET_QK_W = RET_HEADS * RET_QK_DIM
RET_V_W = RET_HEADS * RET_V_DIM
MEM_W = MEM_HEADS * MEM_HEAD_DIM
IN_SIZES = (SWA_Q_W, SWA_KV_W, SWA_KV_W, RET_QK_W, RET_QK_W, RET_V_W, RET_V_W, MEM_W, N_BRANCH * D_MODEL)

SUBLANES = 8
LANES = 128
ROW_CHUNKS = D_MODEL // LANES
VMEM_LIMIT = 56 * 1024 * 1024

TM_PROJ = 512
TQ_MEM = 512
TR_ROUTE = 1024
TM_EXPERT = 256
TC_DISPATCH = 1024
TQ_COMBINE = 256


def _params(sem, vmem=VMEM_LIMIT):
    return pltpu.CompilerParams(dimension_semantics=sem, vmem_limit_bytes=vmem)


def _const_spec(shape):
    nd = len(shape)
    return pl.BlockSpec(shape, lambda *_: (0,) * nd, pipeline_mode=pl.Buffered(1))


def _rms(x, w):
    return x * lax.rsqrt(jnp.mean(x * x, axis=-1, keepdims=True) + EPS) * w


def _dot(a, b):
    return jnp.dot(a, b, preferred_element_type=F32)


def _dot_nt(a, b):
    return lax.dot_general(a, b, (((1,), (1,)), ((), ())), preferred_element_type=F32)


def _dot_tn(a, b):
    return lax.dot_general(a, b, (((0,), (0,)), ((), ())), preferred_element_type=F32)


def _inproj_body(x_ref, nw_ref, w_ref, *o_refs):
    h = _rms(x_ref[...], nw_ref[...]).astype(BF16)
    off = 0
    for o_ref in o_refs:
        s = o_ref.shape[-1]
        o_ref[...] = _dot(h, w_ref[:, off:off + s]).astype(o_ref.dtype)
        off += s


def _inproj(x2, norm_w, w_in_bf):
    T, D = x2.shape
    tm = TM_PROJ
    return pl.pallas_call(
        _inproj_body,
        grid=(T // tm,),
        in_specs=[pl.BlockSpec((tm, D), lambda i: (i, 0)),
                  _const_spec((1, D)),
                  _const_spec(w_in_bf.shape)],
        out_specs=[pl.BlockSpec((tm, s), lambda i: (i, 0)) for s in IN_SIZES],
        out_shape=[jax.ShapeDtypeStruct((T, s), BF16) for s in IN_SIZES],
        compiler_params=_params(("parallel",)),
        name="inproj",
    )(x2, norm_w.reshape(1, D), w_in_bf)


def _swa_body(sink_ref, q_ref, kc_ref, kp_ref, vc_ref, vp_ref, qw_ref, kw_ref, o_ref):
    n = pl.program_id(1)
    L = q_ref.shape[0]
    q = q_ref[...].astype(F32)
    k = jnp.concatenate([kp_ref[...], kc_ref[...]], axis=0).astype(F32)
    v = jnp.concatenate([vp_ref[...], vc_ref[...]], axis=0)
    qi = lax.broadcasted_iota(I32, (L, 2 * L), 0)
    kj = lax.broadcasted_iota(I32, (L, 2 * L), 1)
    diff = L + qi - kj
    has_prev = jnp.minimum(n, 1) * L
    mask = (diff >= 0) & (diff < WINDOW) & (kj + has_prev >= L)
    d = SWA_HEAD_DIM
    outs = []
    for h in range(SWA_KV_HEADS):
        kh = _rms(k[:, h * d:(h + 1) * d], kw_ref[...]).astype(BF16)
        vh = v[:, h * d:(h + 1) * d]
        for g in range(SWA_GROUP):
            hq = h * SWA_GROUP + g
            qh = _rms(q[:, hq * d:(hq + 1) * d], qw_ref[...]).astype(BF16)
            s = _dot_nt(qh, kh) * (d ** -0.5)
            s = jnp.where(mask, s, NEG_INF)
            sink = sink_ref[hq]
            m = jnp.maximum(jnp.max(s, axis=-1, keepdims=True), sink)
            p = jnp.exp(s - m)
            den = jnp.sum(p, axis=-1, keepdims=True) + jnp.exp(sink - m)
            outs.append(_dot((p / den).astype(BF16), vh))
    o_ref[...] = jnp.concatenate(outs, axis=-1).astype(o_ref.dtype)


def _swa(sq, sk, sv, q_norm_w, k_norm_w, sinks, B, S):
    L = BAND_BLOCK
    nb = S // L
    cur = lambda b, n: (b * nb + n, 0)
    prev = lambda b, n: (b * nb + jnp.maximum(n - 1, 0), 0)
    return pl.pallas_call(
        _swa_body,
        grid=(B, nb),
        in_specs=[pl.BlockSpec(memory_space=pltpu.SMEM),
                  pl.BlockSpec((L, SWA_Q_W), cur),
                  pl.BlockSpec((L, SWA_KV_W), cur),
                  pl.BlockSpec((L, SWA_KV_W), prev),
                  pl.BlockSpec((L, SWA_KV_W), cur),
                  pl.BlockSpec((L, SWA_KV_W), prev),
                  _const_spec((1, SWA_HEAD_DIM)),
                  _const_spec((1, SWA_HEAD_DIM))],
        out_specs=pl.BlockSpec((L, SWA_Q_W), cur),
        out_shape=jax.ShapeDtypeStruct((B * S, SWA_Q_W), BF16),
        compiler_params=_params(("parallel", "parallel")),
        name="swa",
    )(sinks.astype(F32), sq, sk, sk, sv, sv,
      q_norm_w.reshape(1, -1), k_norm_w.reshape(1, -1))


def _ret_body(q_ref, k_ref, v_ref, g_ref, cos_ref, sin_ref, dm_ref, qd_ref, kd_ref, cd_ref, nw_ref,
              o_ref, st_ref):
    n = pl.program_id(1)

    @pl.when(n == 0)
    def _():
        st_ref[...] = jnp.zeros_like(st_ref)

    cos = cos_ref[...]
    sin = sin_ref[...]
    dk, dv = RET_QK_DIM, RET_V_DIM
    outs = []
    for h in range(RET_HEADS):
        qh = q_ref[:, h * dk:(h + 1) * dk].astype(F32)
        kh = k_ref[:, h * dk:(h + 1) * dk].astype(F32)
        qr = qh * cos + pltpu.roll(qh, dk // 2, 1) * sin
        kr = (kh * cos + pltpu.roll(kh, dk // 2, 1) * sin) * (dk ** -0.5)
        vh = v_ref[:, h * dv:(h + 1) * dv]
        st = st_ref[h]
        inner = _dot_nt(qr.astype(BF16), kr.astype(BF16)) * dm_ref[h]
        o = _dot(inner.astype(BF16), vh) + _dot((qr * qd_ref[h]).astype(BF16), st.astype(BF16))
        st_ref[h] = st * cd_ref[h] + _dot_tn((kr * kd_ref[h]).astype(BF16), vh)
        mu = jnp.mean(o, axis=-1, keepdims=True)
        oc = o - mu
        var = jnp.mean(oc * oc, axis=-1, keepdims=True)
        y = oc * lax.rsqrt(var + EPS) * nw_ref[:, h * dv:(h + 1) * dv]
        g = g_ref[:, h * dv:(h + 1) * dv].astype(F32)
        outs.append(g * jax.nn.sigmoid(g) * y)
    o_ref[...] = jnp.concatenate(outs, axis=-1).astype(o_ref.dtype)


def _retention(rq, rk, rv, rg, ret_norm_w, B, S):
    L = RET_CHUNK
    nc = S // L
    half = RET_QK_DIM // 2
    inv = 1.0 / (ROPE_BASE ** (jnp.arange(0, half, dtype=F32) / half))
    ang = jnp.arange(S, dtype=F32)[:, None] * inv[None, :]
    cos = jnp.cos(ang)
    sin = jnp.sin(ang)
    cos2 = jnp.concatenate([cos, cos], axis=-1)
    sin2 = jnp.concatenate([-sin, sin], axis=-1)
    log_g = jnp.log1p(-(2.0 ** (-5.0 - jnp.arange(RET_HEADS, dtype=F32))))
    pos = jnp.arange(L, dtype=F32)
    diff = pos[:, None] - pos[None, :]
    decay_mask = jnp.where(diff[None] >= 0,
                           jnp.exp(jnp.maximum(diff, 0.0)[None] * log_g[:, None, None]), 0.0)
    q_decay = jnp.exp((pos[None, :] + 1.0) * log_g[:, None])[..., None]
    k_decay = jnp.exp((L - 1.0 - pos[None, :]) * log_g[:, None])[..., None]
    chunk_decay = jnp.exp(L * log_g)
    qd = jnp.broadcast_to(q_decay, (RET_HEADS, L, RET_QK_DIM))
    kd = jnp.broadcast_to(k_decay, (RET_HEADS, L, RET_QK_DIM))
    row = lambda b, n: (b * nc + n, 0)
    return pl.pallas_call(
        _ret_body,
        grid=(B, nc),
        in_specs=[pl.BlockSpec((L, RET_QK_W), row),
                  pl.BlockSpec((L, RET_QK_W), row),
                  pl.BlockSpec((L, RET_V_W), row),
                  pl.BlockSpec((L, RET_V_W), row),
                  pl.BlockSpec((L, RET_QK_DIM), lambda b, n: (n, 0)),
                  pl.BlockSpec((L, RET_QK_DIM), lambda b, n: (n, 0)),
                  _const_spec((RET_HEADS, L, L)),
                  _const_spec((RET_HEADS, L, RET_QK_DIM)),
                  _const_spec((RET_HEADS, L, RET_QK_DIM)),
                  pl.BlockSpec(memory_space=pltpu.SMEM),
                  _const_spec((1, RET_V_W))],
        out_specs=pl.BlockSpec((L, RET_V_W), row),
        out_shape=jax.ShapeDtypeStruct((B * S, RET_V_W), BF16),
        scratch_shapes=[pltpu.VMEM((RET_HEADS, RET_QK_DIM, RET_V_DIM), F32)],
        compiler_params=_params(("parallel", "arbitrary")),
        name="retention",
    )(rq, rk, rv, rg, cos2, sin2, decay_mask, qd, kd, chunk_decay, ret_norm_w.reshape(1, -1))


def _memkv_body(m_ref, nw_ref, w_ref, kw_ref, k_ref, v_ref):
    m = _rms(m_ref[...], nw_ref[...]).astype(BF16)
    kv = _dot(m, w_ref[...])
    d = MEM_HEAD_DIM
    ks = [_rms(kv[:, h * d:(h + 1) * d], kw_ref[...]) for h in range(MEM_HEADS)]
    k_ref[...] = jnp.concatenate(ks, axis=-1).astype(k_ref.dtype)
    v_ref[...] = kv[:, MEM_W:].astype(v_ref.dtype)


def _memkv(mem2, mem_norm_w, w_kv_bf, k_norm_w, B, M):
    D = mem2.shape[-1]
    return pl.pallas_call(
        _memkv_body,
        grid=(B,),
        in_specs=[pl.BlockSpec((M, D), lambda b: (b, 0)),
                  _const_spec((1, D)),
                  _const_spec(w_kv_bf.shape),
                  _const_spec((1, MEM_HEAD_DIM))],
        out_specs=[pl.BlockSpec((M, MEM_W), lambda b: (b, 0))] * 2,
        out_shape=[jax.ShapeDtypeStruct((B * M, MEM_W), BF16)] * 2,
        compiler_params=_params(("parallel",)),
        name="memkv",
    )(mem2, mem_norm_w.reshape(1, D), w_kv_bf, k_norm_w.reshape(1, -1))


def _memattn_body(q_ref, k_ref, v_ref, qw_ref, o_ref):
    d = MEM_HEAD_DIM
    outs = []
    for h in range(MEM_HEADS):
        qh = _rms(q_ref[:, h * d:(h + 1) * d].astype(F32), qw_ref[...]).astype(BF16)
        s = _dot_nt(qh, k_ref[:, h * d:(h + 1) * d]) * (d ** -0.5)
        m = jnp.max(s, axis=-1, keepdims=True)
        p = jnp.exp(s - m)
        pr = p / jnp.sum(p, axis=-1, keepdims=True)
        outs.append(_dot(pr.astype(BF16), v_ref[:, h * d:(h + 1) * d]))
    o_ref[...] = jnp.concatenate(outs, axis=-1).astype(o_ref.dtype)


def _memattn(mq, mk, mv, q_norm_w, B, S, M):
    tq = min(TQ_MEM, S)
    nq = S // tq
    return pl.pallas_call(
        _memattn_body,
        grid=(B, nq),
        in_specs=[pl.BlockSpec((tq, MEM_W), lambda b, i: (b * nq + i, 0)),
                  pl.BlockSpec((M, MEM_W), lambda b, i: (b, 0)),
                  pl.BlockSpec((M, MEM_W), lambda b, i: (b, 0)),
                  _const_spec((1, MEM_HEAD_DIM))],
        out_specs=pl.BlockSpec((tq, MEM_W), lambda b, i: (b * nq + i, 0)),
        out_shape=jax.ShapeDtypeStruct((B * S, MEM_W), BF16),
        compiler_params=_params(("parallel", "parallel")),
        name="memattn",
    )(mq, mk, mv, q_norm_w.reshape(1, -1))


def _split_bf16(a):
    hi = a.astype(BF16)
    lo = (a - hi.astype(F32)).astype(BF16)
    return hi, lo


def _merge_body(x_ref, osw_ref, ort_ref, omm_ref, zg_ref, wa_ref, wr_ref, wm_ref, wo_ref, nw_ref,
                wrt_ref, brt_ref, x1_ref, h2_ref, idx_ref, gate_ref):
    D = D_MODEL
    tm = x_ref.shape[0]
    sg = lambda j: jax.nn.sigmoid(zg_ref[:, j * D:(j + 1) * D].astype(F32))
    merged = (sg(0) * _dot(osw_ref[...], wa_ref[...])
              + sg(1) * _dot(ort_ref[...], wr_ref[...])
              + sg(2) * _dot(omm_ref[...], wm_ref[...]))
    x1 = x_ref[...] + _dot(merged.astype(BF16), wo_ref[...])
    x1_ref[...] = x1
    h2 = _rms(x1, nw_ref[...])
    for j in range(ROW_CHUNKS):
        h2_ref[pl.ds(j, tm, stride=ROW_CHUNKS), :] = h2[:, j * LANES:(j + 1) * LANES]
    h_hi, h_lo = _split_bf16(h2)
    w_hi, w_lo = _split_bf16(wrt_ref[...])
    logits = _dot(h_hi, w_hi) + _dot(h_lo, w_hi) + _dot(h_hi, w_lo) + brt_ref[...]
    lane = lax.broadcasted_iota(I32, logits.shape, 1)
    vals, idxs = [], []
    l = logits
    for _ in range(TOP_K):
        m = jnp.max(l, axis=-1, keepdims=True)
        i = jnp.min(jnp.where(l == m, lane, LANES), axis=-1, keepdims=True)
        vals.append(m)
        idxs.append(i)
        l = jnp.where(lane == i, -jnp.inf, l)
    es = [jnp.exp(v - vals[0]) for v in vals]
    den = es[0] + es[1] + es[2] + es[3]
    idx_ref[...] = jnp.concatenate(idxs, axis=-1)
    gate_ref[...] = jnp.concatenate([e / den for e in es], axis=-1)


def _merge(x2, o_swa, o_ret, o_mem, zg, wa, wr, wm, wo, ffn_norm_w, w_router_pad, b_router_pad):
    T, D = x2.shape
    tm = TM_PROJ
    row = lambda i: (i, 0)
    return pl.pallas_call(
        _merge_body,
        grid=(T // tm,),
        in_specs=[pl.BlockSpec((tm, D), row),
                  pl.BlockSpec((tm, SWA_Q_W), row),
                  pl.BlockSpec((tm, RET_V_W), row),
                  pl.BlockSpec((tm, MEM_W), row),
                  pl.BlockSpec((tm, N_BRANCH * D), row),
                  _const_spec(wa.shape), _const_spec(wr.shape), _const_spec(wm.shape), _const_spec(wo.shape),
                  _const_spec((1, D)),
                  _const_spec(w_router_pad.shape), _const_spec((1, LANES))],
        out_specs=[pl.BlockSpec((tm, D), row),
                   pl.BlockSpec((tm * ROW_CHUNKS, LANES), row),
                   pl.BlockSpec((tm, TOP_K), row),
                   pl.BlockSpec((tm, TOP_K), row)],
        out_shape=[jax.ShapeDtypeStruct((T, D), F32),
                   jax.ShapeDtypeStruct((T * ROW_CHUNKS, LANES), F32),
                   jax.ShapeDtypeStruct((T, TOP_K), I32),
                   jax.ShapeDtypeStruct((T, TOP_K), F32)],
        compiler_params=_params(("parallel",)),
        name="merge",
    )(x2, o_swa, o_ret, o_mem, zg, wa, wr, wm, wo, ffn_norm_w.reshape(1, D), w_router_pad, b_router_pad)


def _onehots(idx):
    lane = lax.broadcasted_iota(I32, (idx.shape[0], LANES), 1)
    return [(lane == idx[:, k:k + 1]).astype(F32) for k in range(TOP_K)]


def _count_body(idx_ref, cnt_ref):
    @pl.when(pl.program_id(0) == 0)
    def _():
        cnt_ref[...] = jnp.zeros_like(cnt_ref)

    ohs = _onehots(idx_ref[...])
    m = ohs[0] + ohs[1] + ohs[2] + ohs[3]
    cnt_ref[...] += jnp.sum(m, axis=0, keepdims=True)


def _expert_counts(idx):
    T = idx.shape[0]
    tr = min(TR_ROUTE, T)
    return pl.pallas_call(
        _count_body,
        grid=(T // tr,),
        in_specs=[pl.BlockSpec((tr, TOP_K), lambda i: (i, 0))],
        out_specs=pl.BlockSpec((1, LANES), lambda i: (0, 0)),
        out_shape=jax.ShapeDtypeStruct((1, LANES), F32),
        compiler_params=_params(("arbitrary",)),
        name="expert_counts",
    )(idx)


def _positions_body(idx_ref, cnt_ref, pos_ref, run_ref):
    tr = idx_ref.shape[0]

    @pl.when(pl.program_id(0) == 0)
    def _():
        cnt = jnp.broadcast_to(cnt_ref[...], (SUBLANES, LANES))
        padded = jnp.ceil(cnt * (1.0 / TM_EXPERT)) * TM_EXPERT
        lane = lax.broadcasted_iota(I32, (SUBLANES, LANES), 1)
        acc = padded
        s = 1
        while s < LANES:
            acc = acc + jnp.where(lane >= s, pltpu.roll(acc, s, 1), 0.0)
            s *= 2
        run_ref[...] = (acc - padded)[0:1, :]

    ohs = _onehots(idx_ref[...])
    m = ohs[0] + ohs[1] + ohs[2] + ohs[3]
    r = lax.broadcasted_iota(I32, (tr, tr), 0)
    c = lax.broadcasted_iota(I32, (tr, tr), 1)
    lower = jnp.where(r > c, 1.0, 0.0).astype(BF16)
    before = _dot(lower, m.astype(BF16)) + run_ref[...]
    cols = [jnp.sum(oh * before, axis=-1, keepdims=True) for oh in ohs]
    pos_ref[...] = jnp.concatenate(cols, axis=-1).astype(I32)
    run_ref[...] += jnp.sum(m, axis=0, keepdims=True)


def _positions(idx, cnt):
    T = idx.shape[0]
    tr = min(TR_ROUTE, T)
    return pl.pallas_call(
        _positions_body,
        grid=(T // tr,),
        in_specs=[pl.BlockSpec((tr, TOP_K), lambda i: (i, 0)),
                  pl.BlockSpec((1, LANES), lambda i: (0, 0))],
        out_specs=pl.BlockSpec((tr, TOP_K), lambda i: (i, 0)),
        out_shape=jax.ShapeDtypeStruct((T, TOP_K), I32),
        scratch_shapes=[pltpu.VMEM((1, LANES), F32)],
        compiler_params=_params(("arbitrary",)),
        name="positions",
    )(idx, cnt)


def _row_copy(src_ref, src_row, dst_ref, dst_row, sem):
    return pltpu.make_async_copy(
        src_ref.at[pl.ds(pl.multiple_of(src_row * ROW_CHUNKS, ROW_CHUNKS), ROW_CHUNKS), :],
        dst_ref.at[pl.ds(pl.multiple_of(dst_row * ROW_CHUNKS, ROW_CHUNKS), ROW_CHUNKS), :],
        sem)


def _dispatch_body(cnt_ref, base_ref, pos_ref, h2_ref, xs_ref, zero_ref, sem, zsem):
    i = pl.program_id(0)
    npairs = pos_ref.shape[-1]
    tc = npairs // TOP_K

    @pl.when(i == 0)
    def _():
        zero_ref[...] = jnp.zeros_like(zero_ref)

        def per_expert(e, total):
            c = cnt_ref[e]
            npad = (TM_EXPERT - c % TM_EXPERT) % TM_EXPERT
            start = base_ref[e] + c

            def fill(r, _):
                _row_copy(zero_ref, 0, xs_ref, start + r, zsem).start()
                return 0

            lax.fori_loop(0, npad, fill, 0)
            return total + npad

        total = lax.fori_loop(0, N_EXPERTS, per_expert, 0)

        def drain(r, _):
            _row_copy(zero_ref, 0, xs_ref, 0, zsem).wait()
            return 0

        lax.fori_loop(0, total, drain, 0)

    def issue(p, _):
        _row_copy(h2_ref, i * tc + p // TOP_K, xs_ref, pos_ref[0, 0, p], sem).start()
        return 0

    lax.fori_loop(0, npairs, issue, 0)

    def drain_rows(p, _):
        _row_copy(h2_ref, 0, xs_ref, 0, sem).wait()
        return 0

    lax.fori_loop(0, npairs, drain_rows, 0)


def _dispatch(h2_rows, pos, cnt_i, base_i, n_rows):
    T = pos.shape[0]
    tc = min(TC_DISPATCH, T)
    pos3 = pos.reshape(T // tc, 1, tc * TOP_K)
    return pl.pallas_call(
        _dispatch_body,
        grid_spec=pltpu.PrefetchScalarGridSpec(
            num_scalar_prefetch=2,
            grid=(T // tc,),
            in_specs=[pl.BlockSpec((1, 1, tc * TOP_K), lambda i, *_: (i, 0, 0), memory_space=pltpu.SMEM),
                      pl.BlockSpec(memory_space=pl.ANY)],
            out_specs=pl.BlockSpec(memory_space=pl.ANY),
            scratch_shapes=[pltpu.VMEM((ROW_CHUNKS, LANES), F32),
                            pltpu.SemaphoreType.DMA(()),
                            pltpu.SemaphoreType.DMA(())]),
        out_shape=jax.ShapeDtypeStruct((n_rows * ROW_CHUNKS, LANES), F32),
        compiler_params=pltpu.CompilerParams(dimension_semantics=("arbitrary",), has_side_effects=True),
        name="dispatch",
    )(cnt_i, base_i, pos3, h2_rows)


def _expert_body(te_ref, nu_ref, x_ref, wg_ref, wl_ref, bg_ref, bl_ref, wd_ref, bd_ref, o_ref):
    tm = x_ref.shape[0] // ROW_CHUNKS

    @pl.when(pl.program_id(0) < nu_ref[0])
    def _():
        x = jnp.concatenate([x_ref[pl.ds(j, tm, stride=ROW_CHUNKS), :] for j in range(ROW_CHUNKS)],
                            axis=-1).astype(BF16)
        hg = _dot(x, wg_ref[...]) + bg_ref[...]
        hl = _dot(x, wl_ref[...]) + bl_ref[...]
        x_glu = jnp.minimum(hg, SWIGLU_LIMIT)
        x_lin = jnp.clip(hl, -SWIGLU_LIMIT, SWIGLU_LIMIT)
        act = x_glu * jax.nn.sigmoid(SWIGLU_ALPHA * x_glu) * (x_lin + 1.0)
        out = _dot(act.astype(BF16), wd_ref[...]) + bd_ref[...]
        for j in range(ROW_CHUNKS):
            o_ref[pl.ds(j, tm, stride=ROW_CHUNKS), :] = out[:, j * LANES:(j + 1) * LANES]


def _experts(xs_rows, tile_expert, n_used, wg, wl, bg, bl, wd, bd):
    n_rows = xs_rows.shape[0] // ROW_CHUNKS
    tm = TM_EXPERT
    nt = n_rows // tm
    D, FF = wg.shape[1], wg.shape[2]
    rows = lambda i, te, nu: (jnp.minimum(i, nu[0] - 1), 0)
    wsel = lambda i, te, nu: (te[i], 0, 0)
    return pl.pallas_call(
        _expert_body,
        grid_spec=pltpu.PrefetchScalarGridSpec(
            num_scalar_prefetch=2,
            grid=(nt,),
            in_specs=[pl.BlockSpec((tm * ROW_CHUNKS, LANES), rows),
                      pl.BlockSpec((None, D, FF), wsel),
                      pl.BlockSpec((None, D, FF), wsel),
                      pl.BlockSpec((None, 1, FF), wsel),
                      pl.BlockSpec((None, 1, FF), wsel),
                      pl.BlockSpec((None, FF, D), wsel),
                      pl.BlockSpec((None, 1, D), wsel)],
            out_specs=pl.BlockSpec((tm * ROW_CHUNKS, LANES), rows)),
        out_shape=jax.ShapeDtypeStruct(xs_rows.shape, F32),
        compiler_params=_params(("arbitrary",)),
        name="experts",
    )(tile_expert, n_used, xs_rows, wg, wl, bg, bl, wd, bd)


def _combine_body(pos_ref, ys_ref, x1_ref, gate_ref, o_ref, buf_ref, sem):
    tq = x1_ref.shape[0]
    npairs = tq * TOP_K

    def issue(p, _):
        slot = (p % TOP_K) * tq + p // TOP_K
        _row_copy(ys_ref, pos_ref[0, 0, p], buf_ref, slot, sem).start()
        return 0

    lax.fori_loop(0, npairs, issue, 0)

    def drain(p, _):
        _row_copy(ys_ref, 0, buf_ref, 0, sem).wait()
        return 0

    lax.fori_loop(0, npairs, drain, 0)

    g = gate_ref[...]
    for j in range(ROW_CHUNKS):
        acc = x1_ref[:, j * LANES:(j + 1) * LANES]
        for k in range(TOP_K):
            acc = acc + g[:, k:k + 1] * buf_ref[pl.ds(k * tq * ROW_CHUNKS + j, tq, stride=ROW_CHUNKS), :]
        o_ref[:, j * LANES:(j + 1) * LANES] = acc


def _combine(ys_rows, pos, x1, gates):
    T, D = x1.shape
    tq = min(TQ_COMBINE, T)
    pos3 = pos.reshape(T // tq, 1, tq * TOP_K)
    return pl.pallas_call(
        _combine_body,
        grid=(T // tq,),
        in_specs=[pl.BlockSpec((1, 1, tq * TOP_K), lambda i: (i, 0, 0), memory_space=pltpu.SMEM),
                  pl.BlockSpec(memory_space=pl.ANY),
                  pl.BlockSpec((tq, D), lambda i: (i, 0)),
                  pl.BlockSpec((tq, TOP_K), lambda i: (i, 0))],
        out_specs=pl.BlockSpec((tq, D), lambda i: (i, 0)),
        out_shape=jax.ShapeDtypeStruct((T, D), F32),
        scratch_shapes=[pltpu.VMEM((TOP_K * tq * ROW_CHUNKS, LANES), F32),
                        pltpu.SemaphoreType.DMA(())],
        compiler_params=_params(("arbitrary",)),
        name="combine",
    )(pos3, ys_rows, x1, gates)


def _layer(x, mem, mix_norm_w, mem_norm_w, w_in, swa_q_norm_w, swa_k_norm_w, swa_sinks, ret_norm_w,
           w_mem_kv, mem_q_norm_w, mem_k_norm_w, w_br_swa, w_br_ret, w_br_mem, w_out, ffn_norm_w,
           w_router, b_router, w_up, b_up, w_down, b_down):
    B, S, D = x.shape
    M = mem.shape[1]
    T = B * S
    x2 = x.reshape(T, D)

    sq, sk, sv, rq, rk, rv, rg, mq, zg = _inproj(x2, mix_norm_w, w_in.astype(BF16))
    o_swa = _swa(sq, sk, sv, swa_q_norm_w, swa_k_norm_w, swa_sinks, B, S)
    o_ret = _retention(rq, rk, rv, rg, ret_norm_w, B, S)
    mk, mv = _memkv(mem.reshape(B * M, D), mem_norm_w, w_mem_kv.astype(BF16), mem_k_norm_w, B, M)
    o_mem = _memattn(mq, mk, mv, mem_q_norm_w, B, S, M)

    E = w_router.shape[-1]
    w_router_pad = jnp.zeros((D, LANES), F32).at[:, :E].set(w_router)
    b_router_pad = jnp.full((1, LANES), NEG_INF, F32).at[0, :E].set(b_router)
    x1, h2_rows, idx, gates = _merge(
        x2, o_swa, o_ret, o_mem, zg, w_br_swa.astype(BF16), w_br_ret.astype(BF16), w_br_mem.astype(BF16),
        w_out.astype(BF16), ffn_norm_w, w_router_pad, b_router_pad)

    cnt = _expert_counts(idx)
    pos = _positions(idx, cnt)
    cnt_i = cnt[0, :E].astype(I32)
    tiles_e = (cnt_i + TM_EXPERT - 1) // TM_EXPERT
    tile_end = jnp.cumsum(tiles_e)
    base_i = (tile_end - tiles_e) * TM_EXPERT
    n_used = tile_end[-1:]
    n_rows = T * TOP_K + E * TM_EXPERT
    nt = n_rows // TM_EXPERT
    tile_ids = jnp.minimum(jnp.arange(nt, dtype=I32), n_used[0] - 1)
    tile_expert = jnp.minimum(jnp.searchsorted(tile_end, tile_ids, side="right"), E - 1).astype(I32)

    xs_rows = _dispatch(h2_rows, pos, cnt_i, base_i, n_rows)
    wg = w_up[:, :, 0::2].astype(BF16)
    wl = w_up[:, :, 1::2].astype(BF16)
    bg = b_up[:, None, 0::2]
    bl = b_up[:, None, 1::2]
    ys_rows = _experts(xs_rows, tile_expert, n_used.astype(I32), wg, wl, bg, bl,
                       w_down.astype(BF16), b_down[:, None, :])
    out = _combine(ys_rows, pos, x1, gates)
    return out.reshape(B, S, D)


def kernel(x, mem, mix_norm_w, mem_norm_w, w_in, swa_q_norm_w, swa_k_norm_w, swa_sinks, ret_norm_w, w_mem_kv, mem_q_norm_w, mem_k_norm_w, w_br_swa, w_br_ret, w_br_mem, w_out, ffn_norm_w, w_router, b_router, w_up, b_up, w_down, b_down):
    args = (x, mem, mix_norm_w, mem_norm_w, w_in, swa_q_norm_w, swa_k_norm_w, swa_sinks, ret_norm_w,
            w_mem_kv, mem_q_norm_w, mem_k_norm_w, w_br_swa, w_br_ret, w_br_mem, w_out, ffn_norm_w,
            w_router, b_router, w_up, b_up, w_down, b_down)
    for l in range(w_in.shape[0]):
        x = _layer(x, mem, *[a[l] for a in args[2:]])
    return x
```

```python
import jax
import jax.numpy as jnp
from jax import lax
from jax.experimental import pallas as pl
from jax.experimental.pallas import tpu as pltpu

F32 = jnp.float32
BF16 = jnp.bfloat16
I32 = jnp.int32

D_MODEL = 1024
SWA_HEAD_DIM = 64
SWA_Q_HEADS = 8
SWA_KV_HEADS = 2
SWA_GROUP = SWA_Q_HEADS // SWA_KV_HEADS
WINDOW = 128
BAND_BLOCK = 128
RET_HEADS = 4
RET_QK_DIM = 128
RET_V_DIM = 256
RET_CHUNK = 128
ROPE_BASE = 10000.0
MEM_HEADS = 4
MEM_HEAD_DIM = 128
N_BRANCH = 3
N_EXPERTS = 32
TOP_K = 4
D_FF = 1024
SWIGLU_LIMIT = 7.0
SWIGLU_ALPHA = 1.702
EPS = 1e-6
NEG_INF = -1e30

SWA_Q_W = SWA_Q_HEADS * SWA_HEAD_DIM
SWA_KV_W = SWA_KV_HEADS * SWA_HEAD_DIM
RET_QK_W = RET_HEADS * RET_QK_DIM
RET_V_W = RET_HEADS * RET_V_DIM
MEM_W = MEM_HEADS * MEM_HEAD_DIM
IN_SIZES = (SWA_Q_W, SWA_KV_W, SWA_KV_W, RET_QK_W, RET_QK_W, RET_V_W, RET_V_W, MEM_W, N_BRANCH * D_MODEL)

SUBLANES = 8
LANES = 128
MXU_COLS = 256
VMEM_LIMIT = 56 * 1024 * 1024

TM_PROJ = 512
TQ_MEM = 512
TM_EXPERT = 256
TS_SORT = 256
SEG_ALIGN = SUBLANES
LOCAL_ROWS = TS_SORT * TOP_K + N_EXPERTS * SEG_ALIGN
LOCAL_CHUNKS = LOCAL_ROWS // SEG_ALIGN


def _params(sem, vmem=VMEM_LIMIT):
    return pltpu.CompilerParams(dimension_semantics=sem, vmem_limit_bytes=vmem)


def _const_spec(shape):
    nd = len(shape)
    return pl.BlockSpec(shape, lambda *_: (0,) * nd, pipeline_mode=pl.Buffered(1))


def _rms(x, w):
    return x * lax.rsqrt(jnp.mean(x * x, axis=-1, keepdims=True) + EPS) * w


def _dot(a, b):
    return jnp.dot(a, b, preferred_element_type=F32)


def _dot_nt(a, b):
    return lax.dot_general(a, b, (((1,), (1,)), ((), ())), preferred_element_type=F32)


def _dot_tn(a, b):
    return lax.dot_general(a, b, (((0,), (0,)), ((), ())), preferred_element_type=F32)


def _inproj_body(x_ref, nw_ref, w_ref, *o_refs):
    h = _rms(x_ref[...], nw_ref[...]).astype(BF16)
    off = 0
    for o_ref in o_refs:
        s = o_ref.shape[-1]
        o_ref[...] = _dot(h, w_ref[:, off:off + s]).astype(o_ref.dtype)
        off += s


def _inproj(x2, norm_w, w_in_bf):
    T, D = x2.shape
    tm = TM_PROJ
    return pl.pallas_call(
        _inproj_body,
        grid=(T // tm,),
        in_specs=[pl.BlockSpec((tm, D), lambda i: (i, 0)),
                  _const_spec((1, D)),
                  _const_spec(w_in_bf.shape)],
        out_specs=[pl.BlockSpec((tm, s), lambda i: (i, 0)) for s in IN_SIZES],
        out_shape=[jax.ShapeDtypeStruct((T, s), BF16) for s in IN_SIZES],
        compiler_params=_params(("parallel",)),
        name="inproj",
    )(x2, norm_w.reshape(1, D), w_in_bf)


def _swa_body(sink_ref, q_ref, kc_ref, kp_ref, vc_ref, vp_ref, qw_ref, kw_ref, o_ref):
    n = pl.program_id(1)
    L = q_ref.shape[0]
    q = q_ref[...].astype(F32)
    k = jnp.concatenate([kp_ref[...], kc_ref[...]], axis=0).astype(F32)
    v = jnp.concatenate([vp_ref[...], vc_ref[...]], axis=0)
    qi = lax.broadcasted_iota(I32, (L, 2 * L), 0)
    kj = lax.broadcasted_iota(I32, (L, 2 * L), 1)
    diff = L + qi - kj
    has_prev = jnp.minimum(n, 1) * L
    mask = (diff >= 0) & (diff < WINDOW) & (kj + has_prev >= L)
    d = SWA_HEAD_DIM
    outs = []
    for h in range(SWA_KV_HEADS):
        kh = _rms(k[:, h * d:(h + 1) * d], kw_ref[...]).astype(BF16)
        vh = v[:, h * d:(h + 1) * d]
        for g in range(SWA_GROUP):
            hq = h * SWA_GROUP + g
            qh = _rms(q[:, hq * d:(hq + 1) * d], qw_ref[...]).astype(BF16)
            s = _dot_nt(qh, kh) * (d ** -0.5)
            s = jnp.where(mask, s, NEG_INF)
            sink = sink_ref[hq]
            m = jnp.maximum(jnp.max(s, axis=-1, keepdims=True), sink)
            p = jnp.exp(s - m)
            den = jnp.sum(p, axis=-1, keepdims=True) + jnp.exp(sink - m)
            outs.append(_dot((p / den).astype(BF16), vh))
    o_ref[...] = jnp.concatenate(outs, axis=-1).astype(o_ref.dtype)


def _swa(sq, sk, sv, q_norm_w, k_norm_w, sinks, B, S):
    L = BAND_BLOCK
    nb = S // L
    cur = lambda b, n: (b * nb + n, 0)
    prev = lambda b, n: (b * nb + jnp.maximum(n - 1, 0), 0)
    return pl.pallas_call(
        _swa_body,
        grid=(B, nb),
        in_specs=[pl.BlockSpec(memory_space=pltpu.SMEM),
                  pl.BlockSpec((L, SWA_Q_W), cur),
                  pl.BlockSpec((L, SWA_KV_W), cur),
                  pl.BlockSpec((L, SWA_KV_W), prev),
                  pl.BlockSpec((L, SWA_KV_W), cur),
                  pl.BlockSpec((L, SWA_KV_W), prev),
                  _const_spec((1, SWA_HEAD_DIM)),
                  _const_spec((1, SWA_HEAD_DIM))],
        out_specs=pl.BlockSpec((L, SWA_Q_W), cur),
        out_shape=jax.ShapeDtypeStruct((B * S, SWA_Q_W), BF16),
        compiler_params=_params(("parallel", "parallel")),
        name="swa",
    )(sinks.astype(F32), sq, sk, sk, sv, sv,
      q_norm_w.reshape(1, -1), k_norm_w.reshape(1, -1))


def _ret_body(q_ref, k_ref, v_ref, g_ref, cos_ref, sin_ref, dm_ref, qd_ref, kd_ref, cd_ref, nw_ref,
              o_ref, st_ref):
    n = pl.program_id(1)

    @pl.when(n == 0)
    def _():
        st_ref[...] = jnp.zeros_like(st_ref)

    cos = cos_ref[...]
    sin = sin_ref[...]
    dk, dv = RET_QK_DIM, RET_V_DIM
    outs = []
    for h in range(RET_HEADS):
        qh = q_ref[:, h * dk:(h + 1) * dk].astype(F32)
        kh = k_ref[:, h * dk:(h + 1) * dk].astype(F32)
        qr = qh * cos + pltpu.roll(qh, dk // 2, 1) * sin
        kr = (kh * cos + pltpu.roll(kh, dk // 2, 1) * sin) * (dk ** -0.5)
        vh = v_ref[:, h * dv:(h + 1) * dv]
        st = st_ref[h]
        inner = _dot_nt(qr.astype(BF16), kr.astype(BF16)) * dm_ref[h]
        o = _dot(inner.astype(BF16), vh) + _dot((qr * qd_ref[h]).astype(BF16), st.astype(BF16))
        st_ref[h] = st * cd_ref[h] + _dot_tn((kr * kd_ref[h]).astype(BF16), vh)
        mu = jnp.mean(o, axis=-1, keepdims=True)
        oc = o - mu
        var = jnp.mean(oc * oc, axis=-1, keepdims=True)
        y = oc * lax.rsqrt(var + EPS) * nw_ref[:, h * dv:(h + 1) * dv]
        g = g_ref[:, h * dv:(h + 1) * dv].astype(F32)
        outs.append(g * jax.nn.sigmoid(g) * y)
    o_ref[...] = jnp.concatenate(outs, axis=-1).astype(o_ref.dtype)


def _retention(rq, rk, rv, rg, ret_norm_w, B, S):
    L = RET_CHUNK
    nc = S // L
    half = RET_QK_DIM // 2
    inv = 1.0 / (ROPE_BASE ** (jnp.arange(0, half, dtype=F32) / half))
    ang = jnp.arange(S, dtype=F32)[:, None] * inv[None, :]
    cos = jnp.cos(ang)
    sin = jnp.sin(ang)
    cos2 = jnp.concatenate([cos, cos], axis=-1)
    sin2 = jnp.concatenate([-sin, sin], axis=-1)
    log_g = jnp.log1p(-(2.0 ** (-5.0 - jnp.arange(RET_HEADS, dtype=F32))))
    pos = jnp.arange(L, dtype=F32)
    diff = pos[:, None] - pos[None, :]
    decay_mask = jnp.where(diff[None] >= 0,
                           jnp.exp(jnp.maximum(diff, 0.0)[None] * log_g[:, None, None]), 0.0)
    q_decay = jnp.exp((pos[None, :] + 1.0) * log_g[:, None])[..., None]
    k_decay = jnp.exp((L - 1.0 - pos[None, :]) * log_g[:, None])[..., None]
    chunk_decay = jnp.exp(L * log_g)
    qd = jnp.broadcast_to(q_decay, (RET_HEADS, L, RET_QK_DIM))
    kd = jnp.broadcast_to(k_decay, (RET_HEADS, L, RET_QK_DIM))
    row = lambda b, n: (b * nc + n, 0)
    return pl.pallas_call(
        _ret_body,
        grid=(B, nc),
        in_specs=[pl.BlockSpec((L, RET_QK_W), row),
                  pl.BlockSpec((L, RET_QK_W), row),
                  pl.BlockSpec((L, RET_V_W), row),
                  pl.BlockSpec((L, RET_V_W), row),
                  pl.BlockSpec((L, RET_QK_DIM), lambda b, n: (n, 0)),
                  pl.BlockSpec((L, RET_QK_DIM), lambda b, n: (n, 0)),
                  _const_spec((RET_HEADS, L, L)),
                  _const_spec((RET_HEADS, L, RET_QK_DIM)),
                  _const_spec((RET_HEADS, L, RET_QK_DIM)),
                  pl.BlockSpec(memory_space=pltpu.SMEM),
                  _const_spec((1, RET_V_W))],
        out_specs=pl.BlockSpec((L, RET_V_W), row),
        out_shape=jax.ShapeDtypeStruct((B * S, RET_V_W), BF16),
        scratch_shapes=[pltpu.VMEM((RET_HEADS, RET_QK_DIM, RET_V_DIM), F32)],
        compiler_params=_params(("parallel", "arbitrary")),
        name="retention",
    )(rq, rk, rv, rg, cos2, sin2, decay_mask, qd, kd, chunk_decay, ret_norm_w.reshape(1, -1))


def _memkv_body(m_ref, nw_ref, w_ref, kw_ref, k_ref, v_ref):
    m = _rms(m_ref[...], nw_ref[...]).astype(BF16)
    kv = _dot(m, w_ref[...])
    d = MEM_HEAD_DIM
    ks = [_rms(kv[:, h * d:(h + 1) * d], kw_ref[...]) for h in range(MEM_HEADS)]
    k_ref[...] = jnp.concatenate(ks, axis=-1).astype(k_ref.dtype)
    v_ref[...] = kv[:, MEM_W:].astype(v_ref.dtype)


def _memkv(mem2, mem_norm_w, w_kv_bf, k_norm_w, B, M):
    D = mem2.shape[-1]
    return pl.pallas_call(
        _memkv_body,
        grid=(B,),
        in_specs=[pl.BlockSpec((M, D), lambda b: (b, 0)),
                  _const_spec((1, D)),
                  _const_spec(w_kv_bf.shape),
                  _const_spec((1, MEM_HEAD_DIM))],
        out_specs=[pl.BlockSpec((M, MEM_W), lambda b: (b, 0))] * 2,
        out_shape=[jax.ShapeDtypeStruct((B * M, MEM_W), BF16)] * 2,
        compiler_params=_params(("parallel",)),
        name="memkv",
    )(mem2, mem_norm_w.reshape(1, D), w_kv_bf, k_norm_w.reshape(1, -1))


def _memattn_body(q_ref, k_ref, v_ref, qw_ref, o_ref):
    d = MEM_HEAD_DIM
    outs = []
    for h in range(MEM_HEADS):
        qh = _rms(q_ref[:, h * d:(h + 1) * d].astype(F32), qw_ref[...]).astype(BF16)
        s = _dot_nt(qh, k_ref[:, h * d:(h + 1) * d]) * (d ** -0.5)
        m = jnp.max(s, axis=-1, keepdims=True)
        p = jnp.exp(s - m)
        pr = p / jnp.sum(p, axis=-1, keepdims=True)
        outs.append(_dot(pr.astype(BF16), v_ref[:, h * d:(h + 1) * d]))
    o_ref[...] = jnp.concatenate(outs, axis=-1).astype(o_ref.dtype)


def _memattn(mq, mk, mv, q_norm_w, B, S, M):
    tq = min(TQ_MEM, S)
    nq = S // tq
    return pl.pallas_call(
        _memattn_body,
        grid=(B, nq),
        in_specs=[pl.BlockSpec((tq, MEM_W), lambda b, i: (b * nq + i, 0)),
                  pl.BlockSpec((M, MEM_W), lambda b, i: (b, 0)),
                  pl.BlockSpec((M, MEM_W), lambda b, i: (b, 0)),
                  _const_spec((1, MEM_HEAD_DIM))],
        out_specs=pl.BlockSpec((tq, MEM_W), lambda b, i: (b * nq + i, 0)),
        out_shape=jax.ShapeDtypeStruct((B * S, MEM_W), BF16),
        compiler_params=_params(("parallel", "parallel")),
        name="memattn",
    )(mq, mk, mv, q_norm_w.reshape(1, -1))


def _split_bf16(a):
    hi = a.astype(BF16)
    lo = (a - hi.astype(F32)).astype(BF16)
    return hi, lo


def _merge_body(x_ref, osw_ref, ort_ref, omm_ref, zg_ref, wa_ref, wr_ref, wm_ref, wo_ref, nw_ref,
                wrt_ref, brt_ref, x1_ref, h2_ref, idx_ref, gate_ref):
    D = D_MODEL
    sg = lambda j: jax.nn.sigmoid(zg_ref[:, j * D:(j + 1) * D].astype(F32))
    merged = (sg(0) * _dot(osw_ref[...], wa_ref[...])
              + sg(1) * _dot(ort_ref[...], wr_ref[...])
              + sg(2) * _dot(omm_ref[...], wm_ref[...]))
    x1 = x_ref[...] + _dot(merged.astype(BF16), wo_ref[...])
    x1_ref[...] = x1
    h2 = _rms(x1, nw_ref[...])
    h2_ref[...] = h2.astype(h2_ref.dtype)
    h_hi, h_lo = _split_bf16(h2)
    w_hi, w_lo = _split_bf16(wrt_ref[...])
    logits = _dot(h_hi, w_hi) + _dot(h_lo, w_hi) + _dot(h_hi, w_lo) + brt_ref[...]
    lane = lax.broadcasted_iota(I32, logits.shape, 1)
    vals, idxs = [], []
    l = logits
    for _ in range(TOP_K):
        m = jnp.max(l, axis=-1, keepdims=True)
        i = jnp.min(jnp.where(l == m, lane, LANES), axis=-1, keepdims=True)
        vals.append(m)
        idxs.append(i)
        l = jnp.where(lane == i, -jnp.inf, l)
    es = [jnp.exp(v - vals[0]) for v in vals]
    den = es[0] + es[1] + es[2] + es[3]
    idx_ref[...] = jnp.concatenate(idxs, axis=-1)
    gate_ref[...] = jnp.concatenate([e / den for e in es], axis=-1)


def _merge(x2, o_swa, o_ret, o_mem, zg, wa, wr, wm, wo, ffn_norm_w, w_router_pad, b_router_pad):
    T, D = x2.shape
    tm = TM_PROJ
    row = lambda i: (i, 0)
    return pl.pallas_call(
        _merge_body,
        grid=(T // tm,),
        in_specs=[pl.BlockSpec((tm, D), row),
                  pl.BlockSpec((tm, SWA_Q_W), row),
                  pl.BlockSpec((tm, RET_V_W), row),
                  pl.BlockSpec((tm, MEM_W), row),
                  pl.BlockSpec((tm, N_BRANCH * D), row),
                  _const_spec(wa.shape), _const_spec(wr.shape), _const_spec(wm.shape), _const_spec(wo.shape),
                  _const_spec((1, D)),
                  _const_spec(w_router_pad.shape), _const_spec((1, LANES))],
        out_specs=[pl.BlockSpec((tm, D), row),
                   pl.BlockSpec((tm, D), row),
                   pl.BlockSpec((tm, TOP_K), row),
                   pl.BlockSpec((tm, TOP_K), row)],
        out_shape=[jax.ShapeDtypeStruct((T, D), F32),
                   jax.ShapeDtypeStruct((T, D), BF16),
                   jax.ShapeDtypeStruct((T, TOP_K), I32),
                   jax.ShapeDtypeStruct((T, TOP_K), F32)],
        compiler_params=_params(("parallel",)),
        name="merge",
    )(x2, o_swa, o_ret, o_mem, zg, wa, wr, wm, wo, ffn_norm_w.reshape(1, D), w_router_pad, b_router_pad)


def _wprep_body(w_ref, o_ref):
    half = MXU_COLS // 2
    i = lax.broadcasted_iota(I32, (MXU_COLS, MXU_COLS), 0)
    j = lax.broadcasted_iota(I32, (MXU_COLS, MXU_COLS), 1)
    src = jnp.where(j < half, 2 * j, 2 * (j - half) + 1)
    perm = jnp.where(i == src, 1.0, 0.0).astype(BF16)
    for b in range(w_ref.shape[-1] // MXU_COLS):
        cols = slice(b * MXU_COLS, (b + 1) * MXU_COLS)
        o_ref[:, cols] = _dot(w_ref[:, cols].astype(BF16), perm).astype(o_ref.dtype)


def _wprep(w_up):
    E, D, F2 = w_up.shape
    return pl.pallas_call(
        _wprep_body,
        grid=(E,),
        in_specs=[pl.BlockSpec((None, D, F2), lambda e: (e, 0, 0))],
        out_specs=pl.BlockSpec((None, D, F2), lambda e: (e, 0, 0)),
        out_shape=jax.ShapeDtypeStruct((E, D, F2), BF16),
        compiler_params=_params(("parallel",)),
        name="wprep",
    )(w_up)


def _tile_counts_body(idx_ref, cnt_ref):
    idx = idx_ref[...]
    lane = lax.broadcasted_iota(I32, (idx.shape[0], LANES), 1)
    m = sum((lane == idx[:, k:k + 1]).astype(F32) for k in range(TOP_K))
    cnt_ref[pl.ds(pl.program_id(0), 1), :] = jnp.sum(m, axis=0, keepdims=True)


def _tile_counts(idx, ts):
    T = idx.shape[0]
    nt = T // ts
    return pl.pallas_call(
        _tile_counts_body,
        grid=(nt,),
        in_specs=[pl.BlockSpec((ts, TOP_K), lambda s: (s, 0))],
        out_specs=pl.BlockSpec((nt, LANES), lambda s: (0, 0)),
        out_shape=jax.ShapeDtypeStruct((nt, LANES), F32),
        compiler_params=_params(("arbitrary",)),
        name="tile_counts",
    )(idx)


def _chunk_copy(src_ref, src_chunk, dst_ref, dst_chunk, sem):
    def rows(c):
        start = c * SEG_ALIGN
        return pl.ds(start if isinstance(c, int) else pl.multiple_of(start, SEG_ALIGN), SEG_ALIGN)

    return pltpu.make_async_copy(src_ref.at[rows(src_chunk), :], dst_ref.at[rows(dst_chunk), :], sem)


def _dispatch_body(fill_start_ref, fill_n_ref, gch_ref, h2_ref, idxt_ref, loff_ref,
                   xs_ref, rowt_ref, buf_ref, zero_ref, sem, zsem):
    s = pl.program_id(0)
    ts = h2_ref.shape[0]
    nchunks = gch_ref.shape[-1]
    local_rows = nchunks * SEG_ALIGN

    @pl.when(s == 0)
    def _():
        zero_ref[...] = jnp.zeros_like(zero_ref)

        def per_expert(e, total):
            def fill(r, _):
                _chunk_copy(zero_ref, 0, xs_ref, fill_start_ref[e] + r, zsem).start()
                return 0

            lax.fori_loop(0, fill_n_ref[e], fill, 0)
            return total + fill_n_ref[e]

        total = lax.fori_loop(0, N_EXPERTS, per_expert, 0)

        def drain(r, _):
            _chunk_copy(zero_ref, 0, xs_ref, 0, zsem).wait()
            return 0

        lax.fori_loop(0, total, drain, 0)

    idxt = idxt_ref[...]
    sub = lax.broadcasted_iota(I32, (LANES, ts), 0)
    ohs = [(sub == idxt[k:k + 1, :]).astype(F32) for k in range(TOP_K)]
    m = ohs[0] + ohs[1] + ohs[2] + ohs[3]
    tr = lax.broadcasted_iota(I32, (ts, ts), 0)
    tc = lax.broadcasted_iota(I32, (ts, ts), 1)
    earlier = jnp.where(tr < tc, 1.0, 0.0).astype(BF16)
    before = _dot(m.astype(BF16), earlier) + loff_ref[...]
    rows = [jnp.sum(oh * before, axis=0, keepdims=True).astype(I32) for oh in ohs]
    rowt_ref[...] = jnp.concatenate(rows, axis=0)
    ri = lax.broadcasted_iota(I32, (local_rows, ts), 0)
    sel = (ri == rows[0]) | (ri == rows[1]) | (ri == rows[2]) | (ri == rows[3])
    buf_ref[...] = _dot(jnp.where(sel, 1.0, 0.0).astype(BF16), h2_ref[...])
    copies = [_chunk_copy(buf_ref, c, xs_ref, gch_ref[0, 0, c], sem) for c in range(nchunks)]
    for cp in copies:
        cp.start()
    for cp in copies:
        cp.wait()


def _dispatch(h2, idxt, loff_col, gchunk, fill_start, fill_n, n_rows, ts):
    T, D = h2.shape
    nt = T // ts
    nchunks = gchunk.shape[-1]
    return pl.pallas_call(
        _dispatch_body,
        grid_spec=pltpu.PrefetchScalarGridSpec(
            num_scalar_prefetch=2,
            grid=(nt,),
            in_specs=[pl.BlockSpec((1, 1, nchunks), lambda s, *_: (s, 0, 0), memory_space=pltpu.SMEM),
                      pl.BlockSpec((ts, D), lambda s, *_: (s, 0)),
                      pl.BlockSpec((TOP_K, ts), lambda s, *_: (0, s)),
                      pl.BlockSpec((None, LANES, 1), lambda s, *_: (s, 0, 0))],
            out_specs=[pl.BlockSpec(memory_space=pl.ANY),
                       pl.BlockSpec((TOP_K, ts), lambda s, *_: (0, s))],
            scratch_shapes=[pltpu.VMEM((nchunks * SEG_ALIGN, D), F32),
                            pltpu.VMEM((SEG_ALIGN, D), F32),
                            pltpu.SemaphoreType.DMA(()),
                            pltpu.SemaphoreType.DMA(())]),
        out_shape=[jax.ShapeDtypeStruct((n_rows, D), F32),
                   jax.ShapeDtypeStruct((TOP_K, T), I32)],
        compiler_params=_params(("arbitrary",)),
        name="dispatch",
    )(fill_start, fill_n, gchunk.reshape(nt, 1, nchunks), h2, idxt, loff_col)


def _expert_body(te_ref, nu_ref, x_ref, wu_ref, bu_ref, wd_ref, bd_ref, o_ref):
    @pl.when(pl.program_id(0) < nu_ref[0])
    def _():
        hid = _dot(x_ref[...].astype(BF16), wu_ref[...]) + bu_ref[...]
        half = MXU_COLS // 2
        acts = []
        for b in range(hid.shape[-1] // MXU_COLS):
            x_glu = jnp.minimum(hid[:, b * MXU_COLS:b * MXU_COLS + half], SWIGLU_LIMIT)
            x_lin = jnp.clip(hid[:, b * MXU_COLS + half:(b + 1) * MXU_COLS], -SWIGLU_LIMIT, SWIGLU_LIMIT)
            acts.append(x_glu * jax.nn.sigmoid(SWIGLU_ALPHA * x_glu) * (x_lin + 1.0))
        act = jnp.concatenate(acts, axis=-1).astype(BF16)
        o_ref[...] = _dot(act, wd_ref[...]) + bd_ref[...]


def _experts(xs, tile_expert, n_used, wu, bu, wd, bd, n_tiles):
    tm = TM_EXPERT
    D = xs.shape[-1]
    F2 = wu.shape[-1]
    rows = lambda i, te, nu: (jnp.minimum(i, nu[0] - 1), 0)
    wsel = lambda i, te, nu: (te[i], 0, 0)
    return pl.pallas_call(
        _expert_body,
        grid_spec=pltpu.PrefetchScalarGridSpec(
            num_scalar_prefetch=2,
            grid=(n_tiles,),
            in_specs=[pl.BlockSpec((tm, D), rows),
                      pl.BlockSpec((None, D, F2), wsel),
                      pl.BlockSpec((None, 1, F2), wsel),
                      pl.BlockSpec((None, F2 // 2, D), wsel),
                      pl.BlockSpec((None, 1, D), wsel)],
            out_specs=pl.BlockSpec((tm, D), rows)),
        out_shape=jax.ShapeDtypeStruct((n_tiles * tm, D), F32),
        compiler_params=_params(("arbitrary",)),
        name="experts",
    )(tile_expert, n_used, xs, wu, bu, wd, bd)


def _combine_body(gch_ref, ys_ref, x1_ref, row_ref, gate_ref, o_ref, buf_ref, sem):
    nchunks = gch_ref.shape[-1]
    copies = [_chunk_copy(ys_ref, gch_ref[0, 0, c], buf_ref, c, sem) for c in range(nchunks)]
    for cp in copies:
        cp.start()
    for cp in copies:
        cp.wait()
    rows = row_ref[...]
    g = gate_ref[...]
    ci = lax.broadcasted_iota(I32, (rows.shape[0], nchunks * SEG_ALIGN), 1)
    w = sum(jnp.where(ci == rows[:, k:k + 1], g[:, k:k + 1], 0.0) for k in range(TOP_K))
    o_ref[...] = x1_ref[...] + _dot(w.astype(BF16), buf_ref[...].astype(BF16))


def _combine(ys, gchunk, x1, rows, gates, ts):
    T, D = x1.shape
    nt = T // ts
    nchunks = gchunk.shape[-1]
    return pl.pallas_call(
        _combine_body,
        grid=(nt,),
        in_specs=[pl.BlockSpec((1, 1, nchunks), lambda s: (s, 0, 0), memory_space=pltpu.SMEM),
                  pl.BlockSpec(memory_space=pl.ANY),
                  pl.BlockSpec((ts, D), lambda s: (s, 0)),
                  pl.BlockSpec((ts, TOP_K), lambda s: (s, 0)),
                  pl.BlockSpec((ts, TOP_K), lambda s: (s, 0))],
        out_specs=pl.BlockSpec((ts, D), lambda s: (s, 0)),
        out_shape=jax.ShapeDtypeStruct((T, D), F32),
        scratch_shapes=[pltpu.VMEM((nchunks * SEG_ALIGN, D), F32),
                        pltpu.SemaphoreType.DMA(())],
        compiler_params=_params(("arbitrary",)),
        name="combine",
    )(gchunk.reshape(nt, 1, nchunks), ys, x1, rows, gates)


def _excl_cumsum(a, axis):
    return jnp.cumsum(a, axis=axis) - a


def _routed_experts(h2, x1, idx, gates, w_up, b_up, w_down, b_down):
    T, D = x1.shape
    E = w_up.shape[0]
    ts = min(TS_SORT, T)
    nt = T // ts
    nchunks = (ts * TOP_K + E * SEG_ALIGN) // SEG_ALIGN
    max_rows = T * TOP_K + nt * E * (SEG_ALIGN - 1) + E * (TM_EXPERT - 1)
    n_tiles = -(-max_rows // TM_EXPERT)
    n_rows = n_tiles * TM_EXPERT
    dump_chunk = n_rows // SEG_ALIGN

    cnt = _tile_counts(idx, ts)[:, :E].astype(I32)
    seg = -(-cnt // SEG_ALIGN)
    loff = _excl_cumsum(seg, 1)
    chunks_e = jnp.sum(seg, axis=0)
    tiles_e = -(-(chunks_e * SEG_ALIGN) // TM_EXPERT)
    tile_end = jnp.cumsum(tiles_e)
    base = (tile_end - tiles_e) * (TM_EXPERT // SEG_ALIGN)
    gstart = base[None, :] + _excl_cumsum(seg, 0)
    c = jnp.arange(nchunks, dtype=I32)
    owner = jnp.sum((loff + seg)[:, None, :] <= c[None, :, None], axis=-1)
    onehot = owner[:, :, None] == jnp.arange(E, dtype=I32)[None, None, :]
    shift = jnp.sum(jnp.where(onehot, (gstart - loff)[:, None, :], 0), axis=-1)
    used = owner < E
    gchunk = shift + c[None, :]
    gchunk_out = jnp.where(used, gchunk, dump_chunk + c[None, :]).astype(I32)
    gchunk_in = jnp.where(used, gchunk, 0).astype(I32)
    fill_start = (base + chunks_e).astype(I32)
    fill_n = (tiles_e * (TM_EXPERT // SEG_ALIGN) - chunks_e).astype(I32)
    n_used = tile_end[-1:].astype(I32)
    tile_ids = jnp.minimum(jnp.arange(n_tiles, dtype=I32), n_used[0] - 1)
    tile_expert = jnp.minimum(jnp.sum(tile_ids[:, None] >= tile_end[None, :], axis=-1), E - 1).astype(I32)
    loff_col = jnp.zeros((nt, LANES, 1), F32).at[:, :E, 0].set((loff * SEG_ALIGN).astype(F32))

    xs, rowt = _dispatch(h2, idx.T, loff_col, gchunk_out, fill_start, fill_n,
                         n_rows + nchunks * SEG_ALIGN, ts)
    half = MXU_COLS // 2
    bu = b_up.reshape(E, -1, half, 2).transpose(0, 1, 3, 2).reshape(E, 1, -1)
    ys = _experts(xs, tile_expert, n_used, _wprep(w_up), bu, w_down.astype(BF16), b_down[:, None, :], n_tiles)
    return _combine(ys, gchunk_in, x1, rowt.T, gates, ts)


def _layer(x, mem, mix_norm_w, mem_norm_w, w_in, swa_q_norm_w, swa_k_norm_w, swa_sinks, ret_norm_w,
           w_mem_kv, mem_q_norm_w, mem_k_norm_w, w_br_swa, w_br_ret, w_br_mem, w_out, ffn_norm_w,
           w_router, b_router, w_up, b_up, w_down, b_down):
    B, S, D = x.shape
    M = mem.shape[1]
    T = B * S
    x2 = x.reshape(T, D)

    sq, sk, sv, rq, rk, rv, rg, mq, zg = _inproj(x2, mix_norm_w, w_in.astype(BF16))
    o_swa = _swa(sq, sk, sv, swa_q_norm_w, swa_k_norm_w, swa_sinks, B, S)
    o_ret = _retention(rq, rk, rv, rg, ret_norm_w, B, S)
    mk, mv = _memkv(mem.reshape(B * M, D), mem_norm_w, w_mem_kv.astype(BF16), mem_k_norm_w, B, M)
    o_mem = _memattn(mq, mk, mv, mem_q_norm_w, B, S, M)

    E = w_router.shape[-1]
    w_router_pad = jnp.zeros((D, LANES), F32).at[:, :E].set(w_router)
    b_router_pad = jnp.full((1, LANES), NEG_INF, F32).at[0, :E].set(b_router)
    x1, h2, idx, gates = _merge(
        x2, o_swa, o_ret, o_mem, zg, w_br_swa.astype(BF16), w_br_ret.astype(BF16), w_br_mem.astype(BF16),
        w_out.astype(BF16), ffn_norm_w, w_router_pad, b_router_pad)

    out = _routed_experts(h2, x1, idx, gates, w_up, b_up, w_down, b_down)
    return out.reshape(B, S, D)


def kernel(x, mem, mix_norm_w, mem_norm_w, w_in, swa_q_norm_w, swa_k_norm_w, swa_sinks, ret_norm_w, w_mem_kv, mem_q_norm_w, mem_k_norm_w, w_br_swa, w_br_ret, w_br_mem, w_out, ffn_norm_w, w_router, b_router, w_up, b_up, w_down, b_down):
    args = (x, mem, mix_norm_w, mem_norm_w, w_in, swa_q_norm_w, swa_k_norm_w, swa_sinks, ret_norm_w,
            w_mem_kv, mem_q_norm_w, mem_k_norm_w, w_br_swa, w_br_ret, w_br_mem, w_out, ffn_norm_w,
            w_router, b_router, w_up, b_up, w_down, b_down)
    for l in range(w_in.shape[0]):
        x = _layer(x, mem, *[a[l] for a in args[2:]])
    return x
```

```python
import jax
import jax.numpy as jnp
from jax import lax
from jax.experimental import pallas as pl
from jax.experimental.pallas import tpu as pltpu

F32 = jnp.float32
BF16 = jnp.bfloat16
I32 = jnp.int32

D_MODEL = 1024
SWA_HEAD_DIM = 64
SWA_Q_HEADS = 8
SWA_KV_HEADS = 2
SWA_GROUP = SWA_Q_HEADS // SWA_KV_HEADS
WINDOW = 128
BAND_BLOCK = 128
RET_HEADS = 4
RET_QK_DIM = 128
RET_V_DIM = 256
RET_CHUNK = 128
ROPE_BASE = 10000.0
MEM_HEADS = 4
MEM_HEAD_DIM = 128
N_BRANCH = 3
N_EXPERTS = 32
TOP_K = 4
D_FF = 1024
SWIGLU_LIMIT = 7.0
SWIGLU_ALPHA = 1.702
EPS = 1e-6
NEG_INF = -1e30

SWA_Q_W = SWA_Q_HEADS * SWA_HEAD_DIM
SWA_KV_W = SWA_KV_HEADS * SWA_HEAD_DIM
RET_QK_W = RET_HEADS * RET_QK_DIM
RET_V_W = RET_HEADS * RET_V_DIM
MEM_W = MEM_HEADS * MEM_HEAD_DIM
IN_SIZES = (SWA_Q_W, SWA_KV_W, SWA_KV_W, RET_QK_W, RET_QK_W, RET_V_W, RET_V_W, MEM_W, N_BRANCH * D_MODEL)

SUBLANES = 8
LANES = 128
MXU_COLS = 256
VMEM_LIMIT = 56 * 1024 * 1024

TM_PROJ = 512
TQ_MEM = 512
TM_EXPERT = 512
TS_SORT = 256
SEG_ALIGN = SUBLANES
LOCAL_ROWS = TS_SORT * TOP_K + N_EXPERTS * SEG_ALIGN
LOCAL_CHUNKS = LOCAL_ROWS // SEG_ALIGN


def _params(sem, vmem=VMEM_LIMIT):
    return pltpu.CompilerParams(dimension_semantics=sem, vmem_limit_bytes=vmem)


def _const_spec(shape):
    nd = len(shape)
    return pl.BlockSpec(shape, lambda *_: (0,) * nd, pipeline_mode=pl.Buffered(1))


def _rms(x, w):
    return x * lax.rsqrt(jnp.mean(x * x, axis=-1, keepdims=True) + EPS) * w


def _dot(a, b):
    return jnp.dot(a, b, preferred_element_type=F32)


def _dot_nt(a, b):
    return lax.dot_general(a, b, (((1,), (1,)), ((), ())), preferred_element_type=F32)


def _dot_tn(a, b):
    return lax.dot_general(a, b, (((0,), (0,)), ((), ())), preferred_element_type=F32)


def _inproj_body(x_ref, nw_ref, w_ref, *o_refs):
    h = _rms(x_ref[...], nw_ref[...]).astype(BF16)
    off = 0
    for o_ref in o_refs:
        s = o_ref.shape[-1]
        o_ref[...] = _dot(h, w_ref[:, off:off + s]).astype(o_ref.dtype)
        off += s


def _inproj(x2, norm_w, w_in_bf):
    T, D = x2.shape
    tm = TM_PROJ
    return pl.pallas_call(
        _inproj_body,
        grid=(T // tm,),
        in_specs=[pl.BlockSpec((tm, D), lambda i: (i, 0)),
                  _const_spec((1, D)),
                  _const_spec(w_in_bf.shape)],
        out_specs=[pl.BlockSpec((tm, s), lambda i: (i, 0)) for s in IN_SIZES],
        out_shape=[jax.ShapeDtypeStruct((T, s), BF16) for s in IN_SIZES],
        compiler_params=_params(("parallel",)),
        name="inproj",
    )(x2, norm_w.reshape(1, D), w_in_bf)


def _swa_body(sink_ref, q_ref, kc_ref, kp_ref, vc_ref, vp_ref, qw_ref, kw_ref, o_ref):
    n = pl.program_id(1)
    L = q_ref.shape[0]
    q = q_ref[...].astype(F32)
    k = jnp.concatenate([kp_ref[...], kc_ref[...]], axis=0).astype(F32)
    v = jnp.concatenate([vp_ref[...], vc_ref[...]], axis=0)
    qi = lax.broadcasted_iota(I32, (L, 2 * L), 0)
    kj = lax.broadcasted_iota(I32, (L, 2 * L), 1)
    diff = L + qi - kj
    has_prev = jnp.minimum(n, 1) * L
    mask = (diff >= 0) & (diff < WINDOW) & (kj + has_prev >= L)
    d = SWA_HEAD_DIM
    outs = []
    for h in range(SWA_KV_HEADS):
        kh = _rms(k[:, h * d:(h + 1) * d], kw_ref[...]).astype(BF16)
        vh = v[:, h * d:(h + 1) * d]
        for g in range(SWA_GROUP):
            hq = h * SWA_GROUP + g
            qh = _rms(q[:, hq * d:(hq + 1) * d], qw_ref[...]).astype(BF16)
            s = _dot_nt(qh, kh) * (d ** -0.5)
            s = jnp.where(mask, s, NEG_INF)
            sink = sink_ref[hq]
            m = jnp.maximum(jnp.max(s, axis=-1, keepdims=True), sink)
            p = jnp.exp(s - m)
            den = jnp.sum(p, axis=-1, keepdims=True) + jnp.exp(sink - m)
            outs.append(_dot((p / den).astype(BF16), vh))
    o_ref[...] = jnp.concatenate(outs, axis=-1).astype(o_ref.dtype)


def _swa(sq, sk, sv, q_norm_w, k_norm_w, sinks, B, S):
    L = BAND_BLOCK
    nb = S // L
    cur = lambda b, n: (b * nb + n, 0)
    prev = lambda b, n: (b * nb + jnp.maximum(n - 1, 0), 0)
    return pl.pallas_call(
        _swa_body,
        grid=(B, nb),
        in_specs=[pl.BlockSpec(memory_space=pltpu.SMEM),
                  pl.BlockSpec((L, SWA_Q_W), cur),
                  pl.BlockSpec((L, SWA_KV_W), cur),
                  pl.BlockSpec((L, SWA_KV_W), prev),
                  pl.BlockSpec((L, SWA_KV_W), cur),
                  pl.BlockSpec((L, SWA_KV_W), prev),
                  _const_spec((1, SWA_HEAD_DIM)),
                  _const_spec((1, SWA_HEAD_DIM))],
        out_specs=pl.BlockSpec((L, SWA_Q_W), cur),
        out_shape=jax.ShapeDtypeStruct((B * S, SWA_Q_W), BF16),
        compiler_params=_params(("parallel", "parallel")),
        name="swa",
    )(sinks.astype(F32), sq, sk, sk, sv, sv,
      q_norm_w.reshape(1, -1), k_norm_w.reshape(1, -1))


def _ret_body(q_ref, k_ref, v_ref, g_ref, cos_ref, sin_ref, dm_ref, qd_ref, kd_ref, cd_ref, nw_ref,
              o_ref, st_ref):
    n = pl.program_id(1)

    @pl.when(n == 0)
    def _():
        st_ref[...] = jnp.zeros_like(st_ref)

    cos = cos_ref[...]
    sin = sin_ref[...]
    dk, dv = RET_QK_DIM, RET_V_DIM
    outs = []
    for h in range(RET_HEADS):
        qh = q_ref[:, h * dk:(h + 1) * dk].astype(F32)
        kh = k_ref[:, h * dk:(h + 1) * dk].astype(F32)
        qr = qh * cos + pltpu.roll(qh, dk // 2, 1) * sin
        kr = (kh * cos + pltpu.roll(kh, dk // 2, 1) * sin) * (dk ** -0.5)
        vh = v_ref[:, h * dv:(h + 1) * dv]
        st = st_ref[h]
        inner = _dot_nt(qr.astype(BF16), kr.astype(BF16)) * dm_ref[h]
        o = _dot(inner.astype(BF16), vh) + _dot((qr * qd_ref[h]).astype(BF16), st.astype(BF16))
        st_ref[h] = st * cd_ref[h] + _dot_tn((kr * kd_ref[h]).astype(BF16), vh)
        mu = jnp.mean(o, axis=-1, keepdims=True)
        oc = o - mu
        var = jnp.mean(oc * oc, axis=-1, keepdims=True)
        y = oc * lax.rsqrt(var + EPS) * nw_ref[:, h * dv:(h + 1) * dv]
        g = g_ref[:, h * dv:(h + 1) * dv].astype(F32)
        outs.append(g * jax.nn.sigmoid(g) * y)
    o_ref[...] = jnp.concatenate(outs, axis=-1).astype(o_ref.dtype)


def _retention(rq, rk, rv, rg, ret_norm_w, B, S):
    L = RET_CHUNK
    nc = S // L
    half = RET_QK_DIM // 2
    inv = 1.0 / (ROPE_BASE ** (jnp.arange(0, half, dtype=F32) / half))
    ang = jnp.arange(S, dtype=F32)[:, None] * inv[None, :]
    cos = jnp.cos(ang)
    sin = jnp.sin(ang)
    cos2 = jnp.concatenate([cos, cos], axis=-1)
    sin2 = jnp.concatenate([-sin, sin], axis=-1)
    log_g = jnp.log1p(-(2.0 ** (-5.0 - jnp.arange(RET_HEADS, dtype=F32))))
    pos = jnp.arange(L, dtype=F32)
    diff = pos[:, None] - pos[None, :]
    decay_mask = jnp.where(diff[None] >= 0,
                           jnp.exp(jnp.maximum(diff, 0.0)[None] * log_g[:, None, None]), 0.0)
    q_decay = jnp.exp((pos[None, :] + 1.0) * log_g[:, None])[..., None]
    k_decay = jnp.exp((L - 1.0 - pos[None, :]) * log_g[:, None])[..., None]
    chunk_decay = jnp.exp(L * log_g)
    qd = jnp.broadcast_to(q_decay, (RET_HEADS, L, RET_QK_DIM))
    kd = jnp.broadcast_to(k_decay, (RET_HEADS, L, RET_QK_DIM))
    row = lambda b, n: (b * nc + n, 0)
    return pl.pallas_call(
        _ret_body,
        grid=(B, nc),
        in_specs=[pl.BlockSpec((L, RET_QK_W), row),
                  pl.BlockSpec((L, RET_QK_W), row),
                  pl.BlockSpec((L, RET_V_W), row),
                  pl.BlockSpec((L, RET_V_W), row),
                  pl.BlockSpec((L, RET_QK_DIM), lambda b, n: (n, 0)),
                  pl.BlockSpec((L, RET_QK_DIM), lambda b, n: (n, 0)),
                  _const_spec((RET_HEADS, L, L)),
                  _const_spec((RET_HEADS, L, RET_QK_DIM)),
                  _const_spec((RET_HEADS, L, RET_QK_DIM)),
                  pl.BlockSpec(memory_space=pltpu.SMEM),
                  _const_spec((1, RET_V_W))],
        out_specs=pl.BlockSpec((L, RET_V_W), row),
        out_shape=jax.ShapeDtypeStruct((B * S, RET_V_W), BF16),
        scratch_shapes=[pltpu.VMEM((RET_HEADS, RET_QK_DIM, RET_V_DIM), F32)],
        compiler_params=_params(("parallel", "arbitrary")),
        name="retention",
    )(rq, rk, rv, rg, cos2, sin2, decay_mask, qd, kd, chunk_decay, ret_norm_w.reshape(1, -1))


def _memkv_body(m_ref, nw_ref, w_ref, kw_ref, k_ref, v_ref):
    m = _rms(m_ref[...], nw_ref[...]).astype(BF16)
    kv = _dot(m, w_ref[...])
    d = MEM_HEAD_DIM
    ks = [_rms(kv[:, h * d:(h + 1) * d], kw_ref[...]) for h in range(MEM_HEADS)]
    k_ref[...] = jnp.concatenate(ks, axis=-1).astype(k_ref.dtype)
    v_ref[...] = kv[:, MEM_W:].astype(v_ref.dtype)


def _memkv(mem2, mem_norm_w, w_kv_bf, k_norm_w, B, M):
    D = mem2.shape[-1]
    return pl.pallas_call(
        _memkv_body,
        grid=(B,),
        in_specs=[pl.BlockSpec((M, D), lambda b: (b, 0)),
                  _const_spec((1, D)),
                  _const_spec(w_kv_bf.shape),
                  _const_spec((1, MEM_HEAD_DIM))],
        out_specs=[pl.BlockSpec((M, MEM_W), lambda b: (b, 0))] * 2,
        out_shape=[jax.ShapeDtypeStruct((B * M, MEM_W), BF16)] * 2,
        compiler_params=_params(("parallel",)),
        name="memkv",
    )(mem2, mem_norm_w.reshape(1, D), w_kv_bf, k_norm_w.reshape(1, -1))


def _memattn_body(q_ref, k_ref, v_ref, qw_ref, o_ref):
    d = MEM_HEAD_DIM
    outs = []
    for h in range(MEM_HEADS):
        qh = _rms(q_ref[:, h * d:(h + 1) * d].astype(F32), qw_ref[...]).astype(BF16)
        s = _dot_nt(qh, k_ref[:, h * d:(h + 1) * d]) * (d ** -0.5)
        m = jnp.max(s, axis=-1, keepdims=True)
        p = jnp.exp(s - m)
        pr = p / jnp.sum(p, axis=-1, keepdims=True)
        outs.append(_dot(pr.astype(BF16), v_ref[:, h * d:(h + 1) * d]))
    o_ref[...] = jnp.concatenate(outs, axis=-1).astype(o_ref.dtype)


def _memattn(mq, mk, mv, q_norm_w, B, S, M):
    tq = min(TQ_MEM, S)
    nq = S // tq
    return pl.pallas_call(
        _memattn_body,
        grid=(B, nq),
        in_specs=[pl.BlockSpec((tq, MEM_W), lambda b, i: (b * nq + i, 0)),
                  pl.BlockSpec((M, MEM_W), lambda b, i: (b, 0)),
                  pl.BlockSpec((M, MEM_W), lambda b, i: (b, 0)),
                  _const_spec((1, MEM_HEAD_DIM))],
        out_specs=pl.BlockSpec((tq, MEM_W), lambda b, i: (b * nq + i, 0)),
        out_shape=jax.ShapeDtypeStruct((B * S, MEM_W), BF16),
        compiler_params=_params(("parallel", "parallel")),
        name="memattn",
    )(mq, mk, mv, q_norm_w.reshape(1, -1))


def _split_bf16(a):
    hi = a.astype(BF16)
    lo = (a - hi.astype(F32)).astype(BF16)
    return hi, lo


def _merge_body(x_ref, osw_ref, ort_ref, omm_ref, zg_ref, wa_ref, wr_ref, wm_ref, wo_ref, nw_ref,
                wrt_ref, brt_ref, x1_ref, h2_ref, idx_ref, gate_ref, cnt_ref):
    D = D_MODEL
    sg = lambda j: jax.nn.sigmoid(zg_ref[:, j * D:(j + 1) * D].astype(F32))
    merged = (sg(0) * _dot(osw_ref[...], wa_ref[...])
              + sg(1) * _dot(ort_ref[...], wr_ref[...])
              + sg(2) * _dot(omm_ref[...], wm_ref[...]))
    x1 = x_ref[...] + _dot(merged.astype(BF16), wo_ref[...])
    x1_ref[...] = x1
    h2 = _rms(x1, nw_ref[...])
    h2_ref[...] = h2.astype(h2_ref.dtype)
    h_hi, h_lo = _split_bf16(h2)
    w_hi, w_lo = _split_bf16(wrt_ref[...])
    logits = _dot(h_hi, w_hi) + _dot(h_lo, w_hi) + _dot(h_hi, w_lo) + brt_ref[...]
    lane = lax.broadcasted_iota(I32, logits.shape, 1)
    vals, idxs = [], []
    l = logits
    for _ in range(TOP_K):
        m = jnp.max(l, axis=-1, keepdims=True)
        i = jnp.min(jnp.where(l == m, lane, LANES), axis=-1, keepdims=True)
        vals.append(m)
        idxs.append(i)
        l = jnp.where(lane == i, -jnp.inf, l)
    es = [jnp.exp(v - vals[0]) for v in vals]
    den = es[0] + es[1] + es[2] + es[3]
    idx_ref[...] = jnp.concatenate(idxs, axis=-1)
    gate_ref[...] = jnp.concatenate([e / den for e in es], axis=-1)
    chosen = sum((lane == i).astype(F32) for i in idxs)
    ts = min(TS_SORT, chosen.shape[0])
    per_tile = [jnp.sum(chosen[j * ts:(j + 1) * ts], axis=0, keepdims=True) for j in range(chosen.shape[0] // ts)]
    per_tile.append(jnp.zeros((SUBLANES - len(per_tile), LANES), F32))
    cnt_ref[...] = jnp.concatenate(per_tile, axis=0)


def _merge(x2, o_swa, o_ret, o_mem, zg, wa, wr, wm, wo, ffn_norm_w, w_router_pad, b_router_pad):
    T, D = x2.shape
    tm = TM_PROJ
    row = lambda i: (i, 0)
    return pl.pallas_call(
        _merge_body,
        grid=(T // tm,),
        in_specs=[pl.BlockSpec((tm, D), row),
                  pl.BlockSpec((tm, SWA_Q_W), row),
                  pl.BlockSpec((tm, RET_V_W), row),
                  pl.BlockSpec((tm, MEM_W), row),
                  pl.BlockSpec((tm, N_BRANCH * D), row),
                  _const_spec(wa.shape), _const_spec(wr.shape), _const_spec(wm.shape), _const_spec(wo.shape),
                  _const_spec((1, D)),
                  _const_spec(w_router_pad.shape), _const_spec((1, LANES))],
        out_specs=[pl.BlockSpec((tm, D), row),
                   pl.BlockSpec((tm, D), row),
                   pl.BlockSpec((tm, TOP_K), row),
                   pl.BlockSpec((tm, TOP_K), row),
                   pl.BlockSpec((None, SUBLANES, LANES), lambda i: (i, 0, 0))],
        out_shape=[jax.ShapeDtypeStruct((T, D), F32),
                   jax.ShapeDtypeStruct((T, D), BF16),
                   jax.ShapeDtypeStruct((T, TOP_K), I32),
                   jax.ShapeDtypeStruct((T, TOP_K), F32),
                   jax.ShapeDtypeStruct((T // tm, SUBLANES, LANES), F32)],
        compiler_params=_params(("parallel",)),
        name="merge",
    )(x2, o_swa, o_ret, o_mem, zg, wa, wr, wm, wo, ffn_norm_w.reshape(1, D), w_router_pad, b_router_pad)


def _wprep_body(w_ref, o_ref):
    half = MXU_COLS // 2
    i = lax.broadcasted_iota(I32, (MXU_COLS, MXU_COLS), 0)
    j = lax.broadcasted_iota(I32, (MXU_COLS, MXU_COLS), 1)
    src = jnp.where(j < half, 2 * j, 2 * (j - half) + 1)
    perm = jnp.where(i == src, 1.0, 0.0).astype(BF16)
    for b in range(w_ref.shape[-1] // MXU_COLS):
        cols = slice(b * MXU_COLS, (b + 1) * MXU_COLS)
        o_ref[:, cols] = _dot(w_ref[:, cols].astype(BF16), perm).astype(o_ref.dtype)


def _wprep(w_up):
    E, D, F2 = w_up.shape
    return pl.pallas_call(
        _wprep_body,
        grid=(E,),
        in_specs=[pl.BlockSpec((None, D, F2), lambda e: (e, 0, 0))],
        out_specs=pl.BlockSpec((None, D, F2), lambda e: (e, 0, 0)),
        out_shape=jax.ShapeDtypeStruct((E, D, F2), BF16),
        compiler_params=_params(("parallel",)),
        name="wprep",
    )(w_up)


def _chunk_copy(src_ref, src_chunk, dst_ref, dst_chunk, sem):
    def rows(c):
        start = c * SEG_ALIGN
        return pl.ds(start if isinstance(c, int) else pl.multiple_of(start, SEG_ALIGN), SEG_ALIGN)

    return pltpu.make_async_copy(src_ref.at[rows(src_chunk), :], dst_ref.at[rows(dst_chunk), :], sem)


def _dispatch_body(fill_start_ref, fill_n_ref, gch_ref, h2_ref, idxt_ref, loff_ref,
                   xs_ref, rowt_ref, buf_ref, zero_ref, sems, zsem):
    s = pl.program_id(0)
    last = pl.num_programs(0) - 1
    slot = s % 2
    ts = h2_ref.shape[0]
    nchunks = gch_ref.shape[-1]
    local_rows = nchunks * SEG_ALIGN

    def wait_slot(sl):
        for c in range(nchunks):
            _chunk_copy(buf_ref.at[sl], c, xs_ref, c, sems.at[sl]).wait()

    @pl.when(s == 0)
    def _():
        zero_ref[...] = jnp.zeros_like(zero_ref)

        def per_expert(e, total):
            def fill(r, _):
                _chunk_copy(zero_ref, 0, xs_ref, fill_start_ref[e] + r, zsem).start()
                return 0

            lax.fori_loop(0, fill_n_ref[e], fill, 0)
            return total + fill_n_ref[e]

        total = lax.fori_loop(0, N_EXPERTS, per_expert, 0)

        def drain(r, _):
            _chunk_copy(zero_ref, 0, xs_ref, 0, zsem).wait()
            return 0

        lax.fori_loop(0, total, drain, 0)

    @pl.when(s >= 2)
    def _():
        wait_slot(slot)

    idxt = idxt_ref[...]
    sub = lax.broadcasted_iota(I32, (LANES, ts), 0)
    ohs = [(sub == idxt[k:k + 1, :]).astype(F32) for k in range(TOP_K)]
    m = ohs[0] + ohs[1] + ohs[2] + ohs[3]
    tr = lax.broadcasted_iota(I32, (ts, ts), 0)
    tc = lax.broadcasted_iota(I32, (ts, ts), 1)
    earlier = jnp.where(tr < tc, 1.0, 0.0).astype(BF16)
    before = _dot(m.astype(BF16), earlier) + loff_ref[...]
    rows = [jnp.sum(oh * before, axis=0, keepdims=True).astype(I32) for oh in ohs]
    rowt_ref[...] = jnp.concatenate(rows, axis=0)
    ri = lax.broadcasted_iota(I32, (local_rows, ts), 0)
    sel = (ri == rows[0]) | (ri == rows[1]) | (ri == rows[2]) | (ri == rows[3])
    buf_ref[slot] = _dot(jnp.where(sel, 1.0, 0.0).astype(BF16), h2_ref[...])
    for c in range(nchunks):
        _chunk_copy(buf_ref.at[slot], c, xs_ref, gch_ref[0, 0, c], sems.at[slot]).start()

    @pl.when(s == last)
    def _():
        wait_slot(slot)

        @pl.when(s >= 1)
        def _():
            wait_slot(1 - slot)


def _dispatch(h2, idxt, loff_col, gchunk, fill_start, fill_n, n_rows, ts):
    T, D = h2.shape
    nt = T // ts
    nchunks = gchunk.shape[-1]
    return pl.pallas_call(
        _dispatch_body,
        grid_spec=pltpu.PrefetchScalarGridSpec(
            num_scalar_prefetch=2,
            grid=(nt,),
            in_specs=[pl.BlockSpec((1, 1, nchunks), lambda s, *_: (s, 0, 0), memory_space=pltpu.SMEM),
                      pl.BlockSpec((ts, D), lambda s, *_: (s, 0)),
                      pl.BlockSpec((TOP_K, ts), lambda s, *_: (0, s)),
                      pl.BlockSpec((None, LANES, 1), lambda s, *_: (s, 0, 0))],
            out_specs=[pl.BlockSpec(memory_space=pl.ANY),
                       pl.BlockSpec((TOP_K, ts), lambda s, *_: (0, s))],
            scratch_shapes=[pltpu.VMEM((2, nchunks * SEG_ALIGN, D), F32),
                            pltpu.VMEM((SEG_ALIGN, D), F32),
                            pltpu.SemaphoreType.DMA((2,)),
                            pltpu.SemaphoreType.DMA(())]),
        out_shape=[jax.ShapeDtypeStruct((n_rows, D), F32),
                   jax.ShapeDtypeStruct((TOP_K, T), I32)],
        compiler_params=_params(("arbitrary",)),
        name="dispatch",
    )(fill_start, fill_n, gchunk.reshape(nt, 1, nchunks), h2, idxt, loff_col)


def _expert_body(te_ref, nu_ref, x_ref, wu_ref, bu_ref, wd_ref, bd_ref, o_ref):
    @pl.when(pl.program_id(0) < nu_ref[0])
    def _():
        hid = _dot(x_ref[...].astype(BF16), wu_ref[...]) + bu_ref[...]
        half = MXU_COLS // 2
        acts = []
        for b in range(hid.shape[-1] // MXU_COLS):
            x_glu = jnp.minimum(hid[:, b * MXU_COLS:b * MXU_COLS + half], SWIGLU_LIMIT)
            x_lin = jnp.clip(hid[:, b * MXU_COLS + half:(b + 1) * MXU_COLS], -SWIGLU_LIMIT, SWIGLU_LIMIT)
            acts.append(x_glu * jax.nn.sigmoid(SWIGLU_ALPHA * x_glu) * (x_lin + 1.0))
        act = jnp.concatenate(acts, axis=-1).astype(BF16)
        o_ref[...] = _dot(act, wd_ref[...]) + bd_ref[...]


def _experts(xs, tile_expert, n_used, wu, bu, wd, bd, n_tiles):
    tm = TM_EXPERT
    D = xs.shape[-1]
    F2 = wu.shape[-1]
    rows = lambda i, te, nu: (jnp.minimum(i, nu[0] - 1), 0)
    wsel = lambda i, te, nu: (te[i], 0, 0)
    return pl.pallas_call(
        _expert_body,
        grid_spec=pltpu.PrefetchScalarGridSpec(
            num_scalar_prefetch=2,
            grid=(n_tiles,),
            in_specs=[pl.BlockSpec((tm, D), rows),
                      pl.BlockSpec((None, D, F2), wsel),
                      pl.BlockSpec((None, 1, F2), wsel),
                      pl.BlockSpec((None, F2 // 2, D), wsel),
                      pl.BlockSpec((None, 1, D), wsel)],
            out_specs=pl.BlockSpec((tm, D), rows)),
        out_shape=jax.ShapeDtypeStruct((n_tiles * tm, D), F32),
        compiler_params=_params(("arbitrary",)),
        name="experts",
    )(tile_expert, n_used, xs, wu, bu, wd, bd)


def _combine_body(gch_ref, gch_next_ref, ys_ref, x1_ref, row_ref, gate_ref, o_ref, buf_ref, sems):
    s = pl.program_id(0)
    slot = s % 2
    nchunks = gch_ref.shape[-1]

    def gather(g_ref, sl):
        for c in range(nchunks):
            _chunk_copy(ys_ref, g_ref[0, 0, c], buf_ref.at[sl], c, sems.at[sl]).start()

    @pl.when(s == 0)
    def _():
        gather(gch_ref, 0)

    @pl.when(s < pl.num_programs(0) - 1)
    def _():
        gather(gch_next_ref, 1 - slot)

    for c in range(nchunks):
        _chunk_copy(ys_ref, 0, buf_ref.at[slot], c, sems.at[slot]).wait()
    rows = row_ref[...]
    g = gate_ref[...]
    ci = lax.broadcasted_iota(I32, (rows.shape[0], nchunks * SEG_ALIGN), 1)
    w = sum(jnp.where(ci == rows[:, k:k + 1], g[:, k:k + 1], 0.0) for k in range(TOP_K))
    o_ref[...] = x1_ref[...] + _dot(w.astype(BF16), buf_ref[slot].astype(BF16))


def _combine(ys, gchunk, x1, rows, gates, ts):
    T, D = x1.shape
    nt = T // ts
    nchunks = gchunk.shape[-1]
    return pl.pallas_call(
        _combine_body,
        grid=(nt,),
        in_specs=[pl.BlockSpec((1, 1, nchunks), lambda s: (s, 0, 0), memory_space=pltpu.SMEM),
                  pl.BlockSpec((1, 1, nchunks), lambda s: (jnp.minimum(s + 1, nt - 1), 0, 0),
                               memory_space=pltpu.SMEM),
                  pl.BlockSpec(memory_space=pl.ANY),
                  pl.BlockSpec((ts, D), lambda s: (s, 0)),
                  pl.BlockSpec((ts, TOP_K), lambda s: (s, 0)),
                  pl.BlockSpec((ts, TOP_K), lambda s: (s, 0))],
        out_specs=pl.BlockSpec((ts, D), lambda s: (s, 0)),
        out_shape=jax.ShapeDtypeStruct((T, D), F32),
        scratch_shapes=[pltpu.VMEM((2, nchunks * SEG_ALIGN, D), F32),
                        pltpu.SemaphoreType.DMA((2,))],
        compiler_params=_params(("arbitrary",)),
        name="combine",
    )(gchunk.reshape(nt, 1, nchunks), gchunk.reshape(nt, 1, nchunks), ys, x1, rows, gates)


def _excl_cumsum(a, axis):
    return jnp.cumsum(a, axis=axis) - a


def _routed_experts(h2, x1, idx, gates, cnt, w_up, b_up, w_down, b_down):
    T, D = x1.shape
    E = w_up.shape[0]
    ts = min(TS_SORT, T)
    nt = T // ts
    nchunks = (ts * TOP_K + E * SEG_ALIGN) // SEG_ALIGN
    max_rows = T * TOP_K + nt * E * (SEG_ALIGN - 1) + E * (TM_EXPERT - 1)
    n_tiles = -(-max_rows // TM_EXPERT)
    n_rows = n_tiles * TM_EXPERT
    dump_chunk = n_rows // SEG_ALIGN

    cnt = cnt[:, :min(TM_PROJ, T) // ts, :E].reshape(nt, E).astype(I32)
    seg = -(-cnt // SEG_ALIGN)
    loff = _excl_cumsum(seg, 1)
    chunks_e = jnp.sum(seg, axis=0)
    tiles_e = -(-(chunks_e * SEG_ALIGN) // TM_EXPERT)
    tile_end = jnp.cumsum(tiles_e)
    base = (tile_end - tiles_e) * (TM_EXPERT // SEG_ALIGN)
    gstart = base[None, :] + _excl_cumsum(seg, 0)
    c = jnp.arange(nchunks, dtype=I32)
    owner = jnp.sum((loff + seg)[:, None, :] <= c[None, :, None], axis=-1)
    onehot = owner[:, :, None] == jnp.arange(E, dtype=I32)[None, None, :]
    shift = jnp.sum(jnp.where(onehot, (gstart - loff)[:, None, :], 0), axis=-1)
    used = owner < E
    gchunk = shift + c[None, :]
    dump = dump_chunk + (jnp.arange(nt, dtype=I32)[:, None] % 2) * nchunks + c[None, :]
    gchunk_out = jnp.where(used, gchunk, dump).astype(I32)
    gchunk_in = jnp.where(used, gchunk, 0).astype(I32)
    fill_start = (base + chunks_e).astype(I32)
    fill_n = (tiles_e * (TM_EXPERT // SEG_ALIGN) - chunks_e).astype(I32)
    n_used = tile_end[-1:].astype(I32)
    tile_ids = jnp.minimum(jnp.arange(n_tiles, dtype=I32), n_used[0] - 1)
    tile_expert = jnp.minimum(jnp.sum(tile_ids[:, None] >= tile_end[None, :], axis=-1), E - 1).astype(I32)
    loff_col = jnp.zeros((nt, LANES, 1), F32).at[:, :E, 0].set((loff * SEG_ALIGN).astype(F32))

    xs, rowt = _dispatch(h2, idx.T, loff_col, gchunk_out, fill_start, fill_n,
                         n_rows + 2 * nchunks * SEG_ALIGN, ts)
    half = MXU_COLS // 2
    bu = b_up.reshape(E, -1, half, 2).transpose(0, 1, 3, 2).reshape(E, 1, -1)
    ys = _experts(xs, tile_expert, n_used, _wprep(w_up), bu, w_down.astype(BF16), b_down[:, None, :], n_tiles)
    return _combine(ys, gchunk_in, x1, rowt.T, gates, ts)


def _layer(x, mem, mix_norm_w, mem_norm_w, w_in, swa_q_norm_w, swa_k_norm_w, swa_sinks, ret_norm_w,
           w_mem_kv, mem_q_norm_w, mem_k_norm_w, w_br_swa, w_br_ret, w_br_mem, w_out, ffn_norm_w,
           w_router, b_router, w_up, b_up, w_down, b_down):
    B, S, D = x.shape
    M = mem.shape[1]
    T = B * S
    x2 = x.reshape(T, D)

    sq, sk, sv, rq, rk, rv, rg, mq, zg = _inproj(x2, mix_norm_w, w_in.astype(BF16))
    o_swa = _swa(sq, sk, sv, swa_q_norm_w, swa_k_norm_w, swa_sinks, B, S)
    o_ret = _retention(rq, rk, rv, rg, ret_norm_w, B, S)
    mk, mv = _memkv(mem.reshape(B * M, D), mem_norm_w, w_mem_kv.astype(BF16), mem_k_norm_w, B, M)
    o_mem = _memattn(mq, mk, mv, mem_q_norm_w, B, S, M)

    E = w_router.shape[-1]
    w_router_pad = jnp.zeros((D, LANES), F32).at[:, :E].set(w_router)
    b_router_pad = jnp.full((1, LANES), NEG_INF, F32).at[0, :E].set(b_router)
    x1, h2, idx, gates, cnt = _merge(
        x2, o_swa, o_ret, o_mem, zg, w_br_swa.astype(BF16), w_br_ret.astype(BF16), w_br_mem.astype(BF16),
        w_out.astype(BF16), ffn_norm_w, w_router_pad, b_router_pad)

    out = _routed_experts(h2, x1, idx, gates, cnt, w_up, b_up, w_down, b_down)
    return out.reshape(B, S, D)


def kernel(x, mem, mix_norm_w, mem_norm_w, w_in, swa_q_norm_w, swa_k_norm_w, swa_sinks, ret_norm_w, w_mem_kv, mem_q_norm_w, mem_k_norm_w, w_br_swa, w_br_ret, w_br_mem, w_out, ffn_norm_w, w_router, b_router, w_up, b_up, w_down, b_down):
    args = (x, mem, mix_norm_w, mem_norm_w, w_in, swa_q_norm_w, swa_k_norm_w, swa_sinks, ret_norm_w,
            w_mem_kv, mem_q_norm_w, mem_k_norm_w, w_br_swa, w_br_ret, w_br_mem, w_out, ffn_norm_w,
            w_router, b_router, w_up, b_up, w_down, b_down)
    for l in range(w_in.shape[0]):
        x = _layer(x, mem, *[a[l] for a in args[2:]])
    return x
```

```python
import jax
import jax.numpy as jnp
from jax import lax
from jax.experimental import pallas as pl
from jax.experimental.pallas import tpu as pltpu

F32 = jnp.float32
BF16 = jnp.bfloat16
I32 = jnp.int32

D_MODEL = 1024
SWA_HEAD_DIM = 64
SWA_Q_HEADS = 8
SWA_KV_HEADS = 2
SWA_GROUP = SWA_Q_HEADS // SWA_KV_HEADS
WINDOW = 128
BAND_BLOCK = 128
RET_HEADS = 4
RET_QK_DIM = 128
RET_V_DIM = 256
RET_CHUNK = 128
ROPE_BASE = 10000.0
MEM_HEADS = 4
MEM_HEAD_DIM = 128
N_BRANCH = 3
N_EXPERTS = 32
TOP_K = 4
D_FF = 1024
SWIGLU_LIMIT = 7.0
SWIGLU_ALPHA = 1.702
EPS = 1e-6
NEG_INF = -1e30

SWA_Q_W = SWA_Q_HEADS * SWA_HEAD_DIM
SWA_KV_W = SWA_KV_HEADS * SWA_HEAD_DIM
RET_QK_W = RET_HEADS * RET_QK_DIM
RET_V_W = RET_HEADS * RET_V_DIM
MEM_W = MEM_HEADS * MEM_HEAD_DIM
IN_SIZES = (SWA_Q_W, SWA_KV_W, SWA_KV_W, RET_QK_W, RET_QK_W, RET_V_W, RET_V_W, MEM_W, N_BRANCH * D_MODEL)

SUBLANES = 8
LANES = 128
MXU_COLS = 256
VMEM_LIMIT = 56 * 1024 * 1024
VMEM_LIMIT_EXPERTS = 62 * 1024 * 1024

TM_PROJ = 512
TQ_MEM = 512
TM_EXPERT = 512
TS_SORT = 256
SEG_ALIGN = SUBLANES
LOCAL_ROWS = TS_SORT * TOP_K + N_EXPERTS * SEG_ALIGN
LOCAL_CHUNKS = LOCAL_ROWS // SEG_ALIGN


def _params(sem, vmem=VMEM_LIMIT):
    return pltpu.CompilerParams(dimension_semantics=sem, vmem_limit_bytes=vmem)


def _const_spec(shape):
    nd = len(shape)
    return pl.BlockSpec(shape, lambda *_: (0,) * nd, pipeline_mode=pl.Buffered(1))


def _rms(x, w):
    return x * lax.rsqrt(jnp.mean(x * x, axis=-1, keepdims=True) + EPS) * w


def _sigmoid(x):
    return 0.5 * jnp.tanh(0.5 * x) + 0.5


def _dot(a, b):
    return jnp.dot(a, b, preferred_element_type=F32)


def _dot_nt(a, b):
    return lax.dot_general(a, b, (((1,), (1,)), ((), ())), preferred_element_type=F32)


def _dot_tn(a, b):
    return lax.dot_general(a, b, (((0,), (0,)), ((), ())), preferred_element_type=F32)


def _inproj_body(x_ref, nw_ref, w_ref, *o_refs):
    h = _rms(x_ref[...], nw_ref[...]).astype(BF16)
    off = 0
    for o_ref in o_refs:
        s = o_ref.shape[-1]
        o_ref[...] = _dot(h, w_ref[:, off:off + s]).astype(o_ref.dtype)
        off += s


def _inproj(x2, norm_w, w_in_bf):
    T, D = x2.shape
    tm = TM_PROJ
    return pl.pallas_call(
        _inproj_body,
        grid=(T // tm,),
        in_specs=[pl.BlockSpec((tm, D), lambda i: (i, 0)),
                  _const_spec((1, D)),
                  _const_spec(w_in_bf.shape)],
        out_specs=[pl.BlockSpec((tm, s), lambda i: (i, 0)) for s in IN_SIZES],
        out_shape=[jax.ShapeDtypeStruct((T, s), BF16) for s in IN_SIZES],
        compiler_params=_params(("parallel",)),
        name="inproj",
    )(x2, norm_w.reshape(1, D), w_in_bf)


def _head_rms(t, seg_ref, w):
    hi, lo = _split_bf16(t * t)
    ss = _dot(hi, seg_ref[...]) + _dot(lo, seg_ref[...])
    return t * lax.rsqrt(ss * (1.0 / SWA_HEAD_DIM) + EPS) * w


def _swa_body(sink_ref, q_ref, kc_ref, kp_ref, vc_ref, vp_ref, qw_ref, kw_ref, segq_ref, segk_ref, o_ref):
    n = pl.program_id(1)
    L = q_ref.shape[0]
    d = SWA_HEAD_DIM
    q = _head_rms(q_ref[...].astype(F32), segq_ref, qw_ref[...]) * (d ** -0.5)
    q = q.astype(BF16)
    k = jnp.concatenate([kp_ref[...], kc_ref[...]], axis=0).astype(F32)
    k = _head_rms(k, segk_ref, kw_ref[...])
    v = jnp.concatenate([vp_ref[...], vc_ref[...]], axis=0).astype(F32)
    upper = lax.broadcasted_iota(I32, k.shape, 1) >= d
    k_sw = pltpu.roll(k, d, 1)
    v_sw = pltpu.roll(v, d, 1)

    def placed(t, t_sw):
        return [[jnp.where(upper if half == 1 else jnp.logical_not(upper), t if h == half else t_sw,
                           0.0).astype(BF16) for half in range(2)] for h in range(SWA_KV_HEADS)]

    k_at = placed(k, k_sw)
    v_at = placed(v, v_sw)

    qi = lax.broadcasted_iota(I32, (L, 2 * L), 0)
    kj = lax.broadcasted_iota(I32, (L, 2 * L), 1)
    diff = L + qi - kj
    has_prev = jnp.minimum(n, 1) * L
    mask = (diff >= 0) & (diff < WINDOW) & (kj + has_prev >= L)
    pairs = []
    for j in range(SWA_Q_HEADS // 2):
        h = (2 * j) // SWA_GROUP
        qb = q[:, 2 * j * d:2 * (j + 1) * d]
        acc = None
        for half in range(2):
            s = jnp.where(mask, _dot_nt(qb, k_at[h][half]), NEG_INF)
            sink = sink_ref[2 * j + half]
            m = jnp.maximum(jnp.max(s, axis=-1, keepdims=True), sink)
            p = jnp.exp(s - m)
            den = jnp.sum(p, axis=-1, keepdims=True) + jnp.exp(sink - m)
            o = _dot(p.astype(BF16), v_at[h][half]) * (1.0 / den)
            acc = o if acc is None else acc + o
        pairs.append(acc)
    o_ref[...] = jnp.concatenate(pairs, axis=-1).astype(o_ref.dtype)


def _swa(sq, sk, sv, q_norm_w, k_norm_w, sinks, B, S):
    L = BAND_BLOCK
    nb = S // L
    d = SWA_HEAD_DIM
    cur = lambda b, n: (b * nb + n, 0)
    prev = lambda b, n: (b * nb + jnp.maximum(n - 1, 0), 0)
    group = jnp.arange(SWA_Q_W, dtype=I32) // d
    seg_q = (group[:, None] == group[None, :]).astype(BF16)
    seg_k = seg_q[:SWA_KV_W, :SWA_KV_W]
    return pl.pallas_call(
        _swa_body,
        grid=(B, nb),
        in_specs=[pl.BlockSpec(memory_space=pltpu.SMEM),
                  pl.BlockSpec((L, SWA_Q_W), cur),
                  pl.BlockSpec((L, SWA_KV_W), cur),
                  pl.BlockSpec((L, SWA_KV_W), prev),
                  pl.BlockSpec((L, SWA_KV_W), cur),
                  pl.BlockSpec((L, SWA_KV_W), prev),
                  _const_spec((1, SWA_Q_W)),
                  _const_spec((1, SWA_KV_W)),
                  _const_spec(seg_q.shape),
                  _const_spec(seg_k.shape)],
        out_specs=pl.BlockSpec((L, SWA_Q_W), cur),
        out_shape=jax.ShapeDtypeStruct((B * S, SWA_Q_W), BF16),
        compiler_params=_params(("parallel", "parallel")),
        name="swa",
    )(sinks.astype(F32), sq, sk, sk, sv, sv,
      jnp.tile(q_norm_w, SWA_Q_HEADS).reshape(1, -1), jnp.tile(k_norm_w, SWA_KV_HEADS).reshape(1, -1),
      seg_q, seg_k)


def _ret_body(q_ref, k_ref, v_ref, g_ref, cos_ref, sin_ref, dm_ref, qd_ref, kd_ref, cd_ref, nw_ref,
              o_ref, st_ref):
    n = pl.program_id(1)

    @pl.when(n == 0)
    def _():
        st_ref[...] = jnp.zeros_like(st_ref)

    cos = cos_ref[...]
    sin = sin_ref[...]
    dk, dv = RET_QK_DIM, RET_V_DIM
    outs = []
    for h in range(RET_HEADS):
        qh = q_ref[:, h * dk:(h + 1) * dk].astype(F32)
        kh = k_ref[:, h * dk:(h + 1) * dk].astype(F32)
        qr = qh * cos + pltpu.roll(qh, dk // 2, 1) * sin
        kr = (kh * cos + pltpu.roll(kh, dk // 2, 1) * sin) * (dk ** -0.5)
        vh = v_ref[:, h * dv:(h + 1) * dv]
        st = st_ref[h]
        inner = _dot_nt(qr.astype(BF16), kr.astype(BF16)) * dm_ref[h]
        o = _dot(inner.astype(BF16), vh) + _dot((qr * qd_ref[h]).astype(BF16), st.astype(BF16))
        st_ref[h] = st * cd_ref[h] + _dot_tn((kr * kd_ref[h]).astype(BF16), vh)
        mu = jnp.mean(o, axis=-1, keepdims=True)
        oc = o - mu
        var = jnp.mean(oc * oc, axis=-1, keepdims=True)
        y = oc * lax.rsqrt(var + EPS) * nw_ref[:, h * dv:(h + 1) * dv]
        g = g_ref[:, h * dv:(h + 1) * dv].astype(F32)
        outs.append(g * _sigmoid(g) * y)
    o_ref[...] = jnp.concatenate(outs, axis=-1).astype(o_ref.dtype)


def _retention(rq, rk, rv, rg, ret_norm_w, B, S):
    L = RET_CHUNK
    nc = S // L
    half = RET_QK_DIM // 2
    inv = 1.0 / (ROPE_BASE ** (jnp.arange(0, half, dtype=F32) / half))
    ang = jnp.arange(S, dtype=F32)[:, None] * inv[None, :]
    cos = jnp.cos(ang)
    sin = jnp.sin(ang)
    cos2 = jnp.concatenate([cos, cos], axis=-1)
    sin2 = jnp.concatenate([-sin, sin], axis=-1)
    log_g = jnp.log1p(-(2.0 ** (-5.0 - jnp.arange(RET_HEADS, dtype=F32))))
    pos = jnp.arange(L, dtype=F32)
    diff = pos[:, None] - pos[None, :]
    decay_mask = jnp.where(diff[None] >= 0,
                           jnp.exp(jnp.maximum(diff, 0.0)[None] * log_g[:, None, None]), 0.0)
    q_decay = jnp.exp((pos[None, :] + 1.0) * log_g[:, None])[..., None]
    k_decay = jnp.exp((L - 1.0 - pos[None, :]) * log_g[:, None])[..., None]
    chunk_decay = jnp.exp(L * log_g)
    qd = jnp.broadcast_to(q_decay, (RET_HEADS, L, RET_QK_DIM))
    kd = jnp.broadcast_to(k_decay, (RET_HEADS, L, RET_QK_DIM))
    row = lambda b, n: (b * nc + n, 0)
    return pl.pallas_call(
        _ret_body,
        grid=(B, nc),
        in_specs=[pl.BlockSpec((L, RET_QK_W), row),
                  pl.BlockSpec((L, RET_QK_W), row),
                  pl.BlockSpec((L, RET_V_W), row),
                  pl.BlockSpec((L, RET_V_W), row),
                  pl.BlockSpec((L, RET_QK_DIM), lambda b, n: (n, 0)),
                  pl.BlockSpec((L, RET_QK_DIM), lambda b, n: (n, 0)),
                  _const_spec((RET_HEADS, L, L)),
                  _const_spec((RET_HEADS, L, RET_QK_DIM)),
                  _const_spec((RET_HEADS, L, RET_QK_DIM)),
                  pl.BlockSpec(memory_space=pltpu.SMEM),
                  _const_spec((1, RET_V_W))],
        out_specs=pl.BlockSpec((L, RET_V_W), row),
        out_shape=jax.ShapeDtypeStruct((B * S, RET_V_W), BF16),
        scratch_shapes=[pltpu.VMEM((RET_HEADS, RET_QK_DIM, RET_V_DIM), F32)],
        compiler_params=_params(("parallel", "arbitrary")),
        name="retention",
    )(rq, rk, rv, rg, cos2, sin2, decay_mask, qd, kd, chunk_decay, ret_norm_w.reshape(1, -1))


def _memkv_body(m_ref, nw_ref, w_ref, kw_ref, k_ref, v_ref):
    m = _rms(m_ref[...], nw_ref[...]).astype(BF16)
    kv = _dot(m, w_ref[...])
    d = MEM_HEAD_DIM
    ks = [_rms(kv[:, h * d:(h + 1) * d], kw_ref[...]) for h in range(MEM_HEADS)]
    k_ref[...] = jnp.concatenate(ks, axis=-1).astype(k_ref.dtype)
    v_ref[...] = kv[:, MEM_W:].astype(v_ref.dtype)


def _memkv(mem2, mem_norm_w, w_kv_bf, k_norm_w, B, M):
    D = mem2.shape[-1]
    return pl.pallas_call(
        _memkv_body,
        grid=(B,),
        in_specs=[pl.BlockSpec((M, D), lambda b: (b, 0)),
                  _const_spec((1, D)),
                  _const_spec(w_kv_bf.shape),
                  _const_spec((1, MEM_HEAD_DIM))],
        out_specs=[pl.BlockSpec((M, MEM_W), lambda b: (b, 0))] * 2,
        out_shape=[jax.ShapeDtypeStruct((B * M, MEM_W), BF16)] * 2,
        compiler_params=_params(("parallel",)),
        name="memkv",
    )(mem2, mem_norm_w.reshape(1, D), w_kv_bf, k_norm_w.reshape(1, -1))


def _memattn_body(q_ref, k_ref, v_ref, qw_ref, o_ref):
    d = MEM_HEAD_DIM
    outs = []
    for h in range(MEM_HEADS):
        qh = _rms(q_ref[:, h * d:(h + 1) * d].astype(F32), qw_ref[...]).astype(BF16)
        s = _dot_nt(qh, k_ref[:, h * d:(h + 1) * d]) * (d ** -0.5)
        m = jnp.max(s, axis=-1, keepdims=True)
        p = jnp.exp(s - m)
        pr = p / jnp.sum(p, axis=-1, keepdims=True)
        outs.append(_dot(pr.astype(BF16), v_ref[:, h * d:(h + 1) * d]))
    o_ref[...] = jnp.concatenate(outs, axis=-1).astype(o_ref.dtype)


def _memattn(mq, mk, mv, q_norm_w, B, S, M):
    tq = min(TQ_MEM, S)
    nq = S // tq
    return pl.pallas_call(
        _memattn_body,
        grid=(B, nq),
        in_specs=[pl.BlockSpec((tq, MEM_W), lambda b, i: (b * nq + i, 0)),
                  pl.BlockSpec((M, MEM_W), lambda b, i: (b, 0)),
                  pl.BlockSpec((M, MEM_W), lambda b, i: (b, 0)),
                  _const_spec((1, MEM_HEAD_DIM))],
        out_specs=pl.BlockSpec((tq, MEM_W), lambda b, i: (b * nq + i, 0)),
        out_shape=jax.ShapeDtypeStruct((B * S, MEM_W), BF16),
        compiler_params=_params(("parallel", "parallel")),
        name="memattn",
    )(mq, mk, mv, q_norm_w.reshape(1, -1))


def _split_bf16(a):
    hi = a.astype(BF16)
    lo = (a - hi.astype(F32)).astype(BF16)
    return hi, lo


def _merge_body(x_ref, osw_ref, ort_ref, omm_ref, zg_ref, wa_ref, wr_ref, wm_ref, wo_ref, nw_ref,
                wrt_ref, brt_ref, x1_ref, h2_ref, idx_ref, gate_ref, cnt_ref):
    D = D_MODEL
    sg = lambda j: _sigmoid(zg_ref[:, j * D:(j + 1) * D].astype(F32))
    merged = (sg(0) * _dot(osw_ref[...], wa_ref[...])
              + sg(1) * _dot(ort_ref[...], wr_ref[...])
              + sg(2) * _dot(omm_ref[...], wm_ref[...]))
    x1 = x_ref[...] + _dot(merged.astype(BF16), wo_ref[...])
    x1_ref[...] = x1
    h2 = _rms(x1, nw_ref[...])
    h2_ref[...] = h2.astype(h2_ref.dtype)
    h_hi, h_lo = _split_bf16(h2)
    w_hi, w_lo = _split_bf16(wrt_ref[...])
    logits = _dot(h_hi, w_hi) + _dot(h_lo, w_hi) + _dot(h_hi, w_lo) + brt_ref[...]
    lane = lax.broadcasted_iota(I32, logits.shape, 1)
    vals, idxs = [], []
    l = logits
    for _ in range(TOP_K):
        m = jnp.max(l, axis=-1, keepdims=True)
        i = jnp.min(jnp.where(l == m, lane, LANES), axis=-1, keepdims=True)
        vals.append(m)
        idxs.append(i)
        l = jnp.where(lane == i, -jnp.inf, l)
    es = [jnp.exp(v - vals[0]) for v in vals]
    den = es[0] + es[1] + es[2] + es[3]
    idx_ref[...] = jnp.concatenate(idxs, axis=-1)
    gate_ref[...] = jnp.concatenate([e / den for e in es], axis=-1)
    chosen = sum((lane == i).astype(F32) for i in idxs)
    ts = min(TS_SORT, chosen.shape[0])
    per_tile = [jnp.sum(chosen[j * ts:(j + 1) * ts], axis=0, keepdims=True) for j in range(chosen.shape[0] // ts)]
    per_tile.append(jnp.zeros((SUBLANES - len(per_tile), LANES), F32))
    cnt_ref[...] = jnp.concatenate(per_tile, axis=0)


def _merge(x2, o_swa, o_ret, o_mem, zg, wa, wr, wm, wo, ffn_norm_w, w_router_pad, b_router_pad):
    T, D = x2.shape
    tm = TM_PROJ
    row = lambda i: (i, 0)
    return pl.pallas_call(
        _merge_body,
        grid=(T // tm,),
        in_specs=[pl.BlockSpec((tm, D), row),
                  pl.BlockSpec((tm, SWA_Q_W), row),
                  pl.BlockSpec((tm, RET_V_W), row),
                  pl.BlockSpec((tm, MEM_W), row),
                  pl.BlockSpec((tm, N_BRANCH * D), row),
                  _const_spec(wa.shape), _const_spec(wr.shape), _const_spec(wm.shape), _const_spec(wo.shape),
                  _const_spec((1, D)),
                  _const_spec(w_router_pad.shape), _const_spec((1, LANES))],
        out_specs=[pl.BlockSpec((tm, D), row),
                   pl.BlockSpec((tm, D), row),
                   pl.BlockSpec((tm, TOP_K), row),
                   pl.BlockSpec((tm, TOP_K), row),
                   pl.BlockSpec((None, SUBLANES, LANES), lambda i: (i, 0, 0))],
        out_shape=[jax.ShapeDtypeStruct((T, D), F32),
                   jax.ShapeDtypeStruct((T, D), BF16),
                   jax.ShapeDtypeStruct((T, TOP_K), I32),
                   jax.ShapeDtypeStruct((T, TOP_K), F32),
                   jax.ShapeDtypeStruct((T // tm, SUBLANES, LANES), F32)],
        compiler_params=_params(("parallel",)),
        name="merge",
    )(x2, o_swa, o_ret, o_mem, zg, wa, wr, wm, wo, ffn_norm_w.reshape(1, D), w_router_pad, b_router_pad)


def _chunk_copy(src_ref, src_chunk, dst_ref, dst_chunk, sem):
    def rows(c):
        start = c * SEG_ALIGN
        return pl.ds(start if isinstance(c, int) else pl.multiple_of(start, SEG_ALIGN), SEG_ALIGN)

    return pltpu.make_async_copy(src_ref.at[rows(src_chunk), :], dst_ref.at[rows(dst_chunk), :], sem)


def _dispatch_body(fill_start_ref, fill_n_ref, gch_ref, h2_ref, idxt_ref, loff_ref,
                   xs_ref, rowt_ref, buf_ref, zero_ref, sems, zsem):
    s = pl.program_id(0)
    last = pl.num_programs(0) - 1
    slot = s % 2
    ts = h2_ref.shape[0]
    nchunks = gch_ref.shape[-1]
    local_rows = nchunks * SEG_ALIGN

    def wait_slot(sl):
        for c in range(nchunks):
            _chunk_copy(buf_ref.at[sl], c, xs_ref, c, sems.at[sl]).wait()

    @pl.when(s == 0)
    def _():
        zero_ref[...] = jnp.zeros_like(zero_ref)

        def per_expert(e, total):
            def fill(r, _):
                _chunk_copy(zero_ref, 0, xs_ref, fill_start_ref[e] + r, zsem).start()
                return 0

            lax.fori_loop(0, fill_n_ref[e], fill, 0)
            return total + fill_n_ref[e]

        total = lax.fori_loop(0, N_EXPERTS, per_expert, 0)

        def drain(r, _):
            _chunk_copy(zero_ref, 0, xs_ref, 0, zsem).wait()
            return 0

        lax.fori_loop(0, total, drain, 0)

    @pl.when(s >= 2)
    def _():
        wait_slot(slot)

    idxt = idxt_ref[...]
    sub = lax.broadcasted_iota(I32, (LANES, ts), 0)
    ohs = [(sub == idxt[k:k + 1, :]).astype(F32) for k in range(TOP_K)]
    m = ohs[0] + ohs[1] + ohs[2] + ohs[3]
    tr = lax.broadcasted_iota(I32, (ts, ts), 0)
    tc = lax.broadcasted_iota(I32, (ts, ts), 1)
    earlier = jnp.where(tr < tc, 1.0, 0.0).astype(BF16)
    before = _dot(m.astype(BF16), earlier) + loff_ref[...]
    rows = [jnp.sum(oh * before, axis=0, keepdims=True).astype(I32) for oh in ohs]
    rowt_ref[...] = jnp.concatenate(rows, axis=0)
    ri = lax.broadcasted_iota(I32, (local_rows, ts), 0)
    sel = (ri == rows[0]) | (ri == rows[1]) | (ri == rows[2]) | (ri == rows[3])
    buf_ref[slot] = _dot(jnp.where(sel, 1.0, 0.0).astype(BF16), h2_ref[...])
    for c in range(nchunks):
        _chunk_copy(buf_ref.at[slot], c, xs_ref, gch_ref[0, 0, c], sems.at[slot]).start()

    @pl.when(s == last)
    def _():
        wait_slot(slot)

        @pl.when(s >= 1)
        def _():
            wait_slot(1 - slot)


def _dispatch(h2, idxt, loff_col, gchunk, fill_start, fill_n, n_rows, ts):
    T, D = h2.shape
    nt = T // ts
    nchunks = gchunk.shape[-1]
    return pl.pallas_call(
        _dispatch_body,
        grid_spec=pltpu.PrefetchScalarGridSpec(
            num_scalar_prefetch=2,
            grid=(nt,),
            in_specs=[pl.BlockSpec((1, 1, nchunks), lambda s, *_: (s, 0, 0), memory_space=pltpu.SMEM),
                      pl.BlockSpec((ts, D), lambda s, *_: (s, 0)),
                      pl.BlockSpec((TOP_K, ts), lambda s, *_: (0, s)),
                      pl.BlockSpec((None, LANES, 1), lambda s, *_: (s, 0, 0))],
            out_specs=[pl.BlockSpec(memory_space=pl.ANY),
                       pl.BlockSpec((TOP_K, ts), lambda s, *_: (0, s))],
            scratch_shapes=[pltpu.VMEM((2, nchunks * SEG_ALIGN, D), F32),
                            pltpu.VMEM((SEG_ALIGN, D), F32),
                            pltpu.SemaphoreType.DMA((2,)),
                            pltpu.SemaphoreType.DMA(())]),
        out_shape=[jax.ShapeDtypeStruct((n_rows, D), F32),
                   jax.ShapeDtypeStruct((TOP_K, T), I32)],
        compiler_params=_params(("arbitrary",)),
        name="dispatch",
    )(fill_start, fill_n, gchunk.reshape(nt, 1, nchunks), h2, idxt, loff_col)


def _expert_body(te_ref, nu_ref, x_ref, wu_ref, bu_ref, wd_ref, bd_ref, o_ref, wu_bf, wd_bf):
    i = pl.program_id(0)
    active = i < nu_ref[0]
    new_expert = jnp.logical_or(i == 0, te_ref[i] != te_ref[jnp.maximum(i - 1, 0)])
    half = MXU_COLS // 2

    @pl.when(jnp.logical_and(active, new_expert))
    def _():
        r = lax.broadcasted_iota(I32, (MXU_COLS, MXU_COLS), 0)
        c = lax.broadcasted_iota(I32, (MXU_COLS, MXU_COLS), 1)
        perm = jnp.where(r == jnp.where(c < half, 2 * c, 2 * (c - half) + 1), 1.0, 0.0).astype(BF16)
        for b in range(wu_ref.shape[-1] // MXU_COLS):
            cols = slice(b * MXU_COLS, (b + 1) * MXU_COLS)
            wu_bf[:, cols] = _dot(wu_ref[:, cols].astype(BF16), perm).astype(BF16)
        wd_bf[...] = wd_ref[...].astype(BF16)

    @pl.when(active)
    def _():
        hid = _dot(x_ref[...].astype(BF16), wu_bf[...]) + bu_ref[...]
        acts = []
        for b in range(hid.shape[-1] // MXU_COLS):
            x_glu = jnp.minimum(hid[:, b * MXU_COLS:b * MXU_COLS + half], SWIGLU_LIMIT)
            x_lin = jnp.clip(hid[:, b * MXU_COLS + half:(b + 1) * MXU_COLS], -SWIGLU_LIMIT, SWIGLU_LIMIT)
            acts.append(x_glu * _sigmoid(SWIGLU_ALPHA * x_glu) * (x_lin + 1.0))
        act = jnp.concatenate(acts, axis=-1).astype(BF16)
        o_ref[...] = _dot(act, wd_bf[...]) + bd_ref[...]


def _experts(xs, tile_expert, n_used, wu, bu, wd, bd, n_tiles):
    tm = TM_EXPERT
    D = xs.shape[-1]
    F2 = wu.shape[-1]
    rows = lambda i, te, nu: (jnp.minimum(i, nu[0] - 1), 0)
    wsel = lambda i, te, nu: (te[i], 0, 0)
    return pl.pallas_call(
        _expert_body,
        grid_spec=pltpu.PrefetchScalarGridSpec(
            num_scalar_prefetch=2,
            grid=(n_tiles,),
            in_specs=[pl.BlockSpec((tm, D), rows),
                      pl.BlockSpec((None, D, F2), wsel),
                      pl.BlockSpec((None, 1, F2), wsel),
                      pl.BlockSpec((None, F2 // 2, D), wsel),
                      pl.BlockSpec((None, 1, D), wsel)],
            out_specs=pl.BlockSpec((tm, D), rows),
            scratch_shapes=[pltpu.VMEM((D, F2), BF16), pltpu.VMEM((F2 // 2, D), BF16)]),
        out_shape=jax.ShapeDtypeStruct((n_tiles * tm, D), F32),
        compiler_params=_params(("arbitrary",), vmem=VMEM_LIMIT_EXPERTS),
        name="experts",
    )(tile_expert, n_used, xs, wu, bu, wd, bd)


def _combine_body(gch_ref, gch_next_ref, ys_ref, x1_ref, row_ref, gate_ref, o_ref, buf_ref, sems):
    s = pl.program_id(0)
    slot = s % 2
    nchunks = gch_ref.shape[-1]

    def gather(g_ref, sl):
        for c in range(nchunks):
            _chunk_copy(ys_ref, g_ref[0, 0, c], buf_ref.at[sl], c, sems.at[sl]).start()

    @pl.when(s == 0)
    def _():
        gather(gch_ref, 0)

    @pl.when(s < pl.num_programs(0) - 1)
    def _():
        gather(gch_next_ref, 1 - slot)

    for c in range(nchunks):
        _chunk_copy(ys_ref, 0, buf_ref.at[slot], c, sems.at[slot]).wait()
    rows = row_ref[...]
    g = gate_ref[...]
    ci = lax.broadcasted_iota(I32, (rows.shape[0], nchunks * SEG_ALIGN), 1)
    w = sum(jnp.where(ci == rows[:, k:k + 1], g[:, k:k + 1], 0.0) for k in range(TOP_K))
    o_ref[...] = x1_ref[...] + _dot(w.astype(BF16), buf_ref[slot].astype(BF16))


def _combine(ys, gchunk, x1, rows, gates, ts):
    T, D = x1.shape
    nt = T // ts
    nchunks = gchunk.shape[-1]
    return pl.pallas_call(
        _combine_body,
        grid=(nt,),
        in_specs=[pl.BlockSpec((1, 1, nchunks), lambda s: (s, 0, 0), memory_space=pltpu.SMEM),
                  pl.BlockSpec((1, 1, nchunks), lambda s: (jnp.minimum(s + 1, nt - 1), 0, 0),
                               memory_space=pltpu.SMEM),
                  pl.BlockSpec(memory_space=pl.ANY),
                  pl.BlockSpec((ts, D), lambda s: (s, 0)),
                  pl.BlockSpec((ts, TOP_K), lambda s: (s, 0)),
                  pl.BlockSpec((ts, TOP_K), lambda s: (s, 0))],
        out_specs=pl.BlockSpec((ts, D), lambda s: (s, 0)),
        out_shape=jax.ShapeDtypeStruct((T, D), F32),
        scratch_shapes=[pltpu.VMEM((2, nchunks * SEG_ALIGN, D), F32),
                        pltpu.SemaphoreType.DMA((2,))],
        compiler_params=_params(("arbitrary",)),
        name="combine",
    )(gchunk.reshape(nt, 1, nchunks), gchunk.reshape(nt, 1, nchunks), ys, x1, rows, gates)


def _excl_cumsum(a, axis):
    return jnp.cumsum(a, axis=axis) - a


def _routed_experts(h2, x1, idx, gates, cnt, w_up, b_up, w_down, b_down):
    T, D = x1.shape
    E = w_up.shape[0]
    ts = min(TS_SORT, T)
    nt = T // ts
    nchunks = (ts * TOP_K + E * SEG_ALIGN) // SEG_ALIGN
    max_rows = T * TOP_K + nt * E * (SEG_ALIGN - 1) + E * (TM_EXPERT - 1)
    n_tiles = -(-max_rows // TM_EXPERT)
    n_rows = n_tiles * TM_EXPERT
    dump_chunk = n_rows // SEG_ALIGN

    cnt = cnt[:, :min(TM_PROJ, T) // ts, :E].reshape(nt, E).astype(I32)
    seg = -(-cnt // SEG_ALIGN)
    loff = _excl_cumsum(seg, 1)
    chunks_e = jnp.sum(seg, axis=0)
    tiles_e = -(-(chunks_e * SEG_ALIGN) // TM_EXPERT)
    tile_end = jnp.cumsum(tiles_e)
    base = (tile_end - tiles_e) * (TM_EXPERT // SEG_ALIGN)
    gstart = base[None, :] + _excl_cumsum(seg, 0)
    c = jnp.arange(nchunks, dtype=I32)
    owner = jnp.sum((loff + seg)[:, None, :] <= c[None, :, None], axis=-1)
    onehot = owner[:, :, None] == jnp.arange(E, dtype=I32)[None, None, :]
    shift = jnp.sum(jnp.where(onehot, (gstart - loff)[:, None, :], 0), axis=-1)
    used = owner < E
    gchunk = shift + c[None, :]
    dump = dump_chunk + (jnp.arange(nt, dtype=I32)[:, None] % 2) * nchunks + c[None, :]
    gchunk_out = jnp.where(used, gchunk, dump).astype(I32)
    gchunk_in = jnp.where(used, gchunk, 0).astype(I32)
    fill_start = (base + chunks_e).astype(I32)
    fill_n = (tiles_e * (TM_EXPERT // SEG_ALIGN) - chunks_e).astype(I32)
    n_used = tile_end[-1:].astype(I32)
    tile_ids = jnp.minimum(jnp.arange(n_tiles, dtype=I32), n_used[0] - 1)
    tile_expert = jnp.minimum(jnp.sum(tile_ids[:, None] >= tile_end[None, :], axis=-1), E - 1).astype(I32)
    loff_col = jnp.zeros((nt, LANES, 1), F32).at[:, :E, 0].set((loff * SEG_ALIGN).astype(F32))

    xs, rowt = _dispatch(h2, idx.T, loff_col, gchunk_out, fill_start, fill_n,
                         n_rows + 2 * nchunks * SEG_ALIGN, ts)
    half = MXU_COLS // 2
    bu = b_up.reshape(E, -1, half, 2).transpose(0, 1, 3, 2).reshape(E, 1, -1)
    ys = _experts(xs, tile_expert, n_used, w_up, bu, w_down, b_down[:, None, :], n_tiles)
    return _combine(ys, gchunk_in, x1, rowt.T, gates, ts)


def _layer(x, mem, mix_norm_w, mem_norm_w, w_in, swa_q_norm_w, swa_k_norm_w, swa_sinks, ret_norm_w,
           w_mem_kv, mem_q_norm_w, mem_k_norm_w, w_br_swa, w_br_ret, w_br_mem, w_out, ffn_norm_w,
           w_router, b_router, w_up, b_up, w_down, b_down):
    B, S, D = x.shape
    M = mem.shape[1]
    T = B * S
    x2 = x.reshape(T, D)

    sq, sk, sv, rq, rk, rv, rg, mq, zg = _inproj(x2, mix_norm_w, w_in.astype(BF16))
    o_swa = _swa(sq, sk, sv, swa_q_norm_w, swa_k_norm_w, swa_sinks, B, S)
    o_ret = _retention(rq, rk, rv, rg, ret_norm_w, B, S)
    mk, mv = _memkv(mem.reshape(B * M, D), mem_norm_w, w_mem_kv.astype(BF16), mem_k_norm_w, B, M)
    o_mem = _memattn(mq, mk, mv, mem_q_norm_w, B, S, M)

    E = w_router.shape[-1]
    w_router_pad = jnp.zeros((D, LANES), F32).at[:, :E].set(w_router)
    b_router_pad = jnp.full((1, LANES), NEG_INF, F32).at[0, :E].set(b_router)
    x1, h2, idx, gates, cnt = _merge(
        x2, o_swa, o_ret, o_mem, zg, w_br_swa.astype(BF16), w_br_ret.astype(BF16), w_br_mem.astype(BF16),
        w_out.astype(BF16), ffn_norm_w, w_router_pad, b_router_pad)

    out = _routed_experts(h2, x1, idx, gates, cnt, w_up, b_up, w_down, b_down)
    return out.reshape(B, S, D)


def kernel(x, mem, mix_norm_w, mem_norm_w, w_in, swa_q_norm_w, swa_k_norm_w, swa_sinks, ret_norm_w, w_mem_kv, mem_q_norm_w, mem_k_norm_w, w_br_swa, w_br_ret, w_br_mem, w_out, ffn_norm_w, w_router, b_router, w_up, b_up, w_down, b_down):
    args = (x, mem, mix_norm_w, mem_norm_w, w_in, swa_q_norm_w, swa_k_norm_w, swa_sinks, ret_norm_w,
            w_mem_kv, mem_q_norm_w, mem_k_norm_w, w_br_swa, w_br_ret, w_br_mem, w_out, ffn_norm_w,
            w_router, b_router, w_up, b_up, w_down, b_down)
    for l in range(w_in.shape[0]):
        x = _layer(x, mem, *[a[l] for a in args[2:]])
    return x
```

```python
import jax
import jax.numpy as jnp
from jax import lax
from jax.experimental import pallas as pl
from jax.experimental.pallas import tpu as pltpu

F32 = jnp.float32
BF16 = jnp.bfloat16
I32 = jnp.int32

D_MODEL = 1024
SWA_HEAD_DIM = 64
SWA_Q_HEADS = 8
SWA_KV_HEADS = 2
SWA_GROUP = SWA_Q_HEADS // SWA_KV_HEADS
WINDOW = 128
BAND_BLOCK = 128
RET_HEADS = 4
RET_QK_DIM = 128
RET_V_DIM = 256
RET_CHUNK = 128
ROPE_BASE = 10000.0
MEM_HEADS = 4
MEM_HEAD_DIM = 128
N_BRANCH = 3
N_EXPERTS = 32
TOP_K = 4
D_FF = 1024
SWIGLU_LIMIT = 7.0
SWIGLU_ALPHA = 1.702
EPS = 1e-6
NEG_INF = -1e30

SWA_Q_W = SWA_Q_HEADS * SWA_HEAD_DIM
SWA_KV_W = SWA_KV_HEADS * SWA_HEAD_DIM
RET_QK_W = RET_HEADS * RET_QK_DIM
RET_V_W = RET_HEADS * RET_V_DIM
MEM_W = MEM_HEADS * MEM_HEAD_DIM
IN_SIZES = (SWA_Q_W, SWA_KV_W, SWA_KV_W, RET_QK_W, RET_QK_W, RET_V_W, RET_V_W, MEM_W, N_BRANCH * D_MODEL)

SUBLANES = 8
LANES = 128
MXU_COLS = 256
VMEM_LIMIT = 56 * 1024 * 1024
VMEM_LIMIT_EXPERTS = 62 * 1024 * 1024

TM_PROJ = 512
TM_MERGE = 1024
TQ_MEM = 512
TM_EXPERT = 512
BATCH_PER_STEP = 4
TS_SORT = 256
SEG_ALIGN = SUBLANES
LOCAL_ROWS = TS_SORT * TOP_K + N_EXPERTS * SEG_ALIGN
LOCAL_CHUNKS = LOCAL_ROWS // SEG_ALIGN


def _params(sem, vmem=VMEM_LIMIT):
    return pltpu.CompilerParams(dimension_semantics=sem, vmem_limit_bytes=vmem)


def _const_spec(shape):
    nd = len(shape)
    return pl.BlockSpec(shape, lambda *_: (0,) * nd, pipeline_mode=pl.Buffered(1))


def _rms(x, w):
    return x * lax.rsqrt(jnp.mean(x * x, axis=-1, keepdims=True) + EPS) * w


def _sigmoid(x):
    return 0.5 * jnp.tanh(0.5 * x) + 0.5


def _dot(a, b):
    return jnp.dot(a, b, preferred_element_type=F32)


def _dot_nt(a, b):
    return lax.dot_general(a, b, (((1,), (1,)), ((), ())), preferred_element_type=F32)


def _dot_tn(a, b):
    return lax.dot_general(a, b, (((0,), (0,)), ((), ())), preferred_element_type=F32)


def _inproj_body(x_ref, nw_ref, w_ref, *o_refs):
    h = _rms(x_ref[...], nw_ref[...]).astype(BF16)
    off = 0
    for o_ref in o_refs:
        s = o_ref.shape[-1]
        o_ref[...] = _dot(h, w_ref[:, off:off + s]).astype(o_ref.dtype)
        off += s


def _inproj(x2, norm_w, w_in_bf):
    T, D = x2.shape
    tm = TM_PROJ
    return pl.pallas_call(
        _inproj_body,
        grid=(T // tm,),
        in_specs=[pl.BlockSpec((tm, D), lambda i: (i, 0)),
                  _const_spec((1, D)),
                  _const_spec(w_in_bf.shape)],
        out_specs=[pl.BlockSpec((tm, s), lambda i: (i, 0)) for s in IN_SIZES],
        out_shape=[jax.ShapeDtypeStruct((T, s), BF16) for s in IN_SIZES],
        compiler_params=_params(("parallel",)),
        name="inproj",
    )(x2, norm_w.reshape(1, D), w_in_bf)


def _head_rms(t, seg_ref, w):
    hi, lo = _split_bf16(t * t)
    ss = _dot(hi, seg_ref[...]) + _dot(lo, seg_ref[...])
    return t * lax.rsqrt(ss * (1.0 / SWA_HEAD_DIM) + EPS) * w


def _swa_body(sink_ref, q_ref, kc_ref, kp_ref, vc_ref, vp_ref, qw_ref, kw_ref, segq_ref, segk_ref, o_ref):
    n = pl.program_id(1)
    L = q_ref.shape[1]
    qi = lax.broadcasted_iota(I32, (L, 2 * L), 0)
    kj = lax.broadcasted_iota(I32, (L, 2 * L), 1)
    diff = L + qi - kj
    has_prev = jnp.minimum(n, 1) * L
    mask = (diff >= 0) & (diff < WINDOW) & (kj + has_prev >= L)
    for b in range(q_ref.shape[0]):
        o_ref[b] = _swa_block(sink_ref, q_ref[b], kc_ref[b], kp_ref[b], vc_ref[b], vp_ref[b],
                              qw_ref, kw_ref, segq_ref, segk_ref, mask).astype(o_ref.dtype)


def _swa_block(sink_ref, q, kc, kp, vc, vp, qw_ref, kw_ref, segq_ref, segk_ref, mask):
    d = SWA_HEAD_DIM
    q = _head_rms(q.astype(F32), segq_ref, qw_ref[...]) * (d ** -0.5)
    q = q.astype(BF16)
    k = jnp.concatenate([kp, kc], axis=0).astype(F32)
    k = _head_rms(k, segk_ref, kw_ref[...])
    v = jnp.concatenate([vp, vc], axis=0).astype(F32)
    upper = lax.broadcasted_iota(I32, k.shape, 1) >= d
    k_sw = pltpu.roll(k, d, 1)
    v_sw = pltpu.roll(v, d, 1)

    def placed(t, t_sw):
        return [[jnp.where(upper if half == 1 else jnp.logical_not(upper), t if h == half else t_sw,
                           0.0).astype(BF16) for half in range(2)] for h in range(SWA_KV_HEADS)]

    k_at = placed(k, k_sw)
    v_at = placed(v, v_sw)
    pairs = []
    for j in range(SWA_Q_HEADS // 2):
        h = (2 * j) // SWA_GROUP
        qb = q[:, 2 * j * d:2 * (j + 1) * d]
        acc = None
        for half in range(2):
            s = jnp.where(mask, _dot_nt(qb, k_at[h][half]), NEG_INF)
            sink = sink_ref[2 * j + half]
            m = jnp.maximum(jnp.max(s, axis=-1, keepdims=True), sink)
            p = jnp.exp(s - m)
            den = jnp.sum(p, axis=-1, keepdims=True) + jnp.exp(sink - m)
            o = _dot(p.astype(BF16), v_at[h][half]) * (1.0 / den)
            acc = o if acc is None else acc + o
        pairs.append(acc)
    return jnp.concatenate(pairs, axis=-1)


def _swa(sq, sk, sv, q_norm_w, k_norm_w, sinks, B, S):
    L = BAND_BLOCK
    nb = S // L
    d = SWA_HEAD_DIM
    bs = BATCH_PER_STEP if B % BATCH_PER_STEP == 0 else 1
    cur = lambda b, n: (b, n, 0)
    prev = lambda b, n: (b, jnp.maximum(n - 1, 0), 0)
    as3 = lambda t: t.reshape(B, S, t.shape[-1])
    group = jnp.arange(SWA_Q_W, dtype=I32) // d
    seg_q = (group[:, None] == group[None, :]).astype(BF16)
    seg_k = seg_q[:SWA_KV_W, :SWA_KV_W]
    return pl.pallas_call(
        _swa_body,
        grid=(B // bs, nb),
        in_specs=[pl.BlockSpec(memory_space=pltpu.SMEM),
                  pl.BlockSpec((bs, L, SWA_Q_W), cur),
                  pl.BlockSpec((bs, L, SWA_KV_W), cur),
                  pl.BlockSpec((bs, L, SWA_KV_W), prev),
                  pl.BlockSpec((bs, L, SWA_KV_W), cur),
                  pl.BlockSpec((bs, L, SWA_KV_W), prev),
                  _const_spec((1, SWA_Q_W)),
                  _const_spec((1, SWA_KV_W)),
                  _const_spec(seg_q.shape),
                  _const_spec(seg_k.shape)],
        out_specs=pl.BlockSpec((bs, L, SWA_Q_W), cur),
        out_shape=jax.ShapeDtypeStruct((B, S, SWA_Q_W), BF16),
        compiler_params=_params(("parallel", "parallel")),
        name="swa",
    )(sinks.astype(F32), as3(sq), as3(sk), as3(sk), as3(sv), as3(sv),
      jnp.tile(q_norm_w, SWA_Q_HEADS).reshape(1, -1), jnp.tile(k_norm_w, SWA_KV_HEADS).reshape(1, -1),
      seg_q, seg_k).reshape(B * S, SWA_Q_W)


def _ret_body(q_ref, k_ref, v_ref, g_ref, cos_ref, sin_ref, dm_ref, qd_ref, kd_ref, cd_ref, nw_ref,
              o_ref, st_ref):
    n = pl.program_id(1)

    @pl.when(n == 0)
    def _():
        st_ref[...] = jnp.zeros_like(st_ref)

    for b in range(q_ref.shape[0]):
        _ret_block(b, q_ref, k_ref, v_ref, g_ref, cos_ref, sin_ref, dm_ref, qd_ref, kd_ref, cd_ref, nw_ref,
                   o_ref, st_ref)


def _ret_block(b, q_ref, k_ref, v_ref, g_ref, cos_ref, sin_ref, dm_ref, qd_ref, kd_ref, cd_ref, nw_ref,
               o_ref, st_ref):
    cos = cos_ref[...]
    sin = sin_ref[...]
    dk, dv = RET_QK_DIM, RET_V_DIM
    outs = []
    for h in range(RET_HEADS):
        qh = q_ref[b, :, h * dk:(h + 1) * dk].astype(F32)
        kh = k_ref[b, :, h * dk:(h + 1) * dk].astype(F32)
        qr = qh * cos + pltpu.roll(qh, dk // 2, 1) * sin
        kr = (kh * cos + pltpu.roll(kh, dk // 2, 1) * sin) * (dk ** -0.5)
        vh = v_ref[b, :, h * dv:(h + 1) * dv]
        st = st_ref[b, h]
        inner = _dot_nt(qr.astype(BF16), kr.astype(BF16)) * dm_ref[h]
        o = _dot(inner.astype(BF16), vh) + _dot((qr * qd_ref[h]).astype(BF16), st.astype(BF16))
        st_ref[b, h] = st * cd_ref[h] + _dot_tn((kr * kd_ref[h]).astype(BF16), vh)
        mu = jnp.mean(o, axis=-1, keepdims=True)
        oc = o - mu
        var = jnp.mean(oc * oc, axis=-1, keepdims=True)
        y = oc * lax.rsqrt(var + EPS) * nw_ref[:, h * dv:(h + 1) * dv]
        g = g_ref[b, :, h * dv:(h + 1) * dv].astype(F32)
        outs.append(g * _sigmoid(g) * y)
    o_ref[b] = jnp.concatenate(outs, axis=-1).astype(o_ref.dtype)


def _retention(rq, rk, rv, rg, ret_norm_w, B, S):
    L = RET_CHUNK
    nc = S // L
    half = RET_QK_DIM // 2
    inv = 1.0 / (ROPE_BASE ** (jnp.arange(0, half, dtype=F32) / half))
    ang = jnp.arange(S, dtype=F32)[:, None] * inv[None, :]
    cos = jnp.cos(ang)
    sin = jnp.sin(ang)
    cos2 = jnp.concatenate([cos, cos], axis=-1)
    sin2 = jnp.concatenate([-sin, sin], axis=-1)
    log_g = jnp.log1p(-(2.0 ** (-5.0 - jnp.arange(RET_HEADS, dtype=F32))))
    pos = jnp.arange(L, dtype=F32)
    diff = pos[:, None] - pos[None, :]
    decay_mask = jnp.where(diff[None] >= 0,
                           jnp.exp(jnp.maximum(diff, 0.0)[None] * log_g[:, None, None]), 0.0)
    q_decay = jnp.exp((pos[None, :] + 1.0) * log_g[:, None])[..., None]
    k_decay = jnp.exp((L - 1.0 - pos[None, :]) * log_g[:, None])[..., None]
    chunk_decay = jnp.exp(L * log_g)
    qd = jnp.broadcast_to(q_decay, (RET_HEADS, L, RET_QK_DIM))
    kd = jnp.broadcast_to(k_decay, (RET_HEADS, L, RET_QK_DIM))
    bs = BATCH_PER_STEP if B % BATCH_PER_STEP == 0 else 1
    row = lambda b, n: (b, n, 0)
    as3 = lambda t: t.reshape(B, S, t.shape[-1])
    return pl.pallas_call(
        _ret_body,
        grid=(B // bs, nc),
        in_specs=[pl.BlockSpec((bs, L, RET_QK_W), row),
                  pl.BlockSpec((bs, L, RET_QK_W), row),
                  pl.BlockSpec((bs, L, RET_V_W), row),
                  pl.BlockSpec((bs, L, RET_V_W), row),
                  pl.BlockSpec((L, RET_QK_DIM), lambda b, n: (n, 0)),
                  pl.BlockSpec((L, RET_QK_DIM), lambda b, n: (n, 0)),
                  _const_spec((RET_HEADS, L, L)),
                  _const_spec((RET_HEADS, L, RET_QK_DIM)),
                  _const_spec((RET_HEADS, L, RET_QK_DIM)),
                  pl.BlockSpec(memory_space=pltpu.SMEM),
                  _const_spec((1, RET_V_W))],
        out_specs=pl.BlockSpec((bs, L, RET_V_W), row),
        out_shape=jax.ShapeDtypeStruct((B, S, RET_V_W), BF16),
        scratch_shapes=[pltpu.VMEM((bs, RET_HEADS, RET_QK_DIM, RET_V_DIM), F32)],
        compiler_params=_params(("parallel", "arbitrary")),
        name="retention",
    )(as3(rq), as3(rk), as3(rv), as3(rg), cos2, sin2, decay_mask, qd, kd, chunk_decay,
      ret_norm_w.reshape(1, -1)).reshape(B * S, RET_V_W)


def _memkv_body(m_ref, nw_ref, w_ref, kw_ref, k_ref, v_ref):
    m = _rms(m_ref[...], nw_ref[...]).astype(BF16)
    kv = _dot(m, w_ref[...])
    d = MEM_HEAD_DIM
    ks = [_rms(kv[:, h * d:(h + 1) * d], kw_ref[...]) for h in range(MEM_HEADS)]
    k_ref[...] = jnp.concatenate(ks, axis=-1).astype(k_ref.dtype)
    v_ref[...] = kv[:, MEM_W:].astype(v_ref.dtype)


def _memkv(mem2, mem_norm_w, w_kv_bf, k_norm_w, B, M):
    D = mem2.shape[-1]
    return pl.pallas_call(
        _memkv_body,
        grid=(B,),
        in_specs=[pl.BlockSpec((M, D), lambda b: (b, 0)),
                  _const_spec((1, D)),
                  _const_spec(w_kv_bf.shape),
                  _const_spec((1, MEM_HEAD_DIM))],
        out_specs=[pl.BlockSpec((M, MEM_W), lambda b: (b, 0))] * 2,
        out_shape=[jax.ShapeDtypeStruct((B * M, MEM_W), BF16)] * 2,
        compiler_params=_params(("parallel",)),
        name="memkv",
    )(mem2, mem_norm_w.reshape(1, D), w_kv_bf, k_norm_w.reshape(1, -1))


def _memattn_body(q_ref, k_ref, v_ref, qw_ref, o_ref):
    d = MEM_HEAD_DIM
    outs = []
    for h in range(MEM_HEADS):
        qh = _rms(q_ref[:, h * d:(h + 1) * d].astype(F32), qw_ref[...]).astype(BF16)
        s = _dot_nt(qh, k_ref[:, h * d:(h + 1) * d]) * (d ** -0.5)
        m = jnp.max(s, axis=-1, keepdims=True)
        p = jnp.exp(s - m)
        pr = p / jnp.sum(p, axis=-1, keepdims=True)
        outs.append(_dot(pr.astype(BF16), v_ref[:, h * d:(h + 1) * d]))
    o_ref[...] = jnp.concatenate(outs, axis=-1).astype(o_ref.dtype)


def _memattn(mq, mk, mv, q_norm_w, B, S, M):
    tq = min(TQ_MEM, S)
    nq = S // tq
    return pl.pallas_call(
        _memattn_body,
        grid=(B, nq),
        in_specs=[pl.BlockSpec((tq, MEM_W), lambda b, i: (b * nq + i, 0)),
                  pl.BlockSpec((M, MEM_W), lambda b, i: (b, 0)),
                  pl.BlockSpec((M, MEM_W), lambda b, i: (b, 0)),
                  _const_spec((1, MEM_HEAD_DIM))],
        out_specs=pl.BlockSpec((tq, MEM_W), lambda b, i: (b * nq + i, 0)),
        out_shape=jax.ShapeDtypeStruct((B * S, MEM_W), BF16),
        compiler_params=_params(("parallel", "parallel")),
        name="memattn",
    )(mq, mk, mv, q_norm_w.reshape(1, -1))


def _split_bf16(a):
    hi = a.astype(BF16)
    lo = (a - hi.astype(F32)).astype(BF16)
    return hi, lo


def _merge_body(x_ref, osw_ref, ort_ref, omm_ref, zg_ref, wa_ref, wr_ref, wm_ref, wo_ref, nw_ref,
                wrt_ref, brt_ref, x1_ref, h2_ref, idx_ref, gate_ref, cnt_ref):
    D = D_MODEL
    sg = lambda j: _sigmoid(zg_ref[:, j * D:(j + 1) * D].astype(F32))
    merged = (sg(0) * _dot(osw_ref[...], wa_ref[...])
              + sg(1) * _dot(ort_ref[...], wr_ref[...])
              + sg(2) * _dot(omm_ref[...], wm_ref[...]))
    x1 = x_ref[...] + _dot(merged.astype(BF16), wo_ref[...])
    x1_ref[...] = x1
    h2 = _rms(x1, nw_ref[...])
    h2_ref[...] = h2.astype(h2_ref.dtype)
    h_hi, h_lo = _split_bf16(h2)
    w_hi, w_lo = _split_bf16(wrt_ref[...])
    logits = _dot(h_hi, w_hi) + _dot(h_lo, w_hi) + _dot(h_hi, w_lo) + brt_ref[...]
    lane = lax.broadcasted_iota(I32, logits.shape, 1)
    vals, idxs = [], []
    l = logits
    for _ in range(TOP_K):
        m = jnp.max(l, axis=-1, keepdims=True)
        i = jnp.min(jnp.where(l == m, lane, LANES), axis=-1, keepdims=True)
        vals.append(m)
        idxs.append(i)
        l = jnp.where(lane == i, -jnp.inf, l)
    es = [jnp.exp(v - vals[0]) for v in vals]
    den = es[0] + es[1] + es[2] + es[3]
    idx_ref[...] = jnp.concatenate(idxs, axis=-1)
    gate_ref[...] = jnp.concatenate([e / den for e in es], axis=-1)
    chosen = sum((lane == i).astype(F32) for i in idxs)
    ts = min(TS_SORT, chosen.shape[0])
    per_tile = [jnp.sum(chosen[j * ts:(j + 1) * ts], axis=0, keepdims=True) for j in range(chosen.shape[0] // ts)]
    per_tile.append(jnp.zeros((SUBLANES - len(per_tile), LANES), F32))
    cnt_ref[...] = jnp.concatenate(per_tile, axis=0)


def _merge(x2, o_swa, o_ret, o_mem, zg, wa, wr, wm, wo, ffn_norm_w, w_router_pad, b_router_pad):
    T, D = x2.shape
    tm = min(TM_MERGE, T)
    row = lambda i: (i, 0)
    return pl.pallas_call(
        _merge_body,
        grid=(T // tm,),
        in_specs=[pl.BlockSpec((tm, D), row),
                  pl.BlockSpec((tm, SWA_Q_W), row),
                  pl.BlockSpec((tm, RET_V_W), row),
                  pl.BlockSpec((tm, MEM_W), row),
                  pl.BlockSpec((tm, N_BRANCH * D), row),
                  _const_spec(wa.shape), _const_spec(wr.shape), _const_spec(wm.shape), _const_spec(wo.shape),
                  _const_spec((1, D)),
                  _const_spec(w_router_pad.shape), _const_spec((1, LANES))],
        out_specs=[pl.BlockSpec((tm, D), row),
                   pl.BlockSpec((tm, D), row),
                   pl.BlockSpec((tm, TOP_K), row),
                   pl.BlockSpec((tm, TOP_K), row),
                   pl.BlockSpec((None, SUBLANES, LANES), lambda i: (i, 0, 0))],
        out_shape=[jax.ShapeDtypeStruct((T, D), F32),
                   jax.ShapeDtypeStruct((T, D), BF16),
                   jax.ShapeDtypeStruct((T, TOP_K), I32),
                   jax.ShapeDtypeStruct((T, TOP_K), F32),
                   jax.ShapeDtypeStruct((T // tm, SUBLANES, LANES), F32)],
        compiler_params=_params(("parallel",)),
        name="merge",
    )(x2, o_swa, o_ret, o_mem, zg, wa, wr, wm, wo, ffn_norm_w.reshape(1, D), w_router_pad, b_router_pad)


def _chunk_copy(src_ref, src_chunk, dst_ref, dst_chunk, sem):
    def rows(c):
        start = c * SEG_ALIGN
        return pl.ds(start if isinstance(c, int) else pl.multiple_of(start, SEG_ALIGN), SEG_ALIGN)

    return pltpu.make_async_copy(src_ref.at[rows(src_chunk), :], dst_ref.at[rows(dst_chunk), :], sem)


def _dispatch_body(fill_start_ref, fill_n_ref, gch_ref, h2_ref, idxt_ref, loff_ref,
                   xs_ref, rowt_ref, buf_ref, zero_ref, sems, zsem):
    s = pl.program_id(0)
    last = pl.num_programs(0) - 1
    slot = s % 2
    ts = h2_ref.shape[0]
    nchunks = gch_ref.shape[-1]
    local_rows = nchunks * SEG_ALIGN

    def wait_slot(sl):
        for c in range(nchunks):
            _chunk_copy(buf_ref.at[sl], c, xs_ref, c, sems.at[sl]).wait()

    @pl.when(s == 0)
    def _():
        zero_ref[...] = jnp.zeros_like(zero_ref)

        def per_expert(e, total):
            def fill(r, _):
                _chunk_copy(zero_ref, 0, xs_ref, fill_start_ref[e] + r, zsem).start()
                return 0

            lax.fori_loop(0, fill_n_ref[e], fill, 0)
            return total + fill_n_ref[e]

        total = lax.fori_loop(0, N_EXPERTS, per_expert, 0)

        def drain(r, _):
            _chunk_copy(zero_ref, 0, xs_ref, 0, zsem).wait()
            return 0

        lax.fori_loop(0, total, drain, 0)

    @pl.when(s >= 2)
    def _():
        wait_slot(slot)

    idxt = idxt_ref[...]
    sub = lax.broadcasted_iota(I32, (LANES, ts), 0)
    ohs = [(sub == idxt[k:k + 1, :]).astype(F32) for k in range(TOP_K)]
    m = ohs[0] + ohs[1] + ohs[2] + ohs[3]
    tr = lax.broadcasted_iota(I32, (ts, ts), 0)
    tc = lax.broadcasted_iota(I32, (ts, ts), 1)
    earlier = jnp.where(tr < tc, 1.0, 0.0).astype(BF16)
    before = _dot(m.astype(BF16), earlier) + loff_ref[...]
    rows = [jnp.sum(oh * before, axis=0, keepdims=True).astype(I32) for oh in ohs]
    rowt_ref[...] = jnp.concatenate(rows, axis=0)
    ri = lax.broadcasted_iota(I32, (local_rows, ts), 0)
    sel = (ri == rows[0]) | (ri == rows[1]) | (ri == rows[2]) | (ri == rows[3])
    buf_ref[slot] = _dot(jnp.where(sel, 1.0, 0.0).astype(BF16), h2_ref[...])
    for c in range(nchunks):
        _chunk_copy(buf_ref.at[slot], c, xs_ref, gch_ref[0, 0, c], sems.at[slot]).start()

    @pl.when(s == last)
    def _():
        wait_slot(slot)

        @pl.when(s >= 1)
        def _():
            wait_slot(1 - slot)


def _dispatch(h2, idxt, loff_col, gchunk, fill_start, fill_n, n_rows, ts):
    T, D = h2.shape
    nt = T // ts
    nchunks = gchunk.shape[-1]
    return pl.pallas_call(
        _dispatch_body,
        grid_spec=pltpu.PrefetchScalarGridSpec(
            num_scalar_prefetch=2,
            grid=(nt,),
            in_specs=[pl.BlockSpec((1, 1, nchunks), lambda s, *_: (s, 0, 0), memory_space=pltpu.SMEM),
                      pl.BlockSpec((ts, D), lambda s, *_: (s, 0)),
                      pl.BlockSpec((TOP_K, ts), lambda s, *_: (0, s)),
                      pl.BlockSpec((None, LANES, 1), lambda s, *_: (s, 0, 0))],
            out_specs=[pl.BlockSpec(memory_space=pl.ANY),
                       pl.BlockSpec((TOP_K, ts), lambda s, *_: (0, s))],
            scratch_shapes=[pltpu.VMEM((2, nchunks * SEG_ALIGN, D), F32),
                            pltpu.VMEM((SEG_ALIGN, D), F32),
                            pltpu.SemaphoreType.DMA((2,)),
                            pltpu.SemaphoreType.DMA(())]),
        out_shape=[jax.ShapeDtypeStruct((n_rows, D), F32),
                   jax.ShapeDtypeStruct((TOP_K, T), I32)],
        compiler_params=_params(("arbitrary",)),
        name="dispatch",
    )(fill_start, fill_n, gchunk.reshape(nt, 1, nchunks), h2, idxt, loff_col)


def _expert_body(te_ref, nu_ref, x_ref, wu_ref, bu_ref, wd_ref, bd_ref, o_ref, wu_bf, wd_bf):
    i = pl.program_id(0)
    active = i < nu_ref[0]
    new_expert = jnp.logical_or(i == 0, te_ref[i] != te_ref[jnp.maximum(i - 1, 0)])
    half = MXU_COLS // 2

    @pl.when(jnp.logical_and(active, new_expert))
    def _():
        r = lax.broadcasted_iota(I32, (MXU_COLS, MXU_COLS), 0)
        c = lax.broadcasted_iota(I32, (MXU_COLS, MXU_COLS), 1)
        perm = jnp.where(r == jnp.where(c < half, 2 * c, 2 * (c - half) + 1), 1.0, 0.0).astype(BF16)
        for b in range(wu_ref.shape[-1] // MXU_COLS):
            cols = slice(b * MXU_COLS, (b + 1) * MXU_COLS)
            wu_bf[:, cols] = _dot(wu_ref[:, cols].astype(BF16), perm).astype(BF16)
        wd_bf[...] = wd_ref[...].astype(BF16)

    @pl.when(active)
    def _():
        hid = _dot(x_ref[...].astype(BF16), wu_bf[...]) + bu_ref[...]
        acts = []
        for b in range(hid.shape[-1] // MXU_COLS):
            x_glu = jnp.minimum(hid[:, b * MXU_COLS:b * MXU_COLS + half], SWIGLU_LIMIT)
            x_lin = jnp.clip(hid[:, b * MXU_COLS + half:(b + 1) * MXU_COLS], -SWIGLU_LIMIT, SWIGLU_LIMIT)
            acts.append(x_glu * _sigmoid(SWIGLU_ALPHA * x_glu) * (x_lin + 1.0))
        act = jnp.concatenate(acts, axis=-1).astype(BF16)
        o_ref[...] = _dot(act, wd_bf[...]) + bd_ref[...]


def _experts(xs, tile_expert, n_used, wu, bu, wd, bd, n_tiles):
    tm = TM_EXPERT
    D = xs.shape[-1]
    F2 = wu.shape[-1]
    rows = lambda i, te, nu: (jnp.minimum(i, nu[0] - 1), 0)
    wsel = lambda i, te, nu: (te[i], 0, 0)
    return pl.pallas_call(
        _expert_body,
        grid_spec=pltpu.PrefetchScalarGridSpec(
            num_scalar_prefetch=2,
            grid=(n_tiles,),
            in_specs=[pl.BlockSpec((tm, D), rows),
                      pl.BlockSpec((None, D, F2), wsel),
                      pl.BlockSpec((None, 1, F2), wsel),
                      pl.BlockSpec((None, F2 // 2, D), wsel),
                      pl.BlockSpec((None, 1, D), wsel)],
            out_specs=pl.BlockSpec((tm, D), rows),
            scratch_shapes=[pltpu.VMEM((D, F2), BF16), pltpu.VMEM((F2 // 2, D), BF16)]),
        out_shape=jax.ShapeDtypeStruct((n_tiles * tm, D), F32),
        compiler_params=_params(("arbitrary",), vmem=VMEM_LIMIT_EXPERTS),
        name="experts",
    )(tile_expert, n_used, xs, wu, bu, wd, bd)


def _combine_body(gch_ref, gch_next_ref, ys_ref, x1_ref, row_ref, gate_ref, o_ref, buf_ref, sems):
    s = pl.program_id(0)
    slot = s % 2
    nchunks = gch_ref.shape[-1]

    def gather(g_ref, sl):
        for c in range(nchunks):
            _chunk_copy(ys_ref, g_ref[0, 0, c], buf_ref.at[sl], c, sems.at[sl]).start()

    @pl.when(s == 0)
    def _():
        gather(gch_ref, 0)

    @pl.when(s < pl.num_programs(0) - 1)
    def _():
        gather(gch_next_ref, 1 - slot)

    for c in range(nchunks):
        _chunk_copy(ys_ref, 0, buf_ref.at[slot], c, sems.at[slot]).wait()
    rows = row_ref[...]
    g = gate_ref[...]
    ci = lax.broadcasted_iota(I32, (rows.shape[0], nchunks * SEG_ALIGN), 1)
    w = sum(jnp.where(ci == rows[:, k:k + 1], g[:, k:k + 1], 0.0) for k in range(TOP_K))
    o_ref[...] = x1_ref[...] + _dot(w.astype(BF16), buf_ref[slot].astype(BF16))


def _combine(ys, gchunk, x1, rows, gates, ts):
    T, D = x1.shape
    nt = T // ts
    nchunks = gchunk.shape[-1]
    return pl.pallas_call(
        _combine_body,
        grid=(nt,),
        in_specs=[pl.BlockSpec((1, 1, nchunks), lambda s: (s, 0, 0), memory_space=pltpu.SMEM),
                  pl.BlockSpec((1, 1, nchunks), lambda s: (jnp.minimum(s + 1, nt - 1), 0, 0),
                               memory_space=pltpu.SMEM),
                  pl.BlockSpec(memory_space=pl.ANY),
                  pl.BlockSpec((ts, D), lambda s: (s, 0)),
                  pl.BlockSpec((ts, TOP_K), lambda s: (s, 0)),
                  pl.BlockSpec((ts, TOP_K), lambda s: (s, 0))],
        out_specs=pl.BlockSpec((ts, D), lambda s: (s, 0)),
        out_shape=jax.ShapeDtypeStruct((T, D), F32),
        scratch_shapes=[pltpu.VMEM((2, nchunks * SEG_ALIGN, D), F32),
                        pltpu.SemaphoreType.DMA((2,))],
        compiler_params=_params(("arbitrary",)),
        name="combine",
    )(gchunk.reshape(nt, 1, nchunks), gchunk.reshape(nt, 1, nchunks), ys, x1, rows, gates)


def _excl_cumsum(a, axis):
    return jnp.cumsum(a, axis=axis) - a


def _routed_experts(h2, x1, idx, gates, cnt, w_up, b_up, w_down, b_down):
    T, D = x1.shape
    E = w_up.shape[0]
    ts = min(TS_SORT, T)
    nt = T // ts
    nchunks = (ts * TOP_K + E * SEG_ALIGN) // SEG_ALIGN
    max_rows = T * TOP_K + nt * E * (SEG_ALIGN - 1) + E * (TM_EXPERT - 1)
    n_tiles = -(-max_rows // TM_EXPERT)
    n_rows = n_tiles * TM_EXPERT
    dump_chunk = n_rows // SEG_ALIGN

    cnt = cnt[:, :min(TM_MERGE, T) // ts, :E].reshape(nt, E).astype(I32)
    seg = -(-cnt // SEG_ALIGN)
    loff = _excl_cumsum(seg, 1)
    chunks_e = jnp.sum(seg, axis=0)
    tiles_e = -(-(chunks_e * SEG_ALIGN) // TM_EXPERT)
    tile_end = jnp.cumsum(tiles_e)
    base = (tile_end - tiles_e) * (TM_EXPERT // SEG_ALIGN)
    gstart = base[None, :] + _excl_cumsum(seg, 0)
    c = jnp.arange(nchunks, dtype=I32)
    owner = jnp.sum((loff + seg)[:, None, :] <= c[None, :, None], axis=-1)
    onehot = owner[:, :, None] == jnp.arange(E, dtype=I32)[None, None, :]
    shift = jnp.sum(jnp.where(onehot, (gstart - loff)[:, None, :], 0), axis=-1)
    used = owner < E
    gchunk = shift + c[None, :]
    dump = dump_chunk + (jnp.arange(nt, dtype=I32)[:, None] % 2) * nchunks + c[None, :]
    gchunk_out = jnp.where(used, gchunk, dump).astype(I32)
    gchunk_in = jnp.where(used, gchunk, 0).astype(I32)
    fill_start = (base + chunks_e).astype(I32)
    fill_n = (tiles_e * (TM_EXPERT // SEG_ALIGN) - chunks_e).astype(I32)
    n_used = tile_end[-1:].astype(I32)
    tile_ids = jnp.minimum(jnp.arange(n_tiles, dtype=I32), n_used[0] - 1)
    tile_expert = jnp.minimum(jnp.sum(tile_ids[:, None] >= tile_end[None, :], axis=-1), E - 1).astype(I32)
    loff_col = jnp.zeros((nt, LANES, 1), F32).at[:, :E, 0].set((loff * SEG_ALIGN).astype(F32))

    xs, rowt = _dispatch(h2, idx.T, loff_col, gchunk_out, fill_start, fill_n,
                         n_rows + 2 * nchunks * SEG_ALIGN, ts)
    half = MXU_COLS // 2
    bu = b_up.reshape(E, -1, half, 2).transpose(0, 1, 3, 2).reshape(E, 1, -1)
    ys = _experts(xs, tile_expert, n_used, w_up, bu, w_down, b_down[:, None, :], n_tiles)
    return _combine(ys, gchunk_in, x1, rowt.T, gates, ts)


def _layer(x, mem, mix_norm_w, mem_norm_w, w_in, swa_q_norm_w, swa_k_norm_w, swa_sinks, ret_norm_w,
           w_mem_kv, mem_q_norm_w, mem_k_norm_w, w_br_swa, w_br_ret, w_br_mem, w_out, ffn_norm_w,
           w_router, b_router, w_up, b_up, w_down, b_down):
    B, S, D = x.shape
    M = mem.shape[1]
    T = B * S
    x2 = x.reshape(T, D)

    sq, sk, sv, rq, rk, rv, rg, mq, zg = _inproj(x2, mix_norm_w, w_in.astype(BF16))
    o_swa = _swa(sq, sk, sv, swa_q_norm_w, swa_k_norm_w, swa_sinks, B, S)
    o_ret = _retention(rq, rk, rv, rg, ret_norm_w, B, S)
    mk, mv = _memkv(mem.reshape(B * M, D), mem_norm_w, w_mem_kv.astype(BF16), mem_k_norm_w, B, M)
    o_mem = _memattn(mq, mk, mv, mem_q_norm_w, B, S, M)

    E = w_router.shape[-1]
    w_router_pad = jnp.zeros((D, LANES), F32).at[:, :E].set(w_router)
    b_router_pad = jnp.full((1, LANES), NEG_INF, F32).at[0, :E].set(b_router)
    x1, h2, idx, gates, cnt = _merge(
        x2, o_swa, o_ret, o_mem, zg, w_br_swa.astype(BF16), w_br_ret.astype(BF16), w_br_mem.astype(BF16),
        w_out.astype(BF16), ffn_norm_w, w_router_pad, b_router_pad)

    out = _routed_experts(h2, x1, idx, gates, cnt, w_up, b_up, w_down, b_down)
    return out.reshape(B, S, D)


def kernel(x, mem, mix_norm_w, mem_norm_w, w_in, swa_q_norm_w, swa_k_norm_w, swa_sinks, ret_norm_w, w_mem_kv, mem_q_norm_w, mem_k_norm_w, w_br_swa, w_br_ret, w_br_mem, w_out, ffn_norm_w, w_router, b_router, w_up, b_up, w_down, b_down):
    args = (x, mem, mix_norm_w, mem_norm_w, w_in, swa_q_norm_w, swa_k_norm_w, swa_sinks, ret_norm_w,
            w_mem_kv, mem_q_norm_w, mem_k_norm_w, w_br_swa, w_br_ret, w_br_mem, w_out, ffn_norm_w,
            w_router, b_router, w_up, b_up, w_down, b_down)
    for l in range(w_in.shape[0]):
        x = _layer(x, mem, *[a[l] for a in args[2:]])
    return x
```

```python
import jax
import jax.numpy as jnp
from jax import lax
from jax.experimental import pallas as pl
from jax.experimental.pallas import tpu as pltpu

F32 = jnp.float32
BF16 = jnp.bfloat16
I32 = jnp.int32
U32 = jnp.uint32

D_MODEL = 1024
SWA_HEAD_DIM = 64
SWA_Q_HEADS = 8
SWA_KV_HEADS = 2
SWA_GROUP = SWA_Q_HEADS // SWA_KV_HEADS
WINDOW = 128
BAND_BLOCK = 128
RET_HEADS = 4
RET_QK_DIM = 128
RET_V_DIM = 256
RET_CHUNK = 128
ROPE_BASE = 10000.0
MEM_HEADS = 4
MEM_HEAD_DIM = 128
N_BRANCH = 3
N_EXPERTS = 32
TOP_K = 4
D_FF = 1024
SWIGLU_LIMIT = 7.0
SWIGLU_ALPHA = 1.702
EPS = 1e-6
NEG_INF = -1e30

SWA_Q_W = SWA_Q_HEADS * SWA_HEAD_DIM
SWA_KV_W = SWA_KV_HEADS * SWA_HEAD_DIM
RET_QK_W = RET_HEADS * RET_QK_DIM
RET_V_W = RET_HEADS * RET_V_DIM
MEM_W = MEM_HEADS * MEM_HEAD_DIM
IN_SIZES = (SWA_Q_W, SWA_KV_W, SWA_KV_W, RET_QK_W, RET_QK_W, RET_V_W, RET_V_W, MEM_W, N_BRANCH * D_MODEL)

SUBLANES = 8
LANES = 128
MXU_COLS = 256
VMEM_LIMIT = 56 * 1024 * 1024
VMEM_LIMIT_EXPERTS = 62 * 1024 * 1024

TM_PROJ = 512
TM_MERGE = 1024
TQ_MEM = 512
TM_EXPERT = 512
BATCH_PER_STEP = 4
TS_SORT = 256
SEG_ALIGN = SUBLANES
LOCAL_ROWS = TS_SORT * TOP_K + N_EXPERTS * SEG_ALIGN
LOCAL_CHUNKS = LOCAL_ROWS // SEG_ALIGN


def _params(sem, vmem=VMEM_LIMIT):
    return pltpu.CompilerParams(dimension_semantics=sem, vmem_limit_bytes=vmem)


def _const_spec(shape):
    nd = len(shape)
    return pl.BlockSpec(shape, lambda *_: (0,) * nd, pipeline_mode=pl.Buffered(1))


def _rms(x, w):
    return x * lax.rsqrt(jnp.mean(x * x, axis=-1, keepdims=True) + EPS) * w


def _sigmoid(x):
    return 0.5 * jnp.tanh(0.5 * x) + 0.5


def _pack_halves(a):
    n = a.shape[-1] // 2
    lo = lax.bitcast_convert_type(a[:, :n], U32) >> 16
    hi = lax.bitcast_convert_type(a[:, n:], U32) & jnp.uint32(0xFFFF0000)
    return lo | hi


def _unpack_halves(w):
    lo = lax.bitcast_convert_type(w << 16, F32)
    hi = lax.bitcast_convert_type(w & jnp.uint32(0xFFFF0000), F32)
    return lo.astype(BF16), hi.astype(BF16)


def _dot(a, b):
    return jnp.dot(a, b, preferred_element_type=F32)


def _dot_nt(a, b):
    return lax.dot_general(a, b, (((1,), (1,)), ((), ())), preferred_element_type=F32)


def _dot_tn(a, b):
    return lax.dot_general(a, b, (((0,), (0,)), ((), ())), preferred_element_type=F32)


def _inproj_body(x_ref, nw_ref, w_ref, *o_refs):
    h = _rms(x_ref[...], nw_ref[...]).astype(BF16)
    off = 0
    for o_ref in o_refs:
        s = o_ref.shape[-1]
        o_ref[...] = _dot(h, w_ref[:, off:off + s]).astype(o_ref.dtype)
        off += s


def _inproj(x2, norm_w, w_in_bf):
    T, D = x2.shape
    tm = TM_PROJ
    return pl.pallas_call(
        _inproj_body,
        grid=(T // tm,),
        in_specs=[pl.BlockSpec((tm, D), lambda i: (i, 0)),
                  _const_spec((1, D)),
                  _const_spec(w_in_bf.shape)],
        out_specs=[pl.BlockSpec((tm, s), lambda i: (i, 0)) for s in IN_SIZES],
        out_shape=[jax.ShapeDtypeStruct((T, s), BF16) for s in IN_SIZES],
        compiler_params=_params(("parallel",)),
        name="inproj",
    )(x2, norm_w.reshape(1, D), w_in_bf)


def _head_rms(t, seg_ref, w):
    hi, lo = _split_bf16(t * t)
    ss = _dot(hi, seg_ref[...]) + _dot(lo, seg_ref[...])
    return t * lax.rsqrt(ss * (1.0 / SWA_HEAD_DIM) + EPS) * w


def _swa_body(sink_ref, q_ref, kc_ref, kp_ref, vc_ref, vp_ref, qw_ref, kw_ref, segq_ref, segk_ref, o_ref):
    n = pl.program_id(1)
    L = q_ref.shape[1]
    qi = lax.broadcasted_iota(I32, (L, 2 * L), 0)
    kj = lax.broadcasted_iota(I32, (L, 2 * L), 1)
    diff = L + qi - kj
    has_prev = jnp.minimum(n, 1) * L
    mask = (diff >= 0) & (diff < WINDOW) & (kj + has_prev >= L)
    for b in range(q_ref.shape[0]):
        o_ref[b] = _swa_block(sink_ref, q_ref[b], kc_ref[b], kp_ref[b], vc_ref[b], vp_ref[b],
                              qw_ref, kw_ref, segq_ref, segk_ref, mask).astype(o_ref.dtype)


def _swa_block(sink_ref, q, kc, kp, vc, vp, qw_ref, kw_ref, segq_ref, segk_ref, mask):
    d = SWA_HEAD_DIM
    q = _head_rms(q.astype(F32), segq_ref, qw_ref[...]) * (d ** -0.5)
    q = q.astype(BF16)
    k = jnp.concatenate([kp, kc], axis=0).astype(F32)
    k = _head_rms(k, segk_ref, kw_ref[...])
    v = jnp.concatenate([vp, vc], axis=0).astype(F32)
    upper = lax.broadcasted_iota(I32, k.shape, 1) >= d
    k_sw = pltpu.roll(k, d, 1)
    v_sw = pltpu.roll(v, d, 1)

    def placed(t, t_sw):
        return [[jnp.where(upper if half == 1 else jnp.logical_not(upper), t if h == half else t_sw,
                           0.0).astype(BF16) for half in range(2)] for h in range(SWA_KV_HEADS)]

    k_at = placed(k, k_sw)
    v_at = placed(v, v_sw)
    pairs = []
    for j in range(SWA_Q_HEADS // 2):
        h = (2 * j) // SWA_GROUP
        qb = q[:, 2 * j * d:2 * (j + 1) * d]
        acc = None
        for half in range(2):
            s = jnp.where(mask, _dot_nt(qb, k_at[h][half]), NEG_INF)
            sink = sink_ref[2 * j + half]
            m = jnp.maximum(jnp.max(s, axis=-1, keepdims=True), sink)
            p = jnp.exp(s - m)
            den = jnp.sum(p, axis=-1, keepdims=True) + jnp.exp(sink - m)
            o = _dot(p.astype(BF16), v_at[h][half]) * (1.0 / den)
            acc = o if acc is None else acc + o
        pairs.append(acc)
    return jnp.concatenate(pairs, axis=-1)


def _swa(sq, sk, sv, q_norm_w, k_norm_w, sinks, B, S):
    L = BAND_BLOCK
    nb = S // L
    d = SWA_HEAD_DIM
    bs = BATCH_PER_STEP if B % BATCH_PER_STEP == 0 else 1
    cur = lambda b, n: (b, n, 0)
    prev = lambda b, n: (b, jnp.maximum(n - 1, 0), 0)
    as3 = lambda t: t.reshape(B, S, t.shape[-1])
    group = jnp.arange(SWA_Q_W, dtype=I32) // d
    seg_q = (group[:, None] == group[None, :]).astype(BF16)
    seg_k = seg_q[:SWA_KV_W, :SWA_KV_W]
    return pl.pallas_call(
        _swa_body,
        grid=(B // bs, nb),
        in_specs=[pl.BlockSpec(memory_space=pltpu.SMEM),
                  pl.BlockSpec((bs, L, SWA_Q_W), cur),
                  pl.BlockSpec((bs, L, SWA_KV_W), cur),
                  pl.BlockSpec((bs, L, SWA_KV_W), prev),
                  pl.BlockSpec((bs, L, SWA_KV_W), cur),
                  pl.BlockSpec((bs, L, SWA_KV_W), prev),
                  _const_spec((1, SWA_Q_W)),
                  _const_spec((1, SWA_KV_W)),
                  _const_spec(seg_q.shape),
                  _const_spec(seg_k.shape)],
        out_specs=pl.BlockSpec((bs, L, SWA_Q_W), cur),
        out_shape=jax.ShapeDtypeStruct((B, S, SWA_Q_W), BF16),
        compiler_params=_params(("parallel", "parallel")),
        name="swa",
    )(sinks.astype(F32), as3(sq), as3(sk), as3(sk), as3(sv), as3(sv),
      jnp.tile(q_norm_w, SWA_Q_HEADS).reshape(1, -1), jnp.tile(k_norm_w, SWA_KV_HEADS).reshape(1, -1),
      seg_q, seg_k).reshape(B * S, SWA_Q_W)


def _ret_body(q_ref, k_ref, v_ref, g_ref, cos_ref, sin_ref, dm_ref, qd_ref, kd_ref, cd_ref, nw_ref,
              o_ref, st_ref):
    n = pl.program_id(1)

    @pl.when(n == 0)
    def _():
        st_ref[...] = jnp.zeros_like(st_ref)

    for b in range(q_ref.shape[0]):
        _ret_block(b, q_ref, k_ref, v_ref, g_ref, cos_ref, sin_ref, dm_ref, qd_ref, kd_ref, cd_ref, nw_ref,
                   o_ref, st_ref)


def _ret_block(b, q_ref, k_ref, v_ref, g_ref, cos_ref, sin_ref, dm_ref, qd_ref, kd_ref, cd_ref, nw_ref,
               o_ref, st_ref):
    cos = cos_ref[...]
    sin = sin_ref[...]
    dk, dv = RET_QK_DIM, RET_V_DIM
    outs = []
    for h in range(RET_HEADS):
        qh = q_ref[b, :, h * dk:(h + 1) * dk].astype(F32)
        kh = k_ref[b, :, h * dk:(h + 1) * dk].astype(F32)
        qr = qh * cos + pltpu.roll(qh, dk // 2, 1) * sin
        kr = (kh * cos + pltpu.roll(kh, dk // 2, 1) * sin) * (dk ** -0.5)
        vh = v_ref[b, :, h * dv:(h + 1) * dv]
        st = st_ref[b, h]
        inner = _dot_nt(qr.astype(BF16), kr.astype(BF16)) * dm_ref[h]
        o = _dot(inner.astype(BF16), vh) + _dot((qr * qd_ref[h]).astype(BF16), st.astype(BF16))
        st_ref[b, h] = st * cd_ref[h] + _dot_tn((kr * kd_ref[h]).astype(BF16), vh)
        mu = jnp.mean(o, axis=-1, keepdims=True)
        oc = o - mu
        var = jnp.mean(oc * oc, axis=-1, keepdims=True)
        y = oc * lax.rsqrt(var + EPS) * nw_ref[:, h * dv:(h + 1) * dv]
        g = g_ref[b, :, h * dv:(h + 1) * dv].astype(F32)
        outs.append(g * _sigmoid(g) * y)
    o_ref[b] = jnp.concatenate(outs, axis=-1).astype(o_ref.dtype)


def _retention(rq, rk, rv, rg, ret_norm_w, B, S):
    L = RET_CHUNK
    nc = S // L
    half = RET_QK_DIM // 2
    inv = 1.0 / (ROPE_BASE ** (jnp.arange(0, half, dtype=F32) / half))
    ang = jnp.arange(S, dtype=F32)[:, None] * inv[None, :]
    cos = jnp.cos(ang)
    sin = jnp.sin(ang)
    cos2 = jnp.concatenate([cos, cos], axis=-1)
    sin2 = jnp.concatenate([-sin, sin], axis=-1)
    log_g = jnp.log1p(-(2.0 ** (-5.0 - jnp.arange(RET_HEADS, dtype=F32))))
    pos = jnp.arange(L, dtype=F32)
    diff = pos[:, None] - pos[None, :]
    decay_mask = jnp.where(diff[None] >= 0,
                           jnp.exp(jnp.maximum(diff, 0.0)[None] * log_g[:, None, None]), 0.0)
    q_decay = jnp.exp((pos[None, :] + 1.0) * log_g[:, None])[..., None]
    k_decay = jnp.exp((L - 1.0 - pos[None, :]) * log_g[:, None])[..., None]
    chunk_decay = jnp.exp(L * log_g)
    qd = jnp.broadcast_to(q_decay, (RET_HEADS, L, RET_QK_DIM))
    kd = jnp.broadcast_to(k_decay, (RET_HEADS, L, RET_QK_DIM))
    bs = BATCH_PER_STEP if B % BATCH_PER_STEP == 0 else 1
    row = lambda b, n: (b, n, 0)
    as3 = lambda t: t.reshape(B, S, t.shape[-1])
    return pl.pallas_call(
        _ret_body,
        grid=(B // bs, nc),
        in_specs=[pl.BlockSpec((bs, L, RET_QK_W), row),
                  pl.BlockSpec((bs, L, RET_QK_W), row),
                  pl.BlockSpec((bs, L, RET_V_W), row),
                  pl.BlockSpec((bs, L, RET_V_W), row),
                  pl.BlockSpec((L, RET_QK_DIM), lambda b, n: (n, 0)),
                  pl.BlockSpec((L, RET_QK_DIM), lambda b, n: (n, 0)),
                  _const_spec((RET_HEADS, L, L)),
                  _const_spec((RET_HEADS, L, RET_QK_DIM)),
                  _const_spec((RET_HEADS, L, RET_QK_DIM)),
                  pl.BlockSpec(memory_space=pltpu.SMEM),
                  _const_spec((1, RET_V_W))],
        out_specs=pl.BlockSpec((bs, L, RET_V_W), row),
        out_shape=jax.ShapeDtypeStruct((B, S, RET_V_W), BF16),
        scratch_shapes=[pltpu.VMEM((bs, RET_HEADS, RET_QK_DIM, RET_V_DIM), F32)],
        compiler_params=_params(("parallel", "arbitrary")),
        name="retention",
    )(as3(rq), as3(rk), as3(rv), as3(rg), cos2, sin2, decay_mask, qd, kd, chunk_decay,
      ret_norm_w.reshape(1, -1)).reshape(B * S, RET_V_W)


def _memkv_body(m_ref, nw_ref, w_ref, kw_ref, k_ref, v_ref):
    m = _rms(m_ref[...], nw_ref[...]).astype(BF16)
    kv = _dot(m, w_ref[...])
    d = MEM_HEAD_DIM
    ks = [_rms(kv[:, h * d:(h + 1) * d], kw_ref[...]) for h in range(MEM_HEADS)]
    k_ref[...] = jnp.concatenate(ks, axis=-1).astype(k_ref.dtype)
    v_ref[...] = kv[:, MEM_W:].astype(v_ref.dtype)


def _memkv(mem2, mem_norm_w, w_kv_bf, k_norm_w, B, M):
    D = mem2.shape[-1]
    return pl.pallas_call(
        _memkv_body,
        grid=(B,),
        in_specs=[pl.BlockSpec((M, D), lambda b: (b, 0)),
                  _const_spec((1, D)),
                  _const_spec(w_kv_bf.shape),
                  _const_spec((1, MEM_HEAD_DIM))],
        out_specs=[pl.BlockSpec((M, MEM_W), lambda b: (b, 0))] * 2,
        out_shape=[jax.ShapeDtypeStruct((B * M, MEM_W), BF16)] * 2,
        compiler_params=_params(("parallel",)),
        name="memkv",
    )(mem2, mem_norm_w.reshape(1, D), w_kv_bf, k_norm_w.reshape(1, -1))


def _memattn_body(q_ref, k_ref, v_ref, qw_ref, o_ref):
    d = MEM_HEAD_DIM
    outs = []
    for h in range(MEM_HEADS):
        qh = _rms(q_ref[:, h * d:(h + 1) * d].astype(F32), qw_ref[...]).astype(BF16)
        s = _dot_nt(qh, k_ref[:, h * d:(h + 1) * d]) * (d ** -0.5)
        m = jnp.max(s, axis=-1, keepdims=True)
        p = jnp.exp(s - m)
        pr = p / jnp.sum(p, axis=-1, keepdims=True)
        outs.append(_dot(pr.astype(BF16), v_ref[:, h * d:(h + 1) * d]))
    o_ref[...] = jnp.concatenate(outs, axis=-1).astype(o_ref.dtype)


def _memattn(mq, mk, mv, q_norm_w, B, S, M):
    tq = min(TQ_MEM, S)
    nq = S // tq
    return pl.pallas_call(
        _memattn_body,
        grid=(B, nq),
        in_specs=[pl.BlockSpec((tq, MEM_W), lambda b, i: (b * nq + i, 0)),
                  pl.BlockSpec((M, MEM_W), lambda b, i: (b, 0)),
                  pl.BlockSpec((M, MEM_W), lambda b, i: (b, 0)),
                  _const_spec((1, MEM_HEAD_DIM))],
        out_specs=pl.BlockSpec((tq, MEM_W), lambda b, i: (b * nq + i, 0)),
        out_shape=jax.ShapeDtypeStruct((B * S, MEM_W), BF16),
        compiler_params=_params(("parallel", "parallel")),
        name="memattn",
    )(mq, mk, mv, q_norm_w.reshape(1, -1))


def _split_bf16(a):
    hi = a.astype(BF16)
    lo = (a - hi.astype(F32)).astype(BF16)
    return hi, lo


def _merge_body(x_ref, osw_ref, ort_ref, omm_ref, zg_ref, wa_ref, wr_ref, wm_ref, wo_ref, nw_ref,
                wrt_ref, brt_ref, x1_ref, h2_ref, idx_ref, gate_ref, cnt_ref):
    D = D_MODEL
    sg = lambda j: _sigmoid(zg_ref[:, j * D:(j + 1) * D].astype(F32))
    merged = (sg(0) * _dot(osw_ref[...], wa_ref[...])
              + sg(1) * _dot(ort_ref[...], wr_ref[...])
              + sg(2) * _dot(omm_ref[...], wm_ref[...]))
    x1 = x_ref[...] + _dot(merged.astype(BF16), wo_ref[...])
    x1_ref[...] = x1
    h2 = _rms(x1, nw_ref[...])
    h2_ref[...] = h2.astype(h2_ref.dtype)
    h_hi, h_lo = _split_bf16(h2)
    w_hi, w_lo = _split_bf16(wrt_ref[...])
    logits = _dot(h_hi, w_hi) + _dot(h_lo, w_hi) + _dot(h_hi, w_lo) + brt_ref[...]
    lane = lax.broadcasted_iota(I32, logits.shape, 1)
    vals, idxs = [], []
    l = logits
    for _ in range(TOP_K):
        m = jnp.max(l, axis=-1, keepdims=True)
        i = jnp.min(jnp.where(l == m, lane, LANES), axis=-1, keepdims=True)
        vals.append(m)
        idxs.append(i)
        l = jnp.where(lane == i, -jnp.inf, l)
    es = [jnp.exp(v - vals[0]) for v in vals]
    den = es[0] + es[1] + es[2] + es[3]
    idx_ref[...] = jnp.concatenate(idxs, axis=-1)
    gate_ref[...] = jnp.concatenate([e / den for e in es], axis=-1)
    chosen = sum((lane == i).astype(F32) for i in idxs)
    ts = min(TS_SORT, chosen.shape[0])
    per_tile = [jnp.sum(chosen[j * ts:(j + 1) * ts], axis=0, keepdims=True) for j in range(chosen.shape[0] // ts)]
    per_tile.append(jnp.zeros((SUBLANES - len(per_tile), LANES), F32))
    cnt_ref[...] = jnp.concatenate(per_tile, axis=0)


def _merge(x2, o_swa, o_ret, o_mem, zg, wa, wr, wm, wo, ffn_norm_w, w_router_pad, b_router_pad):
    T, D = x2.shape
    tm = min(TM_MERGE, T)
    row = lambda i: (i, 0)
    return pl.pallas_call(
        _merge_body,
        grid=(T // tm,),
        in_specs=[pl.BlockSpec((tm, D), row),
                  pl.BlockSpec((tm, SWA_Q_W), row),
                  pl.BlockSpec((tm, RET_V_W), row),
                  pl.BlockSpec((tm, MEM_W), row),
                  pl.BlockSpec((tm, N_BRANCH * D), row),
                  _const_spec(wa.shape), _const_spec(wr.shape), _const_spec(wm.shape), _const_spec(wo.shape),
                  _const_spec((1, D)),
                  _const_spec(w_router_pad.shape), _const_spec((1, LANES))],
        out_specs=[pl.BlockSpec((tm, D), row),
                   pl.BlockSpec((tm, D), row),
                   pl.BlockSpec((tm, TOP_K), row),
                   pl.BlockSpec((tm, TOP_K), row),
                   pl.BlockSpec((None, SUBLANES, LANES), lambda i: (i, 0, 0))],
        out_shape=[jax.ShapeDtypeStruct((T, D), F32),
                   jax.ShapeDtypeStruct((T, D), BF16),
                   jax.ShapeDtypeStruct((T, TOP_K), I32),
                   jax.ShapeDtypeStruct((T, TOP_K), F32),
                   jax.ShapeDtypeStruct((T // tm, SUBLANES, LANES), F32)],
        compiler_params=_params(("parallel",)),
        name="merge",
    )(x2, o_swa, o_ret, o_mem, zg, wa, wr, wm, wo, ffn_norm_w.reshape(1, D), w_router_pad, b_router_pad)


def _chunk_copy(src_ref, src_chunk, dst_ref, dst_chunk, sem):
    def rows(c):
        start = c * SEG_ALIGN
        return pl.ds(start if isinstance(c, int) else pl.multiple_of(start, SEG_ALIGN), SEG_ALIGN)

    return pltpu.make_async_copy(src_ref.at[rows(src_chunk), :], dst_ref.at[rows(dst_chunk), :], sem)


def _dispatch_body(fill_start_ref, fill_n_ref, gch_ref, h2_ref, idxt_ref, loff_ref,
                   xs_ref, rowt_ref, buf_ref, zero_ref, sems, zsem):
    s = pl.program_id(0)
    last = pl.num_programs(0) - 1
    slot = s % 2
    ts = h2_ref.shape[0]
    nchunks = gch_ref.shape[-1]
    local_rows = nchunks * SEG_ALIGN

    def wait_slot(sl):
        for c in range(nchunks):
            _chunk_copy(buf_ref.at[sl], c, xs_ref, c, sems.at[sl]).wait()

    @pl.when(s == 0)
    def _():
        zero_ref[...] = jnp.zeros_like(zero_ref)

        def per_expert(e, total):
            def fill(r, _):
                _chunk_copy(zero_ref, 0, xs_ref, fill_start_ref[e] + r, zsem).start()
                return 0

            lax.fori_loop(0, fill_n_ref[e], fill, 0)
            return total + fill_n_ref[e]

        total = lax.fori_loop(0, N_EXPERTS, per_expert, 0)

        def drain(r, _):
            _chunk_copy(zero_ref, 0, xs_ref, 0, zsem).wait()
            return 0

        lax.fori_loop(0, total, drain, 0)

    @pl.when(s >= 2)
    def _():
        wait_slot(slot)

    idxt = idxt_ref[...]
    sub = lax.broadcasted_iota(I32, (LANES, ts), 0)
    ohs = [(sub == idxt[k:k + 1, :]).astype(F32) for k in range(TOP_K)]
    m = ohs[0] + ohs[1] + ohs[2] + ohs[3]
    tr = lax.broadcasted_iota(I32, (ts, ts), 0)
    tc = lax.broadcasted_iota(I32, (ts, ts), 1)
    earlier = jnp.where(tr < tc, 1.0, 0.0).astype(BF16)
    before = _dot(m.astype(BF16), earlier) + loff_ref[...]
    rows = [jnp.sum(oh * before, axis=0, keepdims=True).astype(I32) for oh in ohs]
    rowt_ref[...] = jnp.concatenate(rows, axis=0)
    ri = lax.broadcasted_iota(I32, (local_rows, ts), 0)
    sel = (ri == rows[0]) | (ri == rows[1]) | (ri == rows[2]) | (ri == rows[3])
    buf_ref[slot] = _pack_halves(_dot(jnp.where(sel, 1.0, 0.0).astype(BF16), h2_ref[...]))
    for c in range(nchunks):
        _chunk_copy(buf_ref.at[slot], c, xs_ref, gch_ref[0, 0, c], sems.at[slot]).start()

    @pl.when(s == last)
    def _():
        wait_slot(slot)

        @pl.when(s >= 1)
        def _():
            wait_slot(1 - slot)


def _dispatch(h2, idxt, loff_col, gchunk, fill_start, fill_n, n_rows, ts):
    T, D = h2.shape
    nt = T // ts
    nchunks = gchunk.shape[-1]
    return pl.pallas_call(
        _dispatch_body,
        grid_spec=pltpu.PrefetchScalarGridSpec(
            num_scalar_prefetch=2,
            grid=(nt,),
            in_specs=[pl.BlockSpec((1, 1, nchunks), lambda s, *_: (s, 0, 0), memory_space=pltpu.SMEM),
                      pl.BlockSpec((ts, D), lambda s, *_: (s, 0)),
                      pl.BlockSpec((TOP_K, ts), lambda s, *_: (0, s)),
                      pl.BlockSpec((None, LANES, 1), lambda s, *_: (s, 0, 0))],
            out_specs=[pl.BlockSpec(memory_space=pl.ANY),
                       pl.BlockSpec((TOP_K, ts), lambda s, *_: (0, s))],
            scratch_shapes=[pltpu.VMEM((2, nchunks * SEG_ALIGN, D // 2), U32),
                            pltpu.VMEM((SEG_ALIGN, D // 2), U32),
                            pltpu.SemaphoreType.DMA((2,)),
                            pltpu.SemaphoreType.DMA(())]),
        out_shape=[jax.ShapeDtypeStruct((n_rows, D // 2), U32),
                   jax.ShapeDtypeStruct((TOP_K, T), I32)],
        compiler_params=_params(("arbitrary",)),
        name="dispatch",
    )(fill_start, fill_n, gchunk.reshape(nt, 1, nchunks), h2, idxt, loff_col)


def _expert_body(te_ref, nu_ref, x_ref, wu_ref, bu_ref, wd_ref, bd_ref, o_ref, wu_bf, wd_bf):
    i = pl.program_id(0)
    active = i < nu_ref[0]
    new_expert = jnp.logical_or(i == 0, te_ref[i] != te_ref[jnp.maximum(i - 1, 0)])
    half = MXU_COLS // 2

    @pl.when(jnp.logical_and(active, new_expert))
    def _():
        r = lax.broadcasted_iota(I32, (MXU_COLS, MXU_COLS), 0)
        c = lax.broadcasted_iota(I32, (MXU_COLS, MXU_COLS), 1)
        perm = jnp.where(r == jnp.where(c < half, 2 * c, 2 * (c - half) + 1), 1.0, 0.0).astype(BF16)
        for b in range(wu_ref.shape[-1] // MXU_COLS):
            cols = slice(b * MXU_COLS, (b + 1) * MXU_COLS)
            wu_bf[:, cols] = _dot(wu_ref[:, cols].astype(BF16), perm).astype(BF16)
        wd_bf[...] = wd_ref[...].astype(BF16)

    @pl.when(active)
    def _():
        x = jnp.concatenate(_unpack_halves(x_ref[...]), axis=-1)
        hid = _dot(x, wu_bf[...]) + bu_ref[...]
        acts = []
        for b in range(hid.shape[-1] // MXU_COLS):
            x_glu = jnp.minimum(hid[:, b * MXU_COLS:b * MXU_COLS + half], SWIGLU_LIMIT)
            x_lin = jnp.clip(hid[:, b * MXU_COLS + half:(b + 1) * MXU_COLS], -SWIGLU_LIMIT, SWIGLU_LIMIT)
            acts.append(x_glu * _sigmoid(SWIGLU_ALPHA * x_glu) * (x_lin + 1.0))
        act = jnp.concatenate(acts, axis=-1).astype(BF16)
        out = _dot(act, wd_bf[...]) + bd_ref[...]
        o_ref[...] = _pack_halves(out.astype(BF16).astype(F32))


def _experts(xs, tile_expert, n_used, wu, bu, wd, bd, n_tiles):
    tm = TM_EXPERT
    D = wu.shape[1]
    F2 = wu.shape[-1]
    rows = lambda i, te, nu: (jnp.minimum(i, nu[0] - 1), 0)
    wsel = lambda i, te, nu: (te[i], 0, 0)
    return pl.pallas_call(
        _expert_body,
        grid_spec=pltpu.PrefetchScalarGridSpec(
            num_scalar_prefetch=2,
            grid=(n_tiles,),
            in_specs=[pl.BlockSpec((tm, D // 2), rows),
                      pl.BlockSpec((None, D, F2), wsel),
                      pl.BlockSpec((None, 1, F2), wsel),
                      pl.BlockSpec((None, F2 // 2, D), wsel),
                      pl.BlockSpec((None, 1, D), wsel)],
            out_specs=pl.BlockSpec((tm, D // 2), rows),
            scratch_shapes=[pltpu.VMEM((D, F2), BF16), pltpu.VMEM((F2 // 2, D), BF16)]),
        out_shape=jax.ShapeDtypeStruct((n_tiles * tm, D // 2), U32),
        compiler_params=_params(("arbitrary",), vmem=VMEM_LIMIT_EXPERTS),
        name="experts",
    )(tile_expert, n_used, xs, wu, bu, wd, bd)


def _combine_body(gch_ref, gch_next_ref, ys_ref, x1_ref, row_ref, gate_ref, o_ref, buf_ref, sems):
    s = pl.program_id(0)
    slot = s % 2
    nchunks = gch_ref.shape[-1]

    def gather(g_ref, sl):
        for c in range(nchunks):
            _chunk_copy(ys_ref, g_ref[0, 0, c], buf_ref.at[sl], c, sems.at[sl]).start()

    @pl.when(s == 0)
    def _():
        gather(gch_ref, 0)

    @pl.when(s < pl.num_programs(0) - 1)
    def _():
        gather(gch_next_ref, 1 - slot)

    for c in range(nchunks):
        _chunk_copy(ys_ref, 0, buf_ref.at[slot], c, sems.at[slot]).wait()
    rows = row_ref[...]
    g = gate_ref[...]
    ci = lax.broadcasted_iota(I32, (rows.shape[0], nchunks * SEG_ALIGN), 1)
    w = sum(jnp.where(ci == rows[:, k:k + 1], g[:, k:k + 1], 0.0) for k in range(TOP_K))
    w = w.astype(BF16)
    half = x1_ref.shape[-1] // 2
    lo, hi = _unpack_halves(buf_ref[slot])
    o_ref[:, :half] = x1_ref[:, :half] + _dot(w, lo)
    o_ref[:, half:] = x1_ref[:, half:] + _dot(w, hi)


def _combine(ys, gchunk, x1, rows, gates, ts):
    T, D = x1.shape
    nt = T // ts
    nchunks = gchunk.shape[-1]
    return pl.pallas_call(
        _combine_body,
        grid=(nt,),
        in_specs=[pl.BlockSpec((1, 1, nchunks), lambda s: (s, 0, 0), memory_space=pltpu.SMEM),
                  pl.BlockSpec((1, 1, nchunks), lambda s: (jnp.minimum(s + 1, nt - 1), 0, 0),
                               memory_space=pltpu.SMEM),
                  pl.BlockSpec(memory_space=pl.ANY),
                  pl.BlockSpec((ts, D), lambda s: (s, 0)),
                  pl.BlockSpec((ts, TOP_K), lambda s: (s, 0)),
                  pl.BlockSpec((ts, TOP_K), lambda s: (s, 0))],
        out_specs=pl.BlockSpec((ts, D), lambda s: (s, 0)),
        out_shape=jax.ShapeDtypeStruct((T, D), F32),
        scratch_shapes=[pltpu.VMEM((2, nchunks * SEG_ALIGN, D // 2), U32),
                        pltpu.SemaphoreType.DMA((2,))],
        compiler_params=_params(("arbitrary",)),
        name="combine",
    )(gchunk.reshape(nt, 1, nchunks), gchunk.reshape(nt, 1, nchunks), ys, x1, rows, gates)


def _excl_cumsum(a, axis):
    return jnp.cumsum(a, axis=axis) - a


def _routed_experts(h2, x1, idx, gates, cnt, w_up, b_up, w_down, b_down):
    T, D = x1.shape
    E = w_up.shape[0]
    ts = min(TS_SORT, T)
    nt = T // ts
    nchunks = (ts * TOP_K + E * SEG_ALIGN) // SEG_ALIGN
    max_rows = T * TOP_K + nt * E * (SEG_ALIGN - 1) + E * (TM_EXPERT - 1)
    n_tiles = -(-max_rows // TM_EXPERT)
    n_rows = n_tiles * TM_EXPERT
    dump_chunk = n_rows // SEG_ALIGN

    cnt = cnt[:, :min(TM_MERGE, T) // ts, :E].reshape(nt, E).astype(I32)
    seg = -(-cnt // SEG_ALIGN)
    loff = _excl_cumsum(seg, 1)
    chunks_e = jnp.sum(seg, axis=0)
    tiles_e = -(-(chunks_e * SEG_ALIGN) // TM_EXPERT)
    tile_end = jnp.cumsum(tiles_e)
    base = (tile_end - tiles_e) * (TM_EXPERT // SEG_ALIGN)
    gstart = base[None, :] + _excl_cumsum(seg, 0)
    c = jnp.arange(nchunks, dtype=I32)
    owner = jnp.sum((loff + seg)[:, None, :] <= c[None, :, None], axis=-1)
    onehot = owner[:, :, None] == jnp.arange(E, dtype=I32)[None, None, :]
    shift = jnp.sum(jnp.where(onehot, (gstart - loff)[:, None, :], 0), axis=-1)
    used = owner < E
    gchunk = shift + c[None, :]
    dump = dump_chunk + (jnp.arange(nt, dtype=I32)[:, None] % 2) * nchunks + c[None, :]
    gchunk_out = jnp.where(used, gchunk, dump).astype(I32)
    gchunk_in = jnp.where(used, gchunk, 0).astype(I32)
    fill_start = (base + chunks_e).astype(I32)
    fill_n = (tiles_e * (TM_EXPERT // SEG_ALIGN) - chunks_e).astype(I32)
    n_used = tile_end[-1:].astype(I32)
    tile_ids = jnp.minimum(jnp.arange(n_tiles, dtype=I32), n_used[0] - 1)
    tile_expert = jnp.minimum(jnp.sum(tile_ids[:, None] >= tile_end[None, :], axis=-1), E - 1).astype(I32)
    loff_col = jnp.zeros((nt, LANES, 1), F32).at[:, :E, 0].set((loff * SEG_ALIGN).astype(F32))

    xs, rowt = _dispatch(h2, idx.T, loff_col, gchunk_out, fill_start, fill_n,
                         n_rows + 2 * nchunks * SEG_ALIGN, ts)
    half = MXU_COLS // 2
    bu = b_up.reshape(E, -1, half, 2).transpose(0, 1, 3, 2).reshape(E, 1, -1)
    ys = _experts(xs, tile_expert, n_used, w_up, bu, w_down, b_down[:, None, :], n_tiles)
    return _combine(ys, gchunk_in, x1, rowt.T, gates, ts)


def _layer(x, mem, mix_norm_w, mem_norm_w, w_in, swa_q_norm_w, swa_k_norm_w, swa_sinks, ret_norm_w,
           w_mem_kv, mem_q_norm_w, mem_k_norm_w, w_br_swa, w_br_ret, w_br_mem, w_out, ffn_norm_w,
           w_router, b_router, w_up, b_up, w_down, b_down):
    B, S, D = x.shape
    M = mem.shape[1]
    T = B * S
    x2 = x.reshape(T, D)

    sq, sk, sv, rq, rk, rv, rg, mq, zg = _inproj(x2, mix_norm_w, w_in.astype(BF16))
    o_swa = _swa(sq, sk, sv, swa_q_norm_w, swa_k_norm_w, swa_sinks, B, S)
    o_ret = _retention(rq, rk, rv, rg, ret_norm_w, B, S)
    mk, mv = _memkv(mem.reshape(B * M, D), mem_norm_w, w_mem_kv.astype(BF16), mem_k_norm_w, B, M)
    o_mem = _memattn(mq, mk, mv, mem_q_norm_w, B, S, M)

    E = w_router.shape[-1]
    w_router_pad = jnp.zeros((D, LANES), F32).at[:, :E].set(w_router)
    b_router_pad = jnp.full((1, LANES), NEG_INF, F32).at[0, :E].set(b_router)
    x1, h2, idx, gates, cnt = _merge(
        x2, o_swa, o_ret, o_mem, zg, w_br_swa.astype(BF16), w_br_ret.astype(BF16), w_br_mem.astype(BF16),
        w_out.astype(BF16), ffn_norm_w, w_router_pad, b_router_pad)

    out = _routed_experts(h2, x1, idx, gates, cnt, w_up, b_up, w_down, b_down)
    return out.reshape(B, S, D)


def kernel(x, mem, mix_norm_w, mem_norm_w, w_in, swa_q_norm_w, swa_k_norm_w, swa_sinks, ret_norm_w, w_mem_kv, mem_q_norm_w, mem_k_norm_w, w_br_swa, w_br_ret, w_br_mem, w_out, ffn_norm_w, w_router, b_router, w_up, b_up, w_down, b_down):
    args = (x, mem, mix_norm_w, mem_norm_w, w_in, swa_q_norm_w, swa_k_norm_w, swa_sinks, ret_norm_w,
            w_mem_kv, mem_q_norm_w, mem_k_norm_w, w_br_swa, w_br_ret, w_br_mem, w_out, ffn_norm_w,
            w_router, b_router, w_up, b_up, w_down, b_down)
    for l in range(w_in.shape[0]):
        x = _layer(x, mem, *[a[l] for a in args[2:]])
    return x
```

```python
import jax
import jax.numpy as jnp
from jax import lax
from jax.experimental import pallas as pl
from jax.experimental.pallas import tpu as pltpu

F32 = jnp.float32
BF16 = jnp.bfloat16
I32 = jnp.int32
U32 = jnp.uint32

D_MODEL = 1024
SWA_HEAD_DIM = 64
SWA_Q_HEADS = 8
SWA_KV_HEADS = 2
SWA_GROUP = SWA_Q_HEADS // SWA_KV_HEADS
WINDOW = 128
BAND_BLOCK = 128
RET_HEADS = 4
RET_QK_DIM = 128
RET_V_DIM = 256
RET_CHUNK = 128
ROPE_BASE = 10000.0
MEM_HEADS = 4
MEM_HEAD_DIM = 128
N_BRANCH = 3
N_EXPERTS = 32
TOP_K = 4
D_FF = 1024
SWIGLU_LIMIT = 7.0
SWIGLU_ALPHA = 1.702
EPS = 1e-6
NEG_INF = -1e30

SWA_Q_W = SWA_Q_HEADS * SWA_HEAD_DIM
SWA_KV_W = SWA_KV_HEADS * SWA_HEAD_DIM
RET_QK_W = RET_HEADS * RET_QK_DIM
RET_V_W = RET_HEADS * RET_V_DIM
MEM_W = MEM_HEADS * MEM_HEAD_DIM
IN_SIZES = (SWA_Q_W, SWA_KV_W, SWA_KV_W, RET_QK_W, RET_QK_W, RET_V_W, RET_V_W, MEM_W, N_BRANCH * D_MODEL)

SUBLANES = 8
LANES = 128
MXU_COLS = 256
VMEM_LIMIT = 56 * 1024 * 1024
VMEM_LIMIT_EXPERTS = 62 * 1024 * 1024

TM_PROJ = 512
TM_MERGE = 1024
TQ_MEM = 512
TM_EXPERT = 512
BATCH_PER_STEP = 4
TS_SORT = 256
SEG_ALIGN = SUBLANES
LOCAL_ROWS = TS_SORT * TOP_K + N_EXPERTS * SEG_ALIGN
LOCAL_CHUNKS = LOCAL_ROWS // SEG_ALIGN


def _params(sem, vmem=VMEM_LIMIT):
    return pltpu.CompilerParams(dimension_semantics=sem, vmem_limit_bytes=vmem)


def _const_spec(shape):
    nd = len(shape)
    return pl.BlockSpec(shape, lambda *_: (0,) * nd, pipeline_mode=pl.Buffered(1))


def _rms(x, w):
    return x * lax.rsqrt(jnp.mean(x * x, axis=-1, keepdims=True) + EPS) * w


def _sigmoid(x):
    return 0.5 * jnp.tanh(0.5 * x) + 0.5


def _pack_halves(a):
    n = a.shape[-1] // 2
    lo = lax.bitcast_convert_type(a[:, :n], U32) >> 16
    hi = lax.bitcast_convert_type(a[:, n:], U32) & jnp.uint32(0xFFFF0000)
    return lo | hi


def _unpack_halves(w):
    lo = lax.bitcast_convert_type(w << 16, F32)
    hi = lax.bitcast_convert_type(w & jnp.uint32(0xFFFF0000), F32)
    return lo.astype(BF16), hi.astype(BF16)


def _dot(a, b):
    return jnp.dot(a, b, preferred_element_type=F32)


def _dot_nt(a, b):
    return lax.dot_general(a, b, (((1,), (1,)), ((), ())), preferred_element_type=F32)


def _dot_tn(a, b):
    return lax.dot_general(a, b, (((0,), (0,)), ((), ())), preferred_element_type=F32)


def _inproj_body(x_ref, nw_ref, w_ref, *o_refs):
    h = _rms(x_ref[...], nw_ref[...]).astype(BF16)
    off = 0
    for o_ref in o_refs:
        s = o_ref.shape[-1]
        o_ref[...] = _dot(h, w_ref[:, off:off + s]).astype(o_ref.dtype)
        off += s


def _inproj(x2, norm_w, w_in_bf):
    T, D = x2.shape
    tm = TM_PROJ
    return pl.pallas_call(
        _inproj_body,
        grid=(T // tm,),
        in_specs=[pl.BlockSpec((tm, D), lambda i: (i, 0)),
                  _const_spec((1, D)),
                  _const_spec(w_in_bf.shape)],
        out_specs=[pl.BlockSpec((tm, s), lambda i: (i, 0)) for s in IN_SIZES],
        out_shape=[jax.ShapeDtypeStruct((T, s), BF16) for s in IN_SIZES],
        compiler_params=_params(("parallel",)),
        name="inproj",
    )(x2, norm_w.reshape(1, D), w_in_bf)


def _head_rms(t, seg_ref, w):
    hi, lo = _split_bf16(t * t)
    ss = _dot(hi, seg_ref[...]) + _dot(lo, seg_ref[...])
    return t * lax.rsqrt(ss * (1.0 / SWA_HEAD_DIM) + EPS) * w


def _swa_body(sink_ref, q_ref, kc_ref, kp_ref, vc_ref, vp_ref, qw_ref, kw_ref, segq_ref, segk_ref, o_ref):
    n = pl.program_id(1)
    L = q_ref.shape[1]
    qi = lax.broadcasted_iota(I32, (L, 2 * L), 0)
    kj = lax.broadcasted_iota(I32, (L, 2 * L), 1)
    diff = L + qi - kj
    has_prev = jnp.minimum(n, 1) * L
    mask = (diff >= 0) & (diff < WINDOW) & (kj + has_prev >= L)
    for b in range(q_ref.shape[0]):
        o_ref[b] = _swa_block(sink_ref, q_ref[b], kc_ref[b], kp_ref[b], vc_ref[b], vp_ref[b],
                              qw_ref, kw_ref, segq_ref, segk_ref, mask).astype(o_ref.dtype)


def _swa_block(sink_ref, q, kc, kp, vc, vp, qw_ref, kw_ref, segq_ref, segk_ref, mask):
    d = SWA_HEAD_DIM
    q = _head_rms(q.astype(F32), segq_ref, qw_ref[...]) * (d ** -0.5)
    q = q.astype(BF16)
    k = jnp.concatenate([kp, kc], axis=0).astype(F32)
    k = _head_rms(k, segk_ref, kw_ref[...])
    v = jnp.concatenate([vp, vc], axis=0).astype(F32)
    upper = lax.broadcasted_iota(I32, k.shape, 1) >= d
    k_sw = pltpu.roll(k, d, 1)
    v_sw = pltpu.roll(v, d, 1)

    def placed(t, t_sw):
        return [[jnp.where(upper if half == 1 else jnp.logical_not(upper), t if h == half else t_sw,
                           0.0).astype(BF16) for half in range(2)] for h in range(SWA_KV_HEADS)]

    k_at = placed(k, k_sw)
    v_at = placed(v, v_sw)
    pairs = []
    for j in range(SWA_Q_HEADS // 2):
        h = (2 * j) // SWA_GROUP
        qb = q[:, 2 * j * d:2 * (j + 1) * d]
        acc = None
        for half in range(2):
            s = jnp.where(mask, _dot_nt(qb, k_at[h][half]), NEG_INF)
            sink = sink_ref[2 * j + half]
            m = jnp.maximum(jnp.max(s, axis=-1, keepdims=True), sink)
            p = jnp.exp(s - m)
            den = jnp.sum(p, axis=-1, keepdims=True) + jnp.exp(sink - m)
            o = _dot(p.astype(BF16), v_at[h][half]) * (1.0 / den)
            acc = o if acc is None else acc + o
        pairs.append(acc)
    return jnp.concatenate(pairs, axis=-1)


def _swa(sq, sk, sv, q_norm_w, k_norm_w, sinks, B, S):
    L = BAND_BLOCK
    nb = S // L
    d = SWA_HEAD_DIM
    bs = BATCH_PER_STEP if B % BATCH_PER_STEP == 0 else 1
    cur = lambda b, n: (b, n, 0)
    prev = lambda b, n: (b, jnp.maximum(n - 1, 0), 0)
    as3 = lambda t: t.reshape(B, S, t.shape[-1])
    group = jnp.arange(SWA_Q_W, dtype=I32) // d
    seg_q = (group[:, None] == group[None, :]).astype(BF16)
    seg_k = seg_q[:SWA_KV_W, :SWA_KV_W]
    return pl.pallas_call(
        _swa_body,
        grid=(B // bs, nb),
        in_specs=[pl.BlockSpec(memory_space=pltpu.SMEM),
                  pl.BlockSpec((bs, L, SWA_Q_W), cur),
                  pl.BlockSpec((bs, L, SWA_KV_W), cur),
                  pl.BlockSpec((bs, L, SWA_KV_W), prev),
                  pl.BlockSpec((bs, L, SWA_KV_W), cur),
                  pl.BlockSpec((bs, L, SWA_KV_W), prev),
                  _const_spec((1, SWA_Q_W)),
                  _const_spec((1, SWA_KV_W)),
                  _const_spec(seg_q.shape),
                  _const_spec(seg_k.shape)],
        out_specs=pl.BlockSpec((bs, L, SWA_Q_W), cur),
        out_shape=jax.ShapeDtypeStruct((B, S, SWA_Q_W), BF16),
        compiler_params=_params(("parallel", "parallel")),
        name="swa",
    )(sinks.astype(F32), as3(sq), as3(sk), as3(sk), as3(sv), as3(sv),
      jnp.tile(q_norm_w, SWA_Q_HEADS).reshape(1, -1), jnp.tile(k_norm_w, SWA_KV_HEADS).reshape(1, -1),
      seg_q, seg_k).reshape(B * S, SWA_Q_W)


def _ret_body(q_ref, k_ref, v_ref, g_ref, cos_ref, sin_ref, dm_ref, qd_ref, kd_ref, cd_ref, nw_ref,
              o_ref, st_ref):
    n = pl.program_id(1)

    @pl.when(n == 0)
    def _():
        st_ref[...] = jnp.zeros_like(st_ref)

    for b in range(q_ref.shape[0]):
        _ret_block(b, q_ref, k_ref, v_ref, g_ref, cos_ref, sin_ref, dm_ref, qd_ref, kd_ref, cd_ref, nw_ref,
                   o_ref, st_ref)


def _ret_block(b, q_ref, k_ref, v_ref, g_ref, cos_ref, sin_ref, dm_ref, qd_ref, kd_ref, cd_ref, nw_ref,
               o_ref, st_ref):
    cos = cos_ref[...]
    sin = sin_ref[...]
    dk, dv = RET_QK_DIM, RET_V_DIM
    outs = []
    for h in range(RET_HEADS):
        qh = q_ref[b, :, h * dk:(h + 1) * dk].astype(F32)
        kh = k_ref[b, :, h * dk:(h + 1) * dk].astype(F32)
        qr = qh * cos + pltpu.roll(qh, dk // 2, 1) * sin
        kr = (kh * cos + pltpu.roll(kh, dk // 2, 1) * sin) * (dk ** -0.5)
        vh = v_ref[b, :, h * dv:(h + 1) * dv]
        st = st_ref[b, h]
        inner = _dot_nt(qr.astype(BF16), kr.astype(BF16)) * dm_ref[h]
        o = _dot(inner.astype(BF16), vh) + _dot((qr * qd_ref[h]).astype(BF16), st.astype(BF16))
        st_ref[b, h] = st * cd_ref[h] + _dot_tn((kr * kd_ref[h]).astype(BF16), vh)
        mu = jnp.mean(o, axis=-1, keepdims=True)
        oc = o - mu
        var = jnp.mean(oc * oc, axis=-1, keepdims=True)
        y = oc * lax.rsqrt(var + EPS) * nw_ref[:, h * dv:(h + 1) * dv]
        g = g_ref[b, :, h * dv:(h + 1) * dv].astype(F32)
        outs.append(g * _sigmoid(g) * y)
    o_ref[b] = jnp.concatenate(outs, axis=-1).astype(o_ref.dtype)


def _retention(rq, rk, rv, rg, ret_norm_w, B, S):
    L = RET_CHUNK
    nc = S // L
    half = RET_QK_DIM // 2
    inv = 1.0 / (ROPE_BASE ** (jnp.arange(0, half, dtype=F32) / half))
    ang = jnp.arange(S, dtype=F32)[:, None] * inv[None, :]
    cos = jnp.cos(ang)
    sin = jnp.sin(ang)
    cos2 = jnp.concatenate([cos, cos], axis=-1)
    sin2 = jnp.concatenate([-sin, sin], axis=-1)
    log_g = jnp.log1p(-(2.0 ** (-5.0 - jnp.arange(RET_HEADS, dtype=F32))))
    pos = jnp.arange(L, dtype=F32)
    diff = pos[:, None] - pos[None, :]
    decay_mask = jnp.where(diff[None] >= 0,
                           jnp.exp(jnp.maximum(diff, 0.0)[None] * log_g[:, None, None]), 0.0)
    q_decay = jnp.exp((pos[None, :] + 1.0) * log_g[:, None])[..., None]
    k_decay = jnp.exp((L - 1.0 - pos[None, :]) * log_g[:, None])[..., None]
    chunk_decay = jnp.exp(L * log_g)
    qd = jnp.broadcast_to(q_decay, (RET_HEADS, L, RET_QK_DIM))
    kd = jnp.broadcast_to(k_decay, (RET_HEADS, L, RET_QK_DIM))
    bs = BATCH_PER_STEP if B % BATCH_PER_STEP == 0 else 1
    row = lambda b, n: (b, n, 0)
    as3 = lambda t: t.reshape(B, S, t.shape[-1])
    return pl.pallas_call(
        _ret_body,
        grid=(B // bs, nc),
        in_specs=[pl.BlockSpec((bs, L, RET_QK_W), row),
                  pl.BlockSpec((bs, L, RET_QK_W), row),
                  pl.BlockSpec((bs, L, RET_V_W), row),
                  pl.BlockSpec((bs, L, RET_V_W), row),
                  pl.BlockSpec((L, RET_QK_DIM), lambda b, n: (n, 0)),
                  pl.BlockSpec((L, RET_QK_DIM), lambda b, n: (n, 0)),
                  _const_spec((RET_HEADS, L, L)),
                  _const_spec((RET_HEADS, L, RET_QK_DIM)),
                  _const_spec((RET_HEADS, L, RET_QK_DIM)),
                  pl.BlockSpec(memory_space=pltpu.SMEM),
                  _const_spec((1, RET_V_W))],
        out_specs=pl.BlockSpec((bs, L, RET_V_W), row),
        out_shape=jax.ShapeDtypeStruct((B, S, RET_V_W), BF16),
        scratch_shapes=[pltpu.VMEM((bs, RET_HEADS, RET_QK_DIM, RET_V_DIM), F32)],
        compiler_params=_params(("parallel", "arbitrary")),
        name="retention",
    )(as3(rq), as3(rk), as3(rv), as3(rg), cos2, sin2, decay_mask, qd, kd, chunk_decay,
      ret_norm_w.reshape(1, -1)).reshape(B * S, RET_V_W)


def _memkv_body(m_ref, nw_ref, w_ref, kw_ref, k_ref, v_ref):
    m = _rms(m_ref[...], nw_ref[...]).astype(BF16)
    kv = _dot(m, w_ref[...])
    d = MEM_HEAD_DIM
    ks = [_rms(kv[:, h * d:(h + 1) * d], kw_ref[...]) for h in range(MEM_HEADS)]
    k_ref[...] = jnp.concatenate(ks, axis=-1).astype(k_ref.dtype)
    v_ref[...] = kv[:, MEM_W:].astype(v_ref.dtype)


def _memkv(mem2, mem_norm_w, w_kv_bf, k_norm_w, B, M):
    D = mem2.shape[-1]
    return pl.pallas_call(
        _memkv_body,
        grid=(B,),
        in_specs=[pl.BlockSpec((M, D), lambda b: (b, 0)),
                  _const_spec((1, D)),
                  _const_spec(w_kv_bf.shape),
                  _const_spec((1, MEM_HEAD_DIM))],
        out_specs=[pl.BlockSpec((M, MEM_W), lambda b: (b, 0))] * 2,
        out_shape=[jax.ShapeDtypeStruct((B * M, MEM_W), BF16)] * 2,
        compiler_params=_params(("parallel",)),
        name="memkv",
    )(mem2, mem_norm_w.reshape(1, D), w_kv_bf, k_norm_w.reshape(1, -1))


def _memattn_body(q_ref, k_ref, v_ref, qw_ref, o_ref):
    d = MEM_HEAD_DIM
    outs = []
    for h in range(MEM_HEADS):
        qh = _rms(q_ref[:, h * d:(h + 1) * d].astype(F32), qw_ref[...]).astype(BF16)
        s = _dot_nt(qh, k_ref[:, h * d:(h + 1) * d]) * (d ** -0.5)
        m = jnp.max(s, axis=-1, keepdims=True)
        p = jnp.exp(s - m)
        pr = p / jnp.sum(p, axis=-1, keepdims=True)
        outs.append(_dot(pr.astype(BF16), v_ref[:, h * d:(h + 1) * d]))
    o_ref[...] = jnp.concatenate(outs, axis=-1).astype(o_ref.dtype)


def _memattn(mq, mk, mv, q_norm_w, B, S, M):
    tq = min(TQ_MEM, S)
    nq = S // tq
    return pl.pallas_call(
        _memattn_body,
        grid=(B, nq),
        in_specs=[pl.BlockSpec((tq, MEM_W), lambda b, i: (b * nq + i, 0)),
                  pl.BlockSpec((M, MEM_W), lambda b, i: (b, 0)),
                  pl.BlockSpec((M, MEM_W), lambda b, i: (b, 0)),
                  _const_spec((1, MEM_HEAD_DIM))],
        out_specs=pl.BlockSpec((tq, MEM_W), lambda b, i: (b * nq + i, 0)),
        out_shape=jax.ShapeDtypeStruct((B * S, MEM_W), BF16),
        compiler_params=_params(("parallel", "parallel")),
        name="memattn",
    )(mq, mk, mv, q_norm_w.reshape(1, -1))


def _mem_block(q, k_ref, v_ref, b, qw_ref):
    d = MEM_HEAD_DIM
    outs = []
    for h in range(MEM_HEADS):
        qh = _rms(q[:, h * d:(h + 1) * d].astype(F32), qw_ref[...]).astype(BF16)
        s = _dot_nt(qh, k_ref[b, :, h * d:(h + 1) * d]) * (d ** -0.5)
        m = jnp.max(s, axis=-1, keepdims=True)
        p = jnp.exp(s - m)
        o = _dot(p.astype(BF16), v_ref[b, :, h * d:(h + 1) * d])
        outs.append(o * (1.0 / jnp.sum(p, axis=-1, keepdims=True)))
    return jnp.concatenate(outs, axis=-1)


def _mixers_body(sink_ref, cd_ref, x_ref, nw_ref, w_ref, sqw_ref, skw_ref, segq_ref, segk_ref,
                 cos_ref, sin_ref, dm_ref, qd_ref, kd_ref, rnw_ref, mk_ref, mv_ref, mqw_ref,
                 osw_ref, ort_ref, omm_ref, zg_ref, st_ref, kprev_ref, vprev_ref):
    n = pl.program_id(1)
    bs, L, D = x_ref.shape

    @pl.when(n == 0)
    def _():
        st_ref[...] = jnp.zeros_like(st_ref)
        kprev_ref[...] = jnp.zeros_like(kprev_ref)
        vprev_ref[...] = jnp.zeros_like(vprev_ref)

    h = _rms(x_ref[...].reshape(bs * L, D), nw_ref[...]).astype(BF16)
    offs = [sum(IN_SIZES[:i]) for i in range(len(IN_SIZES) + 1)]

    def proj(i):
        z = _dot(h, w_ref[:, offs[i]:offs[i + 1]]).astype(BF16)
        return z.reshape(bs, L, IN_SIZES[i])

    sq, sk, sv = proj(0), proj(1), proj(2)
    qi = lax.broadcasted_iota(I32, (L, 2 * L), 0)
    kj = lax.broadcasted_iota(I32, (L, 2 * L), 1)
    diff = L + qi - kj
    has_prev = jnp.minimum(n, 1) * L
    mask = (diff >= 0) & (diff < WINDOW) & (kj + has_prev >= L)
    for b in range(bs):
        osw_ref[b] = _swa_block(sink_ref, sq[b], sk[b], kprev_ref[b], sv[b], vprev_ref[b],
                                sqw_ref, skw_ref, segq_ref, segk_ref, mask).astype(osw_ref.dtype)
    kprev_ref[...] = sk
    vprev_ref[...] = sv
    rq, rk, rv, rg = proj(3), proj(4), proj(5), proj(6)
    for b in range(bs):
        _ret_block(b, rq, rk, rv, rg, cos_ref, sin_ref, dm_ref, qd_ref, kd_ref, cd_ref, rnw_ref, ort_ref, st_ref)
    mq = proj(7)
    for b in range(bs):
        omm_ref[b] = _mem_block(mq[b], mk_ref, mv_ref, b, mqw_ref).astype(omm_ref.dtype)
    zg_ref[...] = proj(8)


def _mixers(x, mix_norm_w, w_in_bf, swa_q_norm_w, swa_k_norm_w, sinks, ret_norm_w, mk, mv, mem_q_norm_w):
    B, S, D = x.shape
    M = mk.shape[0] // B
    L = BAND_BLOCK
    nb = S // L
    bs = BATCH_PER_STEP if B % BATCH_PER_STEP == 0 else 1
    d = SWA_HEAD_DIM
    group = jnp.arange(SWA_Q_W, dtype=I32) // d
    seg_q = (group[:, None] == group[None, :]).astype(BF16)
    seg_k = seg_q[:SWA_KV_W, :SWA_KV_W]
    half = RET_QK_DIM // 2
    inv = 1.0 / (ROPE_BASE ** (jnp.arange(0, half, dtype=F32) / half))
    ang = jnp.arange(S, dtype=F32)[:, None] * inv[None, :]
    cos = jnp.cos(ang)
    sin = jnp.sin(ang)
    cos2 = jnp.concatenate([cos, cos], axis=-1)
    sin2 = jnp.concatenate([-sin, sin], axis=-1)
    log_g = jnp.log1p(-(2.0 ** (-5.0 - jnp.arange(RET_HEADS, dtype=F32))))
    pos = jnp.arange(L, dtype=F32)
    dpos = pos[:, None] - pos[None, :]
    decay_mask = jnp.where(dpos[None] >= 0,
                           jnp.exp(jnp.maximum(dpos, 0.0)[None] * log_g[:, None, None]), 0.0)
    q_decay = jnp.exp((pos[None, :] + 1.0) * log_g[:, None])[..., None]
    k_decay = jnp.exp((L - 1.0 - pos[None, :]) * log_g[:, None])[..., None]
    chunk_decay = jnp.exp(L * log_g)
    qd = jnp.broadcast_to(q_decay, (RET_HEADS, L, RET_QK_DIM))
    kd = jnp.broadcast_to(k_decay, (RET_HEADS, L, RET_QK_DIM))
    blk = lambda w: pl.BlockSpec((bs, L, w), lambda b, n: (b, n, 0))
    smem = pl.BlockSpec(memory_space=pltpu.SMEM)
    outs = pl.pallas_call(
        _mixers_body,
        grid=(B // bs, nb),
        in_specs=[smem, smem,
                  blk(D), _const_spec((1, D)), _const_spec(w_in_bf.shape),
                  _const_spec((1, SWA_Q_W)), _const_spec((1, SWA_KV_W)),
                  _const_spec(seg_q.shape), _const_spec(seg_k.shape),
                  pl.BlockSpec((L, RET_QK_DIM), lambda b, n: (n, 0)),
                  pl.BlockSpec((L, RET_QK_DIM), lambda b, n: (n, 0)),
                  _const_spec((RET_HEADS, L, L)),
                  _const_spec((RET_HEADS, L, RET_QK_DIM)),
                  _const_spec((RET_HEADS, L, RET_QK_DIM)),
                  _const_spec((1, RET_V_W)),
                  pl.BlockSpec((bs, M, MEM_W), lambda b, n: (b, 0, 0)),
                  pl.BlockSpec((bs, M, MEM_W), lambda b, n: (b, 0, 0)),
                  _const_spec((1, MEM_HEAD_DIM))],
        out_specs=[blk(SWA_Q_W), blk(RET_V_W), blk(MEM_W), blk(N_BRANCH * D)],
        out_shape=[jax.ShapeDtypeStruct((B, S, w), BF16) for w in (SWA_Q_W, RET_V_W, MEM_W, N_BRANCH * D)],
        scratch_shapes=[pltpu.VMEM((bs, RET_HEADS, RET_QK_DIM, RET_V_DIM), F32),
                        pltpu.VMEM((bs, L, SWA_KV_W), BF16),
                        pltpu.VMEM((bs, L, SWA_KV_W), BF16)],
        compiler_params=_params(("parallel", "arbitrary"), vmem=VMEM_LIMIT_EXPERTS),
        name="mixers",
    )(sinks.astype(F32), chunk_decay, x, mix_norm_w.reshape(1, D), w_in_bf,
      jnp.tile(swa_q_norm_w, SWA_Q_HEADS).reshape(1, -1), jnp.tile(swa_k_norm_w, SWA_KV_HEADS).reshape(1, -1),
      seg_q, seg_k, cos2, sin2, decay_mask, qd, kd, ret_norm_w.reshape(1, -1),
      mk.reshape(B, M, MEM_W), mv.reshape(B, M, MEM_W), mem_q_norm_w.reshape(1, -1))
    return [o.reshape(B * S, o.shape[-1]) for o in outs]


def _split_bf16(a):
    hi = a.astype(BF16)
    lo = (a - hi.astype(F32)).astype(BF16)
    return hi, lo


def _merge_body(x_ref, osw_ref, ort_ref, omm_ref, zg_ref, wa_ref, wr_ref, wm_ref, wo_ref, nw_ref,
                wrt_ref, brt_ref, x1_ref, h2_ref, idx_ref, gate_ref, cnt_ref):
    D = D_MODEL
    sg = lambda j: _sigmoid(zg_ref[:, j * D:(j + 1) * D].astype(F32))
    merged = (sg(0) * _dot(osw_ref[...], wa_ref[...])
              + sg(1) * _dot(ort_ref[...], wr_ref[...])
              + sg(2) * _dot(omm_ref[...], wm_ref[...]))
    x1 = x_ref[...] + _dot(merged.astype(BF16), wo_ref[...])
    x1_ref[...] = x1
    h2 = _rms(x1, nw_ref[...])
    h2_ref[...] = h2.astype(h2_ref.dtype)
    h_hi, h_lo = _split_bf16(h2)
    w_hi, w_lo = _split_bf16(wrt_ref[...])
    logits = _dot(h_hi, w_hi) + _dot(h_lo, w_hi) + _dot(h_hi, w_lo) + brt_ref[...]
    lane = lax.broadcasted_iota(I32, logits.shape, 1)
    vals, idxs = [], []
    l = logits
    for _ in range(TOP_K):
        m = jnp.max(l, axis=-1, keepdims=True)
        i = jnp.min(jnp.where(l == m, lane, LANES), axis=-1, keepdims=True)
        vals.append(m)
        idxs.append(i)
        l = jnp.where(lane == i, -jnp.inf, l)
    es = [jnp.exp(v - vals[0]) for v in vals]
    den = es[0] + es[1] + es[2] + es[3]
    idx_ref[...] = jnp.concatenate(idxs, axis=-1)
    gate_ref[...] = jnp.concatenate([e / den for e in es], axis=-1)
    chosen = sum((lane == i).astype(F32) for i in idxs)
    ts = min(TS_SORT, chosen.shape[0])
    per_tile = [jnp.sum(chosen[j * ts:(j + 1) * ts], axis=0, keepdims=True) for j in range(chosen.shape[0] // ts)]
    per_tile.append(jnp.zeros((SUBLANES - len(per_tile), LANES), F32))
    cnt_ref[...] = jnp.concatenate(per_tile, axis=0)


def _merge(x2, o_swa, o_ret, o_mem, zg, wa, wr, wm, wo, ffn_norm_w, w_router_pad, b_router_pad):
    T, D = x2.shape
    tm = min(TM_MERGE, T)
    row = lambda i: (i, 0)
    return pl.pallas_call(
        _merge_body,
        grid=(T // tm,),
        in_specs=[pl.BlockSpec((tm, D), row),
                  pl.BlockSpec((tm, SWA_Q_W), row),
                  pl.BlockSpec((tm, RET_V_W), row),
                  pl.BlockSpec((tm, MEM_W), row),
                  pl.BlockSpec((tm, N_BRANCH * D), row),
                  _const_spec(wa.shape), _const_spec(wr.shape), _const_spec(wm.shape), _const_spec(wo.shape),
                  _const_spec((1, D)),
                  _const_spec(w_router_pad.shape), _const_spec((1, LANES))],
        out_specs=[pl.BlockSpec((tm, D), row),
                   pl.BlockSpec((tm, D), row),
                   pl.BlockSpec((tm, TOP_K), row),
                   pl.BlockSpec((tm, TOP_K), row),
                   pl.BlockSpec((None, SUBLANES, LANES), lambda i: (i, 0, 0))],
        out_shape=[jax.ShapeDtypeStruct((T, D), F32),
                   jax.ShapeDtypeStruct((T, D), BF16),
                   jax.ShapeDtypeStruct((T, TOP_K), I32),
                   jax.ShapeDtypeStruct((T, TOP_K), F32),
                   jax.ShapeDtypeStruct((T // tm, SUBLANES, LANES), F32)],
        compiler_params=_params(("parallel",)),
        name="merge",
    )(x2, o_swa, o_ret, o_mem, zg, wa, wr, wm, wo, ffn_norm_w.reshape(1, D), w_router_pad, b_router_pad)


def _chunk_copy(src_ref, src_chunk, dst_ref, dst_chunk, sem):
    def rows(c):
        start = c * SEG_ALIGN
        return pl.ds(start if isinstance(c, int) else pl.multiple_of(start, SEG_ALIGN), SEG_ALIGN)

    return pltpu.make_async_copy(src_ref.at[rows(src_chunk), :], dst_ref.at[rows(dst_chunk), :], sem)


def _dispatch_body(fill_start_ref, fill_n_ref, gch_ref, h2_ref, idxt_ref, loff_ref,
                   xs_ref, rowt_ref, buf_ref, zero_ref, sems, zsem):
    s = pl.program_id(0)
    last = pl.num_programs(0) - 1
    slot = s % 2
    ts = h2_ref.shape[0]
    nchunks = gch_ref.shape[-1]
    local_rows = nchunks * SEG_ALIGN

    def wait_slot(sl):
        for c in range(nchunks):
            _chunk_copy(buf_ref.at[sl], c, xs_ref, c, sems.at[sl]).wait()

    @pl.when(s == 0)
    def _():
        zero_ref[...] = jnp.zeros_like(zero_ref)

        def per_expert(e, total):
            def fill(r, _):
                _chunk_copy(zero_ref, 0, xs_ref, fill_start_ref[e] + r, zsem).start()
                return 0

            lax.fori_loop(0, fill_n_ref[e], fill, 0)
            return total + fill_n_ref[e]

        total = lax.fori_loop(0, N_EXPERTS, per_expert, 0)

        def drain(r, _):
            _chunk_copy(zero_ref, 0, xs_ref, 0, zsem).wait()
            return 0

        lax.fori_loop(0, total, drain, 0)

    @pl.when(s >= 2)
    def _():
        wait_slot(slot)

    idxt = idxt_ref[...]
    sub = lax.broadcasted_iota(I32, (LANES, ts), 0)
    ohs = [(sub == idxt[k:k + 1, :]).astype(F32) for k in range(TOP_K)]
    m = ohs[0] + ohs[1] + ohs[2] + ohs[3]
    tr = lax.broadcasted_iota(I32, (ts, ts), 0)
    tc = lax.broadcasted_iota(I32, (ts, ts), 1)
    earlier = jnp.where(tr < tc, 1.0, 0.0).astype(BF16)
    before = _dot(m.astype(BF16), earlier) + loff_ref[...]
    rows = [jnp.sum(oh * before, axis=0, keepdims=True).astype(I32) for oh in ohs]
    rowt_ref[...] = jnp.concatenate(rows, axis=0)
    ri = lax.broadcasted_iota(I32, (local_rows, ts), 0)
    sel = (ri == rows[0]) | (ri == rows[1]) | (ri == rows[2]) | (ri == rows[3])
    buf_ref[slot] = _pack_halves(_dot(jnp.where(sel, 1.0, 0.0).astype(BF16), h2_ref[...]))
    for c in range(nchunks):
        _chunk_copy(buf_ref.at[slot], c, xs_ref, gch_ref[0, 0, c], sems.at[slot]).start()

    @pl.when(s == last)
    def _():
        wait_slot(slot)

        @pl.when(s >= 1)
        def _():
            wait_slot(1 - slot)


def _dispatch(h2, idxt, loff_col, gchunk, fill_start, fill_n, n_rows, ts):
    T, D = h2.shape
    nt = T // ts
    nchunks = gchunk.shape[-1]
    return pl.pallas_call(
        _dispatch_body,
        grid_spec=pltpu.PrefetchScalarGridSpec(
            num_scalar_prefetch=2,
            grid=(nt,),
            in_specs=[pl.BlockSpec((1, 1, nchunks), lambda s, *_: (s, 0, 0), memory_space=pltpu.SMEM),
                      pl.BlockSpec((ts, D), lambda s, *_: (s, 0)),
                      pl.BlockSpec((TOP_K, ts), lambda s, *_: (0, s)),
                      pl.BlockSpec((None, LANES, 1), lambda s, *_: (s, 0, 0))],
            out_specs=[pl.BlockSpec(memory_space=pl.ANY),
                       pl.BlockSpec((TOP_K, ts), lambda s, *_: (0, s))],
            scratch_shapes=[pltpu.VMEM((2, nchunks * SEG_ALIGN, D // 2), U32),
                            pltpu.VMEM((SEG_ALIGN, D // 2), U32),
                            pltpu.SemaphoreType.DMA((2,)),
                            pltpu.SemaphoreType.DMA(())]),
        out_shape=[jax.ShapeDtypeStruct((n_rows, D // 2), U32),
                   jax.ShapeDtypeStruct((TOP_K, T), I32)],
        compiler_params=_params(("arbitrary",)),
        name="dispatch",
    )(fill_start, fill_n, gchunk.reshape(nt, 1, nchunks), h2, idxt, loff_col)


def _expert_body(te_ref, nu_ref, x_ref, wu_ref, bu_ref, wd_ref, bd_ref, o_ref, wu_bf, wd_bf):
    i = pl.program_id(0)
    active = i < nu_ref[0]
    new_expert = jnp.logical_or(i == 0, te_ref[i] != te_ref[jnp.maximum(i - 1, 0)])
    half = MXU_COLS // 2

    @pl.when(jnp.logical_and(active, new_expert))
    def _():
        r = lax.broadcasted_iota(I32, (MXU_COLS, MXU_COLS), 0)
        c = lax.broadcasted_iota(I32, (MXU_COLS, MXU_COLS), 1)
        perm = jnp.where(r == jnp.where(c < half, 2 * c, 2 * (c - half) + 1), 1.0, 0.0).astype(BF16)
        for b in range(wu_ref.shape[-1] // MXU_COLS):
            cols = slice(b * MXU_COLS, (b + 1) * MXU_COLS)
            wu_bf[:, cols] = _dot(wu_ref[:, cols].astype(BF16), perm).astype(BF16)
        wd_bf[...] = wd_ref[...].astype(BF16)

    @pl.when(active)
    def _():
        x = jnp.concatenate(_unpack_halves(x_ref[...]), axis=-1)
        hid = _dot(x, wu_bf[...]) + bu_ref[...]
        acts = []
        for b in range(hid.shape[-1] // MXU_COLS):
            x_glu = jnp.minimum(hid[:, b * MXU_COLS:b * MXU_COLS + half], SWIGLU_LIMIT)
            x_lin = jnp.clip(hid[:, b * MXU_COLS + half:(b + 1) * MXU_COLS], -SWIGLU_LIMIT, SWIGLU_LIMIT)
            acts.append(x_glu * _sigmoid(SWIGLU_ALPHA * x_glu) * (x_lin + 1.0))
        act = jnp.concatenate(acts, axis=-1).astype(BF16)
        out = _dot(act, wd_bf[...]) + bd_ref[...]
        o_ref[...] = _pack_halves(out.astype(BF16).astype(F32))


def _experts(xs, tile_expert, n_used, wu, bu, wd, bd, n_tiles):
    tm = TM_EXPERT
    D = wu.shape[1]
    F2 = wu.shape[-1]
    rows = lambda i, te, nu: (jnp.minimum(i, nu[0] - 1), 0)
    wsel = lambda i, te, nu: (te[i], 0, 0)
    return pl.pallas_call(
        _expert_body,
        grid_spec=pltpu.PrefetchScalarGridSpec(
            num_scalar_prefetch=2,
            grid=(n_tiles,),
            in_specs=[pl.BlockSpec((tm, D // 2), rows),
                      pl.BlockSpec((None, D, F2), wsel),
                      pl.BlockSpec((None, 1, F2), wsel),
                      pl.BlockSpec((None, F2 // 2, D), wsel),
                      pl.BlockSpec((None, 1, D), wsel)],
            out_specs=pl.BlockSpec((tm, D // 2), rows),
            scratch_shapes=[pltpu.VMEM((D, F2), BF16), pltpu.VMEM((F2 // 2, D), BF16)]),
        out_shape=jax.ShapeDtypeStruct((n_tiles * tm, D // 2), U32),
        compiler_params=_params(("arbitrary",), vmem=VMEM_LIMIT_EXPERTS),
        name="experts",
    )(tile_expert, n_used, xs, wu, bu, wd, bd)


def _combine_body(gch_ref, gch_next_ref, ys_ref, x1_ref, row_ref, gate_ref, o_ref, buf_ref, sems):
    s = pl.program_id(0)
    slot = s % 2
    nchunks = gch_ref.shape[-1]

    def gather(g_ref, sl):
        for c in range(nchunks):
            _chunk_copy(ys_ref, g_ref[0, 0, c], buf_ref.at[sl], c, sems.at[sl]).start()

    @pl.when(s == 0)
    def _():
        gather(gch_ref, 0)

    @pl.when(s < pl.num_programs(0) - 1)
    def _():
        gather(gch_next_ref, 1 - slot)

    for c in range(nchunks):
        _chunk_copy(ys_ref, 0, buf_ref.at[slot], c, sems.at[slot]).wait()
    rows = row_ref[...]
    g = gate_ref[...]
    ci = lax.broadcasted_iota(I32, (rows.shape[0], nchunks * SEG_ALIGN), 1)
    w = sum(jnp.where(ci == rows[:, k:k + 1], g[:, k:k + 1], 0.0) for k in range(TOP_K))
    w = w.astype(BF16)
    half = x1_ref.shape[-1] // 2
    lo, hi = _unpack_halves(buf_ref[slot])
    o_ref[:, :half] = x1_ref[:, :half] + _dot(w, lo)
    o_ref[:, half:] = x1_ref[:, half:] + _dot(w, hi)


def _combine(ys, gchunk, x1, rows, gates, ts):
    T, D = x1.shape
    nt = T // ts
    nchunks = gchunk.shape[-1]
    return pl.pallas_call(
        _combine_body,
        grid=(nt,),
        in_specs=[pl.BlockSpec((1, 1, nchunks), lambda s: (s, 0, 0), memory_space=pltpu.SMEM),
                  pl.BlockSpec((1, 1, nchunks), lambda s: (jnp.minimum(s + 1, nt - 1), 0, 0),
                               memory_space=pltpu.SMEM),
                  pl.BlockSpec(memory_space=pl.ANY),
                  pl.BlockSpec((ts, D), lambda s: (s, 0)),
                  pl.BlockSpec((ts, TOP_K), lambda s: (s, 0)),
                  pl.BlockSpec((ts, TOP_K), lambda s: (s, 0))],
        out_specs=pl.BlockSpec((ts, D), lambda s: (s, 0)),
        out_shape=jax.ShapeDtypeStruct((T, D), F32),
        scratch_shapes=[pltpu.VMEM((2, nchunks * SEG_ALIGN, D // 2), U32),
                        pltpu.SemaphoreType.DMA((2,))],
        compiler_params=_params(("arbitrary",)),
        name="combine",
    )(gchunk.reshape(nt, 1, nchunks), gchunk.reshape(nt, 1, nchunks), ys, x1, rows, gates)


def _excl_cumsum(a, axis):
    return jnp.cumsum(a, axis=axis) - a


def _routed_experts(h2, x1, idx, gates, cnt, w_up, b_up, w_down, b_down):
    T, D = x1.shape
    E = w_up.shape[0]
    ts = min(TS_SORT, T)
    nt = T // ts
    nchunks = (ts * TOP_K + E * SEG_ALIGN) // SEG_ALIGN
    max_rows = T * TOP_K + nt * E * (SEG_ALIGN - 1) + E * (TM_EXPERT - 1)
    n_tiles = -(-max_rows // TM_EXPERT)
    n_rows = n_tiles * TM_EXPERT
    dump_chunk = n_rows // SEG_ALIGN

    cnt = cnt[:, :min(TM_MERGE, T) // ts, :E].reshape(nt, E).astype(I32)
    seg = -(-cnt // SEG_ALIGN)
    loff = _excl_cumsum(seg, 1)
    chunks_e = jnp.sum(seg, axis=0)
    tiles_e = -(-(chunks_e * SEG_ALIGN) // TM_EXPERT)
    tile_end = jnp.cumsum(tiles_e)
    base = (tile_end - tiles_e) * (TM_EXPERT // SEG_ALIGN)
    gstart = base[None, :] + _excl_cumsum(seg, 0)
    c = jnp.arange(nchunks, dtype=I32)
    owner = jnp.sum((loff + seg)[:, None, :] <= c[None, :, None], axis=-1)
    onehot = owner[:, :, None] == jnp.arange(E, dtype=I32)[None, None, :]
    shift = jnp.sum(jnp.where(onehot, (gstart - loff)[:, None, :], 0), axis=-1)
    used = owner < E
    gchunk = shift + c[None, :]
    dump = dump_chunk + (jnp.arange(nt, dtype=I32)[:, None] % 2) * nchunks + c[None, :]
    gchunk_out = jnp.where(used, gchunk, dump).astype(I32)
    gchunk_in = jnp.where(used, gchunk, 0).astype(I32)
    fill_start = (base + chunks_e).astype(I32)
    fill_n = (tiles_e * (TM_EXPERT // SEG_ALIGN) - chunks_e).astype(I32)
    n_used = tile_end[-1:].astype(I32)
    tile_ids = jnp.minimum(jnp.arange(n_tiles, dtype=I32), n_used[0] - 1)
    tile_expert = jnp.minimum(jnp.sum(tile_ids[:, None] >= tile_end[None, :], axis=-1), E - 1).astype(I32)
    loff_col = jnp.zeros((nt, LANES, 1), F32).at[:, :E, 0].set((loff * SEG_ALIGN).astype(F32))

    xs, rowt = _dispatch(h2, idx.T, loff_col, gchunk_out, fill_start, fill_n,
                         n_rows + 2 * nchunks * SEG_ALIGN, ts)
    half = MXU_COLS // 2
    bu = b_up.reshape(E, -1, half, 2).transpose(0, 1, 3, 2).reshape(E, 1, -1)
    ys = _experts(xs, tile_expert, n_used, w_up, bu, w_down, b_down[:, None, :], n_tiles)
    return _combine(ys, gchunk_in, x1, rowt.T, gates, ts)


def _layer(x, mem, mix_norm_w, mem_norm_w, w_in, swa_q_norm_w, swa_k_norm_w, swa_sinks, ret_norm_w,
           w_mem_kv, mem_q_norm_w, mem_k_norm_w, w_br_swa, w_br_ret, w_br_mem, w_out, ffn_norm_w,
           w_router, b_router, w_up, b_up, w_down, b_down):
    B, S, D = x.shape
    M = mem.shape[1]
    T = B * S
    x2 = x.reshape(T, D)

    mk, mv = _memkv(mem.reshape(B * M, D), mem_norm_w, w_mem_kv.astype(BF16), mem_k_norm_w, B, M)
    o_swa, o_ret, o_mem, zg = _mixers(x, mix_norm_w, w_in.astype(BF16), swa_q_norm_w, swa_k_norm_w, swa_sinks,
                                      ret_norm_w, mk, mv, mem_q_norm_w)

    E = w_router.shape[-1]
    w_router_pad = jnp.zeros((D, LANES), F32).at[:, :E].set(w_router)
    b_router_pad = jnp.full((1, LANES), NEG_INF, F32).at[0, :E].set(b_router)
    x1, h2, idx, gates, cnt = _merge(
        x2, o_swa, o_ret, o_mem, zg, w_br_swa.astype(BF16), w_br_ret.astype(BF16), w_br_mem.astype(BF16),
        w_out.astype(BF16), ffn_norm_w, w_router_pad, b_router_pad)

    out = _routed_experts(h2, x1, idx, gates, cnt, w_up, b_up, w_down, b_down)
    return out.reshape(B, S, D)


def kernel(x, mem, mix_norm_w, mem_norm_w, w_in, swa_q_norm_w, swa_k_norm_w, swa_sinks, ret_norm_w, w_mem_kv, mem_q_norm_w, mem_k_norm_w, w_br_swa, w_br_ret, w_br_mem, w_out, ffn_norm_w, w_router, b_router, w_up, b_up, w_down, b_down):
    args = (x, mem, mix_norm_w, mem_norm_w, w_in, swa_q_norm_w, swa_k_norm_w, swa_sinks, ret_norm_w,
            w_mem_kv, mem_q_norm_w, mem_k_norm_w, w_br_swa, w_br_ret, w_br_mem, w_out, ffn_norm_w,
            w_router, b_router, w_up, b_up, w_down, b_down)
    for l in range(w_in.shape[0]):
        x = _layer(x, mem, *[a[l] for a in args[2:]])
    return x
```

```python
import functools

import jax
import jax.numpy as jnp
from jax import lax
from jax.experimental import pallas as pl
from jax.experimental.pallas import tpu as pltpu

F32 = jnp.float32
BF16 = jnp.bfloat16
I32 = jnp.int32
U32 = jnp.uint32

D_MODEL = 1024
SWA_HEAD_DIM = 64
SWA_Q_HEADS = 8
SWA_KV_HEADS = 2
SWA_GROUP = SWA_Q_HEADS // SWA_KV_HEADS
WINDOW = 128
BAND_BLOCK = 128
RET_HEADS = 4
RET_QK_DIM = 128
RET_V_DIM = 256
RET_CHUNK = 128
ROPE_BASE = 10000.0
MEM_HEADS = 4
MEM_HEAD_DIM = 128
N_BRANCH = 3
N_EXPERTS = 32
TOP_K = 4
D_FF = 1024
SWIGLU_LIMIT = 7.0
SWIGLU_ALPHA = 1.702
EPS = 1e-6
NEG_INF = -1e30

SWA_Q_W = SWA_Q_HEADS * SWA_HEAD_DIM
SWA_KV_W = SWA_KV_HEADS * SWA_HEAD_DIM
RET_QK_W = RET_HEADS * RET_QK_DIM
RET_V_W = RET_HEADS * RET_V_DIM
MEM_W = MEM_HEADS * MEM_HEAD_DIM
IN_SIZES = (SWA_Q_W, SWA_KV_W, SWA_KV_W, RET_QK_W, RET_QK_W, RET_V_W, RET_V_W, MEM_W, N_BRANCH * D_MODEL)

SUBLANES = 8
LANES = 128
MXU_COLS = 256
VMEM_LIMIT = 56 * 1024 * 1024
VMEM_LIMIT_EXPERTS = 62 * 1024 * 1024

TM_PROJ = 512
TM_MERGE = 1024
TQ_MEM = 512
TM_EXPERT = 512
BATCH_PER_STEP = 4
TS_SORT = 256
SEG_ALIGN = SUBLANES
LOCAL_ROWS = TS_SORT * TOP_K + N_EXPERTS * SEG_ALIGN
LOCAL_CHUNKS = LOCAL_ROWS // SEG_ALIGN


def _params(sem, vmem=VMEM_LIMIT):
    return pltpu.CompilerParams(dimension_semantics=sem, vmem_limit_bytes=vmem)


def _const_spec(shape):
    nd = len(shape)
    return pl.BlockSpec(shape, lambda *_: (0,) * nd, pipeline_mode=pl.Buffered(1))


def _rms(x, w):
    return x * lax.rsqrt(jnp.mean(x * x, axis=-1, keepdims=True) + EPS) * w


def _sigmoid(x):
    return 0.5 * jnp.tanh(0.5 * x) + 0.5


def _pack_halves(a):
    n = a.shape[-1] // 2
    lo = lax.bitcast_convert_type(a[:, :n], U32) >> 16
    hi = lax.bitcast_convert_type(a[:, n:], U32) & jnp.uint32(0xFFFF0000)
    return lo | hi


def _unpack_halves(w):
    lo = lax.bitcast_convert_type(w << 16, F32)
    hi = lax.bitcast_convert_type(w & jnp.uint32(0xFFFF0000), F32)
    return lo.astype(BF16), hi.astype(BF16)


def _dot(a, b):
    return jnp.dot(a, b, preferred_element_type=F32)


def _dot_nt(a, b):
    return lax.dot_general(a, b, (((1,), (1,)), ((), ())), preferred_element_type=F32)


def _dot_tn(a, b):
    return lax.dot_general(a, b, (((0,), (0,)), ((), ())), preferred_element_type=F32)


def _inproj_body(x_ref, nw_ref, w_ref, *o_refs):
    h = _rms(x_ref[...], nw_ref[...]).astype(BF16)
    off = 0
    for o_ref in o_refs:
        s = o_ref.shape[-1]
        o_ref[...] = _dot(h, w_ref[:, off:off + s]).astype(o_ref.dtype)
        off += s


def _inproj(x2, norm_w, w_in_bf):
    T, D = x2.shape
    tm = TM_PROJ
    return pl.pallas_call(
        _inproj_body,
        grid=(T // tm,),
        in_specs=[pl.BlockSpec((tm, D), lambda i: (i, 0)),
                  _const_spec((1, D)),
                  _const_spec(w_in_bf.shape)],
        out_specs=[pl.BlockSpec((tm, s), lambda i: (i, 0)) for s in IN_SIZES],
        out_shape=[jax.ShapeDtypeStruct((T, s), BF16) for s in IN_SIZES],
        compiler_params=_params(("parallel",)),
        name="inproj",
    )(x2, norm_w.reshape(1, D), w_in_bf)


def _head_rms(t, seg_ref, w):
    hi, lo = _split_bf16(t * t)
    ss = _dot(hi, seg_ref[...]) + _dot(lo, seg_ref[...])
    return t * lax.rsqrt(ss * (1.0 / SWA_HEAD_DIM) + EPS) * w


def _swa_body(sink_ref, q_ref, kc_ref, kp_ref, vc_ref, vp_ref, qw_ref, kw_ref, segq_ref, segk_ref, o_ref):
    n = pl.program_id(1)
    L = q_ref.shape[1]
    qi = lax.broadcasted_iota(I32, (L, 2 * L), 0)
    kj = lax.broadcasted_iota(I32, (L, 2 * L), 1)
    diff = L + qi - kj
    has_prev = jnp.minimum(n, 1) * L
    mask = (diff >= 0) & (diff < WINDOW) & (kj + has_prev >= L)
    for b in range(q_ref.shape[0]):
        o_ref[b] = _swa_block(sink_ref, q_ref[b], kc_ref[b], kp_ref[b], vc_ref[b], vp_ref[b],
                              qw_ref, kw_ref, segq_ref, segk_ref, mask).astype(o_ref.dtype)


def _swa_block(sink_ref, q, kc, kp, vc, vp, qw_ref, kw_ref, segq_ref, segk_ref, mask):
    d = SWA_HEAD_DIM
    q = _head_rms(q.astype(F32), segq_ref, qw_ref[...]) * (d ** -0.5)
    q = q.astype(BF16)
    k = jnp.concatenate([kp, kc], axis=0).astype(F32)
    k = _head_rms(k, segk_ref, kw_ref[...])
    v = jnp.concatenate([vp, vc], axis=0).astype(F32)
    upper = lax.broadcasted_iota(I32, k.shape, 1) >= d
    k_sw = pltpu.roll(k, d, 1)
    v_sw = pltpu.roll(v, d, 1)

    def placed(t, t_sw):
        return [[jnp.where(upper if half == 1 else jnp.logical_not(upper), t if h == half else t_sw,
                           0.0).astype(BF16) for half in range(2)] for h in range(SWA_KV_HEADS)]

    k_at = placed(k, k_sw)
    v_at = placed(v, v_sw)
    pairs = []
    for j in range(SWA_Q_HEADS // 2):
        h = (2 * j) // SWA_GROUP
        qb = q[:, 2 * j * d:2 * (j + 1) * d]
        acc = None
        for half in range(2):
            s = jnp.where(mask, _dot_nt(qb, k_at[h][half]), NEG_INF)
            sink = sink_ref[2 * j + half]
            m = jnp.maximum(jnp.max(s, axis=-1, keepdims=True), sink)
            p = jnp.exp(s - m)
            den = jnp.sum(p, axis=-1, keepdims=True) + jnp.exp(sink - m)
            o = _dot(p.astype(BF16), v_at[h][half]) * (1.0 / den)
            acc = o if acc is None else acc + o
        pairs.append(acc)
    return jnp.concatenate(pairs, axis=-1)


def _swa(sq, sk, sv, q_norm_w, k_norm_w, sinks, B, S):
    L = BAND_BLOCK
    nb = S // L
    d = SWA_HEAD_DIM
    bs = BATCH_PER_STEP if B % BATCH_PER_STEP == 0 else 1
    cur = lambda b, n: (b, n, 0)
    prev = lambda b, n: (b, jnp.maximum(n - 1, 0), 0)
    as3 = lambda t: t.reshape(B, S, t.shape[-1])
    group = jnp.arange(SWA_Q_W, dtype=I32) // d
    seg_q = (group[:, None] == group[None, :]).astype(BF16)
    seg_k = seg_q[:SWA_KV_W, :SWA_KV_W]
    return pl.pallas_call(
        _swa_body,
        grid=(B // bs, nb),
        in_specs=[pl.BlockSpec(memory_space=pltpu.SMEM),
                  pl.BlockSpec((bs, L, SWA_Q_W), cur),
                  pl.BlockSpec((bs, L, SWA_KV_W), cur),
                  pl.BlockSpec((bs, L, SWA_KV_W), prev),
                  pl.BlockSpec((bs, L, SWA_KV_W), cur),
                  pl.BlockSpec((bs, L, SWA_KV_W), prev),
                  _const_spec((1, SWA_Q_W)),
                  _const_spec((1, SWA_KV_W)),
                  _const_spec(seg_q.shape),
                  _const_spec(seg_k.shape)],
        out_specs=pl.BlockSpec((bs, L, SWA_Q_W), cur),
        out_shape=jax.ShapeDtypeStruct((B, S, SWA_Q_W), BF16),
        compiler_params=_params(("parallel", "parallel")),
        name="swa",
    )(sinks.astype(F32), as3(sq), as3(sk), as3(sk), as3(sv), as3(sv),
      jnp.tile(q_norm_w, SWA_Q_HEADS).reshape(1, -1), jnp.tile(k_norm_w, SWA_KV_HEADS).reshape(1, -1),
      seg_q, seg_k).reshape(B * S, SWA_Q_W)


def _ret_body(q_ref, k_ref, v_ref, g_ref, cos_ref, sin_ref, dm_ref, qd_ref, kd_ref, cd_ref, nw_ref,
              o_ref, st_ref):
    n = pl.program_id(1)

    @pl.when(n == 0)
    def _():
        st_ref[...] = jnp.zeros_like(st_ref)

    for b in range(q_ref.shape[0]):
        _ret_block(b, q_ref, k_ref, v_ref, g_ref, cos_ref, sin_ref, dm_ref, qd_ref, kd_ref, cd_ref, nw_ref,
                   o_ref, st_ref)


def _ret_block(b, q_ref, k_ref, v_ref, g_ref, cos_ref, sin_ref, dm_ref, qd_ref, kd_ref, cd_ref, nw_ref,
               o_ref, st_ref):
    cos = cos_ref[...]
    sin = sin_ref[...]
    dk, dv = RET_QK_DIM, RET_V_DIM
    outs = []
    for h in range(RET_HEADS):
        qh = q_ref[b, :, h * dk:(h + 1) * dk].astype(F32)
        kh = k_ref[b, :, h * dk:(h + 1) * dk].astype(F32)
        qr = qh * cos + pltpu.roll(qh, dk // 2, 1) * sin
        kr = (kh * cos + pltpu.roll(kh, dk // 2, 1) * sin) * (dk ** -0.5)
        vh = v_ref[b, :, h * dv:(h + 1) * dv]
        st = st_ref[b, h]
        inner = _dot_nt(qr.astype(BF16), kr.astype(BF16)) * dm_ref[h]
        o = _dot(inner.astype(BF16), vh) + _dot((qr * qd_ref[h]).astype(BF16), st.astype(BF16))
        st_ref[b, h] = st * cd_ref[h] + _dot_tn((kr * kd_ref[h]).astype(BF16), vh)
        mu = jnp.mean(o, axis=-1, keepdims=True)
        oc = o - mu
        var = jnp.mean(oc * oc, axis=-1, keepdims=True)
        y = oc * lax.rsqrt(var + EPS) * nw_ref[:, h * dv:(h + 1) * dv]
        g = g_ref[b, :, h * dv:(h + 1) * dv].astype(F32)
        outs.append(g * _sigmoid(g) * y)
    o_ref[b] = jnp.concatenate(outs, axis=-1).astype(o_ref.dtype)


def _retention(rq, rk, rv, rg, ret_norm_w, B, S):
    L = RET_CHUNK
    nc = S // L
    half = RET_QK_DIM // 2
    inv = 1.0 / (ROPE_BASE ** (jnp.arange(0, half, dtype=F32) / half))
    ang = jnp.arange(S, dtype=F32)[:, None] * inv[None, :]
    cos = jnp.cos(ang)
    sin = jnp.sin(ang)
    cos2 = jnp.concatenate([cos, cos], axis=-1)
    sin2 = jnp.concatenate([-sin, sin], axis=-1)
    log_g = jnp.log1p(-(2.0 ** (-5.0 - jnp.arange(RET_HEADS, dtype=F32))))
    pos = jnp.arange(L, dtype=F32)
    diff = pos[:, None] - pos[None, :]
    decay_mask = jnp.where(diff[None] >= 0,
                           jnp.exp(jnp.maximum(diff, 0.0)[None] * log_g[:, None, None]), 0.0)
    q_decay = jnp.exp((pos[None, :] + 1.0) * log_g[:, None])[..., None]
    k_decay = jnp.exp((L - 1.0 - pos[None, :]) * log_g[:, None])[..., None]
    chunk_decay = jnp.exp(L * log_g)
    qd = jnp.broadcast_to(q_decay, (RET_HEADS, L, RET_QK_DIM))
    kd = jnp.broadcast_to(k_decay, (RET_HEADS, L, RET_QK_DIM))
    bs = BATCH_PER_STEP if B % BATCH_PER_STEP == 0 else 1
    row = lambda b, n: (b, n, 0)
    as3 = lambda t: t.reshape(B, S, t.shape[-1])
    return pl.pallas_call(
        _ret_body,
        grid=(B // bs, nc),
        in_specs=[pl.BlockSpec((bs, L, RET_QK_W), row),
                  pl.BlockSpec((bs, L, RET_QK_W), row),
                  pl.BlockSpec((bs, L, RET_V_W), row),
                  pl.BlockSpec((bs, L, RET_V_W), row),
                  pl.BlockSpec((L, RET_QK_DIM), lambda b, n: (n, 0)),
                  pl.BlockSpec((L, RET_QK_DIM), lambda b, n: (n, 0)),
                  _const_spec((RET_HEADS, L, L)),
                  _const_spec((RET_HEADS, L, RET_QK_DIM)),
                  _const_spec((RET_HEADS, L, RET_QK_DIM)),
                  pl.BlockSpec(memory_space=pltpu.SMEM),
                  _const_spec((1, RET_V_W))],
        out_specs=pl.BlockSpec((bs, L, RET_V_W), row),
        out_shape=jax.ShapeDtypeStruct((B, S, RET_V_W), BF16),
        scratch_shapes=[pltpu.VMEM((bs, RET_HEADS, RET_QK_DIM, RET_V_DIM), F32)],
        compiler_params=_params(("parallel", "arbitrary")),
        name="retention",
    )(as3(rq), as3(rk), as3(rv), as3(rg), cos2, sin2, decay_mask, qd, kd, chunk_decay,
      ret_norm_w.reshape(1, -1)).reshape(B * S, RET_V_W)


def _memkv_body(m_ref, nw_ref, w_ref, kw_ref, k_ref, v_ref):
    m = _rms(m_ref[...], nw_ref[...]).astype(BF16)
    kv = _dot(m, w_ref[...])
    d = MEM_HEAD_DIM
    ks = [_rms(kv[:, h * d:(h + 1) * d], kw_ref[...]) for h in range(MEM_HEADS)]
    k_ref[...] = jnp.concatenate(ks, axis=-1).astype(k_ref.dtype)
    v_ref[...] = kv[:, MEM_W:].astype(v_ref.dtype)


def _memkv(mem2, mem_norm_w, w_kv_bf, k_norm_w, B, M):
    D = mem2.shape[-1]
    return pl.pallas_call(
        _memkv_body,
        grid=(B,),
        in_specs=[pl.BlockSpec((M, D), lambda b: (b, 0)),
                  _const_spec((1, D)),
                  _const_spec(w_kv_bf.shape),
                  _const_spec((1, MEM_HEAD_DIM))],
        out_specs=[pl.BlockSpec((M, MEM_W), lambda b: (b, 0))] * 2,
        out_shape=[jax.ShapeDtypeStruct((B * M, MEM_W), BF16)] * 2,
        compiler_params=_params(("parallel",)),
        name="memkv",
    )(mem2, mem_norm_w.reshape(1, D), w_kv_bf, k_norm_w.reshape(1, -1))


def _memattn_body(q_ref, k_ref, v_ref, qw_ref, o_ref):
    d = MEM_HEAD_DIM
    outs = []
    for h in range(MEM_HEADS):
        qh = _rms(q_ref[:, h * d:(h + 1) * d].astype(F32), qw_ref[...]).astype(BF16)
        s = _dot_nt(qh, k_ref[:, h * d:(h + 1) * d]) * (d ** -0.5)
        m = jnp.max(s, axis=-1, keepdims=True)
        p = jnp.exp(s - m)
        pr = p / jnp.sum(p, axis=-1, keepdims=True)
        outs.append(_dot(pr.astype(BF16), v_ref[:, h * d:(h + 1) * d]))
    o_ref[...] = jnp.concatenate(outs, axis=-1).astype(o_ref.dtype)


def _memattn(mq, mk, mv, q_norm_w, B, S, M):
    tq = min(TQ_MEM, S)
    nq = S // tq
    return pl.pallas_call(
        _memattn_body,
        grid=(B, nq),
        in_specs=[pl.BlockSpec((tq, MEM_W), lambda b, i: (b * nq + i, 0)),
                  pl.BlockSpec((M, MEM_W), lambda b, i: (b, 0)),
                  pl.BlockSpec((M, MEM_W), lambda b, i: (b, 0)),
                  _const_spec((1, MEM_HEAD_DIM))],
        out_specs=pl.BlockSpec((tq, MEM_W), lambda b, i: (b * nq + i, 0)),
        out_shape=jax.ShapeDtypeStruct((B * S, MEM_W), BF16),
        compiler_params=_params(("parallel", "parallel")),
        name="memattn",
    )(mq, mk, mv, q_norm_w.reshape(1, -1))


def _mem_block(q, k_ref, v_ref, b, qw_ref):
    d = MEM_HEAD_DIM
    outs = []
    for h in range(MEM_HEADS):
        qh = _rms(q[:, h * d:(h + 1) * d].astype(F32), qw_ref[...]).astype(BF16)
        s = _dot_nt(qh, k_ref[b, :, h * d:(h + 1) * d]) * (d ** -0.5)
        m = jnp.max(s, axis=-1, keepdims=True)
        p = jnp.exp(s - m)
        o = _dot(p.astype(BF16), v_ref[b, :, h * d:(h + 1) * d])
        outs.append(o * (1.0 / jnp.sum(p, axis=-1, keepdims=True)))
    return jnp.concatenate(outs, axis=-1)


def _mixers_body(sink_ref, cd_ref, x_ref, nw_ref, w_ref, sqw_ref, skw_ref, segq_ref, segk_ref,
                 cos_ref, sin_ref, dm_ref, qd_ref, kd_ref, rnw_ref, mk_ref, mv_ref, mqw_ref,
                 osw_ref, ort_ref, omm_ref, zg_ref, st_ref, kprev_ref, vprev_ref):
    n = pl.program_id(1)
    bs, L, D = x_ref.shape

    @pl.when(n == 0)
    def _():
        st_ref[...] = jnp.zeros_like(st_ref)
        kprev_ref[...] = jnp.zeros_like(kprev_ref)
        vprev_ref[...] = jnp.zeros_like(vprev_ref)

    h = _rms(x_ref[...].reshape(bs * L, D), nw_ref[...]).astype(BF16)
    offs = [sum(IN_SIZES[:i]) for i in range(len(IN_SIZES) + 1)]

    def proj(i):
        z = _dot(h, w_ref[:, offs[i]:offs[i + 1]]).astype(BF16)
        return z.reshape(bs, L, IN_SIZES[i])

    sq, sk, sv = proj(0), proj(1), proj(2)
    qi = lax.broadcasted_iota(I32, (L, 2 * L), 0)
    kj = lax.broadcasted_iota(I32, (L, 2 * L), 1)
    diff = L + qi - kj
    has_prev = jnp.minimum(n, 1) * L
    mask = (diff >= 0) & (diff < WINDOW) & (kj + has_prev >= L)
    for b in range(bs):
        osw_ref[b] = _swa_block(sink_ref, sq[b], sk[b], kprev_ref[b], sv[b], vprev_ref[b],
                                sqw_ref, skw_ref, segq_ref, segk_ref, mask).astype(osw_ref.dtype)
    kprev_ref[...] = sk
    vprev_ref[...] = sv
    rq, rk, rv, rg = proj(3), proj(4), proj(5), proj(6)
    for b in range(bs):
        _ret_block(b, rq, rk, rv, rg, cos_ref, sin_ref, dm_ref, qd_ref, kd_ref, cd_ref, rnw_ref, ort_ref, st_ref)
    mq = proj(7)
    for b in range(bs):
        omm_ref[b] = _mem_block(mq[b], mk_ref, mv_ref, b, mqw_ref).astype(omm_ref.dtype)
    zg_ref[...] = proj(8)


def _mixers(x, mix_norm_w, w_in_bf, swa_q_norm_w, swa_k_norm_w, sinks, ret_norm_w, mk, mv, mem_q_norm_w):
    B, S, D = x.shape
    M = mk.shape[0] // B
    L = BAND_BLOCK
    nb = S // L
    bs = BATCH_PER_STEP if B % BATCH_PER_STEP == 0 else 1
    d = SWA_HEAD_DIM
    group = jnp.arange(SWA_Q_W, dtype=I32) // d
    seg_q = (group[:, None] == group[None, :]).astype(BF16)
    seg_k = seg_q[:SWA_KV_W, :SWA_KV_W]
    half = RET_QK_DIM // 2
    inv = 1.0 / (ROPE_BASE ** (jnp.arange(0, half, dtype=F32) / half))
    ang = jnp.arange(S, dtype=F32)[:, None] * inv[None, :]
    cos = jnp.cos(ang)
    sin = jnp.sin(ang)
    cos2 = jnp.concatenate([cos, cos], axis=-1)
    sin2 = jnp.concatenate([-sin, sin], axis=-1)
    log_g = jnp.log1p(-(2.0 ** (-5.0 - jnp.arange(RET_HEADS, dtype=F32))))
    pos = jnp.arange(L, dtype=F32)
    dpos = pos[:, None] - pos[None, :]
    decay_mask = jnp.where(dpos[None] >= 0,
                           jnp.exp(jnp.maximum(dpos, 0.0)[None] * log_g[:, None, None]), 0.0)
    q_decay = jnp.exp((pos[None, :] + 1.0) * log_g[:, None])[..., None]
    k_decay = jnp.exp((L - 1.0 - pos[None, :]) * log_g[:, None])[..., None]
    chunk_decay = jnp.exp(L * log_g)
    qd = jnp.broadcast_to(q_decay, (RET_HEADS, L, RET_QK_DIM))
    kd = jnp.broadcast_to(k_decay, (RET_HEADS, L, RET_QK_DIM))
    blk = lambda w: pl.BlockSpec((bs, L, w), lambda b, n: (b, n, 0))
    smem = pl.BlockSpec(memory_space=pltpu.SMEM)
    outs = pl.pallas_call(
        _mixers_body,
        grid=(B // bs, nb),
        in_specs=[smem, smem,
                  blk(D), _const_spec((1, D)), _const_spec(w_in_bf.shape),
                  _const_spec((1, SWA_Q_W)), _const_spec((1, SWA_KV_W)),
                  _const_spec(seg_q.shape), _const_spec(seg_k.shape),
                  pl.BlockSpec((L, RET_QK_DIM), lambda b, n: (n, 0)),
                  pl.BlockSpec((L, RET_QK_DIM), lambda b, n: (n, 0)),
                  _const_spec((RET_HEADS, L, L)),
                  _const_spec((RET_HEADS, L, RET_QK_DIM)),
                  _const_spec((RET_HEADS, L, RET_QK_DIM)),
                  _const_spec((1, RET_V_W)),
                  pl.BlockSpec((bs, M, MEM_W), lambda b, n: (b, 0, 0)),
                  pl.BlockSpec((bs, M, MEM_W), lambda b, n: (b, 0, 0)),
                  _const_spec((1, MEM_HEAD_DIM))],
        out_specs=[blk(SWA_Q_W), blk(RET_V_W), blk(MEM_W), blk(N_BRANCH * D)],
        out_shape=[jax.ShapeDtypeStruct((B, S, w), BF16) for w in (SWA_Q_W, RET_V_W, MEM_W, N_BRANCH * D)],
        scratch_shapes=[pltpu.VMEM((bs, RET_HEADS, RET_QK_DIM, RET_V_DIM), F32),
                        pltpu.VMEM((bs, L, SWA_KV_W), BF16),
                        pltpu.VMEM((bs, L, SWA_KV_W), BF16)],
        compiler_params=_params(("parallel", "arbitrary"), vmem=VMEM_LIMIT_EXPERTS),
        name="mixers",
    )(sinks.astype(F32), chunk_decay, x, mix_norm_w.reshape(1, D), w_in_bf,
      jnp.tile(swa_q_norm_w, SWA_Q_HEADS).reshape(1, -1), jnp.tile(swa_k_norm_w, SWA_KV_HEADS).reshape(1, -1),
      seg_q, seg_k, cos2, sin2, decay_mask, qd, kd, ret_norm_w.reshape(1, -1),
      mk.reshape(B, M, MEM_W), mv.reshape(B, M, MEM_W), mem_q_norm_w.reshape(1, -1))
    return [o.reshape(B * S, o.shape[-1]) for o in outs]


def _split_bf16(a):
    hi = a.astype(BF16)
    lo = (a - hi.astype(F32)).astype(BF16)
    return hi, lo


def _merge_body(x_ref, osw_ref, ort_ref, omm_ref, zg_ref, wa_ref, wr_ref, wm_ref, wo_ref, nw_ref,
                wrt_ref, brt_ref, x1_ref, h2_ref, idx_ref, gate_ref, cnt_ref):
    D = D_MODEL
    sg = lambda j: _sigmoid(zg_ref[:, j * D:(j + 1) * D].astype(F32))
    merged = (sg(0) * _dot(osw_ref[...], wa_ref[...])
              + sg(1) * _dot(ort_ref[...], wr_ref[...])
              + sg(2) * _dot(omm_ref[...], wm_ref[...]))
    x1 = x_ref[...] + _dot(merged.astype(BF16), wo_ref[...])
    x1_ref[...] = x1
    h2 = _rms(x1, nw_ref[...])
    h2_ref[...] = h2.astype(h2_ref.dtype)
    h_hi, h_lo = _split_bf16(h2)
    w_hi, w_lo = _split_bf16(wrt_ref[...])
    logits = _dot(h_hi, w_hi) + _dot(h_lo, w_hi) + _dot(h_hi, w_lo) + brt_ref[...]
    lane = lax.broadcasted_iota(I32, logits.shape, 1)
    vals, idxs = [], []
    l = logits
    for _ in range(TOP_K):
        m = jnp.max(l, axis=-1, keepdims=True)
        i = jnp.min(jnp.where(l == m, lane, LANES), axis=-1, keepdims=True)
        vals.append(m)
        idxs.append(i)
        l = jnp.where(lane == i, -jnp.inf, l)
    es = [jnp.exp(v - vals[0]) for v in vals]
    den = es[0] + es[1] + es[2] + es[3]
    idx_ref[...] = jnp.concatenate(idxs, axis=-1)
    gate_ref[...] = jnp.concatenate([e / den for e in es], axis=-1)
    chosen = sum((lane == i).astype(F32) for i in idxs)
    ts = min(TS_SORT, chosen.shape[0])
    per_tile = [jnp.sum(chosen[j * ts:(j + 1) * ts], axis=0, keepdims=True) for j in range(chosen.shape[0] // ts)]
    per_tile.append(jnp.zeros((SUBLANES - len(per_tile), LANES), F32))
    cnt_ref[...] = jnp.concatenate(per_tile, axis=0)


def _merge(x2, o_swa, o_ret, o_mem, zg, wa, wr, wm, wo, ffn_norm_w, w_router_pad, b_router_pad):
    T, D = x2.shape
    tm = min(TM_MERGE, T)
    row = lambda i: (i, 0)
    return pl.pallas_call(
        _merge_body,
        grid=(T // tm,),
        in_specs=[pl.BlockSpec((tm, D), row),
                  pl.BlockSpec((tm, SWA_Q_W), row),
                  pl.BlockSpec((tm, RET_V_W), row),
                  pl.BlockSpec((tm, MEM_W), row),
                  pl.BlockSpec((tm, N_BRANCH * D), row),
                  _const_spec(wa.shape), _const_spec(wr.shape), _const_spec(wm.shape), _const_spec(wo.shape),
                  _const_spec((1, D)),
                  _const_spec(w_router_pad.shape), _const_spec((1, LANES))],
        out_specs=[pl.BlockSpec((tm, D), row),
                   pl.BlockSpec((tm, D), row),
                   pl.BlockSpec((tm, TOP_K), row),
                   pl.BlockSpec((tm, TOP_K), row),
                   pl.BlockSpec((None, SUBLANES, LANES), lambda i: (i, 0, 0))],
        out_shape=[jax.ShapeDtypeStruct((T, D), F32),
                   jax.ShapeDtypeStruct((T, D), BF16),
                   jax.ShapeDtypeStruct((T, TOP_K), I32),
                   jax.ShapeDtypeStruct((T, TOP_K), F32),
                   jax.ShapeDtypeStruct((T // tm, SUBLANES, LANES), F32)],
        compiler_params=_params(("parallel",)),
        name="merge",
    )(x2, o_swa, o_ret, o_mem, zg, wa, wr, wm, wo, ffn_norm_w.reshape(1, D), w_router_pad, b_router_pad)


def _merge_rows(x, osw, ort, omm, gate_logits, wa_ref, wr_ref, wm_ref, wo_ref, nw_ref, wrt_ref, brt_ref):
    sg = lambda j: _sigmoid(gate_logits(j).astype(F32))
    merged = (sg(0) * _dot(osw, wa_ref[...]) + sg(1) * _dot(ort, wr_ref[...]) + sg(2) * _dot(omm, wm_ref[...]))
    x1 = x + _dot(merged.astype(BF16), wo_ref[...])
    h2 = _rms(x1, nw_ref[...])
    h_hi, h_lo = _split_bf16(h2)
    w_hi, w_lo = _split_bf16(wrt_ref[...])
    logits = _dot(h_hi, w_hi) + _dot(h_lo, w_hi) + _dot(h_hi, w_lo) + brt_ref[...]
    lane = lax.broadcasted_iota(I32, logits.shape, 1)
    vals, idxs = [], []
    l = logits
    for _ in range(TOP_K):
        m = jnp.max(l, axis=-1, keepdims=True)
        i = jnp.min(jnp.where(l == m, lane, LANES), axis=-1, keepdims=True)
        vals.append(m)
        idxs.append(i)
        l = jnp.where(lane == i, -jnp.inf, l)
    es = [jnp.exp(v - vals[0]) for v in vals]
    den = es[0] + es[1] + es[2] + es[3]
    gates = jnp.concatenate([e / den for e in es], axis=-1)
    chosen = sum((lane == i).astype(F32) for i in idxs)
    return x1, h2, jnp.concatenate(idxs, axis=-1), gates, chosen


def _mixmerge_body(sink_ref, cd_ref, x_ref, nw_ref, w_ref, sqw_ref, skw_ref, segq_ref, segk_ref,
                   cos_ref, sin_ref, dm_ref, qd_ref, kd_ref, rnw_ref, mk_ref, mv_ref, mqw_ref,
                   wa_ref, wr_ref, wm_ref, wo_ref, fnw_ref, wrt_ref, brt_ref,
                   x1_ref, h2_ref, idx_ref, gate_ref, cnt_ref,
                   st_ref, kprev_ref, vprev_ref, osw_s, ort_s, omm_s, xlag_s, *, nb, nsteps):
    t = pl.program_id(0)
    n = jnp.minimum(t, nsteps - 1) % nb
    slot = t % 2
    bs, L, D = x_ref.shape

    @pl.when(t == 0)
    def _():
        for s_ref in (osw_s, ort_s, omm_s, xlag_s):
            s_ref[...] = jnp.zeros_like(s_ref)

    @pl.when(n == 0)
    def _():
        st_ref[...] = jnp.zeros_like(st_ref)
        kprev_ref[...] = jnp.zeros_like(kprev_ref)
        vprev_ref[...] = jnp.zeros_like(vprev_ref)

    offs = [sum(IN_SIZES[:i]) for i in range(len(IN_SIZES) + 1)]

    rows = bs * L
    prev = 1 - slot
    x_prev = xlag_s[...].reshape(rows, D)
    h_prev = _rms(x_prev, nw_ref[...]).astype(BF16)
    gate_logits = lambda j: _dot(h_prev, w_ref[:, offs[8] + j * D:offs[8] + (j + 1) * D]).astype(BF16)
    x1, h2, idx, gates, chosen = _merge_rows(
        x_prev, osw_s[prev].reshape(rows, -1), ort_s[prev].reshape(rows, -1), omm_s[prev].reshape(rows, -1),
        gate_logits, wa_ref, wr_ref, wm_ref, wo_ref, fnw_ref, wrt_ref, brt_ref)
    x1_ref[...] = x1.reshape(bs, L, D)
    h2_ref[...] = h2.astype(h2_ref.dtype).reshape(bs, L, D)
    idx_ref[...] = idx.reshape(bs, L, TOP_K)
    gate_ref[...] = gates.reshape(bs, L, TOP_K)
    counts = [jnp.sum(chosen[b * L:(b + 1) * L], axis=0, keepdims=True) for b in range(bs)]
    counts.append(jnp.zeros((SUBLANES - bs, LANES), F32))
    cnt_ref[...] = jnp.concatenate(counts, axis=0)

    h = _rms(x_ref[...].reshape(bs * L, D), nw_ref[...]).astype(BF16)

    def proj(i):
        return _dot(h, w_ref[:, offs[i]:offs[i + 1]]).astype(BF16).reshape(bs, L, IN_SIZES[i])

    sq, sk, sv = proj(0), proj(1), proj(2)
    qi = lax.broadcasted_iota(I32, (L, 2 * L), 0)
    kj = lax.broadcasted_iota(I32, (L, 2 * L), 1)
    diff = L + qi - kj
    has_prev = jnp.minimum(n, 1) * L
    mask = (diff >= 0) & (diff < WINDOW) & (kj + has_prev >= L)
    for b in range(bs):
        osw_s[slot, b] = _swa_block(sink_ref, sq[b], sk[b], kprev_ref[b], sv[b], vprev_ref[b],
                                    sqw_ref, skw_ref, segq_ref, segk_ref, mask).astype(osw_s.dtype)
    kprev_ref[...] = sk
    vprev_ref[...] = sv
    rq, rk, rv, rg = proj(3), proj(4), proj(5), proj(6)
    for b in range(bs):
        _ret_block(b, rq, rk, rv, rg, cos_ref, sin_ref, dm_ref, qd_ref, kd_ref, cd_ref, rnw_ref,
                   ort_s.at[slot], st_ref)
    mq = proj(7)
    for b in range(bs):
        omm_s[slot, b] = _mem_block(mq[b], mk_ref, mv_ref, b, mqw_ref).astype(omm_s.dtype)
    xlag_s[...] = x_ref[...]


def _mixmerge(x, mix_norm_w, w_in_bf, swa_q_norm_w, swa_k_norm_w, sinks, ret_norm_w, mk, mv, mem_q_norm_w,
              wa, wr, wm, wo, ffn_norm_w, w_router_pad, b_router_pad):
    B, S, D = x.shape
    M = mk.shape[0] // B
    L = BAND_BLOCK
    nb = S // L
    bs = BATCH_PER_STEP
    nsteps = (B // bs) * nb
    d = SWA_HEAD_DIM
    group = jnp.arange(SWA_Q_W, dtype=I32) // d
    seg_q = (group[:, None] == group[None, :]).astype(BF16)
    seg_k = seg_q[:SWA_KV_W, :SWA_KV_W]
    half = RET_QK_DIM // 2
    inv = 1.0 / (ROPE_BASE ** (jnp.arange(0, half, dtype=F32) / half))
    ang = jnp.arange(S, dtype=F32)[:, None] * inv[None, :]
    cos = jnp.cos(ang)
    sin = jnp.sin(ang)
    cos2 = jnp.concatenate([cos, cos], axis=-1)
    sin2 = jnp.concatenate([-sin, sin], axis=-1)
    log_g = jnp.log1p(-(2.0 ** (-5.0 - jnp.arange(RET_HEADS, dtype=F32))))
    pos = jnp.arange(L, dtype=F32)
    dpos = pos[:, None] - pos[None, :]
    decay_mask = jnp.where(dpos[None] >= 0,
                           jnp.exp(jnp.maximum(dpos, 0.0)[None] * log_g[:, None, None]), 0.0)
    q_decay = jnp.exp((pos[None, :] + 1.0) * log_g[:, None])[..., None]
    k_decay = jnp.exp((L - 1.0 - pos[None, :]) * log_g[:, None])[..., None]
    chunk_decay = jnp.exp(L * log_g)
    qd = jnp.broadcast_to(q_decay, (RET_HEADS, L, RET_QK_DIM))
    kd = jnp.broadcast_to(k_decay, (RET_HEADS, L, RET_QK_DIM))

    cur = lambda t: jnp.minimum(t, nsteps - 1)
    lag = lambda t: jnp.maximum(t - 1, 0)
    blk = lambda w, item: pl.BlockSpec((bs, L, w), lambda t: (item(t) // nb, item(t) % nb, 0))
    smem = pl.BlockSpec(memory_space=pltpu.SMEM)
    x1, h2, idx, gates, cnt = pl.pallas_call(
        functools.partial(_mixmerge_body, nb=nb, nsteps=nsteps),
        grid=(nsteps + 1,),
        in_specs=[smem, smem,
                  blk(D, cur), _const_spec((1, D)), _const_spec(w_in_bf.shape),
                  _const_spec((1, SWA_Q_W)), _const_spec((1, SWA_KV_W)),
                  _const_spec(seg_q.shape), _const_spec(seg_k.shape),
                  pl.BlockSpec((L, RET_QK_DIM), lambda t: (cur(t) % nb, 0)),
                  pl.BlockSpec((L, RET_QK_DIM), lambda t: (cur(t) % nb, 0)),
                  _const_spec((RET_HEADS, L, L)),
                  _const_spec((RET_HEADS, L, RET_QK_DIM)),
                  _const_spec((RET_HEADS, L, RET_QK_DIM)),
                  _const_spec((1, RET_V_W)),
                  pl.BlockSpec((bs, M, MEM_W), lambda t: (cur(t) // nb, 0, 0), pipeline_mode=pl.Buffered(1)),
                  pl.BlockSpec((bs, M, MEM_W), lambda t: (cur(t) // nb, 0, 0), pipeline_mode=pl.Buffered(1)),
                  _const_spec((1, MEM_HEAD_DIM)),
                  _const_spec(wa.shape), _const_spec(wr.shape), _const_spec(wm.shape), _const_spec(wo.shape),
                  _const_spec((1, D)), _const_spec(w_router_pad.shape), _const_spec((1, LANES))],
        out_specs=[blk(D, lag), blk(D, lag), blk(TOP_K, lag), blk(TOP_K, lag),
                   pl.BlockSpec((None, None, SUBLANES, LANES), lambda t: (lag(t) // nb, lag(t) % nb, 0, 0))],
        out_shape=[jax.ShapeDtypeStruct((B, S, D), F32),
                   jax.ShapeDtypeStruct((B, S, D), BF16),
                   jax.ShapeDtypeStruct((B, S, TOP_K), I32),
                   jax.ShapeDtypeStruct((B, S, TOP_K), F32),
                   jax.ShapeDtypeStruct((B // bs, nb, SUBLANES, LANES), F32)],
        scratch_shapes=[pltpu.VMEM((bs, RET_HEADS, RET_QK_DIM, RET_V_DIM), F32),
                        pltpu.VMEM((bs, L, SWA_KV_W), BF16),
                        pltpu.VMEM((bs, L, SWA_KV_W), BF16),
                        pltpu.VMEM((2, bs, L, SWA_Q_W), BF16),
                        pltpu.VMEM((2, bs, L, RET_V_W), BF16),
                        pltpu.VMEM((2, bs, L, MEM_W), BF16),
                        pltpu.VMEM((bs, L, D), F32)],
        compiler_params=_params(("arbitrary",), vmem=VMEM_LIMIT_EXPERTS),
        name="mixmerge",
    )(sinks.astype(F32), chunk_decay, x, mix_norm_w.reshape(1, D), w_in_bf,
      jnp.tile(swa_q_norm_w, SWA_Q_HEADS).reshape(1, -1), jnp.tile(swa_k_norm_w, SWA_KV_HEADS).reshape(1, -1),
      seg_q, seg_k, cos2, sin2, decay_mask, qd, kd, ret_norm_w.reshape(1, -1),
      mk.reshape(B, M, MEM_W), mv.reshape(B, M, MEM_W), mem_q_norm_w.reshape(1, -1),
      wa, wr, wm, wo, ffn_norm_w.reshape(1, D), w_router_pad, b_router_pad)
    T = B * S
    cnt_blocks = cnt[:, :, :bs, :].transpose(0, 2, 1, 3).reshape(B, nb, LANES)
    return x1.reshape(T, D), h2.reshape(T, D), idx.reshape(T, TOP_K), gates.reshape(T, TOP_K), cnt_blocks


def _chunk_copy(src_ref, src_chunk, dst_ref, dst_chunk, sem):
    def rows(c):
        start = c * SEG_ALIGN
        return pl.ds(start if isinstance(c, int) else pl.multiple_of(start, SEG_ALIGN), SEG_ALIGN)

    return pltpu.make_async_copy(src_ref.at[rows(src_chunk), :], dst_ref.at[rows(dst_chunk), :], sem)


def _dispatch_body(fill_start_ref, fill_n_ref, gch_ref, h2_ref, idxt_ref, loff_ref,
                   xs_ref, rowt_ref, buf_ref, zero_ref, sems, zsem):
    s = pl.program_id(0)
    last = pl.num_programs(0) - 1
    slot = s % 2
    ts = h2_ref.shape[0]
    nchunks = gch_ref.shape[-1]
    local_rows = nchunks * SEG_ALIGN

    def wait_slot(sl):
        for c in range(nchunks):
            _chunk_copy(buf_ref.at[sl], c, xs_ref, c, sems.at[sl]).wait()

    @pl.when(s == 0)
    def _():
        zero_ref[...] = jnp.zeros_like(zero_ref)

        def per_expert(e, total):
            def fill(r, _):
                _chunk_copy(zero_ref, 0, xs_ref, fill_start_ref[e] + r, zsem).start()
                return 0

            lax.fori_loop(0, fill_n_ref[e], fill, 0)
            return total + fill_n_ref[e]

        total = lax.fori_loop(0, N_EXPERTS, per_expert, 0)

        def drain(r, _):
            _chunk_copy(zero_ref, 0, xs_ref, 0, zsem).wait()
            return 0

        lax.fori_loop(0, total, drain, 0)

    @pl.when(s >= 2)
    def _():
        wait_slot(slot)

    idxt = idxt_ref[...]
    sub = lax.broadcasted_iota(I32, (LANES, ts), 0)
    ohs = [(sub == idxt[k:k + 1, :]).astype(F32) for k in range(TOP_K)]
    m = ohs[0] + ohs[1] + ohs[2] + ohs[3]
    tr = lax.broadcasted_iota(I32, (ts, ts), 0)
    tc = lax.broadcasted_iota(I32, (ts, ts), 1)
    earlier = jnp.where(tr < tc, 1.0, 0.0).astype(BF16)
    before = _dot(m.astype(BF16), earlier) + loff_ref[...]
    rows = [jnp.sum(oh * before, axis=0, keepdims=True).astype(I32) for oh in ohs]
    rowt_ref[...] = jnp.concatenate(rows, axis=0)
    ri = lax.broadcasted_iota(I32, (local_rows, ts), 0)
    sel = (ri == rows[0]) | (ri == rows[1]) | (ri == rows[2]) | (ri == rows[3])
    buf_ref[slot] = _pack_halves(_dot(jnp.where(sel, 1.0, 0.0).astype(BF16), h2_ref[...]))
    for c in range(nchunks):
        _chunk_copy(buf_ref.at[slot], c, xs_ref, gch_ref[0, 0, c], sems.at[slot]).start()

    @pl.when(s == last)
    def _():
        wait_slot(slot)

        @pl.when(s >= 1)
        def _():
            wait_slot(1 - slot)


def _dispatch(h2, idxt, loff_col, gchunk, fill_start, fill_n, n_rows, ts):
    T, D = h2.shape
    nt = T // ts
    nchunks = gchunk.shape[-1]
    return pl.pallas_call(
        _dispatch_body,
        grid_spec=pltpu.PrefetchScalarGridSpec(
            num_scalar_prefetch=2,
            grid=(nt,),
            in_specs=[pl.BlockSpec((1, 1, nchunks), lambda s, *_: (s, 0, 0), memory_space=pltpu.SMEM),
                      pl.BlockSpec((ts, D), lambda s, *_: (s, 0)),
                      pl.BlockSpec((TOP_K, ts), lambda s, *_: (0, s)),
                      pl.BlockSpec((None, LANES, 1), lambda s, *_: (s, 0, 0))],
            out_specs=[pl.BlockSpec(memory_space=pl.ANY),
                       pl.BlockSpec((TOP_K, ts), lambda s, *_: (0, s))],
            scratch_shapes=[pltpu.VMEM((2, nchunks * SEG_ALIGN, D // 2), U32),
                            pltpu.VMEM((SEG_ALIGN, D // 2), U32),
                            pltpu.SemaphoreType.DMA((2,)),
                            pltpu.SemaphoreType.DMA(())]),
        out_shape=[jax.ShapeDtypeStruct((n_rows, D // 2), U32),
                   jax.ShapeDtypeStruct((TOP_K, T), I32)],
        compiler_params=_params(("arbitrary",)),
        name="dispatch",
    )(fill_start, fill_n, gchunk.reshape(nt, 1, nchunks), h2, idxt, loff_col)


def _expert_body(te_ref, nu_ref, x_ref, wu_ref, bu_ref, wd_ref, bd_ref, o_ref, wu_bf, wd_bf):
    i = pl.program_id(0)
    active = i < nu_ref[0]
    new_expert = jnp.logical_or(i == 0, te_ref[i] != te_ref[jnp.maximum(i - 1, 0)])
    half = MXU_COLS // 2

    @pl.when(jnp.logical_and(active, new_expert))
    def _():
        r = lax.broadcasted_iota(I32, (MXU_COLS, MXU_COLS), 0)
        c = lax.broadcasted_iota(I32, (MXU_COLS, MXU_COLS), 1)
        perm = jnp.where(r == jnp.where(c < half, 2 * c, 2 * (c - half) + 1), 1.0, 0.0).astype(BF16)
        for b in range(wu_ref.shape[-1] // MXU_COLS):
            cols = slice(b * MXU_COLS, (b + 1) * MXU_COLS)
            wu_bf[:, cols] = _dot(wu_ref[:, cols].astype(BF16), perm).astype(BF16)
        wd_bf[...] = wd_ref[...].astype(BF16)

    @pl.when(active)
    def _():
        x = jnp.concatenate(_unpack_halves(x_ref[...]), axis=-1)
        hid = _dot(x, wu_bf[...]) + bu_ref[...]
        acts = []
        for b in range(hid.shape[-1] // MXU_COLS):
            x_glu = jnp.minimum(hid[:, b * MXU_COLS:b * MXU_COLS + half], SWIGLU_LIMIT)
            x_lin = jnp.clip(hid[:, b * MXU_COLS + half:(b + 1) * MXU_COLS], -SWIGLU_LIMIT, SWIGLU_LIMIT)
            acts.append(x_glu * _sigmoid(SWIGLU_ALPHA * x_glu) * (x_lin + 1.0))
        act = jnp.concatenate(acts, axis=-1).astype(BF16)
        out = _dot(act, wd_bf[...]) + bd_ref[...]
        o_ref[...] = _pack_halves(out.astype(BF16).astype(F32))


def _experts(xs, tile_expert, n_used, wu, bu, wd, bd, n_tiles):
    tm = TM_EXPERT
    D = wu.shape[1]
    F2 = wu.shape[-1]
    rows = lambda i, te, nu: (jnp.minimum(i, nu[0] - 1), 0)
    wsel = lambda i, te, nu: (te[i], 0, 0)
    return pl.pallas_call(
        _expert_body,
        grid_spec=pltpu.PrefetchScalarGridSpec(
            num_scalar_prefetch=2,
            grid=(n_tiles,),
            in_specs=[pl.BlockSpec((tm, D // 2), rows),
                      pl.BlockSpec((None, D, F2), wsel),
                      pl.BlockSpec((None, 1, F2), wsel),
                      pl.BlockSpec((None, F2 // 2, D), wsel),
                      pl.BlockSpec((None, 1, D), wsel)],
            out_specs=pl.BlockSpec((tm, D // 2), rows),
            scratch_shapes=[pltpu.VMEM((D, F2), BF16), pltpu.VMEM((F2 // 2, D), BF16)]),
        out_shape=jax.ShapeDtypeStruct((n_tiles * tm, D // 2), U32),
        compiler_params=_params(("arbitrary",), vmem=VMEM_LIMIT_EXPERTS),
        name="experts",
    )(tile_expert, n_used, xs, wu, bu, wd, bd)


def _combine_body(gch_ref, gch_next_ref, ys_ref, x1_ref, row_ref, gate_ref, o_ref, buf_ref, sems):
    s = pl.program_id(0)
    slot = s % 2
    nchunks = gch_ref.shape[-1]

    def gather(g_ref, sl):
        for c in range(nchunks):
            _chunk_copy(ys_ref, g_ref[0, 0, c], buf_ref.at[sl], c, sems.at[sl]).start()

    @pl.when(s == 0)
    def _():
        gather(gch_ref, 0)

    @pl.when(s < pl.num_programs(0) - 1)
    def _():
        gather(gch_next_ref, 1 - slot)

    for c in range(nchunks):
        _chunk_copy(ys_ref, 0, buf_ref.at[slot], c, sems.at[slot]).wait()
    rows = row_ref[...]
    g = gate_ref[...]
    ci = lax.broadcasted_iota(I32, (rows.shape[0], nchunks * SEG_ALIGN), 1)
    w = sum(jnp.where(ci == rows[:, k:k + 1], g[:, k:k + 1], 0.0) for k in range(TOP_K))
    w = w.astype(BF16)
    half = x1_ref.shape[-1] // 2
    lo, hi = _unpack_halves(buf_ref[slot])
    o_ref[:, :half] = x1_ref[:, :half] + _dot(w, lo)
    o_ref[:, half:] = x1_ref[:, half:] + _dot(w, hi)


def _combine(ys, gchunk, x1, rows, gates, ts):
    T, D = x1.shape
    nt = T // ts
    nchunks = gchunk.shape[-1]
    return pl.pallas_call(
        _combine_body,
        grid=(nt,),
        in_specs=[pl.BlockSpec((1, 1, nchunks), lambda s: (s, 0, 0), memory_space=pltpu.SMEM),
                  pl.BlockSpec((1, 1, nchunks), lambda s: (jnp.minimum(s + 1, nt - 1), 0, 0),
                               memory_space=pltpu.SMEM),
                  pl.BlockSpec(memory_space=pl.ANY),
                  pl.BlockSpec((ts, D), lambda s: (s, 0)),
                  pl.BlockSpec((ts, TOP_K), lambda s: (s, 0)),
                  pl.BlockSpec((ts, TOP_K), lambda s: (s, 0))],
        out_specs=pl.BlockSpec((ts, D), lambda s: (s, 0)),
        out_shape=jax.ShapeDtypeStruct((T, D), F32),
        scratch_shapes=[pltpu.VMEM((2, nchunks * SEG_ALIGN, D // 2), U32),
                        pltpu.SemaphoreType.DMA((2,))],
        compiler_params=_params(("arbitrary",)),
        name="combine",
    )(gchunk.reshape(nt, 1, nchunks), gchunk.reshape(nt, 1, nchunks), ys, x1, rows, gates)


def _excl_cumsum(a, axis):
    return jnp.cumsum(a, axis=axis) - a


def _routed_experts(h2, x1, idx, gates, cnt, w_up, b_up, w_down, b_down):
    T, D = x1.shape
    E = w_up.shape[0]
    ts = min(TS_SORT, T)
    nt = T // ts
    nchunks = (ts * TOP_K + E * SEG_ALIGN) // SEG_ALIGN
    max_rows = T * TOP_K + nt * E * (SEG_ALIGN - 1) + E * (TM_EXPERT - 1)
    n_tiles = -(-max_rows // TM_EXPERT)
    n_rows = n_tiles * TM_EXPERT
    dump_chunk = n_rows // SEG_ALIGN

    cnt = cnt.reshape(nt, ts // BAND_BLOCK, LANES).sum(axis=1)[:, :E].astype(I32)
    seg = -(-cnt // SEG_ALIGN)
    loff = _excl_cumsum(seg, 1)
    chunks_e = jnp.sum(seg, axis=0)
    tiles_e = -(-(chunks_e * SEG_ALIGN) // TM_EXPERT)
    tile_end = jnp.cumsum(tiles_e)
    base = (tile_end - tiles_e) * (TM_EXPERT // SEG_ALIGN)
    gstart = base[None, :] + _excl_cumsum(seg, 0)
    c = jnp.arange(nchunks, dtype=I32)
    owner = jnp.sum((loff + seg)[:, None, :] <= c[None, :, None], axis=-1)
    onehot = owner[:, :, None] == jnp.arange(E, dtype=I32)[None, None, :]
    shift = jnp.sum(jnp.where(onehot, (gstart - loff)[:, None, :], 0), axis=-1)
    used = owner < E
    gchunk = shift + c[None, :]
    dump = dump_chunk + (jnp.arange(nt, dtype=I32)[:, None] % 2) * nchunks + c[None, :]
    gchunk_out = jnp.where(used, gchunk, dump).astype(I32)
    gchunk_in = jnp.where(used, gchunk, 0).astype(I32)
    fill_start = (base + chunks_e).astype(I32)
    fill_n = (tiles_e * (TM_EXPERT // SEG_ALIGN) - chunks_e).astype(I32)
    n_used = tile_end[-1:].astype(I32)
    tile_ids = jnp.minimum(jnp.arange(n_tiles, dtype=I32), n_used[0] - 1)
    tile_expert = jnp.minimum(jnp.sum(tile_ids[:, None] >= tile_end[None, :], axis=-1), E - 1).astype(I32)
    loff_col = jnp.zeros((nt, LANES, 1), F32).at[:, :E, 0].set((loff * SEG_ALIGN).astype(F32))

    xs, rowt = _dispatch(h2, idx.T, loff_col, gchunk_out, fill_start, fill_n,
                         n_rows + 2 * nchunks * SEG_ALIGN, ts)
    half = MXU_COLS // 2
    bu = b_up.reshape(E, -1, half, 2).transpose(0, 1, 3, 2).reshape(E, 1, -1)
    ys = _experts(xs, tile_expert, n_used, w_up, bu, w_down, b_down[:, None, :], n_tiles)
    return _combine(ys, gchunk_in, x1, rowt.T, gates, ts)


def _layer(x, mem, mix_norm_w, mem_norm_w, w_in, swa_q_norm_w, swa_k_norm_w, swa_sinks, ret_norm_w,
           w_mem_kv, mem_q_norm_w, mem_k_norm_w, w_br_swa, w_br_ret, w_br_mem, w_out, ffn_norm_w,
           w_router, b_router, w_up, b_up, w_down, b_down):
    B, S, D = x.shape
    M = mem.shape[1]

    mk, mv = _memkv(mem.reshape(B * M, D), mem_norm_w, w_mem_kv.astype(BF16), mem_k_norm_w, B, M)
    E = w_router.shape[-1]
    w_router_pad = jnp.zeros((D, LANES), F32).at[:, :E].set(w_router)
    b_router_pad = jnp.full((1, LANES), NEG_INF, F32).at[0, :E].set(b_router)
    x1, h2, idx, gates, cnt = _mixmerge(
        x, mix_norm_w, w_in.astype(BF16), swa_q_norm_w, swa_k_norm_w, swa_sinks, ret_norm_w, mk, mv, mem_q_norm_w,
        w_br_swa.astype(BF16), w_br_ret.astype(BF16), w_br_mem.astype(BF16), w_out.astype(BF16),
        ffn_norm_w, w_router_pad, b_router_pad)

    out = _routed_experts(h2, x1, idx, gates, cnt, w_up, b_up, w_down, b_down)
    return out.reshape(B, S, D)


def kernel(x, mem, mix_norm_w, mem_norm_w, w_in, swa_q_norm_w, swa_k_norm_w, swa_sinks, ret_norm_w, w_mem_kv, mem_q_norm_w, mem_k_norm_w, w_br_swa, w_br_ret, w_br_mem, w_out, ffn_norm_w, w_router, b_router, w_up, b_up, w_down, b_down):
    args = (x, mem, mix_norm_w, mem_norm_w, w_in, swa_q_norm_w, swa_k_norm_w, swa_sinks, ret_norm_w,
            w_mem_kv, mem_q_norm_w, mem_k_norm_w, w_br_swa, w_br_ret, w_br_mem, w_out, ffn_norm_w,
            w_router, b_router, w_up, b_up, w_down, b_down)
    for l in range(w_in.shape[0]):
        x = _layer(x, mem, *[a[l] for a in args[2:]])
    return x
```

```python
import functools

import jax
import jax.numpy as jnp
from jax import lax
from jax.experimental import pallas as pl
from jax.experimental.pallas import tpu as pltpu

F32 = jnp.float32
BF16 = jnp.bfloat16
I32 = jnp.int32
U32 = jnp.uint32

D_MODEL = 1024
SWA_HEAD_DIM = 64
SWA_Q_HEADS = 8
SWA_KV_HEADS = 2
SWA_GROUP = SWA_Q_HEADS // SWA_KV_HEADS
WINDOW = 128
BAND_BLOCK = 128
RET_HEADS = 4
RET_QK_DIM = 128
RET_V_DIM = 256
RET_CHUNK = 128
ROPE_BASE = 10000.0
MEM_HEADS = 4
MEM_HEAD_DIM = 128
N_BRANCH = 3
N_EXPERTS = 32
TOP_K = 4
D_FF = 1024
SWIGLU_LIMIT = 7.0
SWIGLU_ALPHA = 1.702
EPS = 1e-6
NEG_INF = -1e30

SWA_Q_W = SWA_Q_HEADS * SWA_HEAD_DIM
SWA_KV_W = SWA_KV_HEADS * SWA_HEAD_DIM
RET_QK_W = RET_HEADS * RET_QK_DIM
RET_V_W = RET_HEADS * RET_V_DIM
MEM_W = MEM_HEADS * MEM_HEAD_DIM
IN_SIZES = (SWA_Q_W, SWA_KV_W, SWA_KV_W, RET_QK_W, RET_QK_W, RET_V_W, RET_V_W, MEM_W, N_BRANCH * D_MODEL)

SUBLANES = 8
LANES = 128
MXU_COLS = 256
VMEM_LIMIT = 56 * 1024 * 1024
VMEM_LIMIT_EXPERTS = 62 * 1024 * 1024

TM_PROJ = 512
TM_MERGE = 1024
TQ_MEM = 512
TM_EXPERT = 512
BATCH_PER_STEP = 4
TS_SORT = 256
SEG_ALIGN = SUBLANES
LOCAL_ROWS = TS_SORT * TOP_K + N_EXPERTS * SEG_ALIGN
LOCAL_CHUNKS = LOCAL_ROWS // SEG_ALIGN


def _params(sem, vmem=VMEM_LIMIT):
    return pltpu.CompilerParams(dimension_semantics=sem, vmem_limit_bytes=vmem)


def _const_spec(shape):
    nd = len(shape)
    return pl.BlockSpec(shape, lambda *_: (0,) * nd, pipeline_mode=pl.Buffered(1))


def _rms(x, w):
    return x * lax.rsqrt(jnp.mean(x * x, axis=-1, keepdims=True) + EPS) * w


def _sigmoid(x):
    return 0.5 * jnp.tanh(0.5 * x) + 0.5


def _pack_halves(a):
    n = a.shape[-1] // 2
    lo = lax.bitcast_convert_type(a[:, :n], U32) >> 16
    hi = lax.bitcast_convert_type(a[:, n:], U32) & jnp.uint32(0xFFFF0000)
    return lo | hi


def _unpack_halves(w):
    lo = lax.bitcast_convert_type(w << 16, F32)
    hi = lax.bitcast_convert_type(w & jnp.uint32(0xFFFF0000), F32)
    return lo.astype(BF16), hi.astype(BF16)


def _dot(a, b):
    return jnp.dot(a, b, preferred_element_type=F32)


def _dot_nt(a, b):
    return lax.dot_general(a, b, (((1,), (1,)), ((), ())), preferred_element_type=F32)


def _dot_tn(a, b):
    return lax.dot_general(a, b, (((0,), (0,)), ((), ())), preferred_element_type=F32)


def _inproj_body(x_ref, nw_ref, w_ref, *o_refs):
    h = _rms(x_ref[...], nw_ref[...]).astype(BF16)
    off = 0
    for o_ref in o_refs:
        s = o_ref.shape[-1]
        o_ref[...] = _dot(h, w_ref[:, off:off + s]).astype(o_ref.dtype)
        off += s


def _inproj(x2, norm_w, w_in_bf):
    T, D = x2.shape
    tm = TM_PROJ
    return pl.pallas_call(
        _inproj_body,
        grid=(T // tm,),
        in_specs=[pl.BlockSpec((tm, D), lambda i: (i, 0)),
                  _const_spec((1, D)),
                  _const_spec(w_in_bf.shape)],
        out_specs=[pl.BlockSpec((tm, s), lambda i: (i, 0)) for s in IN_SIZES],
        out_shape=[jax.ShapeDtypeStruct((T, s), BF16) for s in IN_SIZES],
        compiler_params=_params(("parallel",)),
        name="inproj",
    )(x2, norm_w.reshape(1, D), w_in_bf)


def _head_rms(t, seg_ref, w):
    hi, lo = _split_bf16(t * t)
    ss = _dot(hi, seg_ref[...]) + _dot(lo, seg_ref[...])
    return t * lax.rsqrt(ss * (1.0 / SWA_HEAD_DIM) + EPS) * w


def _swa_body(sink_ref, q_ref, kc_ref, kp_ref, vc_ref, vp_ref, qw_ref, kw_ref, segq_ref, segk_ref, o_ref):
    n = pl.program_id(1)
    L = q_ref.shape[1]
    qi = lax.broadcasted_iota(I32, (L, 2 * L), 0)
    kj = lax.broadcasted_iota(I32, (L, 2 * L), 1)
    diff = L + qi - kj
    has_prev = jnp.minimum(n, 1) * L
    mask = (diff >= 0) & (diff < WINDOW) & (kj + has_prev >= L)
    for b in range(q_ref.shape[0]):
        o_ref[b] = _swa_block(sink_ref, q_ref[b], kc_ref[b], kp_ref[b], vc_ref[b], vp_ref[b],
                              qw_ref, kw_ref, segq_ref, segk_ref, mask).astype(o_ref.dtype)


def _swa_block(sink_ref, q, kc, kp, vc, vp, qw_ref, kw_ref, segq_ref, segk_ref, mask):
    d = SWA_HEAD_DIM
    q = _head_rms(q.astype(F32), segq_ref, qw_ref[...]) * (d ** -0.5)
    q = q.astype(BF16)
    k = jnp.concatenate([kp, kc], axis=0).astype(F32)
    k = _head_rms(k, segk_ref, kw_ref[...])
    v = jnp.concatenate([vp, vc], axis=0).astype(F32)
    upper = lax.broadcasted_iota(I32, k.shape, 1) >= d
    k_sw = pltpu.roll(k, d, 1)
    v_sw = pltpu.roll(v, d, 1)

    def placed(t, t_sw):
        return [[jnp.where(upper if half == 1 else jnp.logical_not(upper), t if h == half else t_sw,
                           0.0).astype(BF16) for half in range(2)] for h in range(SWA_KV_HEADS)]

    k_at = placed(k, k_sw)
    v_at = placed(v, v_sw)
    pairs = []
    for j in range(SWA_Q_HEADS // 2):
        h = (2 * j) // SWA_GROUP
        qb = q[:, 2 * j * d:2 * (j + 1) * d]
        acc = None
        for half in range(2):
            s = jnp.where(mask, _dot_nt(qb, k_at[h][half]), NEG_INF)
            sink = sink_ref[2 * j + half]
            m = jnp.maximum(jnp.max(s, axis=-1, keepdims=True), sink)
            p = jnp.exp(s - m)
            den = jnp.sum(p, axis=-1, keepdims=True) + jnp.exp(sink - m)
            o = _dot(p.astype(BF16), v_at[h][half]) * (1.0 / den)
            acc = o if acc is None else acc + o
        pairs.append(acc)
    return jnp.concatenate(pairs, axis=-1)


def _swa(sq, sk, sv, q_norm_w, k_norm_w, sinks, B, S):
    L = BAND_BLOCK
    nb = S // L
    d = SWA_HEAD_DIM
    bs = BATCH_PER_STEP if B % BATCH_PER_STEP == 0 else 1
    cur = lambda b, n: (b, n, 0)
    prev = lambda b, n: (b, jnp.maximum(n - 1, 0), 0)
    as3 = lambda t: t.reshape(B, S, t.shape[-1])
    group = jnp.arange(SWA_Q_W, dtype=I32) // d
    seg_q = (group[:, None] == group[None, :]).astype(BF16)
    seg_k = seg_q[:SWA_KV_W, :SWA_KV_W]
    return pl.pallas_call(
        _swa_body,
        grid=(B // bs, nb),
        in_specs=[pl.BlockSpec(memory_space=pltpu.SMEM),
                  pl.BlockSpec((bs, L, SWA_Q_W), cur),
                  pl.BlockSpec((bs, L, SWA_KV_W), cur),
                  pl.BlockSpec((bs, L, SWA_KV_W), prev),
                  pl.BlockSpec((bs, L, SWA_KV_W), cur),
                  pl.BlockSpec((bs, L, SWA_KV_W), prev),
                  _const_spec((1, SWA_Q_W)),
                  _const_spec((1, SWA_KV_W)),
                  _const_spec(seg_q.shape),
                  _const_spec(seg_k.shape)],
        out_specs=pl.BlockSpec((bs, L, SWA_Q_W), cur),
        out_shape=jax.ShapeDtypeStruct((B, S, SWA_Q_W), BF16),
        compiler_params=_params(("parallel", "parallel")),
        name="swa",
    )(sinks.astype(F32), as3(sq), as3(sk), as3(sk), as3(sv), as3(sv),
      jnp.tile(q_norm_w, SWA_Q_HEADS).reshape(1, -1), jnp.tile(k_norm_w, SWA_KV_HEADS).reshape(1, -1),
      seg_q, seg_k).reshape(B * S, SWA_Q_W)


def _ret_body(q_ref, k_ref, v_ref, g_ref, cos_ref, sin_ref, dm_ref, qd_ref, kd_ref, cd_ref, nw_ref,
              o_ref, st_ref):
    n = pl.program_id(1)

    @pl.when(n == 0)
    def _():
        st_ref[...] = jnp.zeros_like(st_ref)

    for b in range(q_ref.shape[0]):
        _ret_block(b, q_ref, k_ref, v_ref, g_ref, cos_ref, sin_ref, dm_ref, qd_ref, kd_ref, cd_ref, nw_ref,
                   o_ref, st_ref)


def _ret_block(b, q_ref, k_ref, v_ref, g_ref, cos_ref, sin_ref, dm_ref, qd_ref, kd_ref, cd_ref, nw_ref,
               o_ref, st_ref):
    cos = cos_ref[...]
    sin = sin_ref[...]
    dk, dv = RET_QK_DIM, RET_V_DIM
    outs = []
    for h in range(RET_HEADS):
        qh = q_ref[b, :, h * dk:(h + 1) * dk].astype(F32)
        kh = k_ref[b, :, h * dk:(h + 1) * dk].astype(F32)
        qr = qh * cos + pltpu.roll(qh, dk // 2, 1) * sin
        kr = (kh * cos + pltpu.roll(kh, dk // 2, 1) * sin) * (dk ** -0.5)
        vh = v_ref[b, :, h * dv:(h + 1) * dv]
        st = st_ref[b, h]
        inner = _dot_nt(qr.astype(BF16), kr.astype(BF16)) * dm_ref[h]
        o = _dot(inner.astype(BF16), vh) + _dot((qr * qd_ref[h]).astype(BF16), st.astype(BF16))
        st_ref[b, h] = st * cd_ref[h] + _dot_tn((kr * kd_ref[h]).astype(BF16), vh)
        mu = jnp.mean(o, axis=-1, keepdims=True)
        oc = o - mu
        var = jnp.mean(oc * oc, axis=-1, keepdims=True)
        y = oc * lax.rsqrt(var + EPS) * nw_ref[:, h * dv:(h + 1) * dv]
        g = g_ref[b, :, h * dv:(h + 1) * dv].astype(F32)
        outs.append(g * _sigmoid(g) * y)
    o_ref[b] = jnp.concatenate(outs, axis=-1).astype(o_ref.dtype)


def _retention(rq, rk, rv, rg, ret_norm_w, B, S):
    L = RET_CHUNK
    nc = S // L
    half = RET_QK_DIM // 2
    inv = 1.0 / (ROPE_BASE ** (jnp.arange(0, half, dtype=F32) / half))
    ang = jnp.arange(S, dtype=F32)[:, None] * inv[None, :]
    cos = jnp.cos(ang)
    sin = jnp.sin(ang)
    cos2 = jnp.concatenate([cos, cos], axis=-1)
    sin2 = jnp.concatenate([-sin, sin], axis=-1)
    log_g = jnp.log1p(-(2.0 ** (-5.0 - jnp.arange(RET_HEADS, dtype=F32))))
    pos = jnp.arange(L, dtype=F32)
    diff = pos[:, None] - pos[None, :]
    decay_mask = jnp.where(diff[None] >= 0,
                           jnp.exp(jnp.maximum(diff, 0.0)[None] * log_g[:, None, None]), 0.0)
    q_decay = jnp.exp((pos[None, :] + 1.0) * log_g[:, None])[..., None]
    k_decay = jnp.exp((L - 1.0 - pos[None, :]) * log_g[:, None])[..., None]
    chunk_decay = jnp.exp(L * log_g)
    qd = jnp.broadcast_to(q_decay, (RET_HEADS, L, RET_QK_DIM))
    kd = jnp.broadcast_to(k_decay, (RET_HEADS, L, RET_QK_DIM))
    bs = BATCH_PER_STEP if B % BATCH_PER_STEP == 0 else 1
    row = lambda b, n: (b, n, 0)
    as3 = lambda t: t.reshape(B, S, t.shape[-1])
    return pl.pallas_call(
        _ret_body,
        grid=(B // bs, nc),
        in_specs=[pl.BlockSpec((bs, L, RET_QK_W), row),
                  pl.BlockSpec((bs, L, RET_QK_W), row),
                  pl.BlockSpec((bs, L, RET_V_W), row),
                  pl.BlockSpec((bs, L, RET_V_W), row),
                  pl.BlockSpec((L, RET_QK_DIM), lambda b, n: (n, 0)),
                  pl.BlockSpec((L, RET_QK_DIM), lambda b, n: (n, 0)),
                  _const_spec((RET_HEADS, L, L)),
                  _const_spec((RET_HEADS, L, RET_QK_DIM)),
                  _const_spec((RET_HEADS, L, RET_QK_DIM)),
                  pl.BlockSpec(memory_space=pltpu.SMEM),
                  _const_spec((1, RET_V_W))],
        out_specs=pl.BlockSpec((bs, L, RET_V_W), row),
        out_shape=jax.ShapeDtypeStruct((B, S, RET_V_W), BF16),
        scratch_shapes=[pltpu.VMEM((bs, RET_HEADS, RET_QK_DIM, RET_V_DIM), F32)],
        compiler_params=_params(("parallel", "arbitrary")),
        name="retention",
    )(as3(rq), as3(rk), as3(rv), as3(rg), cos2, sin2, decay_mask, qd, kd, chunk_decay,
      ret_norm_w.reshape(1, -1)).reshape(B * S, RET_V_W)


def _memkv_body(m_ref, nw_ref, w_ref, kw_ref, k_ref, v_ref):
    m = _rms(m_ref[...], nw_ref[...]).astype(BF16)
    kv = _dot(m, w_ref[...])
    d = MEM_HEAD_DIM
    ks = [_rms(kv[:, h * d:(h + 1) * d], kw_ref[...]) for h in range(MEM_HEADS)]
    k_ref[...] = jnp.concatenate(ks, axis=-1).astype(k_ref.dtype)
    v_ref[...] = kv[:, MEM_W:].astype(v_ref.dtype)


def _memkv(mem2, mem_norm_w, w_kv_bf, k_norm_w, B, M):
    D = mem2.shape[-1]
    return pl.pallas_call(
        _memkv_body,
        grid=(B,),
        in_specs=[pl.BlockSpec((M, D), lambda b: (b, 0)),
                  _const_spec((1, D)),
                  _const_spec(w_kv_bf.shape),
                  _const_spec((1, MEM_HEAD_DIM))],
        out_specs=[pl.BlockSpec((M, MEM_W), lambda b: (b, 0))] * 2,
        out_shape=[jax.ShapeDtypeStruct((B * M, MEM_W), BF16)] * 2,
        compiler_params=_params(("parallel",)),
        name="memkv",
    )(mem2, mem_norm_w.reshape(1, D), w_kv_bf, k_norm_w.reshape(1, -1))


def _memattn_body(q_ref, k_ref, v_ref, qw_ref, o_ref):
    d = MEM_HEAD_DIM
    outs = []
    for h in range(MEM_HEADS):
        qh = _rms(q_ref[:, h * d:(h + 1) * d].astype(F32), qw_ref[...]).astype(BF16)
        s = _dot_nt(qh, k_ref[:, h * d:(h + 1) * d]) * (d ** -0.5)
        m = jnp.max(s, axis=-1, keepdims=True)
        p = jnp.exp(s - m)
        pr = p / jnp.sum(p, axis=-1, keepdims=True)
        outs.append(_dot(pr.astype(BF16), v_ref[:, h * d:(h + 1) * d]))
    o_ref[...] = jnp.concatenate(outs, axis=-1).astype(o_ref.dtype)


def _memattn(mq, mk, mv, q_norm_w, B, S, M):
    tq = min(TQ_MEM, S)
    nq = S // tq
    return pl.pallas_call(
        _memattn_body,
        grid=(B, nq),
        in_specs=[pl.BlockSpec((tq, MEM_W), lambda b, i: (b * nq + i, 0)),
                  pl.BlockSpec((M, MEM_W), lambda b, i: (b, 0)),
                  pl.BlockSpec((M, MEM_W), lambda b, i: (b, 0)),
                  _const_spec((1, MEM_HEAD_DIM))],
        out_specs=pl.BlockSpec((tq, MEM_W), lambda b, i: (b * nq + i, 0)),
        out_shape=jax.ShapeDtypeStruct((B * S, MEM_W), BF16),
        compiler_params=_params(("parallel", "parallel")),
        name="memattn",
    )(mq, mk, mv, q_norm_w.reshape(1, -1))


def _mem_block(q, k_ref, v_ref, b, qw_ref):
    d = MEM_HEAD_DIM
    outs = []
    for h in range(MEM_HEADS):
        qh = _rms(q[:, h * d:(h + 1) * d].astype(F32), qw_ref[...]).astype(BF16)
        s = _dot_nt(qh, k_ref[b, :, h * d:(h + 1) * d]) * (d ** -0.5)
        m = jnp.max(s, axis=-1, keepdims=True)
        p = jnp.exp(s - m)
        o = _dot(p.astype(BF16), v_ref[b, :, h * d:(h + 1) * d])
        outs.append(o * (1.0 / jnp.sum(p, axis=-1, keepdims=True)))
    return jnp.concatenate(outs, axis=-1)


def _mixers_body(sink_ref, cd_ref, x_ref, nw_ref, w_ref, sqw_ref, skw_ref, segq_ref, segk_ref,
                 cos_ref, sin_ref, dm_ref, qd_ref, kd_ref, rnw_ref, mk_ref, mv_ref, mqw_ref,
                 osw_ref, ort_ref, omm_ref, zg_ref, st_ref, kprev_ref, vprev_ref):
    n = pl.program_id(1)
    bs, L, D = x_ref.shape

    @pl.when(n == 0)
    def _():
        st_ref[...] = jnp.zeros_like(st_ref)
        kprev_ref[...] = jnp.zeros_like(kprev_ref)
        vprev_ref[...] = jnp.zeros_like(vprev_ref)

    h = _rms(x_ref[...].reshape(bs * L, D), nw_ref[...]).astype(BF16)
    offs = [sum(IN_SIZES[:i]) for i in range(len(IN_SIZES) + 1)]

    def proj(i):
        z = _dot(h, w_ref[:, offs[i]:offs[i + 1]]).astype(BF16)
        return z.reshape(bs, L, IN_SIZES[i])

    sq, sk, sv = proj(0), proj(1), proj(2)
    qi = lax.broadcasted_iota(I32, (L, 2 * L), 0)
    kj = lax.broadcasted_iota(I32, (L, 2 * L), 1)
    diff = L + qi - kj
    has_prev = jnp.minimum(n, 1) * L
    mask = (diff >= 0) & (diff < WINDOW) & (kj + has_prev >= L)
    for b in range(bs):
        osw_ref[b] = _swa_block(sink_ref, sq[b], sk[b], kprev_ref[b], sv[b], vprev_ref[b],
                                sqw_ref, skw_ref, segq_ref, segk_ref, mask).astype(osw_ref.dtype)
    kprev_ref[...] = sk
    vprev_ref[...] = sv
    rq, rk, rv, rg = proj(3), proj(4), proj(5), proj(6)
    for b in range(bs):
        _ret_block(b, rq, rk, rv, rg, cos_ref, sin_ref, dm_ref, qd_ref, kd_ref, cd_ref, rnw_ref, ort_ref, st_ref)
    mq = proj(7)
    for b in range(bs):
        omm_ref[b] = _mem_block(mq[b], mk_ref, mv_ref, b, mqw_ref).astype(omm_ref.dtype)
    zg_ref[...] = proj(8)


def _mixers(x, mix_norm_w, w_in_bf, swa_q_norm_w, swa_k_norm_w, sinks, ret_norm_w, mk, mv, mem_q_norm_w):
    B, S, D = x.shape
    M = mk.shape[0] // B
    L = BAND_BLOCK
    nb = S // L
    bs = BATCH_PER_STEP if B % BATCH_PER_STEP == 0 else 1
    d = SWA_HEAD_DIM
    group = jnp.arange(SWA_Q_W, dtype=I32) // d
    seg_q = (group[:, None] == group[None, :]).astype(BF16)
    seg_k = seg_q[:SWA_KV_W, :SWA_KV_W]
    half = RET_QK_DIM // 2
    inv = 1.0 / (ROPE_BASE ** (jnp.arange(0, half, dtype=F32) / half))
    ang = jnp.arange(S, dtype=F32)[:, None] * inv[None, :]
    cos = jnp.cos(ang)
    sin = jnp.sin(ang)
    cos2 = jnp.concatenate([cos, cos], axis=-1)
    sin2 = jnp.concatenate([-sin, sin], axis=-1)
    log_g = jnp.log1p(-(2.0 ** (-5.0 - jnp.arange(RET_HEADS, dtype=F32))))
    pos = jnp.arange(L, dtype=F32)
    dpos = pos[:, None] - pos[None, :]
    decay_mask = jnp.where(dpos[None] >= 0,
                           jnp.exp(jnp.maximum(dpos, 0.0)[None] * log_g[:, None, None]), 0.0)
    q_decay = jnp.exp((pos[None, :] + 1.0) * log_g[:, None])[..., None]
    k_decay = jnp.exp((L - 1.0 - pos[None, :]) * log_g[:, None])[..., None]
    chunk_decay = jnp.exp(L * log_g)
    qd = jnp.broadcast_to(q_decay, (RET_HEADS, L, RET_QK_DIM))
    kd = jnp.broadcast_to(k_decay, (RET_HEADS, L, RET_QK_DIM))
    blk = lambda w: pl.BlockSpec((bs, L, w), lambda b, n: (b, n, 0))
    smem = pl.BlockSpec(memory_space=pltpu.SMEM)
    outs = pl.pallas_call(
        _mixers_body,
        grid=(B // bs, nb),
        in_specs=[smem, smem,
                  blk(D), _const_spec((1, D)), _const_spec(w_in_bf.shape),
                  _const_spec((1, SWA_Q_W)), _const_spec((1, SWA_KV_W)),
                  _const_spec(seg_q.shape), _const_spec(seg_k.shape),
                  pl.BlockSpec((L, RET_QK_DIM), lambda b, n: (n, 0)),
                  pl.BlockSpec((L, RET_QK_DIM), lambda b, n: (n, 0)),
                  _const_spec((RET_HEADS, L, L)),
                  _const_spec((RET_HEADS, L, RET_QK_DIM)),
                  _const_spec((RET_HEADS, L, RET_QK_DIM)),
                  _const_spec((1, RET_V_W)),
                  pl.BlockSpec((bs, M, MEM_W), lambda b, n: (b, 0, 0)),
                  pl.BlockSpec((bs, M, MEM_W), lambda b, n: (b, 0, 0)),
                  _const_spec((1, MEM_HEAD_DIM))],
        out_specs=[blk(SWA_Q_W), blk(RET_V_W), blk(MEM_W), blk(N_BRANCH * D)],
        out_shape=[jax.ShapeDtypeStruct((B, S, w), BF16) for w in (SWA_Q_W, RET_V_W, MEM_W, N_BRANCH * D)],
        scratch_shapes=[pltpu.VMEM((bs, RET_HEADS, RET_QK_DIM, RET_V_DIM), F32),
                        pltpu.VMEM((bs, L, SWA_KV_W), BF16),
                        pltpu.VMEM((bs, L, SWA_KV_W), BF16)],
        compiler_params=_params(("parallel", "arbitrary"), vmem=VMEM_LIMIT_EXPERTS),
        name="mixers",
    )(sinks.astype(F32), chunk_decay, x, mix_norm_w.reshape(1, D), w_in_bf,
      jnp.tile(swa_q_norm_w, SWA_Q_HEADS).reshape(1, -1), jnp.tile(swa_k_norm_w, SWA_KV_HEADS).reshape(1, -1),
      seg_q, seg_k, cos2, sin2, decay_mask, qd, kd, ret_norm_w.reshape(1, -1),
      mk.reshape(B, M, MEM_W), mv.reshape(B, M, MEM_W), mem_q_norm_w.reshape(1, -1))
    return [o.reshape(B * S, o.shape[-1]) for o in outs]


def _split_bf16(a):
    hi = a.astype(BF16)
    lo = (a - hi.astype(F32)).astype(BF16)
    return hi, lo


def _merge_body(x_ref, osw_ref, ort_ref, omm_ref, zg_ref, wa_ref, wr_ref, wm_ref, wo_ref, nw_ref,
                wrt_ref, brt_ref, x1_ref, h2_ref, idx_ref, gate_ref, cnt_ref):
    D = D_MODEL
    sg = lambda j: _sigmoid(zg_ref[:, j * D:(j + 1) * D].astype(F32))
    merged = (sg(0) * _dot(osw_ref[...], wa_ref[...])
              + sg(1) * _dot(ort_ref[...], wr_ref[...])
              + sg(2) * _dot(omm_ref[...], wm_ref[...]))
    x1 = x_ref[...] + _dot(merged.astype(BF16), wo_ref[...])
    x1_ref[...] = x1
    h2 = _rms(x1, nw_ref[...])
    h2_ref[...] = h2.astype(h2_ref.dtype)
    h_hi, h_lo = _split_bf16(h2)
    w_hi, w_lo = _split_bf16(wrt_ref[...])
    logits = _dot(h_hi, w_hi) + _dot(h_lo, w_hi) + _dot(h_hi, w_lo) + brt_ref[...]
    lane = lax.broadcasted_iota(I32, logits.shape, 1)
    vals, idxs = [], []
    l = logits
    for _ in range(TOP_K):
        m = jnp.max(l, axis=-1, keepdims=True)
        i = jnp.min(jnp.where(l == m, lane, LANES), axis=-1, keepdims=True)
        vals.append(m)
        idxs.append(i)
        l = jnp.where(lane == i, -jnp.inf, l)
    es = [jnp.exp(v - vals[0]) for v in vals]
    den = es[0] + es[1] + es[2] + es[3]
    idx_ref[...] = jnp.concatenate(idxs, axis=-1)
    gate_ref[...] = jnp.concatenate([e / den for e in es], axis=-1)
    chosen = sum((lane == i).astype(F32) for i in idxs)
    ts = min(TS_SORT, chosen.shape[0])
    per_tile = [jnp.sum(chosen[j * ts:(j + 1) * ts], axis=0, keepdims=True) for j in range(chosen.shape[0] // ts)]
    per_tile.append(jnp.zeros((SUBLANES - len(per_tile), LANES), F32))
    cnt_ref[...] = jnp.concatenate(per_tile, axis=0)


def _merge(x2, o_swa, o_ret, o_mem, zg, wa, wr, wm, wo, ffn_norm_w, w_router_pad, b_router_pad):
    T, D = x2.shape
    tm = min(TM_MERGE, T)
    row = lambda i: (i, 0)
    return pl.pallas_call(
        _merge_body,
        grid=(T // tm,),
        in_specs=[pl.BlockSpec((tm, D), row),
                  pl.BlockSpec((tm, SWA_Q_W), row),
                  pl.BlockSpec((tm, RET_V_W), row),
                  pl.BlockSpec((tm, MEM_W), row),
                  pl.BlockSpec((tm, N_BRANCH * D), row),
                  _const_spec(wa.shape), _const_spec(wr.shape), _const_spec(wm.shape), _const_spec(wo.shape),
                  _const_spec((1, D)),
                  _const_spec(w_router_pad.shape), _const_spec((1, LANES))],
        out_specs=[pl.BlockSpec((tm, D), row),
                   pl.BlockSpec((tm, D), row),
                   pl.BlockSpec((tm, TOP_K), row),
                   pl.BlockSpec((tm, TOP_K), row),
                   pl.BlockSpec((None, SUBLANES, LANES), lambda i: (i, 0, 0))],
        out_shape=[jax.ShapeDtypeStruct((T, D), F32),
                   jax.ShapeDtypeStruct((T, D), BF16),
                   jax.ShapeDtypeStruct((T, TOP_K), I32),
                   jax.ShapeDtypeStruct((T, TOP_K), F32),
                   jax.ShapeDtypeStruct((T // tm, SUBLANES, LANES), F32)],
        compiler_params=_params(("parallel",)),
        name="merge",
    )(x2, o_swa, o_ret, o_mem, zg, wa, wr, wm, wo, ffn_norm_w.reshape(1, D), w_router_pad, b_router_pad)


def _merge_rows(x, osw, ort, omm, gate_logits, wa_ref, wr_ref, wm_ref, wo_ref, nw_ref, wrt_ref, brt_ref):
    sg = lambda j: _sigmoid(gate_logits(j).astype(F32))
    merged = (sg(0) * _dot(osw, wa_ref[...]) + sg(1) * _dot(ort, wr_ref[...]) + sg(2) * _dot(omm, wm_ref[...]))
    x1 = x + _dot(merged.astype(BF16), wo_ref[...])
    h2 = _rms(x1, nw_ref[...])
    h_hi, h_lo = _split_bf16(h2)
    w_hi, w_lo = _split_bf16(wrt_ref[...])
    logits = _dot(h_hi, w_hi) + _dot(h_lo, w_hi) + _dot(h_hi, w_lo) + brt_ref[...]
    lane = lax.broadcasted_iota(I32, logits.shape, 1)
    vals, idxs = [], []
    l = logits
    for _ in range(TOP_K):
        m = jnp.max(l, axis=-1, keepdims=True)
        i = jnp.min(jnp.where(l == m, lane, LANES), axis=-1, keepdims=True)
        vals.append(m)
        idxs.append(i)
        l = jnp.where(lane == i, -jnp.inf, l)
    es = [jnp.exp(v - vals[0]) for v in vals]
    den = es[0] + es[1] + es[2] + es[3]
    gates = jnp.concatenate([e / den for e in es], axis=-1)
    chosen = sum((lane == i).astype(F32) for i in idxs)
    return x1, h2, jnp.concatenate(idxs, axis=-1), gates, chosen


def _mixmerge_body(sink_ref, cd_ref, x_ref, nw_ref, w_ref, sqw_ref, skw_ref, segq_ref, segk_ref,
                   cos_ref, sin_ref, dm_ref, qd_ref, kd_ref, rnw_ref, mk_ref, mv_ref, mqw_ref,
                   wa_ref, wr_ref, wm_ref, wo_ref, fnw_ref, wrt_ref, brt_ref,
                   x1_ref, h2_ref, idx_ref, gate_ref, cnt_ref,
                   st_ref, kprev_ref, vprev_ref, osw_s, ort_s, omm_s, xlag_s, *, nb, nsteps):
    t = pl.program_id(0)
    n = jnp.minimum(t, nsteps - 1) % nb
    slot = t % 2
    bs, L, D = x_ref.shape

    @pl.when(t == 0)
    def _():
        for s_ref in (osw_s, ort_s, omm_s, xlag_s):
            s_ref[...] = jnp.zeros_like(s_ref)

    @pl.when(n == 0)
    def _():
        st_ref[...] = jnp.zeros_like(st_ref)
        kprev_ref[...] = jnp.zeros_like(kprev_ref)
        vprev_ref[...] = jnp.zeros_like(vprev_ref)

    offs = [sum(IN_SIZES[:i]) for i in range(len(IN_SIZES) + 1)]

    rows = bs * L
    prev = 1 - slot
    x_prev = xlag_s[...].reshape(rows, D)
    h_prev = _rms(x_prev, nw_ref[...]).astype(BF16)
    gate_logits = lambda j: _dot(h_prev, w_ref[:, offs[8] + j * D:offs[8] + (j + 1) * D]).astype(BF16)
    x1, h2, idx, gates, chosen = _merge_rows(
        x_prev, osw_s[prev].reshape(rows, -1), ort_s[prev].reshape(rows, -1), omm_s[prev].reshape(rows, -1),
        gate_logits, wa_ref, wr_ref, wm_ref, wo_ref, fnw_ref, wrt_ref, brt_ref)
    x1_ref[...] = x1.reshape(bs, L, D)
    h2_ref[...] = h2.astype(h2_ref.dtype).reshape(bs, L, D)
    idx_ref[...] = idx.reshape(bs, L, TOP_K)
    gate_ref[...] = gates.reshape(bs, L, TOP_K)
    counts = [jnp.sum(chosen[b * L:(b + 1) * L], axis=0, keepdims=True) for b in range(bs)]
    counts.append(jnp.zeros((SUBLANES - bs, LANES), F32))
    cnt_ref[...] = jnp.concatenate(counts, axis=0)

    h = _rms(x_ref[...].reshape(bs * L, D), nw_ref[...]).astype(BF16)

    def proj(i):
        return _dot(h, w_ref[:, offs[i]:offs[i + 1]]).astype(BF16).reshape(bs, L, IN_SIZES[i])

    sq, sk, sv = proj(0), proj(1), proj(2)
    qi = lax.broadcasted_iota(I32, (L, 2 * L), 0)
    kj = lax.broadcasted_iota(I32, (L, 2 * L), 1)
    diff = L + qi - kj
    has_prev = jnp.minimum(n, 1) * L
    mask = (diff >= 0) & (diff < WINDOW) & (kj + has_prev >= L)
    for b in range(bs):
        osw_s[slot, b] = _swa_block(sink_ref, sq[b], sk[b], kprev_ref[b], sv[b], vprev_ref[b],
                                    sqw_ref, skw_ref, segq_ref, segk_ref, mask).astype(osw_s.dtype)
    kprev_ref[...] = sk
    vprev_ref[...] = sv
    rq, rk, rv, rg = proj(3), proj(4), proj(5), proj(6)
    for b in range(bs):
        _ret_block(b, rq, rk, rv, rg, cos_ref, sin_ref, dm_ref, qd_ref, kd_ref, cd_ref, rnw_ref,
                   ort_s.at[slot], st_ref)
    mq = proj(7)
    for b in range(bs):
        omm_s[slot, b] = _mem_block(mq[b], mk_ref, mv_ref, b, mqw_ref).astype(omm_s.dtype)
    xlag_s[...] = x_ref[...]


def _mixmerge(x, mix_norm_w, w_in_bf, swa_q_norm_w, swa_k_norm_w, sinks, ret_norm_w, mk, mv, mem_q_norm_w,
              wa, wr, wm, wo, ffn_norm_w, w_router_pad, b_router_pad):
    B, S, D = x.shape
    M = mk.shape[0] // B
    L = BAND_BLOCK
    nb = S // L
    bs = BATCH_PER_STEP
    nsteps = (B // bs) * nb
    d = SWA_HEAD_DIM
    group = jnp.arange(SWA_Q_W, dtype=I32) // d
    seg_q = (group[:, None] == group[None, :]).astype(BF16)
    seg_k = seg_q[:SWA_KV_W, :SWA_KV_W]
    half = RET_QK_DIM // 2
    inv = 1.0 / (ROPE_BASE ** (jnp.arange(0, half, dtype=F32) / half))
    ang = jnp.arange(S, dtype=F32)[:, None] * inv[None, :]
    cos = jnp.cos(ang)
    sin = jnp.sin(ang)
    cos2 = jnp.concatenate([cos, cos], axis=-1)
    sin2 = jnp.concatenate([-sin, sin], axis=-1)
    log_g = jnp.log1p(-(2.0 ** (-5.0 - jnp.arange(RET_HEADS, dtype=F32))))
    pos = jnp.arange(L, dtype=F32)
    dpos = pos[:, None] - pos[None, :]
    decay_mask = jnp.where(dpos[None] >= 0,
                           jnp.exp(jnp.maximum(dpos, 0.0)[None] * log_g[:, None, None]), 0.0)
    q_decay = jnp.exp((pos[None, :] + 1.0) * log_g[:, None])[..., None]
    k_decay = jnp.exp((L - 1.0 - pos[None, :]) * log_g[:, None])[..., None]
    chunk_decay = jnp.exp(L * log_g)
    qd = jnp.broadcast_to(q_decay, (RET_HEADS, L, RET_QK_DIM))
    kd = jnp.broadcast_to(k_decay, (RET_HEADS, L, RET_QK_DIM))

    cur = lambda t: jnp.minimum(t, nsteps - 1)
    lag = lambda t: jnp.maximum(t - 1, 0)
    blk = lambda w, item: pl.BlockSpec((bs, L, w), lambda t: (item(t) // nb, item(t) % nb, 0))
    smem = pl.BlockSpec(memory_space=pltpu.SMEM)
    x1, h2, idx, gates, cnt = pl.pallas_call(
        functools.partial(_mixmerge_body, nb=nb, nsteps=nsteps),
        grid=(nsteps + 1,),
        in_specs=[smem, smem,
                  blk(D, cur), _const_spec((1, D)), _const_spec(w_in_bf.shape),
                  _const_spec((1, SWA_Q_W)), _const_spec((1, SWA_KV_W)),
                  _const_spec(seg_q.shape), _const_spec(seg_k.shape),
                  pl.BlockSpec((L, RET_QK_DIM), lambda t: (cur(t) % nb, 0)),
                  pl.BlockSpec((L, RET_QK_DIM), lambda t: (cur(t) % nb, 0)),
                  _const_spec((RET_HEADS, L, L)),
                  _const_spec((RET_HEADS, L, RET_QK_DIM)),
                  _const_spec((RET_HEADS, L, RET_QK_DIM)),
                  _const_spec((1, RET_V_W)),
                  pl.BlockSpec((bs, M, MEM_W), lambda t: (cur(t) // nb, 0, 0), pipeline_mode=pl.Buffered(1)),
                  pl.BlockSpec((bs, M, MEM_W), lambda t: (cur(t) // nb, 0, 0), pipeline_mode=pl.Buffered(1)),
                  _const_spec((1, MEM_HEAD_DIM)),
                  _const_spec(wa.shape), _const_spec(wr.shape), _const_spec(wm.shape), _const_spec(wo.shape),
                  _const_spec((1, D)), _const_spec(w_router_pad.shape), _const_spec((1, LANES))],
        out_specs=[blk(D, lag), blk(D, lag), blk(TOP_K, lag), blk(TOP_K, lag),
                   pl.BlockSpec((None, None, SUBLANES, LANES), lambda t: (lag(t) // nb, lag(t) % nb, 0, 0))],
        out_shape=[jax.ShapeDtypeStruct((B, S, D), F32),
                   jax.ShapeDtypeStruct((B, S, D), BF16),
                   jax.ShapeDtypeStruct((B, S, TOP_K), I32),
                   jax.ShapeDtypeStruct((B, S, TOP_K), F32),
                   jax.ShapeDtypeStruct((B // bs, nb, SUBLANES, LANES), F32)],
        scratch_shapes=[pltpu.VMEM((bs, RET_HEADS, RET_QK_DIM, RET_V_DIM), F32),
                        pltpu.VMEM((bs, L, SWA_KV_W), BF16),
                        pltpu.VMEM((bs, L, SWA_KV_W), BF16),
                        pltpu.VMEM((2, bs, L, SWA_Q_W), BF16),
                        pltpu.VMEM((2, bs, L, RET_V_W), BF16),
                        pltpu.VMEM((2, bs, L, MEM_W), BF16),
                        pltpu.VMEM((bs, L, D), F32)],
        compiler_params=_params(("arbitrary",), vmem=VMEM_LIMIT_EXPERTS),
        name="mixmerge",
    )(sinks.astype(F32), chunk_decay, x, mix_norm_w.reshape(1, D), w_in_bf,
      jnp.tile(swa_q_norm_w, SWA_Q_HEADS).reshape(1, -1), jnp.tile(swa_k_norm_w, SWA_KV_HEADS).reshape(1, -1),
      seg_q, seg_k, cos2, sin2, decay_mask, qd, kd, ret_norm_w.reshape(1, -1),
      mk.reshape(B, M, MEM_W), mv.reshape(B, M, MEM_W), mem_q_norm_w.reshape(1, -1),
      wa, wr, wm, wo, ffn_norm_w.reshape(1, D), w_router_pad, b_router_pad)
    T = B * S
    cnt_blocks = cnt[:, :, :bs, :].transpose(0, 2, 1, 3).reshape(B, nb, LANES)
    return x1.reshape(T, D), h2.reshape(T, D), idx.reshape(T, TOP_K), gates.reshape(T, TOP_K), cnt_blocks


def _chunk_copy(src_ref, src_chunk, dst_ref, dst_chunk, sem):
    def rows(c):
        start = c * SEG_ALIGN
        return pl.ds(start if isinstance(c, int) else pl.multiple_of(start, SEG_ALIGN), SEG_ALIGN)

    return pltpu.make_async_copy(src_ref.at[rows(src_chunk), :], dst_ref.at[rows(dst_chunk), :], sem)


def _dispatch_body(fill_start_ref, fill_n_ref, gch_ref, h2_ref, idxt_ref, loff_ref,
                   xs_ref, rowt_ref, buf_ref, zero_ref, sems, zsem):
    s = pl.program_id(0)
    last = pl.num_programs(0) - 1
    slot = s % 2
    ts = h2_ref.shape[0]
    nchunks = gch_ref.shape[-1]
    local_rows = nchunks * SEG_ALIGN

    def wait_slot(sl):
        for c in range(nchunks):
            _chunk_copy(buf_ref.at[sl], c, xs_ref, c, sems.at[sl]).wait()

    @pl.when(s == 0)
    def _():
        zero_ref[...] = jnp.zeros_like(zero_ref)

        def per_expert(e, total):
            def fill(r, _):
                _chunk_copy(zero_ref, 0, xs_ref, fill_start_ref[e] + r, zsem).start()
                return 0

            lax.fori_loop(0, fill_n_ref[e], fill, 0)
            return total + fill_n_ref[e]

        total = lax.fori_loop(0, N_EXPERTS, per_expert, 0)

        def drain(r, _):
            _chunk_copy(zero_ref, 0, xs_ref, 0, zsem).wait()
            return 0

        lax.fori_loop(0, total, drain, 0)

    @pl.when(s >= 2)
    def _():
        wait_slot(slot)

    idxt = idxt_ref[...]
    sub = lax.broadcasted_iota(I32, (LANES, ts), 0)
    ohs = [(sub == idxt[k:k + 1, :]).astype(F32) for k in range(TOP_K)]
    m = ohs[0] + ohs[1] + ohs[2] + ohs[3]
    tr = lax.broadcasted_iota(I32, (ts, ts), 0)
    tc = lax.broadcasted_iota(I32, (ts, ts), 1)
    earlier = jnp.where(tr < tc, 1.0, 0.0).astype(BF16)
    before = _dot(m.astype(BF16), earlier) + loff_ref[...]
    rows = [jnp.sum(oh * before, axis=0, keepdims=True).astype(I32) for oh in ohs]
    rowt_ref[...] = jnp.concatenate(rows, axis=0)
    ri = lax.broadcasted_iota(I32, (local_rows, ts), 0)
    sel = (ri == rows[0]) | (ri == rows[1]) | (ri == rows[2]) | (ri == rows[3])
    buf_ref[slot] = _pack_halves(_dot(jnp.where(sel, 1.0, 0.0).astype(BF16), h2_ref[...]))
    for c in range(nchunks):
        _chunk_copy(buf_ref.at[slot], c, xs_ref, gch_ref[0, 0, c], sems.at[slot]).start()

    @pl.when(s == last)
    def _():
        wait_slot(slot)

        @pl.when(s >= 1)
        def _():
            wait_slot(1 - slot)


def _dispatch(h2, idxt, loff_col, gchunk, fill_start, fill_n, n_rows, ts):
    T, D = h2.shape
    nt = T // ts
    nchunks = gchunk.shape[-1]
    return pl.pallas_call(
        _dispatch_body,
        grid_spec=pltpu.PrefetchScalarGridSpec(
            num_scalar_prefetch=2,
            grid=(nt,),
            in_specs=[pl.BlockSpec((1, 1, nchunks), lambda s, *_: (s, 0, 0), memory_space=pltpu.SMEM),
                      pl.BlockSpec((ts, D), lambda s, *_: (s, 0)),
                      pl.BlockSpec((TOP_K, ts), lambda s, *_: (0, s)),
                      pl.BlockSpec((None, LANES, 1), lambda s, *_: (s, 0, 0))],
            out_specs=[pl.BlockSpec(memory_space=pl.ANY),
                       pl.BlockSpec((TOP_K, ts), lambda s, *_: (0, s))],
            scratch_shapes=[pltpu.VMEM((2, nchunks * SEG_ALIGN, D // 2), U32),
                            pltpu.VMEM((SEG_ALIGN, D // 2), U32),
                            pltpu.SemaphoreType.DMA((2,)),
                            pltpu.SemaphoreType.DMA(())]),
        out_shape=[jax.ShapeDtypeStruct((n_rows, D // 2), U32),
                   jax.ShapeDtypeStruct((TOP_K, T), I32)],
        compiler_params=_params(("arbitrary",)),
        name="dispatch",
    )(fill_start, fill_n, gchunk.reshape(nt, 1, nchunks), h2, idxt, loff_col)


def _expert_body(first_ref, ntile_ref, xs_ref, wu_ref, bu_ref, wd_ref, bd_ref, ys_ref,
                 wu_bf, wd_bf, xbuf, obuf, isem, osem):
    e = pl.program_id(0)
    n = ntile_ref[e]
    first = first_ref[e]
    tm = xbuf.shape[1]
    half = MXU_COLS // 2

    def tile_rows(i):
        return pl.ds(pl.multiple_of((first + i) * tm, tm), tm)

    def x_copy(i, slot):
        return pltpu.make_async_copy(xs_ref.at[tile_rows(i), :], xbuf.at[slot], isem.at[slot])

    def y_copy(i, slot):
        return pltpu.make_async_copy(obuf.at[slot], ys_ref.at[tile_rows(i), :], osem.at[slot])

    @pl.when(n > 0)
    def _():
        x_copy(0, 0).start()
        r = lax.broadcasted_iota(I32, (MXU_COLS, MXU_COLS), 0)
        c = lax.broadcasted_iota(I32, (MXU_COLS, MXU_COLS), 1)
        perm = jnp.where(r == jnp.where(c < half, 2 * c, 2 * (c - half) + 1), 1.0, 0.0).astype(BF16)
        for b in range(wu_ref.shape[-1] // MXU_COLS):
            cols = slice(b * MXU_COLS, (b + 1) * MXU_COLS)
            wu_bf[:, cols] = _dot(wu_ref[:, cols].astype(BF16), perm).astype(BF16)
        wd_bf[...] = wd_ref[...].astype(BF16)

        def tile(i, carry):
            slot = i % 2
            x_copy(i, slot).wait()

            @pl.when(i + 1 < n)
            def _():
                x_copy(i + 1, 1 - slot).start()

            @pl.when(i >= 2)
            def _():
                y_copy(i - 2, slot).wait()

            x = jnp.concatenate(_unpack_halves(xbuf[slot]), axis=-1)
            hid = _dot(x, wu_bf[...]) + bu_ref[...]
            acts = []
            for b in range(hid.shape[-1] // MXU_COLS):
                x_glu = jnp.minimum(hid[:, b * MXU_COLS:b * MXU_COLS + half], SWIGLU_LIMIT)
                x_lin = jnp.clip(hid[:, b * MXU_COLS + half:(b + 1) * MXU_COLS], -SWIGLU_LIMIT, SWIGLU_LIMIT)
                acts.append(x_glu * _sigmoid(SWIGLU_ALPHA * x_glu) * (x_lin + 1.0))
            act = jnp.concatenate(acts, axis=-1).astype(BF16)
            out = _dot(act, wd_bf[...]) + bd_ref[...]
            obuf[slot] = _pack_halves(out.astype(BF16).astype(F32))
            y_copy(i, slot).start()
            return carry

        lax.fori_loop(0, n, tile, 0)

        @pl.when(n >= 2)
        def _():
            y_copy(n - 2, n % 2).wait()

        y_copy(n - 1, (n - 1) % 2).wait()


def _experts(xs, tile_first, tiles_e, wu, bu, wd, bd, n_tiles):
    tm = TM_EXPERT
    E, D, F2 = wu.shape
    wsel = lambda e, *_: (e, 0, 0)
    return pl.pallas_call(
        _expert_body,
        grid_spec=pltpu.PrefetchScalarGridSpec(
            num_scalar_prefetch=2,
            grid=(E,),
            in_specs=[pl.BlockSpec(memory_space=pl.ANY),
                      pl.BlockSpec((None, D, F2), wsel),
                      pl.BlockSpec((None, 1, F2), wsel),
                      pl.BlockSpec((None, F2 // 2, D), wsel),
                      pl.BlockSpec((None, 1, D), wsel)],
            out_specs=pl.BlockSpec(memory_space=pl.ANY),
            scratch_shapes=[pltpu.VMEM((D, F2), BF16), pltpu.VMEM((F2 // 2, D), BF16),
                            pltpu.VMEM((2, tm, D // 2), U32), pltpu.VMEM((2, tm, D // 2), U32),
                            pltpu.SemaphoreType.DMA((2,)), pltpu.SemaphoreType.DMA((2,))]),
        out_shape=jax.ShapeDtypeStruct((n_tiles * tm, D // 2), U32),
        compiler_params=_params(("arbitrary",), vmem=VMEM_LIMIT_EXPERTS),
        name="experts",
    )(tile_first, tiles_e, xs, wu, bu, wd, bd)


def _combine_body(gch_ref, gch_next_ref, ys_ref, x1_ref, row_ref, gate_ref, o_ref, buf_ref, sems):
    s = pl.program_id(0)
    slot = s % 2
    nchunks = gch_ref.shape[-1]

    def gather(g_ref, sl):
        for c in range(nchunks):
            _chunk_copy(ys_ref, g_ref[0, 0, c], buf_ref.at[sl], c, sems.at[sl]).start()

    @pl.when(s == 0)
    def _():
        gather(gch_ref, 0)

    @pl.when(s < pl.num_programs(0) - 1)
    def _():
        gather(gch_next_ref, 1 - slot)

    for c in range(nchunks):
        _chunk_copy(ys_ref, 0, buf_ref.at[slot], c, sems.at[slot]).wait()
    rows = row_ref[...]
    g = gate_ref[...]
    ci = lax.broadcasted_iota(I32, (rows.shape[0], nchunks * SEG_ALIGN), 1)
    w = sum(jnp.where(ci == rows[:, k:k + 1], g[:, k:k + 1], 0.0) for k in range(TOP_K))
    w = w.astype(BF16)
    half = x1_ref.shape[-1] // 2
    lo, hi = _unpack_halves(buf_ref[slot])
    o_ref[:, :half] = x1_ref[:, :half] + _dot(w, lo)
    o_ref[:, half:] = x1_ref[:, half:] + _dot(w, hi)


def _combine(ys, gchunk, x1, rows, gates, ts):
    T, D = x1.shape
    nt = T // ts
    nchunks = gchunk.shape[-1]
    return pl.pallas_call(
        _combine_body,
        grid=(nt,),
        in_specs=[pl.BlockSpec((1, 1, nchunks), lambda s: (s, 0, 0), memory_space=pltpu.SMEM),
                  pl.BlockSpec((1, 1, nchunks), lambda s: (jnp.minimum(s + 1, nt - 1), 0, 0),
                               memory_space=pltpu.SMEM),
                  pl.BlockSpec(memory_space=pl.ANY),
                  pl.BlockSpec((ts, D), lambda s: (s, 0)),
                  pl.BlockSpec((ts, TOP_K), lambda s: (s, 0)),
                  pl.BlockSpec((ts, TOP_K), lambda s: (s, 0))],
        out_specs=pl.BlockSpec((ts, D), lambda s: (s, 0)),
        out_shape=jax.ShapeDtypeStruct((T, D), F32),
        scratch_shapes=[pltpu.VMEM((2, nchunks * SEG_ALIGN, D // 2), U32),
                        pltpu.SemaphoreType.DMA((2,))],
        compiler_params=_params(("arbitrary",)),
        name="combine",
    )(gchunk.reshape(nt, 1, nchunks), gchunk.reshape(nt, 1, nchunks), ys, x1, rows, gates)


def _excl_cumsum(a, axis):
    return jnp.cumsum(a, axis=axis) - a


def _routed_experts(h2, x1, idx, gates, cnt, w_up, b_up, w_down, b_down):
    T, D = x1.shape
    E = w_up.shape[0]
    ts = min(TS_SORT, T)
    nt = T // ts
    nchunks = (ts * TOP_K + E * SEG_ALIGN) // SEG_ALIGN
    max_rows = T * TOP_K + nt * E * (SEG_ALIGN - 1) + E * (TM_EXPERT - 1)
    n_tiles = -(-max_rows // TM_EXPERT)
    n_rows = n_tiles * TM_EXPERT
    dump_chunk = n_rows // SEG_ALIGN

    cnt = cnt[:, :min(TM_MERGE, T) // ts, :E].reshape(nt, E).astype(I32)
    seg = -(-cnt // SEG_ALIGN)
    loff = _excl_cumsum(seg, 1)
    chunks_e = jnp.sum(seg, axis=0)
    tiles_e = -(-(chunks_e * SEG_ALIGN) // TM_EXPERT)
    tile_end = jnp.cumsum(tiles_e)
    base = (tile_end - tiles_e) * (TM_EXPERT // SEG_ALIGN)
    gstart = base[None, :] + _excl_cumsum(seg, 0)
    c = jnp.arange(nchunks, dtype=I32)
    owner = jnp.sum((loff + seg)[:, None, :] <= c[None, :, None], axis=-1)
    onehot = owner[:, :, None] == jnp.arange(E, dtype=I32)[None, None, :]
    shift = jnp.sum(jnp.where(onehot, (gstart - loff)[:, None, :], 0), axis=-1)
    used = owner < E
    gchunk = shift + c[None, :]
    dump = dump_chunk + (jnp.arange(nt, dtype=I32)[:, None] % 2) * nchunks + c[None, :]
    gchunk_out = jnp.where(used, gchunk, dump).astype(I32)
    gchunk_in = jnp.where(used, gchunk, 0).astype(I32)
    fill_start = (base + chunks_e).astype(I32)
    fill_n = (tiles_e * (TM_EXPERT // SEG_ALIGN) - chunks_e).astype(I32)
    loff_col = jnp.zeros((nt, LANES, 1), F32).at[:, :E, 0].set((loff * SEG_ALIGN).astype(F32))

    xs, rowt = _dispatch(h2, idx.T, loff_col, gchunk_out, fill_start, fill_n,
                         n_rows + 2 * nchunks * SEG_ALIGN, ts)
    half = MXU_COLS // 2
    bu = b_up.reshape(E, -1, half, 2).transpose(0, 1, 3, 2).reshape(E, 1, -1)
    ys = _experts(xs, (tile_end - tiles_e).astype(I32), tiles_e.astype(I32), w_up, bu, w_down,
                  b_down[:, None, :], n_tiles)
    return _combine(ys, gchunk_in, x1, rowt.T, gates, ts)


def _layer(x, mem, mix_norm_w, mem_norm_w, w_in, swa_q_norm_w, swa_k_norm_w, swa_sinks, ret_norm_w,
           w_mem_kv, mem_q_norm_w, mem_k_norm_w, w_br_swa, w_br_ret, w_br_mem, w_out, ffn_norm_w,
           w_router, b_router, w_up, b_up, w_down, b_down):
    B, S, D = x.shape
    M = mem.shape[1]

    mk, mv = _memkv(mem.reshape(B * M, D), mem_norm_w, w_mem_kv.astype(BF16), mem_k_norm_w, B, M)
    E = w_router.shape[-1]
    w_router_pad = jnp.zeros((D, LANES), F32).at[:, :E].set(w_router)
    b_router_pad = jnp.full((1, LANES), NEG_INF, F32).at[0, :E].set(b_router)
    o_swa, o_ret, o_mem, zg = _mixers(x, mix_norm_w, w_in.astype(BF16), swa_q_norm_w, swa_k_norm_w, swa_sinks,
                                      ret_norm_w, mk, mv, mem_q_norm_w)
    x1, h2, idx, gates, cnt = _merge(
        x.reshape(B * S, D), o_swa, o_ret, o_mem, zg, w_br_swa.astype(BF16), w_br_ret.astype(BF16),
        w_br_mem.astype(BF16), w_out.astype(BF16), ffn_norm_w, w_router_pad, b_router_pad)

    out = _routed_experts(h2, x1, idx, gates, cnt, w_up, b_up, w_down, b_down)
    return out.reshape(B, S, D)


def kernel(x, mem, mix_norm_w, mem_norm_w, w_in, swa_q_norm_w, swa_k_norm_w, swa_sinks, ret_norm_w, w_mem_kv, mem_q_norm_w, mem_k_norm_w, w_br_swa, w_br_ret, w_br_mem, w_out, ffn_norm_w, w_router, b_router, w_up, b_up, w_down, b_down):
    args = (x, mem, mix_norm_w, mem_norm_w, w_in, swa_q_norm_w, swa_k_norm_w, swa_sinks, ret_norm_w,
            w_mem_kv, mem_q_norm_w, mem_k_norm_w, w_br_swa, w_br_ret, w_br_mem, w_out, ffn_norm_w,
            w_router, b_router, w_up, b_up, w_down, b_down)
    for l in range(w_in.shape[0]):
        x = _layer(x, mem, *[a[l] for a in args[2:]])
    return x
```

```python
import jax
import jax.numpy as jnp
from jax import lax
from jax.experimental import pallas as pl
from jax.experimental.pallas import tpu as pltpu

F32 = jnp.float32
BF16 = jnp.bfloat16
I32 = jnp.int32
U32 = jnp.uint32

D_MODEL = 1024
SWA_HEAD_DIM = 64
SWA_Q_HEADS = 8
SWA_KV_HEADS = 2
SWA_GROUP = SWA_Q_HEADS // SWA_KV_HEADS
WINDOW = 128
BAND_BLOCK = 128
RET_HEADS = 4
RET_QK_DIM = 128
RET_V_DIM = 256
RET_CHUNK = 128
ROPE_BASE = 10000.0
MEM_HEADS = 4
MEM_HEAD_DIM = 128
N_BRANCH = 3
N_EXPERTS = 32
TOP_K = 4
D_FF = 1024
SWIGLU_LIMIT = 7.0
SWIGLU_ALPHA = 1.702
EPS = 1e-6
NEG_INF = -1e30

SWA_Q_W = SWA_Q_HEADS * SWA_HEAD_DIM
SWA_KV_W = SWA_KV_HEADS * SWA_HEAD_DIM
RET_QK_W = RET_HEADS * RET_QK_DIM
RET_V_W = RET_HEADS * RET_V_DIM
MEM_W = MEM_HEADS * MEM_HEAD_DIM
IN_SIZES = (SWA_Q_W, SWA_KV_W, SWA_KV_W, RET_QK_W, RET_QK_W, RET_V_W, RET_V_W, MEM_W, N_BRANCH * D_MODEL)

SUBLANES = 8
LANES = 128
MXU_COLS = 256
VMEM_LIMIT = 56 * 1024 * 1024
VMEM_LIMIT_LARGE = 62 * 1024 * 1024

TM_MERGE = 1024
TM_EXPERT = 512
BATCH_PER_STEP = 4
TS_SORT = 256
SEG_ALIGN = SUBLANES
LOCAL_ROWS = TS_SORT * TOP_K + N_EXPERTS * SEG_ALIGN
LOCAL_CHUNKS = LOCAL_ROWS // SEG_ALIGN


def _params(sem, vmem=VMEM_LIMIT):
    return pltpu.CompilerParams(dimension_semantics=sem, vmem_limit_bytes=vmem)


def _const_spec(shape):
    nd = len(shape)
    return pl.BlockSpec(shape, lambda *_: (0,) * nd, pipeline_mode=pl.Buffered(1))


def _rms(x, w):
    return x * lax.rsqrt(jnp.mean(x * x, axis=-1, keepdims=True) + EPS) * w


def _sigmoid(x):
    return 0.5 * jnp.tanh(0.5 * x) + 0.5


def _split_bf16(a):
    hi = a.astype(BF16)
    lo = (a - hi.astype(F32)).astype(BF16)
    return hi, lo


def _pack_halves(a):
    n = a.shape[-1] // 2
    lo = lax.bitcast_convert_type(a[:, :n], U32) >> 16
    hi = lax.bitcast_convert_type(a[:, n:], U32) & jnp.uint32(0xFFFF0000)
    return lo | hi


def _unpack_halves(w):
    lo = lax.bitcast_convert_type(w << 16, F32)
    hi = lax.bitcast_convert_type(w & jnp.uint32(0xFFFF0000), F32)
    return lo.astype(BF16), hi.astype(BF16)


def _dot(a, b):
    return jnp.dot(a, b, preferred_element_type=F32)


def _dot_nt(a, b):
    return lax.dot_general(a, b, (((1,), (1,)), ((), ())), preferred_element_type=F32)


def _dot_tn(a, b):
    return lax.dot_general(a, b, (((0,), (0,)), ((), ())), preferred_element_type=F32)


def _head_rms(t, seg_ref, w):
    ss = _dot((t * t).astype(BF16), seg_ref[...])
    return t * lax.rsqrt(ss * (1.0 / SWA_HEAD_DIM) + EPS) * w


def _swa_block(sink_ref, q, kc, kp, vc, vp, qw_ref, kw_ref, segq_ref, segk_ref, mask):
    d = SWA_HEAD_DIM
    q = _head_rms(q.astype(F32), segq_ref, qw_ref[...]) * (d ** -0.5)
    q = q.astype(BF16)
    k = jnp.concatenate([kp, kc], axis=0).astype(F32)
    k = _head_rms(k, segk_ref, kw_ref[...])
    v = jnp.concatenate([vp, vc], axis=0).astype(F32)
    upper = lax.broadcasted_iota(I32, k.shape, 1) >= d
    k_sw = pltpu.roll(k, d, 1)
    v_sw = pltpu.roll(v, d, 1)

    def placed(t, t_sw):
        return [[jnp.where(upper if half == 1 else jnp.logical_not(upper), t if h == half else t_sw,
                           0.0).astype(BF16) for half in range(2)] for h in range(SWA_KV_HEADS)]

    k_at = placed(k, k_sw)
    v_at = placed(v, v_sw)
    pairs = []
    for j in range(SWA_Q_HEADS // 2):
        h = (2 * j) // SWA_GROUP
        qb = q[:, 2 * j * d:2 * (j + 1) * d]
        acc = None
        for half in range(2):
            s = jnp.where(mask, _dot_nt(qb, k_at[h][half]), NEG_INF)
            sink = sink_ref[2 * j + half]
            m = jnp.maximum(jnp.max(s, axis=-1, keepdims=True), sink)
            p = jnp.exp(s - m)
            den = jnp.sum(p, axis=-1, keepdims=True) + jnp.exp(sink - m)
            o = _dot(p.astype(BF16), v_at[h][half]) * (1.0 / den)
            acc = o if acc is None else acc + o
        pairs.append(acc)
    return jnp.concatenate(pairs, axis=-1)


def _ret_block(b, q, k, v, g, cos_ref, sin_ref, dm_ref, qd_ref, kd_ref, cd_ref, nw_ref, o_ref, st_ref):
    cos = cos_ref[...]
    sin = sin_ref[...]
    dk, dv = RET_QK_DIM, RET_V_DIM
    outs = []
    for h in range(RET_HEADS):
        qh = q[b, :, h * dk:(h + 1) * dk].astype(F32)
        kh = k[b, :, h * dk:(h + 1) * dk].astype(F32)
        qr = qh * cos + pltpu.roll(qh, dk // 2, 1) * sin
        kr = (kh * cos + pltpu.roll(kh, dk // 2, 1) * sin) * (dk ** -0.5)
        vh = v[b, :, h * dv:(h + 1) * dv]
        st = st_ref[b, h]
        inner = _dot_nt(qr.astype(BF16), kr.astype(BF16)) * dm_ref[h]
        o = _dot(inner.astype(BF16), vh) + _dot((qr * qd_ref[h]).astype(BF16), st.astype(BF16))
        st_ref[b, h] = st * cd_ref[h] + _dot_tn((kr * kd_ref[h]).astype(BF16), vh)
        mu = jnp.mean(o, axis=-1, keepdims=True)
        oc = o - mu
        var = jnp.mean(oc * oc, axis=-1, keepdims=True)
        y = oc * lax.rsqrt(var + EPS) * nw_ref[:, h * dv:(h + 1) * dv]
        gh = g[b, :, h * dv:(h + 1) * dv].astype(F32)
        outs.append(gh * _sigmoid(gh) * y)
    o_ref[b] = jnp.concatenate(outs, axis=-1).astype(o_ref.dtype)


def _memkv_body(m_ref, nw_ref, w_ref, kw_ref, k_ref, v_ref):
    m = _rms(m_ref[...], nw_ref[...]).astype(BF16)
    kv = _dot(m, w_ref[...])
    d = MEM_HEAD_DIM
    ks = [_rms(kv[:, h * d:(h + 1) * d], kw_ref[...]) for h in range(MEM_HEADS)]
    k_ref[...] = jnp.concatenate(ks, axis=-1).astype(k_ref.dtype)
    v_ref[...] = kv[:, MEM_W:].astype(v_ref.dtype)


def _memkv(mem2, mem_norm_w, w_kv_bf, k_norm_w, B, M):
    D = mem2.shape[-1]
    return pl.pallas_call(
        _memkv_body,
        grid=(B,),
        in_specs=[pl.BlockSpec((M, D), lambda b: (b, 0)),
                  _const_spec((1, D)),
                  _const_spec(w_kv_bf.shape),
                  _const_spec((1, MEM_HEAD_DIM))],
        out_specs=[pl.BlockSpec((M, MEM_W), lambda b: (b, 0))] * 2,
        out_shape=[jax.ShapeDtypeStruct((B * M, MEM_W), BF16)] * 2,
        compiler_params=_params(("parallel",)),
        name="memkv",
    )(mem2, mem_norm_w.reshape(1, D), w_kv_bf, k_norm_w.reshape(1, -1))


def _mem_block(q, k_ref, v_ref, b, qw_ref):
    d = MEM_HEAD_DIM
    outs = []
    for h in range(MEM_HEADS):
        qh = _rms(q[:, h * d:(h + 1) * d].astype(F32), qw_ref[...]).astype(BF16)
        s = _dot_nt(qh, k_ref[b, :, h * d:(h + 1) * d]) * (d ** -0.5)
        m = jnp.max(s, axis=-1, keepdims=True)
        p = jnp.exp(s - m)
        o = _dot(p.astype(BF16), v_ref[b, :, h * d:(h + 1) * d])
        outs.append(o * (1.0 / jnp.sum(p, axis=-1, keepdims=True)))
    return jnp.concatenate(outs, axis=-1)


def _mixers_body(sink_ref, cd_ref, x_ref, nw_ref, w_ref, sqw_ref, skw_ref, segq_ref, segk_ref,
                 cos_ref, sin_ref, dm_ref, qd_ref, kd_ref, rnw_ref, mk_ref, mv_ref, mqw_ref,
                 osw_ref, ort_ref, omm_ref, zg_ref, st_ref, kprev_ref, vprev_ref):
    n = pl.program_id(1)
    bs, L, D = x_ref.shape

    @pl.when(n == 0)
    def _():
        st_ref[...] = jnp.zeros_like(st_ref)
        kprev_ref[...] = jnp.zeros_like(kprev_ref)
        vprev_ref[...] = jnp.zeros_like(vprev_ref)

    h = _rms(x_ref[...].reshape(bs * L, D), nw_ref[...]).astype(BF16)
    offs = [sum(IN_SIZES[:i]) for i in range(len(IN_SIZES) + 1)]

    def proj(i):
        z = _dot(h, w_ref[:, offs[i]:offs[i + 1]]).astype(BF16)
        return z.reshape(bs, L, IN_SIZES[i])

    sq, sk, sv = proj(0), proj(1), proj(2)
    qi = lax.broadcasted_iota(I32, (L, 2 * L), 0)
    kj = lax.broadcasted_iota(I32, (L, 2 * L), 1)
    diff = L + qi - kj
    has_prev = jnp.minimum(n, 1) * L
    mask = (diff >= 0) & (diff < WINDOW) & (kj + has_prev >= L)
    for b in range(bs):
        osw_ref[b] = _swa_block(sink_ref, sq[b], sk[b], kprev_ref[b], sv[b], vprev_ref[b],
                                sqw_ref, skw_ref, segq_ref, segk_ref, mask).astype(osw_ref.dtype)
    kprev_ref[...] = sk
    vprev_ref[...] = sv
    rq, rk, rv, rg = proj(3), proj(4), proj(5), proj(6)
    for b in range(bs):
        _ret_block(b, rq, rk, rv, rg, cos_ref, sin_ref, dm_ref, qd_ref, kd_ref, cd_ref, rnw_ref, ort_ref, st_ref)
    mq = proj(7)
    for b in range(bs):
        omm_ref[b] = _mem_block(mq[b], mk_ref, mv_ref, b, mqw_ref).astype(omm_ref.dtype)
    zg_ref[...] = proj(8)


def _mixers(x, mix_norm_w, w_in_bf, swa_q_norm_w, swa_k_norm_w, sinks, ret_norm_w, mk, mv, mem_q_norm_w):
    B, S, D = x.shape
    M = mk.shape[0] // B
    L = BAND_BLOCK
    nb = S // L
    bs = BATCH_PER_STEP if B % BATCH_PER_STEP == 0 else 1
    d = SWA_HEAD_DIM
    group = jnp.arange(SWA_Q_W, dtype=I32) // d
    seg_q = (group[:, None] == group[None, :]).astype(BF16)
    seg_k = seg_q[:SWA_KV_W, :SWA_KV_W]
    half = RET_QK_DIM // 2
    inv = 1.0 / (ROPE_BASE ** (jnp.arange(0, half, dtype=F32) / half))
    ang = jnp.arange(S, dtype=F32)[:, None] * inv[None, :]
    cos = jnp.cos(ang)
    sin = jnp.sin(ang)
    cos2 = jnp.concatenate([cos, cos], axis=-1)
    sin2 = jnp.concatenate([-sin, sin], axis=-1)
    log_g = jnp.log1p(-(2.0 ** (-5.0 - jnp.arange(RET_HEADS, dtype=F32))))
    pos = jnp.arange(L, dtype=F32)
    dpos = pos[:, None] - pos[None, :]
    decay_mask = jnp.where(dpos[None] >= 0,
                           jnp.exp(jnp.maximum(dpos, 0.0)[None] * log_g[:, None, None]), 0.0)
    q_decay = jnp.exp((pos[None, :] + 1.0) * log_g[:, None])[..., None]
    k_decay = jnp.exp((L - 1.0 - pos[None, :]) * log_g[:, None])[..., None]
    chunk_decay = jnp.exp(L * log_g)
    qd = jnp.broadcast_to(q_decay, (RET_HEADS, L, RET_QK_DIM))
    kd = jnp.broadcast_to(k_decay, (RET_HEADS, L, RET_QK_DIM))
    blk = lambda w: pl.BlockSpec((bs, L, w), lambda b, n: (b, n, 0))
    smem = pl.BlockSpec(memory_space=pltpu.SMEM)
    outs = pl.pallas_call(
        _mixers_body,
        grid=(B // bs, nb),
        in_specs=[smem, smem,
                  blk(D), _const_spec((1, D)), _const_spec(w_in_bf.shape),
                  _const_spec((1, SWA_Q_W)), _const_spec((1, SWA_KV_W)),
                  _const_spec(seg_q.shape), _const_spec(seg_k.shape),
                  pl.BlockSpec((L, RET_QK_DIM), lambda b, n: (n, 0)),
                  pl.BlockSpec((L, RET_QK_DIM), lambda b, n: (n, 0)),
                  _const_spec((RET_HEADS, L, L)),
                  _const_spec((RET_HEADS, L, RET_QK_DIM)),
                  _const_spec((RET_HEADS, L, RET_QK_DIM)),
                  _const_spec((1, RET_V_W)),
                  pl.BlockSpec((bs, M, MEM_W), lambda b, n: (b, 0, 0)),
                  pl.BlockSpec((bs, M, MEM_W), lambda b, n: (b, 0, 0)),
                  _const_spec((1, MEM_HEAD_DIM))],
        out_specs=[blk(SWA_Q_W), blk(RET_V_W), blk(MEM_W), blk(N_BRANCH * D)],
        out_shape=[jax.ShapeDtypeStruct((B, S, w), BF16) for w in (SWA_Q_W, RET_V_W, MEM_W, N_BRANCH * D)],
        scratch_shapes=[pltpu.VMEM((bs, RET_HEADS, RET_QK_DIM, RET_V_DIM), F32),
                        pltpu.VMEM((bs, L, SWA_KV_W), BF16),
                        pltpu.VMEM((bs, L, SWA_KV_W), BF16)],
        compiler_params=_params(("parallel", "arbitrary"), vmem=VMEM_LIMIT_LARGE),
        name="mixers",
    )(sinks.astype(F32), chunk_decay, x, mix_norm_w.reshape(1, D), w_in_bf,
      jnp.tile(swa_q_norm_w, SWA_Q_HEADS).reshape(1, -1), jnp.tile(swa_k_norm_w, SWA_KV_HEADS).reshape(1, -1),
      seg_q, seg_k, cos2, sin2, decay_mask, qd, kd, ret_norm_w.reshape(1, -1),
      mk.reshape(B, M, MEM_W), mv.reshape(B, M, MEM_W), mem_q_norm_w.reshape(1, -1))
    return [o.reshape(B * S, o.shape[-1]) for o in outs]


def _merge_body(x_ref, osw_ref, ort_ref, omm_ref, zg_ref, wa_ref, wr_ref, wm_ref, wo_ref, nw_ref,
                wrt_ref, brt_ref, x1_ref, h2_ref, idx_ref, gate_ref, cnt_ref):
    D = D_MODEL
    sg = lambda j: _sigmoid(zg_ref[:, j * D:(j + 1) * D].astype(F32))
    merged = (sg(0) * _dot(osw_ref[...], wa_ref[...])
              + sg(1) * _dot(ort_ref[...], wr_ref[...])
              + sg(2) * _dot(omm_ref[...], wm_ref[...]))
    x1 = x_ref[...] + _dot(merged.astype(BF16), wo_ref[...])
    x1_ref[...] = x1
    h2 = _rms(x1, nw_ref[...])
    h2_ref[...] = h2.astype(h2_ref.dtype)
    h_hi, h_lo = _split_bf16(h2)
    w_hi, w_lo = _split_bf16(wrt_ref[...])
    logits = _dot(h_hi, w_hi) + _dot(h_lo, w_hi) + _dot(h_hi, w_lo) + brt_ref[...]
    lane = lax.broadcasted_iota(I32, logits.shape, 1)
    vals, idxs = [], []
    l = logits
    for _ in range(TOP_K):
        m = jnp.max(l, axis=-1, keepdims=True)
        i = jnp.min(jnp.where(l == m, lane, LANES), axis=-1, keepdims=True)
        vals.append(m)
        idxs.append(i)
        l = jnp.where(lane == i, -jnp.inf, l)
    es = [jnp.exp(v - vals[0]) for v in vals]
    den = es[0] + es[1] + es[2] + es[3]
    idx_ref[...] = jnp.concatenate(idxs, axis=-1)
    gate_ref[...] = jnp.concatenate([e / den for e in es], axis=-1)
    chosen = sum((lane == i).astype(F32) for i in idxs)
    ts = min(TS_SORT, chosen.shape[0])
    per_tile = [jnp.sum(chosen[j * ts:(j + 1) * ts], axis=0, keepdims=True) for j in range(chosen.shape[0] // ts)]
    per_tile.append(jnp.zeros((SUBLANES - len(per_tile), LANES), F32))
    cnt_ref[...] = jnp.concatenate(per_tile, axis=0)


def _merge(x2, o_swa, o_ret, o_mem, zg, wa, wr, wm, wo, ffn_norm_w, w_router_pad, b_router_pad):
    T, D = x2.shape
    tm = min(TM_MERGE, T)
    row = lambda i: (i, 0)
    return pl.pallas_call(
        _merge_body,
        grid=(T // tm,),
        in_specs=[pl.BlockSpec((tm, D), row),
                  pl.BlockSpec((tm, SWA_Q_W), row),
                  pl.BlockSpec((tm, RET_V_W), row),
                  pl.BlockSpec((tm, MEM_W), row),
                  pl.BlockSpec((tm, N_BRANCH * D), row),
                  _const_spec(wa.shape), _const_spec(wr.shape), _const_spec(wm.shape), _const_spec(wo.shape),
                  _const_spec((1, D)),
                  _const_spec(w_router_pad.shape), _const_spec((1, LANES))],
        out_specs=[pl.BlockSpec((tm, D), row),
                   pl.BlockSpec((tm, D), row),
                   pl.BlockSpec((tm, TOP_K), row),
                   pl.BlockSpec((tm, TOP_K), row),
                   pl.BlockSpec((None, SUBLANES, LANES), lambda i: (i, 0, 0))],
        out_shape=[jax.ShapeDtypeStruct((T, D), F32),
                   jax.ShapeDtypeStruct((T, D), BF16),
                   jax.ShapeDtypeStruct((T, TOP_K), I32),
                   jax.ShapeDtypeStruct((T, TOP_K), F32),
                   jax.ShapeDtypeStruct((T // tm, SUBLANES, LANES), F32)],
        compiler_params=_params(("parallel",)),
        name="merge",
    )(x2, o_swa, o_ret, o_mem, zg, wa, wr, wm, wo, ffn_norm_w.reshape(1, D), w_router_pad, b_router_pad)


def _chunk_copy(src_ref, src_chunk, dst_ref, dst_chunk, sem):
    def rows(c):
        start = c * SEG_ALIGN
        return pl.ds(start if isinstance(c, int) else pl.multiple_of(start, SEG_ALIGN), SEG_ALIGN)

    return pltpu.make_async_copy(src_ref.at[rows(src_chunk), :], dst_ref.at[rows(dst_chunk), :], sem)


def _dispatch_body(fill_start_ref, fill_n_ref, gch_ref, h2_ref, idxt_ref, loff_ref,
                   xs_ref, rowt_ref, buf_ref, zero_ref, sems, zsem):
    s = pl.program_id(0)
    last = pl.num_programs(0) - 1
    slot = s % 2
    ts = h2_ref.shape[0]
    nchunks = gch_ref.shape[-1]
    local_rows = nchunks * SEG_ALIGN

    def wait_slot(sl):
        for c in range(nchunks):
            _chunk_copy(buf_ref.at[sl], c, xs_ref, c, sems.at[sl]).wait()

    @pl.when(s == 0)
    def _():
        zero_ref[...] = jnp.zeros_like(zero_ref)

        def per_expert(e, total):
            def fill(r, _):
                _chunk_copy(zero_ref, 0, xs_ref, fill_start_ref[e] + r, zsem).start()
                return 0

            lax.fori_loop(0, fill_n_ref[e], fill, 0)
            return total + fill_n_ref[e]

        total = lax.fori_loop(0, N_EXPERTS, per_expert, 0)

        def drain(r, _):
            _chunk_copy(zero_ref, 0, xs_ref, 0, zsem).wait()
            return 0

        lax.fori_loop(0, total, drain, 0)

    @pl.when(s >= 2)
    def _():
        wait_slot(slot)

    idxt = idxt_ref[...]
    sub = lax.broadcasted_iota(I32, (LANES, ts), 0)
    ohs = [(sub == idxt[k:k + 1, :]).astype(F32) for k in range(TOP_K)]
    m = ohs[0] + ohs[1] + ohs[2] + ohs[3]
    tr = lax.broadcasted_iota(I32, (ts, ts), 0)
    tc = lax.broadcasted_iota(I32, (ts, ts), 1)
    earlier = jnp.where(tr < tc, 1.0, 0.0).astype(BF16)
    before = _dot(m.astype(BF16), earlier) + loff_ref[...]
    rows = [jnp.sum(oh * before, axis=0, keepdims=True).astype(I32) for oh in ohs]
    rowt_ref[...] = jnp.concatenate(rows, axis=0)
    ri = lax.broadcasted_iota(I32, (local_rows, ts), 0)
    sel = (ri == rows[0]) | (ri == rows[1]) | (ri == rows[2]) | (ri == rows[3])
    buf_ref[slot] = _pack_halves(_dot(jnp.where(sel, 1.0, 0.0).astype(BF16), h2_ref[...]))
    for c in range(nchunks):
        _chunk_copy(buf_ref.at[slot], c, xs_ref, gch_ref[0, 0, c], sems.at[slot]).start()

    @pl.when(s == last)
    def _():
        wait_slot(slot)

        @pl.when(s >= 1)
        def _():
            wait_slot(1 - slot)


def _dispatch(h2, idxt, loff_col, gchunk, fill_start, fill_n, n_rows, ts):
    T, D = h2.shape
    nt = T // ts
    nchunks = gchunk.shape[-1]
    return pl.pallas_call(
        _dispatch_body,
        grid_spec=pltpu.PrefetchScalarGridSpec(
            num_scalar_prefetch=2,
            grid=(nt,),
            in_specs=[pl.BlockSpec((1, 1, nchunks), lambda s, *_: (s, 0, 0), memory_space=pltpu.SMEM),
                      pl.BlockSpec((ts, D), lambda s, *_: (s, 0)),
                      pl.BlockSpec((TOP_K, ts), lambda s, *_: (0, s)),
                      pl.BlockSpec((None, LANES, 1), lambda s, *_: (s, 0, 0))],
            out_specs=[pl.BlockSpec(memory_space=pl.ANY),
                       pl.BlockSpec((TOP_K, ts), lambda s, *_: (0, s))],
            scratch_shapes=[pltpu.VMEM((2, nchunks * SEG_ALIGN, D // 2), U32),
                            pltpu.VMEM((SEG_ALIGN, D // 2), U32),
                            pltpu.SemaphoreType.DMA((2,)),
                            pltpu.SemaphoreType.DMA(())]),
        out_shape=[jax.ShapeDtypeStruct((n_rows, D // 2), U32),
                   jax.ShapeDtypeStruct((TOP_K, T), I32)],
        compiler_params=_params(("arbitrary",)),
        name="dispatch",
    )(fill_start, fill_n, gchunk.reshape(nt, 1, nchunks), h2, idxt, loff_col)


def _expert_body(te_ref, nu_ref, x_ref, wu_ref, bu_ref, wd_ref, bd_ref, o_ref, wu_bf, wd_bf):
    i = pl.program_id(0)
    active = i < nu_ref[0]
    new_expert = jnp.logical_or(i == 0, te_ref[i] != te_ref[jnp.maximum(i - 1, 0)])
    half = MXU_COLS // 2

    @pl.when(jnp.logical_and(active, new_expert))
    def _():
        r = lax.broadcasted_iota(I32, (MXU_COLS, MXU_COLS), 0)
        c = lax.broadcasted_iota(I32, (MXU_COLS, MXU_COLS), 1)
        perm = jnp.where(r == jnp.where(c < half, 2 * c, 2 * (c - half) + 1), 1.0, 0.0).astype(BF16)
        for b in range(wu_ref.shape[-1] // MXU_COLS):
            cols = slice(b * MXU_COLS, (b + 1) * MXU_COLS)
            wu_bf[:, cols] = _dot(wu_ref[:, cols].astype(BF16), perm).astype(BF16)
        wd_bf[...] = wd_ref[...].astype(BF16)

    @pl.when(active)
    def _():
        x = jnp.concatenate(_unpack_halves(x_ref[...]), axis=-1)
        hid = _dot(x, wu_bf[...]) + bu_ref[...]
        acts = []
        for b in range(hid.shape[-1] // MXU_COLS):
            x_glu = jnp.minimum(hid[:, b * MXU_COLS:b * MXU_COLS + half], SWIGLU_LIMIT)
            x_lin = jnp.clip(hid[:, b * MXU_COLS + half:(b + 1) * MXU_COLS], -SWIGLU_LIMIT, SWIGLU_LIMIT)
            acts.append(x_glu * _sigmoid(SWIGLU_ALPHA * x_glu) * (x_lin + 1.0))
        act = jnp.concatenate(acts, axis=-1).astype(BF16)
        out = _dot(act, wd_bf[...]) + bd_ref[...]
        o_ref[...] = _pack_halves(out.astype(BF16).astype(F32))


def _experts(xs, tile_expert, n_used, wu, bu, wd, bd, n_tiles):
    tm = TM_EXPERT
    D = wu.shape[1]
    F2 = wu.shape[-1]
    rows = lambda i, te, nu: (jnp.minimum(i, nu[0] - 1), 0)
    wsel = lambda i, te, nu: (te[i], 0, 0)
    return pl.pallas_call(
        _expert_body,
        grid_spec=pltpu.PrefetchScalarGridSpec(
            num_scalar_prefetch=2,
            grid=(n_tiles,),
            in_specs=[pl.BlockSpec((tm, D // 2), rows),
                      pl.BlockSpec((None, D, F2), wsel),
                      pl.BlockSpec((None, 1, F2), wsel),
                      pl.BlockSpec((None, F2 // 2, D), wsel),
                      pl.BlockSpec((None, 1, D), wsel)],
            out_specs=pl.BlockSpec((tm, D // 2), rows),
            scratch_shapes=[pltpu.VMEM((D, F2), BF16), pltpu.VMEM((F2 // 2, D), BF16)]),
        out_shape=jax.ShapeDtypeStruct((n_tiles * tm, D // 2), U32),
        compiler_params=_params(("arbitrary",), vmem=VMEM_LIMIT_LARGE),
        name="experts",
    )(tile_expert, n_used, xs, wu, bu, wd, bd)


def _combine_body(gch_ref, gch_next_ref, ys_ref, x1_ref, row_ref, gate_ref, o_ref, buf_ref, sems):
    s = pl.program_id(0)
    slot = s % 2
    nchunks = gch_ref.shape[-1]

    def gather(g_ref, sl):
        for c in range(nchunks):
            _chunk_copy(ys_ref, g_ref[0, 0, c], buf_ref.at[sl], c, sems.at[sl]).start()

    @pl.when(s == 0)
    def _():
        gather(gch_ref, 0)

    @pl.when(s < pl.num_programs(0) - 1)
    def _():
        gather(gch_next_ref, 1 - slot)

    for c in range(nchunks):
        _chunk_copy(ys_ref, 0, buf_ref.at[slot], c, sems.at[slot]).wait()
    rows = row_ref[...]
    g = gate_ref[...]
    ci = lax.broadcasted_iota(I32, (rows.shape[0], nchunks * SEG_ALIGN), 1)
    w = sum(jnp.where(ci == rows[:, k:k + 1], g[:, k:k + 1], 0.0) for k in range(TOP_K))
    w = w.astype(BF16)
    half = x1_ref.shape[-1] // 2
    lo, hi = _unpack_halves(buf_ref[slot])
    o_ref[:, :half] = x1_ref[:, :half] + _dot(w, lo)
    o_ref[:, half:] = x1_ref[:, half:] + _dot(w, hi)


def _combine(ys, gchunk, x1, rows, gates, ts):
    T, D = x1.shape
    nt = T // ts
    nchunks = gchunk.shape[-1]
    return pl.pallas_call(
        _combine_body,
        grid=(nt,),
        in_specs=[pl.BlockSpec((1, 1, nchunks), lambda s: (s, 0, 0), memory_space=pltpu.SMEM),
                  pl.BlockSpec((1, 1, nchunks), lambda s: (jnp.minimum(s + 1, nt - 1), 0, 0),
                               memory_space=pltpu.SMEM),
                  pl.BlockSpec(memory_space=pl.ANY),
                  pl.BlockSpec((ts, D), lambda s: (s, 0)),
                  pl.BlockSpec((ts, TOP_K), lambda s: (s, 0)),
                  pl.BlockSpec((ts, TOP_K), lambda s: (s, 0))],
        out_specs=pl.BlockSpec((ts, D), lambda s: (s, 0)),
        out_shape=jax.ShapeDtypeStruct((T, D), F32),
        scratch_shapes=[pltpu.VMEM((2, nchunks * SEG_ALIGN, D // 2), U32),
                        pltpu.SemaphoreType.DMA((2,))],
        compiler_params=_params(("arbitrary",)),
        name="combine",
    )(gchunk.reshape(nt, 1, nchunks), gchunk.reshape(nt, 1, nchunks), ys, x1, rows, gates)


def _excl_cumsum(a, axis):
    return jnp.cumsum(a, axis=axis) - a


def _routed_experts(h2, x1, idx, gates, cnt, w_up, b_up, w_down, b_down):
    T, D = x1.shape
    E = w_up.shape[0]
    ts = min(TS_SORT, T)
    nt = T // ts
    nchunks = (ts * TOP_K + E * SEG_ALIGN) // SEG_ALIGN
    max_rows = T * TOP_K + nt * E * (SEG_ALIGN - 1) + E * (TM_EXPERT - 1)
    n_tiles = -(-max_rows // TM_EXPERT)
    n_rows = n_tiles * TM_EXPERT
    dump_chunk = n_rows // SEG_ALIGN

    cnt = cnt[:, :min(TM_MERGE, T) // ts, :E].reshape(nt, E).astype(I32)
    seg = -(-cnt // SEG_ALIGN)
    loff = _excl_cumsum(seg, 1)
    chunks_e = jnp.sum(seg, axis=0)
    tiles_e = -(-(chunks_e * SEG_ALIGN) // TM_EXPERT)
    tile_end = jnp.cumsum(tiles_e)
    base = (tile_end - tiles_e) * (TM_EXPERT // SEG_ALIGN)
    gstart = base[None, :] + _excl_cumsum(seg, 0)
    c = jnp.arange(nchunks, dtype=I32)
    owner = jnp.sum((loff + seg)[:, None, :] <= c[None, :, None], axis=-1)
    onehot = owner[:, :, None] == jnp.arange(E, dtype=I32)[None, None, :]
    shift = jnp.sum(jnp.where(onehot, (gstart - loff)[:, None, :], 0), axis=-1)
    used = owner < E
    gchunk = shift + c[None, :]
    dump = dump_chunk + (jnp.arange(nt, dtype=I32)[:, None] % 2) * nchunks + c[None, :]
    gchunk_out = jnp.where(used, gchunk, dump).astype(I32)
    gchunk_in = jnp.where(used, gchunk, 0).astype(I32)
    fill_start = (base + chunks_e).astype(I32)
    fill_n = (tiles_e * (TM_EXPERT // SEG_ALIGN) - chunks_e).astype(I32)
    n_used = tile_end[-1:].astype(I32)
    tile_ids = jnp.minimum(jnp.arange(n_tiles, dtype=I32), n_used[0] - 1)
    tile_expert = jnp.minimum(jnp.sum(tile_ids[:, None] >= tile_end[None, :], axis=-1), E - 1).astype(I32)
    loff_col = jnp.zeros((nt, LANES, 1), F32).at[:, :E, 0].set((loff * SEG_ALIGN).astype(F32))

    xs, rowt = _dispatch(h2, idx.T, loff_col, gchunk_out, fill_start, fill_n,
                         n_rows + 2 * nchunks * SEG_ALIGN, ts)
    half = MXU_COLS // 2
    bu = b_up.reshape(E, -1, half, 2).transpose(0, 1, 3, 2).reshape(E, 1, -1)
    ys = _experts(xs, tile_expert, n_used, w_up, bu, w_down, b_down[:, None, :], n_tiles)
    return _combine(ys, gchunk_in, x1, rowt.T, gates, ts)


def _layer(x, mem, mix_norm_w, mem_norm_w, w_in, swa_q_norm_w, swa_k_norm_w, swa_sinks, ret_norm_w,
           w_mem_kv, mem_q_norm_w, mem_k_norm_w, w_br_swa, w_br_ret, w_br_mem, w_out, ffn_norm_w,
           w_router, b_router, w_up, b_up, w_down, b_down):
    B, S, D = x.shape
    M = mem.shape[1]
    T = B * S

    mk, mv = _memkv(mem.reshape(B * M, D), mem_norm_w, w_mem_kv.astype(BF16), mem_k_norm_w, B, M)
    o_swa, o_ret, o_mem, zg = _mixers(x, mix_norm_w, w_in.astype(BF16), swa_q_norm_w, swa_k_norm_w, swa_sinks,
                                      ret_norm_w, mk, mv, mem_q_norm_w)

    E = w_router.shape[-1]
    w_router_pad = jnp.zeros((D, LANES), F32).at[:, :E].set(w_router)
    b_router_pad = jnp.full((1, LANES), NEG_INF, F32).at[0, :E].set(b_router)
    x1, h2, idx, gates, cnt = _merge(
        x.reshape(T, D), o_swa, o_ret, o_mem, zg, w_br_swa.astype(BF16), w_br_ret.astype(BF16),
        w_br_mem.astype(BF16), w_out.astype(BF16), ffn_norm_w, w_router_pad, b_router_pad)

    out = _routed_experts(h2, x1, idx, gates, cnt, w_up, b_up, w_down, b_down)
    return out.reshape(B, S, D)


def kernel(x, mem, mix_norm_w, mem_norm_w, w_in, swa_q_norm_w, swa_k_norm_w, swa_sinks, ret_norm_w, w_mem_kv, mem_q_norm_w, mem_k_norm_w, w_br_swa, w_br_ret, w_br_mem, w_out, ffn_norm_w, w_router, b_router, w_up, b_up, w_down, b_down):
    args = (x, mem, mix_norm_w, mem_norm_w, w_in, swa_q_norm_w, swa_k_norm_w, swa_sinks, ret_norm_w,
            w_mem_kv, mem_q_norm_w, mem_k_norm_w, w_br_swa, w_br_ret, w_br_mem, w_out, ffn_norm_w,
            w_router, b_router, w_up, b_up, w_down, b_down)
    for l in range(w_in.shape[0]):
        x = _layer(x, mem, *[a[l] for a in args[2:]])
    return x
```

```python
import jax
import jax.numpy as jnp
from jax import lax
from jax.experimental import pallas as pl
from jax.experimental.pallas import tpu as pltpu

F32 = jnp.float32
BF16 = jnp.bfloat16
I32 = jnp.int32
U32 = jnp.uint32

D_MODEL = 1024
SWA_HEAD_DIM = 64
SWA_Q_HEADS = 8
SWA_KV_HEADS = 2
SWA_GROUP = SWA_Q_HEADS // SWA_KV_HEADS
WINDOW = 128
BAND_BLOCK = 128
RET_HEADS = 4
RET_QK_DIM = 128
RET_V_DIM = 256
RET_CHUNK = 128
ROPE_BASE = 10000.0
MEM_HEADS = 4
MEM_HEAD_DIM = 128
N_BRANCH = 3
N_EXPERTS = 32
TOP_K = 4
D_FF = 1024
SWIGLU_LIMIT = 7.0
SWIGLU_ALPHA = 1.702
EPS = 1e-6
NEG_INF = -1e30

SWA_Q_W = SWA_Q_HEADS * SWA_HEAD_DIM
SWA_KV_W = SWA_KV_HEADS * SWA_HEAD_DIM
RET_QK_W = RET_HEADS * RET_QK_DIM
RET_V_W = RET_HEADS * RET_V_DIM
MEM_W = MEM_HEADS * MEM_HEAD_DIM
IN_SIZES = (SWA_Q_W, SWA_KV_W, SWA_KV_W, RET_QK_W, RET_QK_W, RET_V_W, RET_V_W, MEM_W, N_BRANCH * D_MODEL)

SUBLANES = 8
LANES = 128
MXU_COLS = 256
VMEM_LIMIT = 56 * 1024 * 1024
VMEM_LIMIT_LARGE = 62 * 1024 * 1024

TM_MERGE = 1024
TM_EXPERT = 512
BATCH_PER_STEP = 4
TS_SORT = 256
SEG_ALIGN = SUBLANES
LOCAL_ROWS = TS_SORT * TOP_K + N_EXPERTS * SEG_ALIGN
LOCAL_CHUNKS = LOCAL_ROWS // SEG_ALIGN


def _params(sem, vmem=VMEM_LIMIT):
    return pltpu.CompilerParams(dimension_semantics=sem, vmem_limit_bytes=vmem)


def _const_spec(shape):
    nd = len(shape)
    return pl.BlockSpec(shape, lambda *_: (0,) * nd, pipeline_mode=pl.Buffered(1))


def _rms(x, w):
    return x * lax.rsqrt(jnp.mean(x * x, axis=-1, keepdims=True) + EPS) * w


def _sigmoid(x):
    return 0.5 * jnp.tanh(0.5 * x) + 0.5


def _split_bf16(a):
    hi = a.astype(BF16)
    lo = (a - hi.astype(F32)).astype(BF16)
    return hi, lo


def _pack_halves(a):
    n = a.shape[-1] // 2
    lo = lax.bitcast_convert_type(a[:, :n], U32) >> 16
    hi = lax.bitcast_convert_type(a[:, n:], U32) & jnp.uint32(0xFFFF0000)
    return lo | hi


def _unpack_halves(w):
    lo = lax.bitcast_convert_type(w << 16, F32)
    hi = lax.bitcast_convert_type(w & jnp.uint32(0xFFFF0000), F32)
    return lo.astype(BF16), hi.astype(BF16)


def _dot(a, b):
    return jnp.dot(a, b, preferred_element_type=F32)


def _dot_nt(a, b):
    return lax.dot_general(a, b, (((1,), (1,)), ((), ())), preferred_element_type=F32)


def _dot_tn(a, b):
    return lax.dot_general(a, b, (((0,), (0,)), ((), ())), preferred_element_type=F32)


def _head_rms(t, seg_ref, w):
    ss = _dot((t * t).astype(BF16), seg_ref[...])
    return t * lax.rsqrt(ss * (1.0 / SWA_HEAD_DIM) + EPS) * w


def _swa_block(sink_ref, q, kc, kp, vc, vp, qw_ref, kw_ref, segq_ref, segk_ref, mask):
    d = SWA_HEAD_DIM
    q = _head_rms(q.astype(F32), segq_ref, qw_ref[...]) * (d ** -0.5)
    q = q.astype(BF16)
    k = jnp.concatenate([kp, kc], axis=0).astype(F32)
    k = _head_rms(k, segk_ref, kw_ref[...])
    v = jnp.concatenate([vp, vc], axis=0).astype(F32)
    upper = lax.broadcasted_iota(I32, k.shape, 1) >= d
    k_sw = pltpu.roll(k, d, 1)
    v_sw = pltpu.roll(v, d, 1)

    def placed(t, t_sw):
        return [[jnp.where(upper if half == 1 else jnp.logical_not(upper), t if h == half else t_sw,
                           0.0).astype(BF16) for half in range(2)] for h in range(SWA_KV_HEADS)]

    k_at = placed(k, k_sw)
    v_at = placed(v, v_sw)
    pairs = []
    for j in range(SWA_Q_HEADS // 2):
        h = (2 * j) // SWA_GROUP
        qb = q[:, 2 * j * d:2 * (j + 1) * d]
        acc = None
        for half in range(2):
            s = jnp.where(mask, _dot_nt(qb, k_at[h][half]), NEG_INF)
            sink = sink_ref[2 * j + half]
            m = jnp.maximum(jnp.max(s, axis=-1, keepdims=True), sink)
            p = jnp.exp(s - m)
            den = jnp.sum(p, axis=-1, keepdims=True) + jnp.exp(sink - m)
            o = _dot(p.astype(BF16), v_at[h][half]) * (1.0 / den)
            acc = o if acc is None else acc + o
        pairs.append(acc)
    return jnp.concatenate(pairs, axis=-1)


def _ret_block(b, q, k, v, g, cos_ref, sin_ref, dm_ref, qd_ref, kd_ref, cd_ref, nw_ref, o_ref, st_ref):
    cos = cos_ref[...]
    sin = sin_ref[...]
    dk, dv = RET_QK_DIM, RET_V_DIM
    outs = []
    for h in range(RET_HEADS):
        qh = q[b, :, h * dk:(h + 1) * dk].astype(F32)
        kh = k[b, :, h * dk:(h + 1) * dk].astype(F32)
        qr = qh * cos + pltpu.roll(qh, dk // 2, 1) * sin
        kr = (kh * cos + pltpu.roll(kh, dk // 2, 1) * sin) * (dk ** -0.5)
        vh = v[b, :, h * dv:(h + 1) * dv]
        st = st_ref[b, h]
        inner = _dot_nt(qr.astype(BF16), kr.astype(BF16)) * dm_ref[h]
        o = _dot(inner.astype(BF16), vh) + _dot((qr * qd_ref[h]).astype(BF16), st.astype(BF16))
        st_ref[b, h] = st * cd_ref[h] + _dot_tn((kr * kd_ref[h]).astype(BF16), vh)
        mu = jnp.mean(o, axis=-1, keepdims=True)
        oc = o - mu
        var = jnp.mean(oc * oc, axis=-1, keepdims=True)
        y = oc * lax.rsqrt(var + EPS) * nw_ref[:, h * dv:(h + 1) * dv]
        gh = g[b, :, h * dv:(h + 1) * dv].astype(F32)
        outs.append(gh * _sigmoid(gh) * y)
    o_ref[b] = jnp.concatenate(outs, axis=-1).astype(o_ref.dtype)


def _memkv_body(m_ref, nw_ref, w_ref, kw_ref, k_ref, v_ref):
    m = _rms(m_ref[...], nw_ref[...]).astype(BF16)
    kv = _dot(m, w_ref[...])
    d = MEM_HEAD_DIM
    ks = [_rms(kv[:, h * d:(h + 1) * d], kw_ref[...]) for h in range(MEM_HEADS)]
    k_ref[...] = jnp.concatenate(ks, axis=-1).astype(k_ref.dtype)
    v_ref[...] = kv[:, MEM_W:].astype(v_ref.dtype)


def _memkv(mem2, mem_norm_w, w_kv_bf, k_norm_w, B, M):
    D = mem2.shape[-1]
    return pl.pallas_call(
        _memkv_body,
        grid=(B,),
        in_specs=[pl.BlockSpec((M, D), lambda b: (b, 0)),
                  _const_spec((1, D)),
                  _const_spec(w_kv_bf.shape),
                  _const_spec((1, MEM_HEAD_DIM))],
        out_specs=[pl.BlockSpec((M, MEM_W), lambda b: (b, 0))] * 2,
        out_shape=[jax.ShapeDtypeStruct((B * M, MEM_W), BF16)] * 2,
        compiler_params=_params(("parallel",)),
        name="memkv",
    )(mem2, mem_norm_w.reshape(1, D), w_kv_bf, k_norm_w.reshape(1, -1))


def _mem_block(q, k_ref, v_ref, b, qw_ref):
    d = MEM_HEAD_DIM
    outs = []
    for h in range(MEM_HEADS):
        qh = _rms(q[:, h * d:(h + 1) * d].astype(F32), qw_ref[...]).astype(BF16)
        s = _dot_nt(qh, k_ref[b, :, h * d:(h + 1) * d]) * (d ** -0.5)
        m = jnp.max(s, axis=-1, keepdims=True)
        p = jnp.exp(s - m)
        o = _dot(p.astype(BF16), v_ref[b, :, h * d:(h + 1) * d])
        outs.append(o * (1.0 / jnp.sum(p, axis=-1, keepdims=True)))
    return jnp.concatenate(outs, axis=-1)


def _mixers_body(sink_ref, cd_ref, x_ref, nw_ref, w_ref, sqw_ref, skw_ref, segq_ref, segk_ref,
                 cos_ref, sin_ref, dm_ref, qd_ref, kd_ref, rnw_ref, mk_ref, mv_ref, mqw_ref,
                 osw_ref, ort_ref, omm_ref, zg_ref, st_ref, kprev_ref, vprev_ref):
    n = pl.program_id(1)
    bs, L, D = x_ref.shape

    @pl.when(n == 0)
    def _():
        st_ref[...] = jnp.zeros_like(st_ref)
        kprev_ref[...] = jnp.zeros_like(kprev_ref)
        vprev_ref[...] = jnp.zeros_like(vprev_ref)

    h = _rms(x_ref[...].reshape(bs * L, D), nw_ref[...]).astype(BF16)
    offs = [sum(IN_SIZES[:i]) for i in range(len(IN_SIZES) + 1)]

    def proj(i):
        z = _dot(h, w_ref[:, offs[i]:offs[i + 1]]).astype(BF16)
        return z.reshape(bs, L, IN_SIZES[i])

    sq, sk, sv = proj(0), proj(1), proj(2)
    qi = lax.broadcasted_iota(I32, (L, 2 * L), 0)
    kj = lax.broadcasted_iota(I32, (L, 2 * L), 1)
    diff = L + qi - kj
    has_prev = jnp.minimum(n, 1) * L
    mask = (diff >= 0) & (diff < WINDOW) & (kj + has_prev >= L)
    for b in range(bs):
        osw_ref[b] = _swa_block(sink_ref, sq[b], sk[b], kprev_ref[b], sv[b], vprev_ref[b],
                                sqw_ref, skw_ref, segq_ref, segk_ref, mask).astype(osw_ref.dtype)
    kprev_ref[...] = sk
    vprev_ref[...] = sv
    rq, rk, rv, rg = proj(3), proj(4), proj(5), proj(6)
    for b in range(bs):
        _ret_block(b, rq, rk, rv, rg, cos_ref, sin_ref, dm_ref, qd_ref, kd_ref, cd_ref, rnw_ref, ort_ref, st_ref)
    mq = proj(7)
    for b in range(bs):
        omm_ref[b] = _mem_block(mq[b], mk_ref, mv_ref, b, mqw_ref).astype(omm_ref.dtype)
    zg_ref[...] = proj(8)


def _mixers(x, mix_norm_w, w_in_bf, swa_q_norm_w, swa_k_norm_w, sinks, ret_norm_w, mk, mv, mem_q_norm_w):
    B, S, D = x.shape
    M = mk.shape[0] // B
    L = BAND_BLOCK
    nb = S // L
    bs = BATCH_PER_STEP if B % BATCH_PER_STEP == 0 else 1
    d = SWA_HEAD_DIM
    group = jnp.arange(SWA_Q_W, dtype=I32) // d
    seg_q = (group[:, None] == group[None, :]).astype(BF16)
    seg_k = seg_q[:SWA_KV_W, :SWA_KV_W]
    half = RET_QK_DIM // 2
    inv = 1.0 / (ROPE_BASE ** (jnp.arange(0, half, dtype=F32) / half))
    ang = jnp.arange(S, dtype=F32)[:, None] * inv[None, :]
    cos = jnp.cos(ang)
    sin = jnp.sin(ang)
    cos2 = jnp.concatenate([cos, cos], axis=-1)
    sin2 = jnp.concatenate([-sin, sin], axis=-1)
    log_g = jnp.log1p(-(2.0 ** (-5.0 - jnp.arange(RET_HEADS, dtype=F32))))
    pos = jnp.arange(L, dtype=F32)
    dpos = pos[:, None] - pos[None, :]
    decay_mask = jnp.where(dpos[None] >= 0,
                           jnp.exp(jnp.maximum(dpos, 0.0)[None] * log_g[:, None, None]), 0.0)
    q_decay = jnp.exp((pos[None, :] + 1.0) * log_g[:, None])[..., None]
    k_decay = jnp.exp((L - 1.0 - pos[None, :]) * log_g[:, None])[..., None]
    chunk_decay = jnp.exp(L * log_g)
    qd = jnp.broadcast_to(q_decay, (RET_HEADS, L, RET_QK_DIM))
    kd = jnp.broadcast_to(k_decay, (RET_HEADS, L, RET_QK_DIM))
    blk = lambda w: pl.BlockSpec((bs, L, w), lambda b, n: (b, n, 0))
    smem = pl.BlockSpec(memory_space=pltpu.SMEM)
    outs = pl.pallas_call(
        _mixers_body,
        grid=(B // bs, nb),
        in_specs=[smem, smem,
                  blk(D), _const_spec((1, D)), _const_spec(w_in_bf.shape),
                  _const_spec((1, SWA_Q_W)), _const_spec((1, SWA_KV_W)),
                  _const_spec(seg_q.shape), _const_spec(seg_k.shape),
                  pl.BlockSpec((L, RET_QK_DIM), lambda b, n: (n, 0)),
                  pl.BlockSpec((L, RET_QK_DIM), lambda b, n: (n, 0)),
                  _const_spec((RET_HEADS, L, L)),
                  _const_spec((RET_HEADS, L, RET_QK_DIM)),
                  _const_spec((RET_HEADS, L, RET_QK_DIM)),
                  _const_spec((1, RET_V_W)),
                  pl.BlockSpec((bs, M, MEM_W), lambda b, n: (b, 0, 0)),
                  pl.BlockSpec((bs, M, MEM_W), lambda b, n: (b, 0, 0)),
                  _const_spec((1, MEM_HEAD_DIM))],
        out_specs=[blk(SWA_Q_W), blk(RET_V_W), blk(MEM_W), blk(N_BRANCH * D)],
        out_shape=[jax.ShapeDtypeStruct((B, S, w), BF16) for w in (SWA_Q_W, RET_V_W, MEM_W, N_BRANCH * D)],
        scratch_shapes=[pltpu.VMEM((bs, RET_HEADS, RET_QK_DIM, RET_V_DIM), F32),
                        pltpu.VMEM((bs, L, SWA_KV_W), BF16),
                        pltpu.VMEM((bs, L, SWA_KV_W), BF16)],
        compiler_params=_params(("parallel", "arbitrary"), vmem=VMEM_LIMIT_LARGE),
        name="mixers",
    )(sinks.astype(F32), chunk_decay, x, mix_norm_w.reshape(1, D), w_in_bf,
      jnp.tile(swa_q_norm_w, SWA_Q_HEADS).reshape(1, -1), jnp.tile(swa_k_norm_w, SWA_KV_HEADS).reshape(1, -1),
      seg_q, seg_k, cos2, sin2, decay_mask, qd, kd, ret_norm_w.reshape(1, -1),
      mk.reshape(B, M, MEM_W), mv.reshape(B, M, MEM_W), mem_q_norm_w.reshape(1, -1))
    return [o.reshape(B * S, o.shape[-1]) for o in outs]


def _merge_body(x_ref, osw_ref, ort_ref, omm_ref, zg_ref, wa_ref, wr_ref, wm_ref, wo_ref, nw_ref,
                wrt_ref, brt_ref, x1_ref, h2_ref, idx_ref, gate_ref, cnt_ref):
    D = D_MODEL
    sg = lambda j: _sigmoid(zg_ref[:, j * D:(j + 1) * D].astype(F32))
    merged = (sg(0) * _dot(osw_ref[...], wa_ref[...])
              + sg(1) * _dot(ort_ref[...], wr_ref[...])
              + sg(2) * _dot(omm_ref[...], wm_ref[...]))
    x1 = x_ref[...] + _dot(merged.astype(BF16), wo_ref[...])
    x1_ref[...] = x1
    h2 = _rms(x1, nw_ref[...])
    h2_ref[...] = h2.astype(h2_ref.dtype)
    h_hi, h_lo = _split_bf16(h2)
    w_hi, w_lo = _split_bf16(wrt_ref[...])
    logits = _dot_nt(w_hi, h_hi) + _dot_nt(w_hi, h_lo) + _dot_nt(w_lo, h_hi) + brt_ref[...]
    n_exp, tm = logits.shape
    sub = lax.broadcasted_iota(I32, logits.shape, 0)
    vals, idxs = [], []
    l = logits
    for _ in range(TOP_K):
        m = jnp.max(l, axis=0, keepdims=True)
        i = jnp.min(jnp.where(l == m, sub, n_exp), axis=0, keepdims=True)
        vals.append(m)
        idxs.append(i)
        l = jnp.where(sub == i, -jnp.inf, l)
    es = [jnp.exp(v - vals[0]) for v in vals]
    den = es[0] + es[1] + es[2] + es[3]
    idx_ref[...] = jnp.concatenate(idxs, axis=0)
    gate_ref[...] = jnp.concatenate([e / den for e in es], axis=0)
    chosen = sum((sub == i).astype(F32) for i in idxs)
    ts = min(TS_SORT, tm)
    lane = lax.broadcasted_iota(I32, cnt_ref.shape, 1)
    cnt_ref[...] = sum(jnp.where(lane == j, jnp.sum(chosen[:, j * ts:(j + 1) * ts], axis=1, keepdims=True), 0.0)
                       for j in range(tm // ts))


def _merge(x2, o_swa, o_ret, o_mem, zg, wa, wr, wm, wo, ffn_norm_w, w_router_t, b_router_col):
    T, D = x2.shape
    E = w_router_t.shape[0]
    tm = min(TM_MERGE, T)
    row = lambda i: (i, 0)
    return pl.pallas_call(
        _merge_body,
        grid=(T // tm,),
        in_specs=[pl.BlockSpec((tm, D), row),
                  pl.BlockSpec((tm, SWA_Q_W), row),
                  pl.BlockSpec((tm, RET_V_W), row),
                  pl.BlockSpec((tm, MEM_W), row),
                  pl.BlockSpec((tm, N_BRANCH * D), row),
                  _const_spec(wa.shape), _const_spec(wr.shape), _const_spec(wm.shape), _const_spec(wo.shape),
                  _const_spec((1, D)),
                  _const_spec(w_router_t.shape), _const_spec((E, 1))],
        out_specs=[pl.BlockSpec((tm, D), row),
                   pl.BlockSpec((tm, D), row),
                   pl.BlockSpec((TOP_K, tm), lambda i: (0, i)),
                   pl.BlockSpec((TOP_K, tm), lambda i: (0, i)),
                   pl.BlockSpec((None, E, LANES), lambda i: (i, 0, 0))],
        out_shape=[jax.ShapeDtypeStruct((T, D), F32),
                   jax.ShapeDtypeStruct((T, D), BF16),
                   jax.ShapeDtypeStruct((TOP_K, T), I32),
                   jax.ShapeDtypeStruct((TOP_K, T), F32),
                   jax.ShapeDtypeStruct((T // tm, E, LANES), F32)],
        compiler_params=_params(("parallel",)),
        name="merge",
    )(x2, o_swa, o_ret, o_mem, zg, wa, wr, wm, wo, ffn_norm_w.reshape(1, D), w_router_t, b_router_col)


def _chunk_copy(src_ref, src_chunk, dst_ref, dst_chunk, sem):
    def rows(c):
        start = c * SEG_ALIGN
        return pl.ds(start if isinstance(c, int) else pl.multiple_of(start, SEG_ALIGN), SEG_ALIGN)

    return pltpu.make_async_copy(src_ref.at[rows(src_chunk), :], dst_ref.at[rows(dst_chunk), :], sem)


def _dispatch_body(fill_start_ref, fill_n_ref, gch_ref, h2_ref, idxt_ref, loff_ref,
                   xs_ref, rowt_ref, buf_ref, zero_ref, sems, zsem):
    s = pl.program_id(0)
    last = pl.num_programs(0) - 1
    slot = s % 2
    ts = h2_ref.shape[0]
    nchunks = gch_ref.shape[-1]
    local_rows = nchunks * SEG_ALIGN

    def wait_slot(sl):
        for c in range(nchunks):
            _chunk_copy(buf_ref.at[sl], c, xs_ref, c, sems.at[sl]).wait()

    @pl.when(s == 0)
    def _():
        zero_ref[...] = jnp.zeros_like(zero_ref)

        def per_expert(e, total):
            def fill(r, _):
                _chunk_copy(zero_ref, 0, xs_ref, fill_start_ref[e] + r, zsem).start()
                return 0

            lax.fori_loop(0, fill_n_ref[e], fill, 0)
            return total + fill_n_ref[e]

        total = lax.fori_loop(0, N_EXPERTS, per_expert, 0)

        def drain(r, _):
            _chunk_copy(zero_ref, 0, xs_ref, 0, zsem).wait()
            return 0

        lax.fori_loop(0, total, drain, 0)

    @pl.when(s >= 2)
    def _():
        wait_slot(slot)

    idxt = idxt_ref[...]
    sub = lax.broadcasted_iota(I32, (LANES, ts), 0)
    ohs = [(sub == idxt[k:k + 1, :]).astype(F32) for k in range(TOP_K)]
    m = ohs[0] + ohs[1] + ohs[2] + ohs[3]
    tr = lax.broadcasted_iota(I32, (ts, ts), 0)
    tc = lax.broadcasted_iota(I32, (ts, ts), 1)
    earlier = jnp.where(tr < tc, 1.0, 0.0).astype(BF16)
    before = _dot(m.astype(BF16), earlier) + loff_ref[...]
    rows = [jnp.sum(oh * before, axis=0, keepdims=True).astype(I32) for oh in ohs]
    rowt_ref[...] = jnp.concatenate(rows, axis=0)
    ri = lax.broadcasted_iota(I32, (local_rows, ts), 0)
    sel = (ri == rows[0]) | (ri == rows[1]) | (ri == rows[2]) | (ri == rows[3])
    buf_ref[slot] = _pack_halves(_dot(jnp.where(sel, 1.0, 0.0).astype(BF16), h2_ref[...]))
    for c in range(nchunks):
        _chunk_copy(buf_ref.at[slot], c, xs_ref, gch_ref[0, 0, c], sems.at[slot]).start()

    @pl.when(s == last)
    def _():
        wait_slot(slot)

        @pl.when(s >= 1)
        def _():
            wait_slot(1 - slot)


def _dispatch(h2, idxt, loff_col, gchunk, fill_start, fill_n, n_rows, ts):
    T, D = h2.shape
    nt = T // ts
    nchunks = gchunk.shape[-1]
    return pl.pallas_call(
        _dispatch_body,
        grid_spec=pltpu.PrefetchScalarGridSpec(
            num_scalar_prefetch=2,
            grid=(nt,),
            in_specs=[pl.BlockSpec((1, 1, nchunks), lambda s, *_: (s, 0, 0), memory_space=pltpu.SMEM),
                      pl.BlockSpec((ts, D), lambda s, *_: (s, 0)),
                      pl.BlockSpec((TOP_K, ts), lambda s, *_: (0, s)),
                      pl.BlockSpec((None, LANES, 1), lambda s, *_: (s, 0, 0))],
            out_specs=[pl.BlockSpec(memory_space=pl.ANY),
                       pl.BlockSpec((TOP_K, ts), lambda s, *_: (0, s))],
            scratch_shapes=[pltpu.VMEM((2, nchunks * SEG_ALIGN, D // 2), U32),
                            pltpu.VMEM((SEG_ALIGN, D // 2), U32),
                            pltpu.SemaphoreType.DMA((2,)),
                            pltpu.SemaphoreType.DMA(())]),
        out_shape=[jax.ShapeDtypeStruct((n_rows, D // 2), U32),
                   jax.ShapeDtypeStruct((TOP_K, T), I32)],
        compiler_params=_params(("arbitrary",)),
        name="dispatch",
    )(fill_start, fill_n, gchunk.reshape(nt, 1, nchunks), h2, idxt, loff_col)


def _expert_body(te_ref, nu_ref, x_ref, wu_ref, bu_ref, wd_ref, bd_ref, o_ref, wu_bf, wd_bf):
    i = pl.program_id(0)
    active = i < nu_ref[0]
    new_expert = jnp.logical_or(i == 0, te_ref[i] != te_ref[jnp.maximum(i - 1, 0)])
    half = MXU_COLS // 2

    @pl.when(jnp.logical_and(active, new_expert))
    def _():
        r = lax.broadcasted_iota(I32, (MXU_COLS, MXU_COLS), 0)
        c = lax.broadcasted_iota(I32, (MXU_COLS, MXU_COLS), 1)
        perm = jnp.where(r == jnp.where(c < half, 2 * c, 2 * (c - half) + 1), 1.0, 0.0).astype(BF16)
        for b in range(wu_ref.shape[-1] // MXU_COLS):
            cols = slice(b * MXU_COLS, (b + 1) * MXU_COLS)
            wu_bf[:, cols] = _dot(wu_ref[:, cols].astype(BF16), perm).astype(BF16)
        wd_bf[...] = wd_ref[...].astype(BF16)

    @pl.when(active)
    def _():
        x = jnp.concatenate(_unpack_halves(x_ref[...]), axis=-1)
        hid = _dot(x, wu_bf[...]) + bu_ref[...]
        acts = []
        for b in range(hid.shape[-1] // MXU_COLS):
            x_glu = jnp.minimum(hid[:, b * MXU_COLS:b * MXU_COLS + half], SWIGLU_LIMIT)
            x_lin = jnp.clip(hid[:, b * MXU_COLS + half:(b + 1) * MXU_COLS], -SWIGLU_LIMIT, SWIGLU_LIMIT)
            acts.append(x_glu * _sigmoid(SWIGLU_ALPHA * x_glu) * (x_lin + 1.0))
        act = jnp.concatenate(acts, axis=-1).astype(BF16)
        out = _dot(act, wd_bf[...]) + bd_ref[...]
        o_ref[...] = _pack_halves(out.astype(BF16).astype(F32))


def _experts(xs, tile_expert, n_used, wu, bu, wd, bd, n_tiles):
    tm = TM_EXPERT
    D = wu.shape[1]
    F2 = wu.shape[-1]
    rows = lambda i, te, nu: (jnp.minimum(i, nu[0] - 1), 0)
    wsel = lambda i, te, nu: (te[i], 0, 0)
    return pl.pallas_call(
        _expert_body,
        grid_spec=pltpu.PrefetchScalarGridSpec(
            num_scalar_prefetch=2,
            grid=(n_tiles,),
            in_specs=[pl.BlockSpec((tm, D // 2), rows),
                      pl.BlockSpec((None, D, F2), wsel),
                      pl.BlockSpec((None, 1, F2), wsel),
                      pl.BlockSpec((None, F2 // 2, D), wsel),
                      pl.BlockSpec((None, 1, D), wsel)],
            out_specs=pl.BlockSpec((tm, D // 2), rows),
            scratch_shapes=[pltpu.VMEM((D, F2), BF16), pltpu.VMEM((F2 // 2, D), BF16)]),
        out_shape=jax.ShapeDtypeStruct((n_tiles * tm, D // 2), U32),
        compiler_params=_params(("arbitrary",), vmem=VMEM_LIMIT_LARGE),
        name="experts",
    )(tile_expert, n_used, xs, wu, bu, wd, bd)


def _combine_body(gch_ref, gch_next_ref, ys_ref, x1_ref, row_ref, gate_ref, o_ref, buf_ref, sems):
    s = pl.program_id(0)
    slot = s % 2
    nchunks = gch_ref.shape[-1]

    def gather(g_ref, sl):
        for c in range(nchunks):
            _chunk_copy(ys_ref, g_ref[0, 0, c], buf_ref.at[sl], c, sems.at[sl]).start()

    @pl.when(s == 0)
    def _():
        gather(gch_ref, 0)

    @pl.when(s < pl.num_programs(0) - 1)
    def _():
        gather(gch_next_ref, 1 - slot)

    for c in range(nchunks):
        _chunk_copy(ys_ref, 0, buf_ref.at[slot], c, sems.at[slot]).wait()
    rows = row_ref[...]
    g = gate_ref[...]
    ci = lax.broadcasted_iota(I32, (rows.shape[0], nchunks * SEG_ALIGN), 1)
    w = jnp.zeros(ci.shape, F32)
    for k in range(TOP_K):
        w = jnp.where(ci == rows[:, k:k + 1], g[:, k:k + 1], w)
    w = w.astype(BF16)
    half = x1_ref.shape[-1] // 2
    lo, hi = _unpack_halves(buf_ref[slot])
    o_ref[:, :half] = x1_ref[:, :half] + _dot(w, lo)
    o_ref[:, half:] = x1_ref[:, half:] + _dot(w, hi)


def _combine(ys, gchunk, x1, rows, gates, ts):
    T, D = x1.shape
    nt = T // ts
    nchunks = gchunk.shape[-1]
    return pl.pallas_call(
        _combine_body,
        grid=(nt,),
        in_specs=[pl.BlockSpec((1, 1, nchunks), lambda s: (s, 0, 0), memory_space=pltpu.SMEM),
                  pl.BlockSpec((1, 1, nchunks), lambda s: (jnp.minimum(s + 1, nt - 1), 0, 0),
                               memory_space=pltpu.SMEM),
                  pl.BlockSpec(memory_space=pl.ANY),
                  pl.BlockSpec((ts, D), lambda s: (s, 0)),
                  pl.BlockSpec((ts, TOP_K), lambda s: (s, 0)),
                  pl.BlockSpec((ts, TOP_K), lambda s: (s, 0))],
        out_specs=pl.BlockSpec((ts, D), lambda s: (s, 0)),
        out_shape=jax.ShapeDtypeStruct((T, D), F32),
        scratch_shapes=[pltpu.VMEM((2, nchunks * SEG_ALIGN, D // 2), U32),
                        pltpu.SemaphoreType.DMA((2,))],
        compiler_params=_params(("arbitrary",)),
        name="combine",
    )(gchunk.reshape(nt, 1, nchunks), gchunk.reshape(nt, 1, nchunks), ys, x1, rows, gates)


def _excl_cumsum(a, axis):
    return jnp.cumsum(a, axis=axis) - a


def _routed_experts(h2, x1, idxt, gatest, cnt, w_up, b_up, w_down, b_down):
    T, D = x1.shape
    E = w_up.shape[0]
    ts = min(TS_SORT, T)
    nt = T // ts
    nchunks = (ts * TOP_K + E * SEG_ALIGN) // SEG_ALIGN
    max_rows = T * TOP_K + nt * E * (SEG_ALIGN - 1) + E * (TM_EXPERT - 1)
    n_tiles = -(-max_rows // TM_EXPERT)
    n_rows = n_tiles * TM_EXPERT
    dump_chunk = n_rows // SEG_ALIGN

    cnt = cnt[:, :, :min(TM_MERGE, T) // ts].transpose(0, 2, 1).reshape(nt, E).astype(I32)
    seg = -(-cnt // SEG_ALIGN)
    loff = _excl_cumsum(seg, 1)
    chunks_e = jnp.sum(seg, axis=0)
    tiles_e = -(-(chunks_e * SEG_ALIGN) // TM_EXPERT)
    tile_end = jnp.cumsum(tiles_e)
    base = (tile_end - tiles_e) * (TM_EXPERT // SEG_ALIGN)
    gstart = base[None, :] + _excl_cumsum(seg, 0)
    c = jnp.arange(nchunks, dtype=I32)
    owner = jnp.sum((loff + seg)[:, None, :] <= c[None, :, None], axis=-1)
    onehot = owner[:, :, None] == jnp.arange(E, dtype=I32)[None, None, :]
    shift = jnp.sum(jnp.where(onehot, (gstart - loff)[:, None, :], 0), axis=-1)
    used = owner < E
    gchunk = shift + c[None, :]
    dump = dump_chunk + (jnp.arange(nt, dtype=I32)[:, None] % 2) * nchunks + c[None, :]
    gchunk_out = jnp.where(used, gchunk, dump).astype(I32)
    gchunk_in = jnp.where(used, gchunk, 0).astype(I32)
    fill_start = (base + chunks_e).astype(I32)
    fill_n = (tiles_e * (TM_EXPERT // SEG_ALIGN) - chunks_e).astype(I32)
    n_used = tile_end[-1:].astype(I32)
    tile_ids = jnp.minimum(jnp.arange(n_tiles, dtype=I32), n_used[0] - 1)
    tile_expert = jnp.minimum(jnp.sum(tile_ids[:, None] >= tile_end[None, :], axis=-1), E - 1).astype(I32)
    loff_col = jnp.zeros((nt, LANES, 1), F32).at[:, :E, 0].set((loff * SEG_ALIGN).astype(F32))

    xs, rowt = _dispatch(h2, idxt, loff_col, gchunk_out, fill_start, fill_n,
                         n_rows + 2 * nchunks * SEG_ALIGN, ts)
    half = MXU_COLS // 2
    bu = b_up.reshape(E, -1, half, 2).transpose(0, 1, 3, 2).reshape(E, 1, -1)
    ys = _experts(xs, tile_expert, n_used, w_up, bu, w_down, b_down[:, None, :], n_tiles)
    return _combine(ys, gchunk_in, x1, rowt.T, gatest.T, ts)


def _layer(x, mem, mix_norm_w, mem_norm_w, w_in, swa_q_norm_w, swa_k_norm_w, swa_sinks, ret_norm_w,
           w_mem_kv, mem_q_norm_w, mem_k_norm_w, w_br_swa, w_br_ret, w_br_mem, w_out, ffn_norm_w,
           w_router, b_router, w_up, b_up, w_down, b_down):
    B, S, D = x.shape
    M = mem.shape[1]
    T = B * S

    mk, mv = _memkv(mem.reshape(B * M, D), mem_norm_w, w_mem_kv.astype(BF16), mem_k_norm_w, B, M)
    o_swa, o_ret, o_mem, zg = _mixers(x, mix_norm_w, w_in.astype(BF16), swa_q_norm_w, swa_k_norm_w, swa_sinks,
                                      ret_norm_w, mk, mv, mem_q_norm_w)

    E = w_router.shape[-1]
    x1, h2, idxt, gatest, cnt = _merge(
        x.reshape(T, D), o_swa, o_ret, o_mem, zg, w_br_swa.astype(BF16), w_br_ret.astype(BF16),
        w_br_mem.astype(BF16), w_out.astype(BF16), ffn_norm_w, w_router.T, b_router.reshape(E, 1))

    out = _routed_experts(h2, x1, idxt, gatest, cnt, w_up, b_up, w_down, b_down)
    return out.reshape(B, S, D)


def kernel(x, mem, mix_norm_w, mem_norm_w, w_in, swa_q_norm_w, swa_k_norm_w, swa_sinks, ret_norm_w, w_mem_kv, mem_q_norm_w, mem_k_norm_w, w_br_swa, w_br_ret, w_br_mem, w_out, ffn_norm_w, w_router, b_router, w_up, b_up, w_down, b_down):
    args = (x, mem, mix_norm_w, mem_norm_w, w_in, swa_q_norm_w, swa_k_norm_w, swa_sinks, ret_norm_w,
            w_mem_kv, mem_q_norm_w, mem_k_norm_w, w_br_swa, w_br_ret, w_br_mem, w_out, ffn_norm_w,
            w_router, b_router, w_up, b_up, w_down, b_down)
    for l in range(w_in.shape[0]):
        x = _layer(x, mem, *[a[l] for a in args[2:]])
    return x
```

```python
import jax
import jax.numpy as jnp
from jax import lax
from jax.experimental import pallas as pl
from jax.experimental.pallas import tpu as pltpu

F32 = jnp.float32
BF16 = jnp.bfloat16
I32 = jnp.int32
U32 = jnp.uint32

D_MODEL = 1024
SWA_HEAD_DIM = 64
SWA_Q_HEADS = 8
SWA_KV_HEADS = 2
SWA_GROUP = SWA_Q_HEADS // SWA_KV_HEADS
WINDOW = 128
BAND_BLOCK = 128
RET_HEADS = 4
RET_QK_DIM = 128
RET_V_DIM = 256
RET_CHUNK = 128
ROPE_BASE = 10000.0
MEM_HEADS = 4
MEM_HEAD_DIM = 128
N_BRANCH = 3
N_EXPERTS = 32
TOP_K = 4
D_FF = 1024
SWIGLU_LIMIT = 7.0
SWIGLU_ALPHA = 1.702
EPS = 1e-6
NEG_INF = -1e30

SWA_Q_W = SWA_Q_HEADS * SWA_HEAD_DIM
SWA_KV_W = SWA_KV_HEADS * SWA_HEAD_DIM
RET_QK_W = RET_HEADS * RET_QK_DIM
RET_V_W = RET_HEADS * RET_V_DIM
MEM_W = MEM_HEADS * MEM_HEAD_DIM
IN_SIZES = (SWA_Q_W, SWA_KV_W, SWA_KV_W, RET_QK_W, RET_QK_W, RET_V_W, RET_V_W, MEM_W, N_BRANCH * D_MODEL)

SUBLANES = 8
LANES = 128
MXU_COLS = 256
VMEM_LIMIT = 56 * 1024 * 1024
VMEM_LIMIT_LARGE = 62 * 1024 * 1024

TM_MERGE = 1024
TM_EXPERT = 512
BATCH_PER_STEP = 4
TS_SORT = 256
SEG_ALIGN = SUBLANES
LOCAL_ROWS = TS_SORT * TOP_K + N_EXPERTS * SEG_ALIGN
LOCAL_CHUNKS = LOCAL_ROWS // SEG_ALIGN


def _params(sem, vmem=VMEM_LIMIT):
    return pltpu.CompilerParams(dimension_semantics=sem, vmem_limit_bytes=vmem)


def _const_spec(shape):
    nd = len(shape)
    return pl.BlockSpec(shape, lambda *_: (0,) * nd, pipeline_mode=pl.Buffered(1))


def _rms(x, w):
    return x * lax.rsqrt(jnp.mean(x * x, axis=-1, keepdims=True) + EPS) * w


def _sigmoid(x):
    return 0.5 * jnp.tanh(0.5 * x) + 0.5


def _split_bf16(a):
    hi = a.astype(BF16)
    lo = (a - hi.astype(F32)).astype(BF16)
    return hi, lo


def _pack_halves(a):
    n = a.shape[-1] // 2
    lo = lax.bitcast_convert_type(a[:, :n], U32) >> 16
    hi = lax.bitcast_convert_type(a[:, n:], U32) & jnp.uint32(0xFFFF0000)
    return lo | hi


def _unpack_halves(w):
    lo = lax.bitcast_convert_type(w << 16, F32)
    hi = lax.bitcast_convert_type(w & jnp.uint32(0xFFFF0000), F32)
    return lo.astype(BF16), hi.astype(BF16)


def _dot(a, b):
    return jnp.dot(a, b, preferred_element_type=F32)


def _dot_nt(a, b):
    return lax.dot_general(a, b, (((1,), (1,)), ((), ())), preferred_element_type=F32)


def _dot_tn(a, b):
    return lax.dot_general(a, b, (((0,), (0,)), ((), ())), preferred_element_type=F32)


def _head_rms(t, seg_ref, w):
    ss = _dot((t * t).astype(BF16), seg_ref[...])
    return t * lax.rsqrt(ss * (1.0 / SWA_HEAD_DIM) + EPS) * w


def _swa_block(sink_ref, q, kc, kp, vc, vp, qw_ref, kw_ref, segq_ref, segk_ref, mask):
    d = SWA_HEAD_DIM
    q = _head_rms(q.astype(F32), segq_ref, qw_ref[...]) * (d ** -0.5)
    q = q.astype(BF16)
    k = jnp.concatenate([kp, kc], axis=0).astype(F32)
    k = _head_rms(k, segk_ref, kw_ref[...])
    v = jnp.concatenate([vp, vc], axis=0).astype(F32)
    upper = lax.broadcasted_iota(I32, k.shape, 1) >= d
    k_sw = pltpu.roll(k, d, 1)
    v_sw = pltpu.roll(v, d, 1)

    def placed(t, t_sw):
        return [[jnp.where(upper if half == 1 else jnp.logical_not(upper), t if h == half else t_sw,
                           0.0).astype(BF16) for half in range(2)] for h in range(SWA_KV_HEADS)]

    k_at = placed(k, k_sw)
    v_at = placed(v, v_sw)
    pairs = []
    for j in range(SWA_Q_HEADS // 2):
        h = (2 * j) // SWA_GROUP
        qb = q[:, 2 * j * d:2 * (j + 1) * d]
        acc = None
        for half in range(2):
            s = jnp.where(mask, _dot_nt(qb, k_at[h][half]), NEG_INF)
            sink = sink_ref[2 * j + half]
            m = jnp.maximum(jnp.max(s, axis=-1, keepdims=True), sink)
            p = jnp.exp(s - m)
            den = jnp.sum(p, axis=-1, keepdims=True) + jnp.exp(sink - m)
            o = _dot(p.astype(BF16), v_at[h][half]) * (1.0 / den)
            acc = o if acc is None else acc + o
        pairs.append(acc)
    return jnp.concatenate(pairs, axis=-1)


def _ret_block(b, q, k, v, g, cos_ref, sin_ref, dm_ref, qd_ref, kd_ref, cd_ref, nw_ref, o_ref, st_ref):
    cos = cos_ref[...]
    sin = sin_ref[...]
    dk, dv = RET_QK_DIM, RET_V_DIM
    outs = []
    for h in range(RET_HEADS):
        qh = q[b, :, h * dk:(h + 1) * dk].astype(F32)
        kh = k[b, :, h * dk:(h + 1) * dk].astype(F32)
        qr = qh * cos + pltpu.roll(qh, dk // 2, 1) * sin
        kr = (kh * cos + pltpu.roll(kh, dk // 2, 1) * sin) * (dk ** -0.5)
        vh = v[b, :, h * dv:(h + 1) * dv]
        st = st_ref[b, h]
        inner = _dot_nt(qr.astype(BF16), kr.astype(BF16)) * dm_ref[h]
        o = _dot(inner.astype(BF16), vh) + _dot((qr * qd_ref[h]).astype(BF16), st.astype(BF16))
        st_ref[b, h] = st * cd_ref[h] + _dot_tn((kr * kd_ref[h]).astype(BF16), vh)
        mu = jnp.mean(o, axis=-1, keepdims=True)
        oc = o - mu
        var = jnp.mean(oc * oc, axis=-1, keepdims=True)
        y = oc * lax.rsqrt(var + EPS) * nw_ref[:, h * dv:(h + 1) * dv]
        gh = g[b, :, h * dv:(h + 1) * dv].astype(F32)
        outs.append(gh * _sigmoid(gh) * y)
    o_ref[b] = jnp.concatenate(outs, axis=-1).astype(o_ref.dtype)


def _memkv_body(m_ref, nw_ref, w_ref, kw_ref, k_ref, v_ref):
    m = _rms(m_ref[...], nw_ref[...]).astype(BF16)
    kv = _dot(m, w_ref[...])
    d = MEM_HEAD_DIM
    ks = [_rms(kv[:, h * d:(h + 1) * d], kw_ref[...]) for h in range(MEM_HEADS)]
    k_ref[...] = jnp.concatenate(ks, axis=-1).astype(k_ref.dtype)
    v_ref[...] = kv[:, MEM_W:].astype(v_ref.dtype)


def _memkv(mem2, mem_norm_w, w_kv_bf, k_norm_w, B, M):
    D = mem2.shape[-1]
    return pl.pallas_call(
        _memkv_body,
        grid=(B,),
        in_specs=[pl.BlockSpec((M, D), lambda b: (b, 0)),
                  _const_spec((1, D)),
                  _const_spec(w_kv_bf.shape),
                  _const_spec((1, MEM_HEAD_DIM))],
        out_specs=[pl.BlockSpec((M, MEM_W), lambda b: (b, 0))] * 2,
        out_shape=[jax.ShapeDtypeStruct((B * M, MEM_W), BF16)] * 2,
        compiler_params=_params(("parallel",)),
        name="memkv",
    )(mem2, mem_norm_w.reshape(1, D), w_kv_bf, k_norm_w.reshape(1, -1))


def _mem_block(q, k_ref, v_ref, b, qw_ref):
    d = MEM_HEAD_DIM
    outs = []
    for h in range(MEM_HEADS):
        qh = _rms(q[:, h * d:(h + 1) * d].astype(F32), qw_ref[...]).astype(BF16)
        s = _dot_nt(qh, k_ref[b, :, h * d:(h + 1) * d]) * (d ** -0.5)
        m = jnp.max(s, axis=-1, keepdims=True)
        p = jnp.exp(s - m)
        o = _dot(p.astype(BF16), v_ref[b, :, h * d:(h + 1) * d])
        outs.append(o * (1.0 / jnp.sum(p, axis=-1, keepdims=True)))
    return jnp.concatenate(outs, axis=-1)


def _mixers_body(sink_ref, cd_ref, x_ref, nw_ref, w_ref, sqw_ref, skw_ref, segq_ref, segk_ref,
                 cos_ref, sin_ref, dm_ref, qd_ref, kd_ref, rnw_ref, mk_ref, mv_ref, mqw_ref,
                 osw_ref, ort_ref, omm_ref, zg_ref, st_ref, kprev_ref, vprev_ref):
    n = pl.program_id(1)
    bs, L, D = x_ref.shape

    @pl.when(n == 0)
    def _():
        st_ref[...] = jnp.zeros_like(st_ref)
        kprev_ref[...] = jnp.zeros_like(kprev_ref)
        vprev_ref[...] = jnp.zeros_like(vprev_ref)

    h = _rms(x_ref[...].reshape(bs * L, D), nw_ref[...]).astype(BF16)
    offs = [sum(IN_SIZES[:i]) for i in range(len(IN_SIZES) + 1)]

    def proj(i):
        z = _dot(h, w_ref[:, offs[i]:offs[i + 1]]).astype(BF16)
        return z.reshape(bs, L, IN_SIZES[i])

    sq, sk, sv = proj(0), proj(1), proj(2)
    qi = lax.broadcasted_iota(I32, (L, 2 * L), 0)
    kj = lax.broadcasted_iota(I32, (L, 2 * L), 1)
    diff = L + qi - kj
    has_prev = jnp.minimum(n, 1) * L
    mask = (diff >= 0) & (diff < WINDOW) & (kj + has_prev >= L)
    for b in range(bs):
        osw_ref[b] = _swa_block(sink_ref, sq[b], sk[b], kprev_ref[b], sv[b], vprev_ref[b],
                                sqw_ref, skw_ref, segq_ref, segk_ref, mask).astype(osw_ref.dtype)
    kprev_ref[...] = sk
    vprev_ref[...] = sv
    rq, rk, rv, rg = proj(3), proj(4), proj(5), proj(6)
    for b in range(bs):
        _ret_block(b, rq, rk, rv, rg, cos_ref, sin_ref, dm_ref, qd_ref, kd_ref, cd_ref, rnw_ref, ort_ref, st_ref)
    mq = proj(7)
    for b in range(bs):
        omm_ref[b] = _mem_block(mq[b], mk_ref, mv_ref, b, mqw_ref).astype(omm_ref.dtype)
    zg_ref[...] = proj(8)


def _mixers(x, mix_norm_w, w_in_bf, swa_q_norm_w, swa_k_norm_w, sinks, ret_norm_w, mk, mv, mem_q_norm_w):
    B, S, D = x.shape
    M = mk.shape[0] // B
    L = BAND_BLOCK
    nb = S // L
    bs = BATCH_PER_STEP if B % BATCH_PER_STEP == 0 else 1
    d = SWA_HEAD_DIM
    group = jnp.arange(SWA_Q_W, dtype=I32) // d
    seg_q = (group[:, None] == group[None, :]).astype(BF16)
    seg_k = seg_q[:SWA_KV_W, :SWA_KV_W]
    half = RET_QK_DIM // 2
    inv = 1.0 / (ROPE_BASE ** (jnp.arange(0, half, dtype=F32) / half))
    ang = jnp.arange(S, dtype=F32)[:, None] * inv[None, :]
    cos = jnp.cos(ang)
    sin = jnp.sin(ang)
    cos2 = jnp.concatenate([cos, cos], axis=-1)
    sin2 = jnp.concatenate([-sin, sin], axis=-1)
    log_g = jnp.log1p(-(2.0 ** (-5.0 - jnp.arange(RET_HEADS, dtype=F32))))
    pos = jnp.arange(L, dtype=F32)
    dpos = pos[:, None] - pos[None, :]
    decay_mask = jnp.where(dpos[None] >= 0,
                           jnp.exp(jnp.maximum(dpos, 0.0)[None] * log_g[:, None, None]), 0.0)
    q_decay = jnp.exp((pos[None, :] + 1.0) * log_g[:, None])[..., None]
    k_decay = jnp.exp((L - 1.0 - pos[None, :]) * log_g[:, None])[..., None]
    chunk_decay = jnp.exp(L * log_g)
    qd = jnp.broadcast_to(q_decay, (RET_HEADS, L, RET_QK_DIM))
    kd = jnp.broadcast_to(k_decay, (RET_HEADS, L, RET_QK_DIM))
    blk = lambda w: pl.BlockSpec((bs, L, w), lambda b, n: (b, n, 0))
    smem = pl.BlockSpec(memory_space=pltpu.SMEM)
    outs = pl.pallas_call(
        _mixers_body,
        grid=(B // bs, nb),
        in_specs=[smem, smem,
                  blk(D), _const_spec((1, D)), _const_spec(w_in_bf.shape),
                  _const_spec((1, SWA_Q_W)), _const_spec((1, SWA_KV_W)),
                  _const_spec(seg_q.shape), _const_spec(seg_k.shape),
                  pl.BlockSpec((L, RET_QK_DIM), lambda b, n: (n, 0)),
                  pl.BlockSpec((L, RET_QK_DIM), lambda b, n: (n, 0)),
                  _const_spec((RET_HEADS, L, L)),
                  _const_spec((RET_HEADS, L, RET_QK_DIM)),
                  _const_spec((RET_HEADS, L, RET_QK_DIM)),
                  _const_spec((1, RET_V_W)),
                  pl.BlockSpec((bs, M, MEM_W), lambda b, n: (b, 0, 0)),
                  pl.BlockSpec((bs, M, MEM_W), lambda b, n: (b, 0, 0)),
                  _const_spec((1, MEM_HEAD_DIM))],
        out_specs=[blk(SWA_Q_W), blk(RET_V_W), blk(MEM_W), blk(N_BRANCH * D)],
        out_shape=[jax.ShapeDtypeStruct((B, S, w), BF16) for w in (SWA_Q_W, RET_V_W, MEM_W, N_BRANCH * D)],
        scratch_shapes=[pltpu.VMEM((bs, RET_HEADS, RET_QK_DIM, RET_V_DIM), F32),
                        pltpu.VMEM((bs, L, SWA_KV_W), BF16),
                        pltpu.VMEM((bs, L, SWA_KV_W), BF16)],
        compiler_params=_params(("parallel", "arbitrary"), vmem=VMEM_LIMIT_LARGE),
        name="mixers",
    )(sinks.astype(F32), chunk_decay, x, mix_norm_w.reshape(1, D), w_in_bf,
      jnp.tile(swa_q_norm_w, SWA_Q_HEADS).reshape(1, -1), jnp.tile(swa_k_norm_w, SWA_KV_HEADS).reshape(1, -1),
      seg_q, seg_k, cos2, sin2, decay_mask, qd, kd, ret_norm_w.reshape(1, -1),
      mk.reshape(B, M, MEM_W), mv.reshape(B, M, MEM_W), mem_q_norm_w.reshape(1, -1))
    return [o.reshape(B * S, o.shape[-1]) for o in outs]


def _merge_body(x_ref, osw_ref, ort_ref, omm_ref, zg_ref, wa_ref, wr_ref, wm_ref, wo_ref, nw_ref,
                wrt_ref, brt_ref, x1_ref, h2_ref, idx_ref, gate_ref, cnt_ref):
    D = D_MODEL
    sg = lambda j: _sigmoid(zg_ref[:, j * D:(j + 1) * D].astype(F32))
    merged = (sg(0) * _dot(osw_ref[...], wa_ref[...])
              + sg(1) * _dot(ort_ref[...], wr_ref[...])
              + sg(2) * _dot(omm_ref[...], wm_ref[...]))
    x1 = x_ref[...] + _dot(merged.astype(BF16), wo_ref[...])
    x1_ref[...] = x1
    h2 = _rms(x1, nw_ref[...])
    h2_ref[...] = h2.astype(h2_ref.dtype)
    h_hi, h_lo = _split_bf16(h2)
    w_hi, w_lo = _split_bf16(wrt_ref[...])
    logits = _dot_nt(w_hi, h_hi) + _dot_nt(w_hi, h_lo) + _dot_nt(w_lo, h_hi) + brt_ref[...]
    n_exp, tm = logits.shape
    sub = lax.broadcasted_iota(I32, logits.shape, 0)
    vals, idxs = [], []
    l = logits
    for _ in range(TOP_K):
        m = jnp.max(l, axis=0, keepdims=True)
        i = jnp.min(jnp.where(l == m, sub, n_exp), axis=0, keepdims=True)
        vals.append(m)
        idxs.append(i)
        l = jnp.where(sub == i, -jnp.inf, l)
    es = [jnp.exp(v - vals[0]) for v in vals]
    den = es[0] + es[1] + es[2] + es[3]
    idx_ref[...] = jnp.concatenate(idxs, axis=0)
    gate_ref[...] = jnp.concatenate([e / den for e in es], axis=0)
    chosen = sum((sub == i).astype(F32) for i in idxs)
    ts = min(TS_SORT, tm)
    lane = lax.broadcasted_iota(I32, cnt_ref.shape, 1)
    cnt_ref[...] = sum(jnp.where(lane == j, jnp.sum(chosen[:, j * ts:(j + 1) * ts], axis=1, keepdims=True), 0.0)
                       for j in range(tm // ts))


def _merge(x2, o_swa, o_ret, o_mem, zg, wa, wr, wm, wo, ffn_norm_w, w_router_t, b_router_col):
    T, D = x2.shape
    E = w_router_t.shape[0]
    tm = min(TM_MERGE, T)
    row = lambda i: (i, 0)
    return pl.pallas_call(
        _merge_body,
        grid=(T // tm,),
        in_specs=[pl.BlockSpec((tm, D), row),
                  pl.BlockSpec((tm, SWA_Q_W), row),
                  pl.BlockSpec((tm, RET_V_W), row),
                  pl.BlockSpec((tm, MEM_W), row),
                  pl.BlockSpec((tm, N_BRANCH * D), row),
                  _const_spec(wa.shape), _const_spec(wr.shape), _const_spec(wm.shape), _const_spec(wo.shape),
                  _const_spec((1, D)),
                  _const_spec(w_router_t.shape), _const_spec((E, 1))],
        out_specs=[pl.BlockSpec((tm, D), row),
                   pl.BlockSpec((tm, D), row),
                   pl.BlockSpec((TOP_K, tm), lambda i: (0, i)),
                   pl.BlockSpec((TOP_K, tm), lambda i: (0, i)),
                   pl.BlockSpec((None, E, LANES), lambda i: (i, 0, 0))],
        out_shape=[jax.ShapeDtypeStruct((T, D), F32),
                   jax.ShapeDtypeStruct((T, D), BF16),
                   jax.ShapeDtypeStruct((TOP_K, T), I32),
                   jax.ShapeDtypeStruct((TOP_K, T), F32),
                   jax.ShapeDtypeStruct((T // tm, E, LANES), F32)],
        compiler_params=_params(("parallel",)),
        name="merge",
    )(x2, o_swa, o_ret, o_mem, zg, wa, wr, wm, wo, ffn_norm_w.reshape(1, D), w_router_t, b_router_col)


def _chunk_copy(src_ref, src_chunk, dst_ref, dst_chunk, sem):
    def rows(c):
        start = c * SEG_ALIGN
        return pl.ds(start if isinstance(c, int) else pl.multiple_of(start, SEG_ALIGN), SEG_ALIGN)

    return pltpu.make_async_copy(src_ref.at[rows(src_chunk), :], dst_ref.at[rows(dst_chunk), :], sem)


def _dispatch_body(fill_start_ref, fill_n_ref, gch_ref, h2_ref, idxt_ref, loff_ref,
                   xs_ref, rowt_ref, buf_ref, zero_ref, sems, zsem):
    s = pl.program_id(0)
    last = pl.num_programs(0) - 1
    slot = s % 2
    ts = h2_ref.shape[0]
    nchunks = gch_ref.shape[-1]
    local_rows = nchunks * SEG_ALIGN

    def wait_slot(sl):
        for c in range(nchunks):
            _chunk_copy(buf_ref.at[sl], c, xs_ref, c, sems.at[sl]).wait()

    @pl.when(s == 0)
    def _():
        zero_ref[...] = jnp.zeros_like(zero_ref)

        def per_expert(e, total):
            def fill(r, _):
                _chunk_copy(zero_ref, 0, xs_ref, fill_start_ref[e] + r, zsem).start()
                return 0

            lax.fori_loop(0, fill_n_ref[e], fill, 0)
            return total + fill_n_ref[e]

        total = lax.fori_loop(0, N_EXPERTS, per_expert, 0)

        def drain(r, _):
            _chunk_copy(zero_ref, 0, xs_ref, 0, zsem).wait()
            return 0

        lax.fori_loop(0, total, drain, 0)

    @pl.when(s >= 2)
    def _():
        wait_slot(slot)

    idxt = idxt_ref[...]
    sub = lax.broadcasted_iota(I32, (LANES, ts), 0)
    ohs = [(sub == idxt[k:k + 1, :]).astype(F32) for k in range(TOP_K)]
    m = ohs[0] + ohs[1] + ohs[2] + ohs[3]
    tr = lax.broadcasted_iota(I32, (ts, ts), 0)
    tc = lax.broadcasted_iota(I32, (ts, ts), 1)
    earlier = jnp.where(tr < tc, 1.0, 0.0).astype(BF16)
    before = _dot(m.astype(BF16), earlier) + loff_ref[...]
    rows = [jnp.sum(oh * before, axis=0, keepdims=True).astype(I32) for oh in ohs]
    rowt_ref[...] = jnp.concatenate(rows, axis=0)
    ri = lax.broadcasted_iota(I32, (local_rows, ts), 0)
    sel = (ri == rows[0]) | (ri == rows[1]) | (ri == rows[2]) | (ri == rows[3])
    buf_ref[slot] = _pack_halves(_dot(jnp.where(sel, 1.0, 0.0).astype(BF16), h2_ref[...]))
    for c in range(nchunks):
        _chunk_copy(buf_ref.at[slot], c, xs_ref, gch_ref[0, 0, c], sems.at[slot]).start()

    @pl.when(s == last)
    def _():
        wait_slot(slot)

        @pl.when(s >= 1)
        def _():
            wait_slot(1 - slot)


def _dispatch(h2, idxt, loff_col, gchunk, fill_start, fill_n, n_rows, ts):
    T, D = h2.shape
    nt = T // ts
    nchunks = gchunk.shape[-1]
    return pl.pallas_call(
        _dispatch_body,
        grid_spec=pltpu.PrefetchScalarGridSpec(
            num_scalar_prefetch=2,
            grid=(nt,),
            in_specs=[pl.BlockSpec((1, 1, nchunks), lambda s, *_: (s, 0, 0), memory_space=pltpu.SMEM),
                      pl.BlockSpec((ts, D), lambda s, *_: (s, 0)),
                      pl.BlockSpec((TOP_K, ts), lambda s, *_: (0, s)),
                      pl.BlockSpec((None, LANES, 1), lambda s, *_: (s, 0, 0))],
            out_specs=[pl.BlockSpec(memory_space=pl.ANY),
                       pl.BlockSpec((TOP_K, ts), lambda s, *_: (0, s))],
            scratch_shapes=[pltpu.VMEM((2, nchunks * SEG_ALIGN, D // 2), U32),
                            pltpu.VMEM((SEG_ALIGN, D // 2), U32),
                            pltpu.SemaphoreType.DMA((2,)),
                            pltpu.SemaphoreType.DMA(())]),
        out_shape=[jax.ShapeDtypeStruct((n_rows, D // 2), U32),
                   jax.ShapeDtypeStruct((TOP_K, T), I32)],
        compiler_params=_params(("arbitrary",)),
        name="dispatch",
    )(fill_start, fill_n, gchunk.reshape(nt, 1, nchunks), h2, idxt, loff_col)


def _expert_body(te_ref, nu_ref, tend_ref, x_ref, wu_hbm, bu_ref, wd_hbm, bd_ref, o_ref,
                 wu_f32, wd_f32, wu_bf, wd_bf, slot_ref, wsem):
    i = pl.program_id(0)
    active = i < nu_ref[0]
    e = te_ref[i]
    new_expert = jnp.logical_or(i == 0, e != te_ref[jnp.maximum(i - 1, 0)])
    half = MXU_COLS // 2

    def weight_copies(expert, slot):
        return (pltpu.make_async_copy(wu_hbm.at[expert], wu_f32.at[slot], wsem.at[0, slot]),
                pltpu.make_async_copy(wd_hbm.at[expert], wd_f32.at[slot], wsem.at[1, slot]))

    @pl.when(i == 0)
    def _():
        slot_ref[0] = 1
        for cp in weight_copies(e, 0):
            cp.start()

    @pl.when(jnp.logical_and(active, new_expert))
    def _():
        slot = 1 - slot_ref[0]
        slot_ref[0] = slot
        for cp in weight_copies(e, slot):
            cp.wait()
        first_of_next = tend_ref[e]

        @pl.when(first_of_next < nu_ref[0])
        def _():
            for cp in weight_copies(te_ref[jnp.minimum(first_of_next, te_ref.shape[0] - 1)], 1 - slot):
                cp.start()

        r = lax.broadcasted_iota(I32, (MXU_COLS, MXU_COLS), 0)
        c = lax.broadcasted_iota(I32, (MXU_COLS, MXU_COLS), 1)
        perm = jnp.where(r == jnp.where(c < half, 2 * c, 2 * (c - half) + 1), 1.0, 0.0).astype(BF16)
        for b in range(wu_bf.shape[-1] // MXU_COLS):
            cols = slice(b * MXU_COLS, (b + 1) * MXU_COLS)
            wu_bf[:, cols] = _dot(wu_f32[slot, :, cols].astype(BF16), perm).astype(BF16)
        wd_bf[...] = wd_f32[slot].astype(BF16)

    @pl.when(active)
    def _():
        x = jnp.concatenate(_unpack_halves(x_ref[...]), axis=-1)
        hid = _dot(x, wu_bf[...]) + bu_ref[...]
        acts = []
        for b in range(hid.shape[-1] // MXU_COLS):
            x_glu = jnp.minimum(hid[:, b * MXU_COLS:b * MXU_COLS + half], SWIGLU_LIMIT)
            x_lin = jnp.clip(hid[:, b * MXU_COLS + half:(b + 1) * MXU_COLS], -SWIGLU_LIMIT, SWIGLU_LIMIT)
            acts.append(x_glu * _sigmoid(SWIGLU_ALPHA * x_glu) * (x_lin + 1.0))
        act = jnp.concatenate(acts, axis=-1).astype(BF16)
        out = _dot(act, wd_bf[...]) + bd_ref[...]
        o_ref[...] = _pack_halves(out.astype(BF16).astype(F32))


def _experts(xs, tile_expert, n_used, tile_end, wu, bu, wd, bd, n_tiles):
    tm = TM_EXPERT
    D = wu.shape[1]
    F2 = wu.shape[-1]
    rows = lambda i, te, nu, tend: (jnp.minimum(i, nu[0] - 1), 0)
    bsel = lambda i, te, nu, tend: (te[i], 0, 0)
    return pl.pallas_call(
        _expert_body,
        grid_spec=pltpu.PrefetchScalarGridSpec(
            num_scalar_prefetch=3,
            grid=(n_tiles,),
            in_specs=[pl.BlockSpec((tm, D // 2), rows),
                      pl.BlockSpec(memory_space=pl.ANY),
                      pl.BlockSpec((None, 1, F2), bsel),
                      pl.BlockSpec(memory_space=pl.ANY),
                      pl.BlockSpec((None, 1, D), bsel)],
            out_specs=pl.BlockSpec((tm, D // 2), rows),
            scratch_shapes=[pltpu.VMEM((2, D, F2), F32), pltpu.VMEM((2, F2 // 2, D), F32),
                            pltpu.VMEM((D, F2), BF16), pltpu.VMEM((F2 // 2, D), BF16),
                            pltpu.SMEM((1,), I32), pltpu.SemaphoreType.DMA((2, 2))]),
        out_shape=jax.ShapeDtypeStruct((n_tiles * tm, D // 2), U32),
        compiler_params=_params(("arbitrary",), vmem=VMEM_LIMIT_LARGE),
        name="experts",
    )(tile_expert, n_used, tile_end, xs, wu, bu, wd, bd)


def _combine_body(gch_ref, gch_next_ref, ys_ref, x1_ref, row_ref, gate_ref, o_ref, buf_ref, sems):
    s = pl.program_id(0)
    slot = s % 2
    nchunks = gch_ref.shape[-1]

    def gather(g_ref, sl):
        for c in range(nchunks):
            _chunk_copy(ys_ref, g_ref[0, 0, c], buf_ref.at[sl], c, sems.at[sl]).start()

    @pl.when(s == 0)
    def _():
        gather(gch_ref, 0)

    @pl.when(s < pl.num_programs(0) - 1)
    def _():
        gather(gch_next_ref, 1 - slot)

    for c in range(nchunks):
        _chunk_copy(ys_ref, 0, buf_ref.at[slot], c, sems.at[slot]).wait()
    rows = row_ref[...]
    g = gate_ref[...]
    ci = lax.broadcasted_iota(I32, (rows.shape[0], nchunks * SEG_ALIGN), 1)
    w = jnp.zeros(ci.shape, F32)
    for k in range(TOP_K):
        w = jnp.where(ci == rows[:, k:k + 1], g[:, k:k + 1], w)
    w = w.astype(BF16)
    half = x1_ref.shape[-1] // 2
    lo, hi = _unpack_halves(buf_ref[slot])
    o_ref[:, :half] = x1_ref[:, :half] + _dot(w, lo)
    o_ref[:, half:] = x1_ref[:, half:] + _dot(w, hi)


def _combine(ys, gchunk, x1, rows, gates, ts):
    T, D = x1.shape
    nt = T // ts
    nchunks = gchunk.shape[-1]
    return pl.pallas_call(
        _combine_body,
        grid=(nt,),
        in_specs=[pl.BlockSpec((1, 1, nchunks), lambda s: (s, 0, 0), memory_space=pltpu.SMEM),
                  pl.BlockSpec((1, 1, nchunks), lambda s: (jnp.minimum(s + 1, nt - 1), 0, 0),
                               memory_space=pltpu.SMEM),
                  pl.BlockSpec(memory_space=pl.ANY),
                  pl.BlockSpec((ts, D), lambda s: (s, 0)),
                  pl.BlockSpec((ts, TOP_K), lambda s: (s, 0)),
                  pl.BlockSpec((ts, TOP_K), lambda s: (s, 0))],
        out_specs=pl.BlockSpec((ts, D), lambda s: (s, 0)),
        out_shape=jax.ShapeDtypeStruct((T, D), F32),
        scratch_shapes=[pltpu.VMEM((2, nchunks * SEG_ALIGN, D // 2), U32),
                        pltpu.SemaphoreType.DMA((2,))],
        compiler_params=_params(("arbitrary",)),
        name="combine",
    )(gchunk.reshape(nt, 1, nchunks), gchunk.reshape(nt, 1, nchunks), ys, x1, rows, gates)


def _excl_cumsum(a, axis):
    return jnp.cumsum(a, axis=axis) - a


def _routed_experts(h2, x1, idxt, gatest, cnt, w_up, b_up, w_down, b_down):
    T, D = x1.shape
    E = w_up.shape[0]
    ts = min(TS_SORT, T)
    nt = T // ts
    nchunks = (ts * TOP_K + E * SEG_ALIGN) // SEG_ALIGN
    max_rows = T * TOP_K + nt * E * (SEG_ALIGN - 1) + E * (TM_EXPERT - 1)
    n_tiles = -(-max_rows // TM_EXPERT)
    n_rows = n_tiles * TM_EXPERT
    dump_chunk = n_rows // SEG_ALIGN

    cnt = cnt[:, :, :min(TM_MERGE, T) // ts].transpose(0, 2, 1).reshape(nt, E).astype(I32)
    seg = -(-cnt // SEG_ALIGN)
    loff = _excl_cumsum(seg, 1)
    chunks_e = jnp.sum(seg, axis=0)
    tiles_e = -(-(chunks_e * SEG_ALIGN) // TM_EXPERT)
    tile_end = jnp.cumsum(tiles_e)
    base = (tile_end - tiles_e) * (TM_EXPERT // SEG_ALIGN)
    gstart = base[None, :] + _excl_cumsum(seg, 0)
    c = jnp.arange(nchunks, dtype=I32)
    owner = jnp.sum((loff + seg)[:, None, :] <= c[None, :, None], axis=-1)
    onehot = owner[:, :, None] == jnp.arange(E, dtype=I32)[None, None, :]
    shift = jnp.sum(jnp.where(onehot, (gstart - loff)[:, None, :], 0), axis=-1)
    used = owner < E
    gchunk = shift + c[None, :]
    dump = dump_chunk + (jnp.arange(nt, dtype=I32)[:, None] % 2) * nchunks + c[None, :]
    gchunk_out = jnp.where(used, gchunk, dump).astype(I32)
    gchunk_in = jnp.where(used, gchunk, 0).astype(I32)
    fill_start = (base + chunks_e).astype(I32)
    fill_n = (tiles_e * (TM_EXPERT // SEG_ALIGN) - chunks_e).astype(I32)
    n_used = tile_end[-1:].astype(I32)
    tile_ids = jnp.minimum(jnp.arange(n_tiles, dtype=I32), n_used[0] - 1)
    tile_expert = jnp.minimum(jnp.sum(tile_ids[:, None] >= tile_end[None, :], axis=-1), E - 1).astype(I32)
    loff_col = jnp.zeros((nt, LANES, 1), F32).at[:, :E, 0].set((loff * SEG_ALIGN).astype(F32))

    xs, rowt = _dispatch(h2, idxt, loff_col, gchunk_out, fill_start, fill_n,
                         n_rows + 2 * nchunks * SEG_ALIGN, ts)
    half = MXU_COLS // 2
    bu = b_up.reshape(E, -1, half, 2).transpose(0, 1, 3, 2).reshape(E, 1, -1)
    ys = _experts(xs, tile_expert, n_used, tile_end.astype(I32), w_up, bu, w_down, b_down[:, None, :], n_tiles)
    return _combine(ys, gchunk_in, x1, rowt.T, gatest.T, ts)


def _layer(x, mem, mix_norm_w, mem_norm_w, w_in, swa_q_norm_w, swa_k_norm_w, swa_sinks, ret_norm_w,
           w_mem_kv, mem_q_norm_w, mem_k_norm_w, w_br_swa, w_br_ret, w_br_mem, w_out, ffn_norm_w,
           w_router, b_router, w_up, b_up, w_down, b_down):
    B, S, D = x.shape
    M = mem.shape[1]
    T = B * S

    mk, mv = _memkv(mem.reshape(B * M, D), mem_norm_w, w_mem_kv.astype(BF16), mem_k_norm_w, B, M)
    o_swa, o_ret, o_mem, zg = _mixers(x, mix_norm_w, w_in.astype(BF16), swa_q_norm_w, swa_k_norm_w, swa_sinks,
                                      ret_norm_w, mk, mv, mem_q_norm_w)

    E = w_router.shape[-1]
    x1, h2, idxt, gatest, cnt = _merge(
        x.reshape(T, D), o_swa, o_ret, o_mem, zg, w_br_swa.astype(BF16), w_br_ret.astype(BF16),
        w_br_mem.astype(BF16), w_out.astype(BF16), ffn_norm_w, w_router.T, b_router.reshape(E, 1))

    out = _routed_experts(h2, x1, idxt, gatest, cnt, w_up, b_up, w_down, b_down)
    return out.reshape(B, S, D)


def kernel(x, mem, mix_norm_w, mem_norm_w, w_in, swa_q_norm_w, swa_k_norm_w, swa_sinks, ret_norm_w, w_mem_kv, mem_q_norm_w, mem_k_norm_w, w_br_swa, w_br_ret, w_br_mem, w_out, ffn_norm_w, w_router, b_router, w_up, b_up, w_down, b_down):
    args = (x, mem, mix_norm_w, mem_norm_w, w_in, swa_q_norm_w, swa_k_norm_w, swa_sinks, ret_norm_w,
            w_mem_kv, mem_q_norm_w, mem_k_norm_w, w_br_swa, w_br_ret, w_br_mem, w_out, ffn_norm_w,
            w_router, b_router, w_up, b_up, w_down, b_down)
    for l in range(w_in.shape[0]):
        x = _layer(x, mem, *[a[l] for a in args[2:]])
    return x
```

```python
import jax
import jax.numpy as jnp
from jax import lax
from jax.experimental import pallas as pl
from jax.experimental.pallas import tpu as pltpu

F32 = jnp.float32
BF16 = jnp.bfloat16
I32 = jnp.int32
U32 = jnp.uint32

D_MODEL = 1024
SWA_HEAD_DIM = 64
SWA_Q_HEADS = 8
SWA_KV_HEADS = 2
SWA_GROUP = SWA_Q_HEADS // SWA_KV_HEADS
WINDOW = 128
BAND_BLOCK = 128
RET_HEADS = 4
RET_QK_DIM = 128
RET_V_DIM = 256
RET_CHUNK = 128
ROPE_BASE = 10000.0
MEM_HEADS = 4
MEM_HEAD_DIM = 128
N_BRANCH = 3
N_EXPERTS = 32
TOP_K = 4
D_FF = 1024
SWIGLU_LIMIT = 7.0
SWIGLU_ALPHA = 1.702
EPS = 1e-6
NEG_INF = -1e30

SWA_Q_W = SWA_Q_HEADS * SWA_HEAD_DIM
SWA_KV_W = SWA_KV_HEADS * SWA_HEAD_DIM
RET_QK_W = RET_HEADS * RET_QK_DIM
RET_V_W = RET_HEADS * RET_V_DIM
MEM_W = MEM_HEADS * MEM_HEAD_DIM
IN_SIZES = (SWA_Q_W, SWA_KV_W, SWA_KV_W, RET_QK_W, RET_QK_W, RET_V_W, RET_V_W, MEM_W, N_BRANCH * D_MODEL)

SUBLANES = 8
LANES = 128
MXU_COLS = 256
VMEM_LIMIT = 56 * 1024 * 1024
VMEM_LIMIT_LARGE = 62 * 1024 * 1024

TM_MERGE = 1024
TM_EXPERT = 512
BATCH_PER_STEP = 4
TS_SORT = 256
SEG_ALIGN = SUBLANES
LOCAL_ROWS = TS_SORT * TOP_K + N_EXPERTS * SEG_ALIGN
LOCAL_CHUNKS = LOCAL_ROWS // SEG_ALIGN


def _params(sem, vmem=VMEM_LIMIT):
    return pltpu.CompilerParams(dimension_semantics=sem, vmem_limit_bytes=vmem)


def _const_spec(shape):
    nd = len(shape)
    return pl.BlockSpec(shape, lambda *_: (0,) * nd, pipeline_mode=pl.Buffered(1))


def _rms(x, w):
    return x * lax.rsqrt(jnp.mean(x * x, axis=-1, keepdims=True) + EPS) * w


def _sigmoid(x):
    return 0.5 * jnp.tanh(0.5 * x) + 0.5


def _split_bf16(a):
    hi = a.astype(BF16)
    lo = (a - hi.astype(F32)).astype(BF16)
    return hi, lo


def _pack_halves(a):
    n = a.shape[-1] // 2
    lo = lax.bitcast_convert_type(a[:, :n], U32) >> 16
    hi = lax.bitcast_convert_type(a[:, n:], U32) & jnp.uint32(0xFFFF0000)
    return lo | hi


def _unpack_halves(w):
    lo = lax.bitcast_convert_type(w << 16, F32)
    hi = lax.bitcast_convert_type(w & jnp.uint32(0xFFFF0000), F32)
    return lo.astype(BF16), hi.astype(BF16)


def _dot(a, b):
    return jnp.dot(a, b, preferred_element_type=F32)


def _dot_nt(a, b):
    return lax.dot_general(a, b, (((1,), (1,)), ((), ())), preferred_element_type=F32)


def _dot_tn(a, b):
    return lax.dot_general(a, b, (((0,), (0,)), ((), ())), preferred_element_type=F32)


def _head_rms(t, seg_ref, w):
    ss = _dot((t * t).astype(BF16), seg_ref[...])
    return t * lax.rsqrt(ss * (1.0 / SWA_HEAD_DIM) + EPS) * w


def _swa_block(sink_ref, q, kc, kp, vc, vp, qw_ref, kw_ref, segq_ref, segk_ref, mask):
    d = SWA_HEAD_DIM
    q = _head_rms(q.astype(F32), segq_ref, qw_ref[...]) * (d ** -0.5)
    q = q.astype(BF16)
    k = jnp.concatenate([kp, kc], axis=0).astype(F32)
    k = _head_rms(k, segk_ref, kw_ref[...])
    v = jnp.concatenate([vp, vc], axis=0).astype(F32)
    upper = lax.broadcasted_iota(I32, k.shape, 1) >= d
    k_sw = pltpu.roll(k, d, 1)
    v_sw = pltpu.roll(v, d, 1)

    def placed(t, t_sw):
        return [[jnp.where(upper if half == 1 else jnp.logical_not(upper), t if h == half else t_sw,
                           0.0).astype(BF16) for half in range(2)] for h in range(SWA_KV_HEADS)]

    k_at = placed(k, k_sw)
    v_at = placed(v, v_sw)
    pairs = []
    for j in range(SWA_Q_HEADS // 2):
        h = (2 * j) // SWA_GROUP
        qb = q[:, 2 * j * d:2 * (j + 1) * d]
        acc = None
        for half in range(2):
            s = jnp.where(mask, _dot_nt(qb, k_at[h][half]), NEG_INF)
            sink = sink_ref[2 * j + half]
            m = jnp.maximum(jnp.max(s, axis=-1, keepdims=True), sink)
            p = jnp.exp(s - m)
            den = jnp.sum(p, axis=-1, keepdims=True) + jnp.exp(sink - m)
            o = _dot(p.astype(BF16), v_at[h][half]) * (1.0 / den)
            acc = o if acc is None else acc + o
        pairs.append(acc)
    return jnp.concatenate(pairs, axis=-1)


def _ret_block(b, q, k, v, g, cs_ref, dm_ref, qd_ref, kd_ref, cd_ref, nw_ref, o_ref, st_ref):
    dk, dv = RET_QK_DIM, RET_V_DIM
    cos = cs_ref[:, :dk]
    sin = cs_ref[:, dk:]
    outs = []
    for h in range(RET_HEADS):
        qh = q[b, :, h * dk:(h + 1) * dk].astype(F32)
        kh = k[b, :, h * dk:(h + 1) * dk].astype(F32)
        qr = qh * cos + pltpu.roll(qh, dk // 2, 1) * sin
        kr = (kh * cos + pltpu.roll(kh, dk // 2, 1) * sin) * (dk ** -0.5)
        vh = v[b, :, h * dv:(h + 1) * dv]
        st = st_ref[b, h]
        inner = _dot_nt(qr.astype(BF16), kr.astype(BF16)) * dm_ref[h]
        o = _dot(inner.astype(BF16), vh) + _dot((qr * qd_ref[h]).astype(BF16), st.astype(BF16))
        st_ref[b, h] = st * cd_ref[h] + _dot_tn((kr * kd_ref[h]).astype(BF16), vh)
        mu = jnp.mean(o, axis=-1, keepdims=True)
        oc = o - mu
        var = jnp.mean(oc * oc, axis=-1, keepdims=True)
        y = oc * lax.rsqrt(var + EPS) * nw_ref[:, h * dv:(h + 1) * dv]
        gh = g[b, :, h * dv:(h + 1) * dv].astype(F32)
        outs.append(gh * _sigmoid(gh) * y)
    o_ref[b] = jnp.concatenate(outs, axis=-1).astype(o_ref.dtype)


def _memkv_body(m_ref, nw_ref, w_ref, kw_ref, kv_ref):
    m = _rms(m_ref[...], nw_ref[...]).astype(BF16)
    kv = _dot(m, w_ref[...])
    d = MEM_HEAD_DIM
    ks = [_rms(kv[:, h * d:(h + 1) * d], kw_ref[...]) for h in range(MEM_HEADS)]
    kv_ref[...] = jnp.concatenate(ks + [kv[:, MEM_W:]], axis=-1).astype(kv_ref.dtype)


def _memkv(mem2, mem_norm_w, w_kv_bf, k_norm_w, B, M):
    D = mem2.shape[-1]
    return pl.pallas_call(
        _memkv_body,
        grid=(B,),
        in_specs=[pl.BlockSpec((M, D), lambda b: (b, 0)),
                  _const_spec((1, D)),
                  _const_spec(w_kv_bf.shape),
                  _const_spec((1, MEM_HEAD_DIM))],
        out_specs=pl.BlockSpec((M, 2 * MEM_W), lambda b: (b, 0)),
        out_shape=jax.ShapeDtypeStruct((B * M, 2 * MEM_W), BF16),
        compiler_params=_params(("parallel",)),
        name="memkv",
    )(mem2, mem_norm_w.reshape(1, D), w_kv_bf, k_norm_w.reshape(1, -1))


def _mem_block(q, kv_ref, b, qw_ref):
    d = MEM_HEAD_DIM
    outs = []
    for h in range(MEM_HEADS):
        qh = _rms(q[:, h * d:(h + 1) * d].astype(F32), qw_ref[...]).astype(BF16)
        s = _dot_nt(qh, kv_ref[b, :, h * d:(h + 1) * d]) * (d ** -0.5)
        m = jnp.max(s, axis=-1, keepdims=True)
        p = jnp.exp(s - m)
        o = _dot(p.astype(BF16), kv_ref[b, :, MEM_W + h * d:MEM_W + (h + 1) * d])
        outs.append(o * (1.0 / jnp.sum(p, axis=-1, keepdims=True)))
    return jnp.concatenate(outs, axis=-1)


def _mixers_body(sink_ref, cd_ref, x_ref, nw_ref, w_ref, sqw_ref, skw_ref, segq_ref, segk_ref,
                 cs_ref, dm_ref, qd_ref, kd_ref, rnw_ref, kv_ref, mqw_ref,
                 osw_ref, ort_ref, omm_ref, zg_ref, st_ref, kprev_ref, vprev_ref):
    n = pl.program_id(1)
    bs, L, D = x_ref.shape

    @pl.when(n == 0)
    def _():
        st_ref[...] = jnp.zeros_like(st_ref)
        kprev_ref[...] = jnp.zeros_like(kprev_ref)
        vprev_ref[...] = jnp.zeros_like(vprev_ref)

    h = _rms(x_ref[...].reshape(bs * L, D), nw_ref[...]).astype(BF16)
    offs = [sum(IN_SIZES[:i]) for i in range(len(IN_SIZES) + 1)]

    def proj(i):
        z = _dot(h, w_ref[:, offs[i]:offs[i + 1]]).astype(BF16)
        return z.reshape(bs, L, IN_SIZES[i])

    sq, sk, sv = proj(0), proj(1), proj(2)
    qi = lax.broadcasted_iota(I32, (L, 2 * L), 0)
    kj = lax.broadcasted_iota(I32, (L, 2 * L), 1)
    diff = L + qi - kj
    has_prev = jnp.minimum(n, 1) * L
    mask = (diff >= 0) & (diff < WINDOW) & (kj + has_prev >= L)
    for b in range(bs):
        osw_ref[b] = _swa_block(sink_ref, sq[b], sk[b], kprev_ref[b], sv[b], vprev_ref[b],
                                sqw_ref, skw_ref, segq_ref, segk_ref, mask).astype(osw_ref.dtype)
    kprev_ref[...] = sk
    vprev_ref[...] = sv
    rq, rk, rv, rg = proj(3), proj(4), proj(5), proj(6)
    for b in range(bs):
        _ret_block(b, rq, rk, rv, rg, cs_ref, dm_ref, qd_ref, kd_ref, cd_ref, rnw_ref, ort_ref, st_ref)
    mq = proj(7)
    for b in range(bs):
        omm_ref[b] = _mem_block(mq[b], kv_ref, b, mqw_ref).astype(omm_ref.dtype)
    zg_ref[...] = proj(8)


def _mixers(x, mix_norm_w, w_in_bf, swa_q_norm_w, swa_k_norm_w, sinks, ret_norm_w, mkv, mem_q_norm_w):
    B, S, D = x.shape
    M = mkv.shape[0] // B
    L = BAND_BLOCK
    nb = S // L
    bs = BATCH_PER_STEP if B % BATCH_PER_STEP == 0 else 1
    d = SWA_HEAD_DIM
    group = jnp.arange(SWA_Q_W, dtype=I32) // d
    seg_q = (group[:, None] == group[None, :]).astype(BF16)
    seg_k = seg_q[:SWA_KV_W, :SWA_KV_W]
    half = RET_QK_DIM // 2
    inv = 1.0 / (ROPE_BASE ** (jnp.arange(0, half, dtype=F32) / half))
    ang = jnp.arange(S, dtype=F32)[:, None] * inv[None, :]
    cos = jnp.cos(ang)
    sin = jnp.sin(ang)
    cos_sin = jnp.concatenate([cos, cos, -sin, sin], axis=-1)
    log_g = jnp.log1p(-(2.0 ** (-5.0 - jnp.arange(RET_HEADS, dtype=F32))))
    pos = jnp.arange(L, dtype=F32)
    dpos = pos[:, None] - pos[None, :]
    decay_mask = jnp.where(dpos[None] >= 0,
                           jnp.exp(jnp.maximum(dpos, 0.0)[None] * log_g[:, None, None]), 0.0)
    q_decay = jnp.exp((pos[None, :] + 1.0) * log_g[:, None])[..., None]
    k_decay = jnp.exp((L - 1.0 - pos[None, :]) * log_g[:, None])[..., None]
    chunk_decay = jnp.exp(L * log_g)
    qd = jnp.broadcast_to(q_decay, (RET_HEADS, L, RET_QK_DIM))
    kd = jnp.broadcast_to(k_decay, (RET_HEADS, L, RET_QK_DIM))
    blk = lambda w: pl.BlockSpec((bs, L, w), lambda b, n: (b, n, 0))
    smem = pl.BlockSpec(memory_space=pltpu.SMEM)
    outs = pl.pallas_call(
        _mixers_body,
        grid=(B // bs, nb),
        in_specs=[smem, smem,
                  blk(D), _const_spec((1, D)), _const_spec(w_in_bf.shape),
                  _const_spec((1, SWA_Q_W)), _const_spec((1, SWA_KV_W)),
                  _const_spec(seg_q.shape), _const_spec(seg_k.shape),
                  pl.BlockSpec((L, 2 * RET_QK_DIM), lambda b, n: (n, 0)),
                  _const_spec((RET_HEADS, L, L)),
                  _const_spec((RET_HEADS, L, RET_QK_DIM)),
                  _const_spec((RET_HEADS, L, RET_QK_DIM)),
                  _const_spec((1, RET_V_W)),
                  pl.BlockSpec((bs, M, 2 * MEM_W), lambda b, n: (b, 0, 0)),
                  _const_spec((1, MEM_HEAD_DIM))],
        out_specs=[blk(SWA_Q_W), blk(RET_V_W), blk(MEM_W), blk(N_BRANCH * D)],
        out_shape=[jax.ShapeDtypeStruct((B, S, w), BF16) for w in (SWA_Q_W, RET_V_W, MEM_W, N_BRANCH * D)],
        scratch_shapes=[pltpu.VMEM((bs, RET_HEADS, RET_QK_DIM, RET_V_DIM), F32),
                        pltpu.VMEM((bs, L, SWA_KV_W), BF16),
                        pltpu.VMEM((bs, L, SWA_KV_W), BF16)],
        compiler_params=_params(("parallel", "arbitrary"), vmem=VMEM_LIMIT_LARGE),
        name="mixers",
    )(sinks.astype(F32), chunk_decay, x, mix_norm_w.reshape(1, D), w_in_bf,
      jnp.tile(swa_q_norm_w, SWA_Q_HEADS).reshape(1, -1), jnp.tile(swa_k_norm_w, SWA_KV_HEADS).reshape(1, -1),
      seg_q, seg_k, cos_sin, decay_mask, qd, kd, ret_norm_w.reshape(1, -1),
      mkv.reshape(B, M, 2 * MEM_W), mem_q_norm_w.reshape(1, -1))
    return [o.reshape(B * S, o.shape[-1]) for o in outs]


def _merge_body(x_ref, osw_ref, ort_ref, omm_ref, zg_ref, wa_ref, wr_ref, wm_ref, wo_ref, nw_ref,
                wrt_ref, brt_ref, x1_ref, h2_ref, idx_ref, gate_ref, cnt_ref):
    D = D_MODEL
    sg = lambda j: _sigmoid(zg_ref[:, j * D:(j + 1) * D].astype(F32))
    merged = (sg(0) * _dot(osw_ref[...], wa_ref[...])
              + sg(1) * _dot(ort_ref[...], wr_ref[...])
              + sg(2) * _dot(omm_ref[...], wm_ref[...]))
    x1 = x_ref[...] + _dot(merged.astype(BF16), wo_ref[...])
    x1_ref[...] = x1
    h2 = _rms(x1, nw_ref[...])
    h2_ref[...] = h2.astype(h2_ref.dtype)
    h_hi, h_lo = _split_bf16(h2)
    w_hi, w_lo = _split_bf16(wrt_ref[...])
    logits = _dot_nt(w_hi, h_hi) + _dot_nt(w_hi, h_lo) + _dot_nt(w_lo, h_hi) + brt_ref[...]
    n_exp, tm = logits.shape
    sub = lax.broadcasted_iota(I32, logits.shape, 0)
    vals, idxs = [], []
    l = logits
    for _ in range(TOP_K):
        m = jnp.max(l, axis=0, keepdims=True)
        i = jnp.min(jnp.where(l == m, sub, n_exp), axis=0, keepdims=True)
        vals.append(m)
        idxs.append(i)
        l = jnp.where(sub == i, -jnp.inf, l)
    es = [jnp.exp(v - vals[0]) for v in vals]
    den = es[0] + es[1] + es[2] + es[3]
    idx_ref[...] = jnp.concatenate(idxs, axis=0)
    gate_ref[...] = jnp.concatenate([e / den for e in es], axis=0)
    chosen = sum((sub == i).astype(F32) for i in idxs)
    ts = min(TS_SORT, tm)
    lane = lax.broadcasted_iota(I32, cnt_ref.shape, 1)
    cnt_ref[...] = sum(jnp.where(lane == j, jnp.sum(chosen[:, j * ts:(j + 1) * ts], axis=1, keepdims=True), 0.0)
                       for j in range(tm // ts))


def _merge(x2, o_swa, o_ret, o_mem, zg, wa, wr, wm, wo, ffn_norm_w, w_router_t, b_router_col):
    T, D = x2.shape
    E = w_router_t.shape[0]
    tm = min(TM_MERGE, T)
    row = lambda i: (i, 0)
    return pl.pallas_call(
        _merge_body,
        grid=(T // tm,),
        in_specs=[pl.BlockSpec((tm, D), row),
                  pl.BlockSpec((tm, SWA_Q_W), row),
                  pl.BlockSpec((tm, RET_V_W), row),
                  pl.BlockSpec((tm, MEM_W), row),
                  pl.BlockSpec((tm, N_BRANCH * D), row),
                  _const_spec(wa.shape), _const_spec(wr.shape), _const_spec(wm.shape), _const_spec(wo.shape),
                  _const_spec((1, D)),
                  _const_spec(w_router_t.shape), _const_spec((E, 1))],
        out_specs=[pl.BlockSpec((tm, D), row),
                   pl.BlockSpec((tm, D), row),
                   pl.BlockSpec((TOP_K, tm), lambda i: (0, i)),
                   pl.BlockSpec((TOP_K, tm), lambda i: (0, i)),
                   pl.BlockSpec((None, E, LANES), lambda i: (i, 0, 0))],
        out_shape=[jax.ShapeDtypeStruct((T, D), F32),
                   jax.ShapeDtypeStruct((T, D), BF16),
                   jax.ShapeDtypeStruct((TOP_K, T), I32),
                   jax.ShapeDtypeStruct((TOP_K, T), F32),
                   jax.ShapeDtypeStruct((T // tm, E, LANES), F32)],
        compiler_params=_params(("parallel",)),
        name="merge",
    )(x2, o_swa, o_ret, o_mem, zg, wa, wr, wm, wo, ffn_norm_w.reshape(1, D), w_router_t, b_router_col)


def _chunk_copy(src_ref, src_chunk, dst_ref, dst_chunk, sem):
    def rows(c):
        start = c * SEG_ALIGN
        return pl.ds(start if isinstance(c, int) else pl.multiple_of(start, SEG_ALIGN), SEG_ALIGN)

    return pltpu.make_async_copy(src_ref.at[rows(src_chunk), :], dst_ref.at[rows(dst_chunk), :], sem)


def _dispatch_body(fill_start_ref, fill_n_ref, gch_ref, h2_ref, idxt_ref, loff_ref,
                   xs_ref, rowt_ref, buf_ref, zero_ref, sems, zsem):
    s = pl.program_id(0)
    last = pl.num_programs(0) - 1
    slot = s % 2
    ts = h2_ref.shape[0]
    nchunks = gch_ref.shape[-1]
    local_rows = nchunks * SEG_ALIGN

    def wait_slot(sl):
        for c in range(nchunks):
            _chunk_copy(buf_ref.at[sl], c, xs_ref, c, sems.at[sl]).wait()

    @pl.when(s == 0)
    def _():
        zero_ref[...] = jnp.zeros_like(zero_ref)

        def per_expert(e, total):
            def fill(r, _):
                _chunk_copy(zero_ref, 0, xs_ref, fill_start_ref[e] + r, zsem).start()
                return 0

            lax.fori_loop(0, fill_n_ref[e], fill, 0)
            return total + fill_n_ref[e]

        total = lax.fori_loop(0, N_EXPERTS, per_expert, 0)

        def drain(r, _):
            _chunk_copy(zero_ref, 0, xs_ref, 0, zsem).wait()
            return 0

        lax.fori_loop(0, total, drain, 0)

    @pl.when(s >= 2)
    def _():
        wait_slot(slot)

    idxt = idxt_ref[...]
    sub = lax.broadcasted_iota(I32, (LANES, ts), 0)
    ohs = [(sub == idxt[k:k + 1, :]).astype(F32) for k in range(TOP_K)]
    m = ohs[0] + ohs[1] + ohs[2] + ohs[3]
    tr = lax.broadcasted_iota(I32, (ts, ts), 0)
    tc = lax.broadcasted_iota(I32, (ts, ts), 1)
    earlier = jnp.where(tr < tc, 1.0, 0.0).astype(BF16)
    before = _dot(m.astype(BF16), earlier) + loff_ref[...]
    rows = [jnp.sum(oh * before, axis=0, keepdims=True).astype(I32) for oh in ohs]
    rowt_ref[...] = jnp.concatenate(rows, axis=0)
    ri = lax.broadcasted_iota(I32, (local_rows, ts), 0)
    sel = (ri == rows[0]) | (ri == rows[1]) | (ri == rows[2]) | (ri == rows[3])
    buf_ref[slot] = _pack_halves(_dot(jnp.where(sel, 1.0, 0.0).astype(BF16), h2_ref[...]))
    for c in range(nchunks):
        _chunk_copy(buf_ref.at[slot], c, xs_ref, gch_ref[0, 0, c], sems.at[slot]).start()

    @pl.when(s == last)
    def _():
        wait_slot(slot)

        @pl.when(s >= 1)
        def _():
            wait_slot(1 - slot)


def _dispatch(h2, idxt, loff_col, gchunk, fill_start, fill_n, n_rows, ts):
    T, D = h2.shape
    nt = T // ts
    nchunks = gchunk.shape[-1]
    return pl.pallas_call(
        _dispatch_body,
        grid_spec=pltpu.PrefetchScalarGridSpec(
            num_scalar_prefetch=2,
            grid=(nt,),
            in_specs=[pl.BlockSpec((1, 1, nchunks), lambda s, *_: (s, 0, 0), memory_space=pltpu.SMEM),
                      pl.BlockSpec((ts, D), lambda s, *_: (s, 0)),
                      pl.BlockSpec((TOP_K, ts), lambda s, *_: (0, s)),
                      pl.BlockSpec((None, LANES, 1), lambda s, *_: (s, 0, 0))],
            out_specs=[pl.BlockSpec(memory_space=pl.ANY),
                       pl.BlockSpec((TOP_K, ts), lambda s, *_: (0, s))],
            scratch_shapes=[pltpu.VMEM((2, nchunks * SEG_ALIGN, D // 2), U32),
                            pltpu.VMEM((SEG_ALIGN, D // 2), U32),
                            pltpu.SemaphoreType.DMA((2,)),
                            pltpu.SemaphoreType.DMA(())]),
        out_shape=[jax.ShapeDtypeStruct((n_rows, D // 2), U32),
                   jax.ShapeDtypeStruct((TOP_K, T), I32)],
        compiler_params=_params(("arbitrary",)),
        name="dispatch",
    )(fill_start, fill_n, gchunk.reshape(nt, 1, nchunks), h2, idxt, loff_col)


def _expert_body(te_ref, nu_ref, tend_ref, x_ref, wu_hbm, bu_ref, wd_hbm, bd_ref, o_ref,
                 wu_f32, wd_f32, wu_bf, wd_bf, slot_ref, wsem):
    i = pl.program_id(0)
    active = i < nu_ref[0]
    e = te_ref[i]
    new_expert = jnp.logical_or(i == 0, e != te_ref[jnp.maximum(i - 1, 0)])
    half = MXU_COLS // 2

    def weight_copies(expert, slot):
        return (pltpu.make_async_copy(wu_hbm.at[expert], wu_f32.at[slot], wsem.at[0, slot]),
                pltpu.make_async_copy(wd_hbm.at[expert], wd_f32.at[slot], wsem.at[1, slot]))

    @pl.when(i == 0)
    def _():
        slot_ref[0] = 1
        for cp in weight_copies(e, 0):
            cp.start()

    @pl.when(jnp.logical_and(active, new_expert))
    def _():
        slot = 1 - slot_ref[0]
        slot_ref[0] = slot
        for cp in weight_copies(e, slot):
            cp.wait()
        first_of_next = tend_ref[e]

        @pl.when(first_of_next < nu_ref[0])
        def _():
            for cp in weight_copies(te_ref[jnp.minimum(first_of_next, te_ref.shape[0] - 1)], 1 - slot):
                cp.start()

        r = lax.broadcasted_iota(I32, (MXU_COLS, MXU_COLS), 0)
        c = lax.broadcasted_iota(I32, (MXU_COLS, MXU_COLS), 1)
        perm = jnp.where(r == jnp.where(c < half, 2 * c, 2 * (c - half) + 1), 1.0, 0.0).astype(BF16)
        for b in range(wu_bf.shape[-1] // MXU_COLS):
            cols = slice(b * MXU_COLS, (b + 1) * MXU_COLS)
            wu_bf[:, cols] = _dot(wu_f32[slot, :, cols].astype(BF16), perm).astype(BF16)
        wd_bf[...] = wd_f32[slot].astype(BF16)

    @pl.when(active)
    def _():
        x = jnp.concatenate(_unpack_halves(x_ref[...]), axis=-1)
        hid = _dot(x, wu_bf[...]) + bu_ref[...]
        acts = []
        for b in range(hid.shape[-1] // MXU_COLS):
            x_glu = jnp.minimum(hid[:, b * MXU_COLS:b * MXU_COLS + half], SWIGLU_LIMIT)
            x_lin = jnp.clip(hid[:, b * MXU_COLS + half:(b + 1) * MXU_COLS], -SWIGLU_LIMIT, SWIGLU_LIMIT)
            acts.append(x_glu * _sigmoid(SWIGLU_ALPHA * x_glu) * (x_lin + 1.0))
        act = jnp.concatenate(acts, axis=-1).astype(BF16)
        out = _dot(act, wd_bf[...]) + bd_ref[...]
        o_ref[...] = _pack_halves(out.astype(BF16).astype(F32))


def _experts(xs, tile_expert, n_used, tile_end, wu, bu, wd, bd, n_tiles):
    tm = TM_EXPERT
    D = wu.shape[1]
    F2 = wu.shape[-1]
    rows = lambda i, te, nu, tend: (jnp.minimum(i, nu[0] - 1), 0)
    bsel = lambda i, te, nu, tend: (te[i], 0, 0)
    return pl.pallas_call(
        _expert_body,
        grid_spec=pltpu.PrefetchScalarGridSpec(
            num_scalar_prefetch=3,
            grid=(n_tiles,),
            in_specs=[pl.BlockSpec((tm, D // 2), rows),
                      pl.BlockSpec(memory_space=pl.ANY),
                      pl.BlockSpec((None, 1, F2), bsel),
                      pl.BlockSpec(memory_space=pl.ANY),
                      pl.BlockSpec((None, 1, D), bsel)],
            out_specs=pl.BlockSpec((tm, D // 2), rows),
            scratch_shapes=[pltpu.VMEM((2, D, F2), F32), pltpu.VMEM((2, F2 // 2, D), F32),
                            pltpu.VMEM((D, F2), BF16), pltpu.VMEM((F2 // 2, D), BF16),
                            pltpu.SMEM((1,), I32), pltpu.SemaphoreType.DMA((2, 2))]),
        out_shape=jax.ShapeDtypeStruct((n_tiles * tm, D // 2), U32),
        compiler_params=_params(("arbitrary",), vmem=VMEM_LIMIT_LARGE),
        name="experts",
    )(tile_expert, n_used, tile_end, xs, wu, bu, wd, bd)


def _combine_body(gch_ref, ys_ref, x1_ref, rg_ref, o_ref, buf_ref, sems):
    s = pl.program_id(0)
    slot = s % 2
    nchunks = gch_ref.shape[-1]

    def gather(which, sl):
        for c in range(nchunks):
            _chunk_copy(ys_ref, gch_ref[0, which, c], buf_ref.at[sl], c, sems.at[sl]).start()

    @pl.when(s == 0)
    def _():
        gather(0, 0)

    @pl.when(s < pl.num_programs(0) - 1)
    def _():
        gather(1, 1 - slot)

    for c in range(nchunks):
        _chunk_copy(ys_ref, 0, buf_ref.at[slot], c, sems.at[slot]).wait()
    rg = rg_ref[...]
    rows = rg[:, :TOP_K]
    g = lax.bitcast_convert_type(rg[:, TOP_K:], F32)
    ci = lax.broadcasted_iota(I32, (rows.shape[0], nchunks * SEG_ALIGN), 1)
    w = jnp.zeros(ci.shape, F32)
    for k in range(TOP_K):
        w = jnp.where(ci == rows[:, k:k + 1], g[:, k:k + 1], w)
    w = w.astype(BF16)
    half = x1_ref.shape[-1] // 2
    lo, hi = _unpack_halves(buf_ref[slot])
    o_ref[:, :half] = x1_ref[:, :half] + _dot(w, lo)
    o_ref[:, half:] = x1_ref[:, half:] + _dot(w, hi)


def _combine(ys, gchunk, x1, rowt, gatest, ts):
    T, D = x1.shape
    nt = T // ts
    nchunks = gchunk.shape[-1]
    gch2 = jnp.stack([gchunk, jnp.concatenate([gchunk[1:], gchunk[-1:]], axis=0)], axis=1)
    rg = jnp.concatenate([rowt, lax.bitcast_convert_type(gatest, I32)], axis=0).T
    return pl.pallas_call(
        _combine_body,
        grid=(nt,),
        in_specs=[pl.BlockSpec((1, 2, nchunks), lambda s: (s, 0, 0), memory_space=pltpu.SMEM),
                  pl.BlockSpec(memory_space=pl.ANY),
                  pl.BlockSpec((ts, D), lambda s: (s, 0)),
                  pl.BlockSpec((ts, 2 * TOP_K), lambda s: (s, 0))],
        out_specs=pl.BlockSpec((ts, D), lambda s: (s, 0)),
        out_shape=jax.ShapeDtypeStruct((T, D), F32),
        scratch_shapes=[pltpu.VMEM((2, nchunks * SEG_ALIGN, D // 2), U32),
                        pltpu.SemaphoreType.DMA((2,))],
        compiler_params=_params(("arbitrary",)),
        name="combine",
    )(gch2, ys, x1, rg)


def _excl_cumsum(a, axis):
    return jnp.cumsum(a, axis=axis) - a


def _routed_experts(h2, x1, idxt, gatest, cnt, w_up, b_up, w_down, b_down):
    T, D = x1.shape
    E = w_up.shape[0]
    ts = min(TS_SORT, T)
    nt = T // ts
    nchunks = (ts * TOP_K + E * SEG_ALIGN) // SEG_ALIGN
    max_rows = T * TOP_K + nt * E * (SEG_ALIGN - 1) + E * (TM_EXPERT - 1)
    n_tiles = -(-max_rows // TM_EXPERT)
    n_rows = n_tiles * TM_EXPERT
    dump_chunk = n_rows // SEG_ALIGN

    cnt = cnt[:, :, :min(TM_MERGE, T) // ts].transpose(0, 2, 1).reshape(nt, E).astype(I32)
    seg = -(-cnt // SEG_ALIGN)
    loff = _excl_cumsum(seg, 1)
    chunks_e = jnp.sum(seg, axis=0)
    tiles_e = -(-(chunks_e * SEG_ALIGN) // TM_EXPERT)
    tile_end = jnp.cumsum(tiles_e)
    base = (tile_end - tiles_e) * (TM_EXPERT // SEG_ALIGN)
    gstart = base[None, :] + _excl_cumsum(seg, 0)
    c = jnp.arange(nchunks, dtype=I32)
    owner = jnp.sum((loff + seg)[:, None, :] <= c[None, :, None], axis=-1)
    onehot = owner[:, :, None] == jnp.arange(E, dtype=I32)[None, None, :]
    shift = jnp.sum(jnp.where(onehot, (gstart - loff)[:, None, :], 0), axis=-1)
    used = owner < E
    gchunk = shift + c[None, :]
    dump = dump_chunk + (jnp.arange(nt, dtype=I32)[:, None] % 2) * nchunks + c[None, :]
    gchunk_out = jnp.where(used, gchunk, dump).astype(I32)
    gchunk_in = jnp.where(used, gchunk, 0).astype(I32)
    fill_start = (base + chunks_e).astype(I32)
    fill_n = (tiles_e * (TM_EXPERT // SEG_ALIGN) - chunks_e).astype(I32)
    n_used = tile_end[-1:].astype(I32)
    tile_ids = jnp.minimum(jnp.arange(n_tiles, dtype=I32), n_used[0] - 1)
    tile_expert = jnp.minimum(jnp.sum(tile_ids[:, None] >= tile_end[None, :], axis=-1), E - 1).astype(I32)
    loff_col = jnp.zeros((nt, LANES, 1), F32).at[:, :E, 0].set((loff * SEG_ALIGN).astype(F32))

    xs, rowt = _dispatch(h2, idxt, loff_col, gchunk_out, fill_start, fill_n,
                         n_rows + 2 * nchunks * SEG_ALIGN, ts)
    half = MXU_COLS // 2
    bu = b_up.reshape(E, -1, half, 2).transpose(0, 1, 3, 2).reshape(E, 1, -1)
    ys = _experts(xs, tile_expert, n_used, tile_end.astype(I32), w_up, bu, w_down, b_down[:, None, :], n_tiles)
    return _combine(ys, gchunk_in, x1, rowt, gatest, ts)


def _layer(x, mem, mix_norm_w, mem_norm_w, w_in, swa_q_norm_w, swa_k_norm_w, swa_sinks, ret_norm_w,
           w_mem_kv, mem_q_norm_w, mem_k_norm_w, w_br_swa, w_br_ret, w_br_mem, w_out, ffn_norm_w,
           w_router, b_router, w_up, b_up, w_down, b_down):
    B, S, D = x.shape
    M = mem.shape[1]
    T = B * S

    mkv = _memkv(mem.reshape(B * M, D), mem_norm_w, w_mem_kv.astype(BF16), mem_k_norm_w, B, M)
    o_swa, o_ret, o_mem, zg = _mixers(x, mix_norm_w, w_in.astype(BF16), swa_q_norm_w, swa_k_norm_w, swa_sinks,
                                      ret_norm_w, mkv, mem_q_norm_w)

    E = w_router.shape[-1]
    x1, h2, idxt, gatest, cnt = _merge(
        x.reshape(T, D), o_swa, o_ret, o_mem, zg, w_br_swa.astype(BF16), w_br_ret.astype(BF16),
        w_br_mem.astype(BF16), w_out.astype(BF16), ffn_norm_w, w_router.T, b_router.reshape(E, 1))

    out = _routed_experts(h2, x1, idxt, gatest, cnt, w_up, b_up, w_down, b_down)
    return out.reshape(B, S, D)


def kernel(x, mem, mix_norm_w, mem_norm_w, w_in, swa_q_norm_w, swa_k_norm_w, swa_sinks, ret_norm_w, w_mem_kv, mem_q_norm_w, mem_k_norm_w, w_br_swa, w_br_ret, w_br_mem, w_out, ffn_norm_w, w_router, b_router, w_up, b_up, w_down, b_down):
    args = (x, mem, mix_norm_w, mem_norm_w, w_in, swa_q_norm_w, swa_k_norm_w, swa_sinks, ret_norm_w,
            w_mem_kv, mem_q_norm_w, mem_k_norm_w, w_br_swa, w_br_ret, w_br_mem, w_out, ffn_norm_w,
            w_router, b_router, w_up, b_up, w_down, b_down)
    for l in range(w_in.shape[0]):
        x = _layer(x, mem, *[a[l] for a in args[2:]])
    return x
```

```python
import jax
import jax.numpy as jnp
from jax import lax
from jax.experimental import pallas as pl
from jax.experimental.pallas import tpu as pltpu

F32 = jnp.float32
BF16 = jnp.bfloat16
I32 = jnp.int32
U32 = jnp.uint32

D_MODEL = 1024
SWA_HEAD_DIM = 64
SWA_Q_HEADS = 8
SWA_KV_HEADS = 2
SWA_GROUP = SWA_Q_HEADS // SWA_KV_HEADS
WINDOW = 128
BAND_BLOCK = 128
RET_HEADS = 4
RET_QK_DIM = 128
RET_V_DIM = 256
RET_CHUNK = 128
ROPE_BASE = 10000.0
MEM_HEADS = 4
MEM_HEAD_DIM = 128
N_BRANCH = 3
N_EXPERTS = 32
TOP_K = 4
D_FF = 1024
SWIGLU_LIMIT = 7.0
SWIGLU_ALPHA = 1.702
EPS = 1e-6
NEG_INF = -1e30

SWA_Q_W = SWA_Q_HEADS * SWA_HEAD_DIM
SWA_KV_W = SWA_KV_HEADS * SWA_HEAD_DIM
RET_QK_W = RET_HEADS * RET_QK_DIM
RET_V_W = RET_HEADS * RET_V_DIM
MEM_W = MEM_HEADS * MEM_HEAD_DIM
IN_SIZES = (SWA_Q_W, SWA_KV_W, SWA_KV_W, RET_QK_W, RET_QK_W, RET_V_W, RET_V_W, MEM_W, N_BRANCH * D_MODEL)

SUBLANES = 8
LANES = 128
MXU_COLS = 256
VMEM_LIMIT = 56 * 1024 * 1024
VMEM_LIMIT_LARGE = 62 * 1024 * 1024

TM_MERGE = 1024
TM_EXPERT = 512
BATCH_PER_STEP = 4
TS_SORT = 256
SEG_ALIGN = SUBLANES
LOCAL_ROWS = TS_SORT * TOP_K + N_EXPERTS * SEG_ALIGN
LOCAL_CHUNKS = LOCAL_ROWS // SEG_ALIGN


def _params(sem, vmem=VMEM_LIMIT):
    return pltpu.CompilerParams(dimension_semantics=sem, vmem_limit_bytes=vmem)


def _const_spec(shape):
    nd = len(shape)
    return pl.BlockSpec(shape, lambda *_: (0,) * nd, pipeline_mode=pl.Buffered(1))


def _rms(x, w):
    return x * lax.rsqrt(jnp.mean(x * x, axis=-1, keepdims=True) + EPS) * w


def _sigmoid(x):
    return 0.5 * jnp.tanh(0.5 * x) + 0.5


def _split_bf16(a):
    hi = a.astype(BF16)
    lo = (a - hi.astype(F32)).astype(BF16)
    return hi, lo


def _pack_halves(a):
    n = a.shape[-1] // 2
    lo = lax.bitcast_convert_type(a[:, :n], U32) >> 16
    hi = lax.bitcast_convert_type(a[:, n:], U32) & jnp.uint32(0xFFFF0000)
    return lo | hi


def _unpack_halves(w):
    lo = lax.bitcast_convert_type(w << 16, F32)
    hi = lax.bitcast_convert_type(w & jnp.uint32(0xFFFF0000), F32)
    return lo.astype(BF16), hi.astype(BF16)


def _dot(a, b):
    return jnp.dot(a, b, preferred_element_type=F32)


def _dot_nt(a, b):
    return lax.dot_general(a, b, (((1,), (1,)), ((), ())), preferred_element_type=F32)


def _dot_tn(a, b):
    return lax.dot_general(a, b, (((0,), (0,)), ((), ())), preferred_element_type=F32)


def _head_rms(t, seg_ref, w):
    ss = _dot((t * t).astype(BF16), seg_ref[...])
    return t * lax.rsqrt(ss * (1.0 / SWA_HEAD_DIM) + EPS) * w


def _swa_block(sink_ref, q, kc, kp, vc, vp, qw_ref, kw_ref, segq_ref, segk_ref, mask):
    d = SWA_HEAD_DIM
    q = _head_rms(q.astype(F32), segq_ref, qw_ref[...]) * (d ** -0.5)
    q = q.astype(BF16)
    k = jnp.concatenate([kp, kc], axis=0).astype(F32)
    k = _head_rms(k, segk_ref, kw_ref[...])
    v = jnp.concatenate([vp, vc], axis=0).astype(F32)
    upper = lax.broadcasted_iota(I32, k.shape, 1) >= d
    k_sw = pltpu.roll(k, d, 1)
    v_sw = pltpu.roll(v, d, 1)

    def placed(t, t_sw):
        return [[jnp.where(upper if half == 1 else jnp.logical_not(upper), t if h == half else t_sw,
                           0.0).astype(BF16) for half in range(2)] for h in range(SWA_KV_HEADS)]

    k_at = placed(k, k_sw)
    v_at = placed(v, v_sw)
    pairs = []
    for j in range(SWA_Q_HEADS // 2):
        h = (2 * j) // SWA_GROUP
        qb = q[:, 2 * j * d:2 * (j + 1) * d]
        acc = None
        for half in range(2):
            s = jnp.where(mask, _dot_nt(qb, k_at[h][half]), NEG_INF)
            sink = sink_ref[2 * j + half]
            m = jnp.maximum(jnp.max(s, axis=-1, keepdims=True), sink)
            p = jnp.exp(s - m)
            den = jnp.sum(p, axis=-1, keepdims=True) + jnp.exp(sink - m)
            o = _dot(p.astype(BF16), v_at[h][half]) * (1.0 / den)
            acc = o if acc is None else acc + o
        pairs.append(acc)
    return jnp.concatenate(pairs, axis=-1)


def _ret_block(b, q, k, v, g, cs_ref, dm_ref, qd_ref, kd_ref, cd_ref, nw_ref, o_ref, st_ref):
    dk, dv = RET_QK_DIM, RET_V_DIM
    cos = cs_ref[:, :dk]
    sin = cs_ref[:, dk:]
    outs = []
    for h in range(RET_HEADS):
        qh = q[b, :, h * dk:(h + 1) * dk].astype(F32)
        kh = k[b, :, h * dk:(h + 1) * dk].astype(F32)
        qr = qh * cos + pltpu.roll(qh, dk // 2, 1) * sin
        kr = (kh * cos + pltpu.roll(kh, dk // 2, 1) * sin) * (dk ** -0.5)
        vh = v[b, :, h * dv:(h + 1) * dv]
        st = st_ref[b, h]
        inner = _dot_nt(qr.astype(BF16), kr.astype(BF16)) * dm_ref[h]
        o = _dot(inner.astype(BF16), vh) + _dot((qr * qd_ref[h]).astype(BF16), st.astype(BF16))
        st_ref[b, h] = st * cd_ref[h] + _dot_tn((kr * kd_ref[h]).astype(BF16), vh)
        mu = jnp.mean(o, axis=-1, keepdims=True)
        oc = o - mu
        var = jnp.mean(oc * oc, axis=-1, keepdims=True)
        y = oc * lax.rsqrt(var + EPS) * nw_ref[:, h * dv:(h + 1) * dv]
        gh = g[b, :, h * dv:(h + 1) * dv].astype(F32)
        outs.append(gh * _sigmoid(gh) * y)
    o_ref[b] = jnp.concatenate(outs, axis=-1).astype(o_ref.dtype)


def _memkv_body(m_ref, nw_ref, w_ref, kw_ref, kv_ref):
    m = _rms(m_ref[...], nw_ref[...]).astype(BF16)
    kv = _dot(m, w_ref[...])
    d = MEM_HEAD_DIM
    ks = [_rms(kv[:, h * d:(h + 1) * d], kw_ref[...]) for h in range(MEM_HEADS)]
    kv_ref[...] = jnp.concatenate(ks + [kv[:, MEM_W:]], axis=-1).astype(kv_ref.dtype)


def _memkv(mem2, mem_norm_w, w_kv_bf, k_norm_w, B, M):
    D = mem2.shape[-1]
    return pl.pallas_call(
        _memkv_body,
        grid=(B,),
        in_specs=[pl.BlockSpec((M, D), lambda b: (b, 0)),
                  _const_spec((1, D)),
                  _const_spec(w_kv_bf.shape),
                  _const_spec((1, MEM_HEAD_DIM))],
        out_specs=pl.BlockSpec((M, 2 * MEM_W), lambda b: (b, 0)),
        out_shape=jax.ShapeDtypeStruct((B * M, 2 * MEM_W), BF16),
        compiler_params=_params(("parallel",)),
        name="memkv",
    )(mem2, mem_norm_w.reshape(1, D), w_kv_bf, k_norm_w.reshape(1, -1))


def _mem_block(q, kv_ref, b, qw_ref):
    d = MEM_HEAD_DIM
    outs = []
    for h in range(MEM_HEADS):
        qh = _rms(q[:, h * d:(h + 1) * d].astype(F32), qw_ref[...]).astype(BF16)
        s = _dot_nt(qh, kv_ref[b, :, h * d:(h + 1) * d]) * (d ** -0.5)
        m = jnp.max(s, axis=-1, keepdims=True)
        p = jnp.exp(s - m)
        o = _dot(p.astype(BF16), kv_ref[b, :, MEM_W + h * d:MEM_W + (h + 1) * d])
        outs.append(o * (1.0 / jnp.sum(p, axis=-1, keepdims=True)))
    return jnp.concatenate(outs, axis=-1)


def _mixers_body(sink_ref, cd_ref, x_ref, nw_ref, w_ref, sqw_ref, skw_ref, segq_ref, segk_ref,
                 cs_ref, dm_ref, qd_ref, kd_ref, rnw_ref, kv_ref, mqw_ref,
                 osw_ref, ort_ref, omm_ref, zg_ref, st_ref, kprev_ref, vprev_ref):
    n = pl.program_id(1)
    bs, L, D = x_ref.shape

    @pl.when(n == 0)
    def _():
        st_ref[...] = jnp.zeros_like(st_ref)
        kprev_ref[...] = jnp.zeros_like(kprev_ref)
        vprev_ref[...] = jnp.zeros_like(vprev_ref)

    h = _rms(x_ref[...].reshape(bs * L, D), nw_ref[...]).astype(BF16)
    offs = [sum(IN_SIZES[:i]) for i in range(len(IN_SIZES) + 1)]

    def proj(i):
        z = _dot(h, w_ref[:, offs[i]:offs[i + 1]]).astype(BF16)
        return z.reshape(bs, L, IN_SIZES[i])

    sq, sk, sv = proj(0), proj(1), proj(2)
    qi = lax.broadcasted_iota(I32, (L, 2 * L), 0)
    kj = lax.broadcasted_iota(I32, (L, 2 * L), 1)
    diff = L + qi - kj
    has_prev = jnp.minimum(n, 1) * L
    mask = (diff >= 0) & (diff < WINDOW) & (kj + has_prev >= L)
    for b in range(bs):
        osw_ref[b] = _swa_block(sink_ref, sq[b], sk[b], kprev_ref[b], sv[b], vprev_ref[b],
                                sqw_ref, skw_ref, segq_ref, segk_ref, mask).astype(osw_ref.dtype)
    kprev_ref[...] = sk
    vprev_ref[...] = sv
    rq, rk, rv, rg = proj(3), proj(4), proj(5), proj(6)
    for b in range(bs):
        _ret_block(b, rq, rk, rv, rg, cs_ref, dm_ref, qd_ref, kd_ref, cd_ref, rnw_ref, ort_ref, st_ref)
    mq = proj(7)
    for b in range(bs):
        omm_ref[b] = _mem_block(mq[b], kv_ref, b, mqw_ref).astype(omm_ref.dtype)
    zg_ref[...] = proj(8)


def _mixers(x, mix_norm_w, w_in_bf, swa_q_norm_w, swa_k_norm_w, sinks, ret_norm_w, mkv, mem_q_norm_w):
    B, S, D = x.shape
    M = mkv.shape[0] // B
    L = BAND_BLOCK
    nb = S // L
    bs = BATCH_PER_STEP if B % BATCH_PER_STEP == 0 else 1
    d = SWA_HEAD_DIM
    group = jnp.arange(SWA_Q_W, dtype=I32) // d
    seg_q = (group[:, None] == group[None, :]).astype(BF16)
    seg_k = seg_q[:SWA_KV_W, :SWA_KV_W]
    half = RET_QK_DIM // 2
    inv = 1.0 / (ROPE_BASE ** (jnp.arange(0, half, dtype=F32) / half))
    ang = jnp.arange(S, dtype=F32)[:, None] * inv[None, :]
    cos = jnp.cos(ang)
    sin = jnp.sin(ang)
    cos_sin = jnp.concatenate([cos, cos, -sin, sin], axis=-1)
    log_g = jnp.log1p(-(2.0 ** (-5.0 - jnp.arange(RET_HEADS, dtype=F32))))
    pos = jnp.arange(L, dtype=F32)
    dpos = pos[:, None] - pos[None, :]
    decay_mask = jnp.where(dpos[None] >= 0,
                           jnp.exp(jnp.maximum(dpos, 0.0)[None] * log_g[:, None, None]), 0.0)
    q_decay = jnp.exp((pos[None, :] + 1.0) * log_g[:, None])[..., None]
    k_decay = jnp.exp((L - 1.0 - pos[None, :]) * log_g[:, None])[..., None]
    chunk_decay = jnp.exp(L * log_g)
    qd = jnp.broadcast_to(q_decay, (RET_HEADS, L, RET_QK_DIM))
    kd = jnp.broadcast_to(k_decay, (RET_HEADS, L, RET_QK_DIM))
    blk = lambda w: pl.BlockSpec((bs, L, w), lambda b, n: (b, n, 0))
    smem = pl.BlockSpec(memory_space=pltpu.SMEM)
    outs = pl.pallas_call(
        _mixers_body,
        grid=(B // bs, nb),
        in_specs=[smem, smem,
                  blk(D), _const_spec((1, D)), _const_spec(w_in_bf.shape),
                  _const_spec((1, SWA_Q_W)), _const_spec((1, SWA_KV_W)),
                  _const_spec(seg_q.shape), _const_spec(seg_k.shape),
                  pl.BlockSpec((L, 2 * RET_QK_DIM), lambda b, n: (n, 0)),
                  _const_spec((RET_HEADS, L, L)),
                  _const_spec((RET_HEADS, L, RET_QK_DIM)),
                  _const_spec((RET_HEADS, L, RET_QK_DIM)),
                  _const_spec((1, RET_V_W)),
                  pl.BlockSpec((bs, M, 2 * MEM_W), lambda b, n: (b, 0, 0)),
                  _const_spec((1, MEM_HEAD_DIM))],
        out_specs=[blk(SWA_Q_W), blk(RET_V_W), blk(MEM_W), blk(N_BRANCH * D)],
        out_shape=[jax.ShapeDtypeStruct((B, S, w), BF16) for w in (SWA_Q_W, RET_V_W, MEM_W, N_BRANCH * D)],
        scratch_shapes=[pltpu.VMEM((bs, RET_HEADS, RET_QK_DIM, RET_V_DIM), F32),
                        pltpu.VMEM((bs, L, SWA_KV_W), BF16),
                        pltpu.VMEM((bs, L, SWA_KV_W), BF16)],
        compiler_params=_params(("parallel", "arbitrary"), vmem=VMEM_LIMIT_LARGE),
        name="mixers",
    )(sinks.astype(F32), chunk_decay, x, mix_norm_w.reshape(1, D), w_in_bf,
      jnp.tile(swa_q_norm_w, SWA_Q_HEADS).reshape(1, -1), jnp.tile(swa_k_norm_w, SWA_KV_HEADS).reshape(1, -1),
      seg_q, seg_k, cos_sin, decay_mask, qd, kd, ret_norm_w.reshape(1, -1),
      mkv.reshape(B, M, 2 * MEM_W), mem_q_norm_w.reshape(1, -1))
    return [o.reshape(B * S, o.shape[-1]) for o in outs]


def _merge_body(x_ref, osw_ref, ort_ref, omm_ref, zg_ref, wa_ref, wr_ref, wm_ref, wo_ref, nw_ref,
                wrt_ref, brt_ref, x1_ref, h2_ref, idx_ref, gate_ref, cnt_ref):
    D = D_MODEL
    sg = lambda j: _sigmoid(zg_ref[:, j * D:(j + 1) * D].astype(F32))
    merged = (sg(0) * _dot(osw_ref[...], wa_ref[...])
              + sg(1) * _dot(ort_ref[...], wr_ref[...])
              + sg(2) * _dot(omm_ref[...], wm_ref[...]))
    x1 = x_ref[...] + _dot(merged.astype(BF16), wo_ref[...])
    x1_ref[...] = x1
    h2 = _rms(x1, nw_ref[...])
    h2_ref[...] = h2.astype(h2_ref.dtype)
    h_hi, h_lo = _split_bf16(h2)
    w_hi, w_lo = _split_bf16(wrt_ref[...])
    logits = _dot_nt(w_hi, h_hi) + _dot_nt(w_hi, h_lo) + _dot_nt(w_lo, h_hi) + brt_ref[...]
    n_exp, tm = logits.shape
    sub = lax.broadcasted_iota(I32, logits.shape, 0)
    vals, idxs = [], []
    l = logits
    for _ in range(TOP_K):
        m = jnp.max(l, axis=0, keepdims=True)
        i = jnp.min(jnp.where(l == m, sub, n_exp), axis=0, keepdims=True)
        vals.append(m)
        idxs.append(i)
        l = jnp.where(sub == i, -jnp.inf, l)
    es = [jnp.exp(v - vals[0]) for v in vals]
    den = es[0] + es[1] + es[2] + es[3]
    idx_ref[...] = jnp.concatenate(idxs, axis=0)
    gate_ref[...] = jnp.concatenate([e / den for e in es], axis=0)
    chosen = sum((sub == i).astype(F32) for i in idxs)
    ts = min(TS_SORT, tm)
    lane = lax.broadcasted_iota(I32, cnt_ref.shape, 1)
    cnt_ref[...] = sum(jnp.where(lane == j, jnp.sum(chosen[:, j * ts:(j + 1) * ts], axis=1, keepdims=True), 0.0)
                       for j in range(tm // ts))


def _merge(x2, o_swa, o_ret, o_mem, zg, wa, wr, wm, wo, ffn_norm_w, w_router_t, b_router_col):
    T, D = x2.shape
    E = w_router_t.shape[0]
    tm = min(TM_MERGE, T)
    row = lambda i: (i, 0)
    return pl.pallas_call(
        _merge_body,
        grid=(T // tm,),
        in_specs=[pl.BlockSpec((tm, D), row),
                  pl.BlockSpec((tm, SWA_Q_W), row),
                  pl.BlockSpec((tm, RET_V_W), row),
                  pl.BlockSpec((tm, MEM_W), row),
                  pl.BlockSpec((tm, N_BRANCH * D), row),
                  _const_spec(wa.shape), _const_spec(wr.shape), _const_spec(wm.shape), _const_spec(wo.shape),
                  _const_spec((1, D)),
                  _const_spec(w_router_t.shape), _const_spec((E, 1))],
        out_specs=[pl.BlockSpec((tm, D), row),
                   pl.BlockSpec((tm, D), row),
                   pl.BlockSpec((TOP_K, tm), lambda i: (0, i)),
                   pl.BlockSpec((TOP_K, tm), lambda i: (0, i)),
                   pl.BlockSpec((None, E, LANES), lambda i: (i, 0, 0))],
        out_shape=[jax.ShapeDtypeStruct((T, D), F32),
                   jax.ShapeDtypeStruct((T, D), BF16),
                   jax.ShapeDtypeStruct((TOP_K, T), I32),
                   jax.ShapeDtypeStruct((TOP_K, T), F32),
                   jax.ShapeDtypeStruct((T // tm, E, LANES), F32)],
        compiler_params=_params(("parallel",)),
        name="merge",
    )(x2, o_swa, o_ret, o_mem, zg, wa, wr, wm, wo, ffn_norm_w.reshape(1, D), w_router_t, b_router_col)


def _chunk_copy(src_ref, src_chunk, dst_ref, dst_chunk, sem):
    def rows(c):
        start = c * SEG_ALIGN
        return pl.ds(start if isinstance(c, int) else pl.multiple_of(start, SEG_ALIGN), SEG_ALIGN)

    return pltpu.make_async_copy(src_ref.at[rows(src_chunk), :], dst_ref.at[rows(dst_chunk), :], sem)


def _dispatch_body(fill_start_ref, fill_n_ref, nuc_ref, gch_ref, h2_ref, idxt_ref, loff_ref,
                   xs_ref, rowt_ref, buf_ref, zero_ref, sems, zsem):
    s = pl.program_id(0)
    last = pl.num_programs(0) - 1
    slot = s % 2
    ts = h2_ref.shape[0]
    nchunks = gch_ref.shape[-1]
    local_rows = nchunks * SEG_ALIGN
    min_chunks = ts * TOP_K // SEG_ALIGN

    def for_used_chunks(n_used, fn):
        for c in range(min_chunks):
            fn(c)

        def body(c, carry):
            fn(c)
            return carry

        lax.fori_loop(min_chunks, n_used, body, 0)

    def wait_slot(sl, n_used):
        for_used_chunks(n_used, lambda c: _chunk_copy(buf_ref.at[sl], c, xs_ref, c, sems.at[sl]).wait())

    @pl.when(s == 0)
    def _():
        zero_ref[...] = jnp.zeros_like(zero_ref)

        def per_expert(e, total):
            def fill(r, _):
                _chunk_copy(zero_ref, 0, xs_ref, fill_start_ref[e] + r, zsem).start()
                return 0

            lax.fori_loop(0, fill_n_ref[e], fill, 0)
            return total + fill_n_ref[e]

        total = lax.fori_loop(0, N_EXPERTS, per_expert, 0)

        def drain(r, _):
            _chunk_copy(zero_ref, 0, xs_ref, 0, zsem).wait()
            return 0

        lax.fori_loop(0, total, drain, 0)

    @pl.when(s >= 2)
    def _():
        wait_slot(slot, nuc_ref[s - 2])

    idxt = idxt_ref[...]
    sub = lax.broadcasted_iota(I32, (LANES, ts), 0)
    ohs = [(sub == idxt[k:k + 1, :]).astype(F32) for k in range(TOP_K)]
    m = ohs[0] + ohs[1] + ohs[2] + ohs[3]
    tr = lax.broadcasted_iota(I32, (ts, ts), 0)
    tc = lax.broadcasted_iota(I32, (ts, ts), 1)
    earlier = jnp.where(tr < tc, 1.0, 0.0).astype(BF16)
    before = _dot(m.astype(BF16), earlier) + loff_ref[...]
    rows = [jnp.sum(oh * before, axis=0, keepdims=True).astype(I32) for oh in ohs]
    rowt_ref[...] = jnp.concatenate(rows, axis=0)
    ri = lax.broadcasted_iota(I32, (local_rows, ts), 0)
    sel = (ri == rows[0]) | (ri == rows[1]) | (ri == rows[2]) | (ri == rows[3])
    buf_ref[slot] = _pack_halves(_dot(jnp.where(sel, 1.0, 0.0).astype(BF16), h2_ref[...]))
    for_used_chunks(nuc_ref[s],
                    lambda c: _chunk_copy(buf_ref.at[slot], c, xs_ref, gch_ref[0, 0, c], sems.at[slot]).start())

    @pl.when(s == last)
    def _():
        wait_slot(slot, nuc_ref[s])

        @pl.when(s >= 1)
        def _():
            wait_slot(1 - slot, nuc_ref[s - 1])


def _dispatch(h2, idxt, loff_col, gchunk, n_used_chunks, fill_start, fill_n, n_rows, ts):
    T, D = h2.shape
    nt = T // ts
    nchunks = gchunk.shape[-1]
    return pl.pallas_call(
        _dispatch_body,
        grid_spec=pltpu.PrefetchScalarGridSpec(
            num_scalar_prefetch=3,
            grid=(nt,),
            in_specs=[pl.BlockSpec((1, 1, nchunks), lambda s, *_: (s, 0, 0), memory_space=pltpu.SMEM),
                      pl.BlockSpec((ts, D), lambda s, *_: (s, 0)),
                      pl.BlockSpec((TOP_K, ts), lambda s, *_: (0, s)),
                      pl.BlockSpec((None, LANES, 1), lambda s, *_: (s, 0, 0))],
            out_specs=[pl.BlockSpec(memory_space=pl.ANY),
                       pl.BlockSpec((TOP_K, ts), lambda s, *_: (0, s))],
            scratch_shapes=[pltpu.VMEM((2, nchunks * SEG_ALIGN, D // 2), U32),
                            pltpu.VMEM((SEG_ALIGN, D // 2), U32),
                            pltpu.SemaphoreType.DMA((2,)),
                            pltpu.SemaphoreType.DMA(())]),
        out_shape=[jax.ShapeDtypeStruct((n_rows, D // 2), U32),
                   jax.ShapeDtypeStruct((TOP_K, T), I32)],
        compiler_params=_params(("arbitrary",)),
        name="dispatch",
    )(fill_start, fill_n, n_used_chunks, gchunk.reshape(nt, 1, nchunks), h2, idxt, loff_col)


def _expert_body(te_ref, nu_ref, tend_ref, x_ref, wu_hbm, bu_ref, wd_hbm, bd_ref, o_ref,
                 wu_f32, wd_f32, wu_bf, wd_bf, slot_ref, wsem):
    i = pl.program_id(0)
    active = i < nu_ref[0]
    e = te_ref[i]
    new_expert = jnp.logical_or(i == 0, e != te_ref[jnp.maximum(i - 1, 0)])
    half = MXU_COLS // 2

    def weight_copies(expert, slot):
        return (pltpu.make_async_copy(wu_hbm.at[expert], wu_f32.at[slot], wsem.at[0, slot]),
                pltpu.make_async_copy(wd_hbm.at[expert], wd_f32.at[slot], wsem.at[1, slot]))

    @pl.when(i == 0)
    def _():
        slot_ref[0] = 1
        for cp in weight_copies(e, 0):
            cp.start()

    @pl.when(jnp.logical_and(active, new_expert))
    def _():
        slot = 1 - slot_ref[0]
        slot_ref[0] = slot
        for cp in weight_copies(e, slot):
            cp.wait()
        first_of_next = tend_ref[e]

        @pl.when(first_of_next < nu_ref[0])
        def _():
            for cp in weight_copies(te_ref[jnp.minimum(first_of_next, te_ref.shape[0] - 1)], 1 - slot):
                cp.start()

        r = lax.broadcasted_iota(I32, (MXU_COLS, MXU_COLS), 0)
        c = lax.broadcasted_iota(I32, (MXU_COLS, MXU_COLS), 1)
        perm = jnp.where(r == jnp.where(c < half, 2 * c, 2 * (c - half) + 1), 1.0, 0.0).astype(BF16)
        for b in range(wu_bf.shape[-1] // MXU_COLS):
            cols = slice(b * MXU_COLS, (b + 1) * MXU_COLS)
            wu_bf[:, cols] = _dot(wu_f32[slot, :, cols].astype(BF16), perm).astype(BF16)
        wd_bf[...] = wd_f32[slot].astype(BF16)

    @pl.when(active)
    def _():
        x = jnp.concatenate(_unpack_halves(x_ref[...]), axis=-1)
        hid = _dot(x, wu_bf[...]) + bu_ref[...]
        acts = []
        for b in range(hid.shape[-1] // MXU_COLS):
            x_glu = jnp.minimum(hid[:, b * MXU_COLS:b * MXU_COLS + half], SWIGLU_LIMIT)
            x_lin = jnp.clip(hid[:, b * MXU_COLS + half:(b + 1) * MXU_COLS], -SWIGLU_LIMIT, SWIGLU_LIMIT)
            acts.append(x_glu * _sigmoid(SWIGLU_ALPHA * x_glu) * (x_lin + 1.0))
        act = jnp.concatenate(acts, axis=-1).astype(BF16)
        out = _dot(act, wd_bf[...]) + bd_ref[...]
        o_ref[...] = _pack_halves(out.astype(BF16).astype(F32))


def _experts(xs, tile_expert, n_used, tile_end, wu, bu, wd, bd, n_tiles):
    tm = TM_EXPERT
    D = wu.shape[1]
    F2 = wu.shape[-1]
    rows = lambda i, te, nu, tend: (jnp.minimum(i, nu[0] - 1), 0)
    bsel = lambda i, te, nu, tend: (te[i], 0, 0)
    return pl.pallas_call(
        _expert_body,
        grid_spec=pltpu.PrefetchScalarGridSpec(
            num_scalar_prefetch=3,
            grid=(n_tiles,),
            in_specs=[pl.BlockSpec((tm, D // 2), rows),
                      pl.BlockSpec(memory_space=pl.ANY),
                      pl.BlockSpec((None, 1, F2), bsel),
                      pl.BlockSpec(memory_space=pl.ANY),
                      pl.BlockSpec((None, 1, D), bsel)],
            out_specs=pl.BlockSpec((tm, D // 2), rows),
            scratch_shapes=[pltpu.VMEM((2, D, F2), F32), pltpu.VMEM((2, F2 // 2, D), F32),
                            pltpu.VMEM((D, F2), BF16), pltpu.VMEM((F2 // 2, D), BF16),
                            pltpu.SMEM((1,), I32), pltpu.SemaphoreType.DMA((2, 2))]),
        out_shape=jax.ShapeDtypeStruct((n_tiles * tm, D // 2), U32),
        compiler_params=_params(("arbitrary",), vmem=VMEM_LIMIT_LARGE),
        name="experts",
    )(tile_expert, n_used, tile_end, xs, wu, bu, wd, bd)


def _combine_body(nuc_ref, gch_ref, ys_ref, x1_ref, rg_ref, o_ref, buf_ref, sems):
    s = pl.program_id(0)
    slot = s % 2
    nchunks = gch_ref.shape[-1]
    min_chunks = x1_ref.shape[0] * TOP_K // SEG_ALIGN

    def for_used_chunks(n_used, fn):
        for c in range(min_chunks):
            fn(c)

        def body(c, carry):
            fn(c)
            return carry

        lax.fori_loop(min_chunks, n_used, body, 0)

    def gather(which, sl, n_used):
        for_used_chunks(n_used, lambda c: _chunk_copy(ys_ref, gch_ref[0, which, c], buf_ref.at[sl], c,
                                                      sems.at[sl]).start())

    @pl.when(s == 0)
    def _():
        buf_ref[:, min_chunks * SEG_ALIGN:, :] = jnp.zeros_like(buf_ref[:, min_chunks * SEG_ALIGN:, :])
        gather(0, 0, nuc_ref[0])

    @pl.when(s < pl.num_programs(0) - 1)
    def _():
        gather(1, 1 - slot, nuc_ref[s + 1])

    for_used_chunks(nuc_ref[s], lambda c: _chunk_copy(ys_ref, 0, buf_ref.at[slot], c, sems.at[slot]).wait())
    rg = rg_ref[...]
    rows = rg[:, :TOP_K]
    g = lax.bitcast_convert_type(rg[:, TOP_K:], F32)
    ci = lax.broadcasted_iota(I32, (rows.shape[0], nchunks * SEG_ALIGN), 1)
    w = jnp.zeros(ci.shape, F32)
    for k in range(TOP_K):
        w = jnp.where(ci == rows[:, k:k + 1], g[:, k:k + 1], w)
    w = w.astype(BF16)
    half = x1_ref.shape[-1] // 2
    lo, hi = _unpack_halves(buf_ref[slot])
    o_ref[:, :half] = x1_ref[:, :half] + _dot(w, lo)
    o_ref[:, half:] = x1_ref[:, half:] + _dot(w, hi)


def _combine(ys, gchunk, n_used_chunks, x1, rowt, gatest, ts):
    T, D = x1.shape
    nt = T // ts
    nchunks = gchunk.shape[-1]
    gch2 = jnp.stack([gchunk, jnp.concatenate([gchunk[1:], gchunk[-1:]], axis=0)], axis=1)
    rg = jnp.concatenate([rowt, lax.bitcast_convert_type(gatest, I32)], axis=0).T
    return pl.pallas_call(
        _combine_body,
        grid_spec=pltpu.PrefetchScalarGridSpec(
            num_scalar_prefetch=1,
            grid=(nt,),
            in_specs=[pl.BlockSpec((1, 2, nchunks), lambda s, *_: (s, 0, 0), memory_space=pltpu.SMEM),
                      pl.BlockSpec(memory_space=pl.ANY),
                      pl.BlockSpec((ts, D), lambda s, *_: (s, 0)),
                      pl.BlockSpec((ts, 2 * TOP_K), lambda s, *_: (s, 0))],
            out_specs=pl.BlockSpec((ts, D), lambda s, *_: (s, 0)),
            scratch_shapes=[pltpu.VMEM((2, nchunks * SEG_ALIGN, D // 2), U32),
                            pltpu.SemaphoreType.DMA((2,))]),
        out_shape=jax.ShapeDtypeStruct((T, D), F32),
        compiler_params=_params(("arbitrary",)),
        name="combine",
    )(n_used_chunks, gch2, ys, x1, rg)


def _excl_cumsum(a, axis):
    return jnp.cumsum(a, axis=axis) - a


def _routed_experts(h2, x1, idxt, gatest, cnt, w_up, b_up, w_down, b_down):
    T, D = x1.shape
    E = w_up.shape[0]
    ts = min(TS_SORT, T)
    nt = T // ts
    nchunks = (ts * TOP_K + E * SEG_ALIGN) // SEG_ALIGN
    max_rows = T * TOP_K + nt * E * (SEG_ALIGN - 1) + E * (TM_EXPERT - 1)
    n_tiles = -(-max_rows // TM_EXPERT)
    n_rows = n_tiles * TM_EXPERT

    cnt = cnt[:, :, :min(TM_MERGE, T) // ts].transpose(0, 2, 1).reshape(nt, E).astype(I32)
    seg = -(-cnt // SEG_ALIGN)
    loff = _excl_cumsum(seg, 1)
    chunks_e = jnp.sum(seg, axis=0)
    tiles_e = -(-(chunks_e * SEG_ALIGN) // TM_EXPERT)
    tile_end = jnp.cumsum(tiles_e)
    base = (tile_end - tiles_e) * (TM_EXPERT // SEG_ALIGN)
    gstart = base[None, :] + _excl_cumsum(seg, 0)
    c = jnp.arange(nchunks, dtype=I32)
    owner = jnp.sum((loff + seg)[:, None, :] <= c[None, :, None], axis=-1)
    onehot = owner[:, :, None] == jnp.arange(E, dtype=I32)[None, None, :]
    shift = jnp.sum(jnp.where(onehot, (gstart - loff)[:, None, :], 0), axis=-1)
    gchunk = jnp.where(owner < E, shift + c[None, :], 0).astype(I32)
    n_used_chunks = jnp.sum(seg, axis=1).astype(I32)
    fill_start = (base + chunks_e).astype(I32)
    fill_n = (tiles_e * (TM_EXPERT // SEG_ALIGN) - chunks_e).astype(I32)
    n_used = tile_end[-1:].astype(I32)
    tile_ids = jnp.minimum(jnp.arange(n_tiles, dtype=I32), n_used[0] - 1)
    tile_expert = jnp.minimum(jnp.sum(tile_ids[:, None] >= tile_end[None, :], axis=-1), E - 1).astype(I32)
    loff_col = jnp.zeros((nt, LANES, 1), F32).at[:, :E, 0].set((loff * SEG_ALIGN).astype(F32))

    xs, rowt = _dispatch(h2, idxt, loff_col, gchunk, n_used_chunks, fill_start, fill_n, n_rows, ts)
    half = MXU_COLS // 2
    bu = b_up.reshape(E, -1, half, 2).transpose(0, 1, 3, 2).reshape(E, 1, -1)
    ys = _experts(xs, tile_expert, n_used, tile_end.astype(I32), w_up, bu, w_down, b_down[:, None, :], n_tiles)
    return _combine(ys, gchunk, n_used_chunks, x1, rowt, gatest, ts)


def _layer(x, mem, mix_norm_w, mem_norm_w, w_in, swa_q_norm_w, swa_k_norm_w, swa_sinks, ret_norm_w,
           w_mem_kv, mem_q_norm_w, mem_k_norm_w, w_br_swa, w_br_ret, w_br_mem, w_out, ffn_norm_w,
           w_router, b_router, w_up, b_up, w_down, b_down):
    B, S, D = x.shape
    M = mem.shape[1]
    T = B * S

    mkv = _memkv(mem.reshape(B * M, D), mem_norm_w, w_mem_kv.astype(BF16), mem_k_norm_w, B, M)
    o_swa, o_ret, o_mem, zg = _mixers(x, mix_norm_w, w_in.astype(BF16), swa_q_norm_w, swa_k_norm_w, swa_sinks,
                                      ret_norm_w, mkv, mem_q_norm_w)

    E = w_router.shape[-1]
    x1, h2, idxt, gatest, cnt = _merge(
        x.reshape(T, D), o_swa, o_ret, o_mem, zg, w_br_swa.astype(BF16), w_br_ret.astype(BF16),
        w_br_mem.astype(BF16), w_out.astype(BF16), ffn_norm_w, w_router.T, b_router.reshape(E, 1))

    out = _routed_experts(h2, x1, idxt, gatest, cnt, w_up, b_up, w_down, b_down)
    return out.reshape(B, S, D)


def kernel(x, mem, mix_norm_w, mem_norm_w, w_in, swa_q_norm_w, swa_k_norm_w, swa_sinks, ret_norm_w, w_mem_kv, mem_q_norm_w, mem_k_norm_w, w_br_swa, w_br_ret, w_br_mem, w_out, ffn_norm_w, w_router, b_router, w_up, b_up, w_down, b_down):
    args = (x, mem, mix_norm_w, mem_norm_w, w_in, swa_q_norm_w, swa_k_norm_w, swa_sinks, ret_norm_w,
            w_mem_kv, mem_q_norm_w, mem_k_norm_w, w_br_swa, w_br_ret, w_br_mem, w_out, ffn_norm_w,
            w_router, b_router, w_up, b_up, w_down, b_down)
    for l in range(w_in.shape[0]):
        x = _layer(x, mem, *[a[l] for a in args[2:]])
    return x
```

```python
import jax
import jax.numpy as jnp
from jax import lax
from jax.experimental import pallas as pl
from jax.experimental.pallas import tpu as pltpu

F32 = jnp.float32
BF16 = jnp.bfloat16
I32 = jnp.int32
U32 = jnp.uint32

D_MODEL = 1024
SWA_HEAD_DIM = 64
SWA_Q_HEADS = 8
SWA_KV_HEADS = 2
SWA_GROUP = SWA_Q_HEADS // SWA_KV_HEADS
WINDOW = 128
BAND_BLOCK = 128
RET_HEADS = 4
RET_QK_DIM = 128
RET_V_DIM = 256
RET_CHUNK = 128
ROPE_BASE = 10000.0
MEM_HEADS = 4
MEM_HEAD_DIM = 128
N_BRANCH = 3
N_EXPERTS = 32
TOP_K = 4
D_FF = 1024
SWIGLU_LIMIT = 7.0
SWIGLU_ALPHA = 1.702
EPS = 1e-6
NEG_INF = -1e30

SWA_Q_W = SWA_Q_HEADS * SWA_HEAD_DIM
SWA_KV_W = SWA_KV_HEADS * SWA_HEAD_DIM
RET_QK_W = RET_HEADS * RET_QK_DIM
RET_V_W = RET_HEADS * RET_V_DIM
MEM_W = MEM_HEADS * MEM_HEAD_DIM
IN_SIZES = (SWA_Q_W, SWA_KV_W, SWA_KV_W, RET_QK_W, RET_QK_W, RET_V_W, RET_V_W, MEM_W, N_BRANCH * D_MODEL)

SUBLANES = 8
LANES = 128
MXU_COLS = 256
VMEM_LIMIT = 56 * 1024 * 1024
VMEM_LIMIT_LARGE = 62 * 1024 * 1024

TM_MERGE = 1024
TM_EXPERT = 512
BATCH_PER_STEP = 4
TS_SORT = 256
SEG_ALIGN = SUBLANES
LOCAL_ROWS = TS_SORT * TOP_K + N_EXPERTS * SEG_ALIGN
LOCAL_CHUNKS = LOCAL_ROWS // SEG_ALIGN


def _params(sem, vmem=VMEM_LIMIT):
    return pltpu.CompilerParams(dimension_semantics=sem, vmem_limit_bytes=vmem)


def _const_spec(shape):
    nd = len(shape)
    return pl.BlockSpec(shape, lambda *_: (0,) * nd, pipeline_mode=pl.Buffered(1))


def _rms(x, w):
    return x * lax.rsqrt(jnp.mean(x * x, axis=-1, keepdims=True) + EPS) * w


def _sigmoid(x):
    return 0.5 * jnp.tanh(0.5 * x) + 0.5


def _split_bf16(a):
    hi = a.astype(BF16)
    lo = (a - hi.astype(F32)).astype(BF16)
    return hi, lo


def _pack_halves(a):
    n = a.shape[-1] // 2
    lo = lax.bitcast_convert_type(a[:, :n], U32) >> 16
    hi = lax.bitcast_convert_type(a[:, n:], U32) & jnp.uint32(0xFFFF0000)
    return lo | hi


def _unpack_halves(w):
    lo = lax.bitcast_convert_type(w << 16, F32)
    hi = lax.bitcast_convert_type(w & jnp.uint32(0xFFFF0000), F32)
    return lo.astype(BF16), hi.astype(BF16)


def _dot(a, b):
    return jnp.dot(a, b, preferred_element_type=F32)


def _dot_nt(a, b):
    return lax.dot_general(a, b, (((1,), (1,)), ((), ())), preferred_element_type=F32)


def _dot_tn(a, b):
    return lax.dot_general(a, b, (((0,), (0,)), ((), ())), preferred_element_type=F32)


def _head_rms(t, seg_ref, w):
    ss = _dot((t * t).astype(BF16), seg_ref[...])
    return t * lax.rsqrt(ss * (1.0 / SWA_HEAD_DIM) + EPS) * w


def _swa_block(sink_ref, q, kc, kp, vc, vp, qw_ref, kw_ref, segq_ref, segk_ref, mask):
    d = SWA_HEAD_DIM
    q = _head_rms(q.astype(F32), segq_ref, qw_ref[...]) * (d ** -0.5)
    q = q.astype(BF16)
    k = jnp.concatenate([kp, kc], axis=0).astype(F32)
    k = _head_rms(k, segk_ref, kw_ref[...])
    v = jnp.concatenate([vp, vc], axis=0).astype(F32)
    upper = lax.broadcasted_iota(I32, k.shape, 1) >= d
    k_sw = pltpu.roll(k, d, 1)
    v_sw = pltpu.roll(v, d, 1)

    def placed(t, t_sw):
        return [[jnp.where(upper if half == 1 else jnp.logical_not(upper), t if h == half else t_sw,
                           0.0).astype(BF16) for half in range(2)] for h in range(SWA_KV_HEADS)]

    k_at = placed(k, k_sw)
    v_at = placed(v, v_sw)
    pairs = []
    for j in range(SWA_Q_HEADS // 2):
        h = (2 * j) // SWA_GROUP
        qb = q[:, 2 * j * d:2 * (j + 1) * d]
        acc = None
        for half in range(2):
            s = jnp.where(mask, _dot_nt(qb, k_at[h][half]), NEG_INF)
            sink = sink_ref[2 * j + half]
            m = jnp.maximum(jnp.max(s, axis=-1, keepdims=True), sink)
            p = jnp.exp(s - m)
            den = jnp.sum(p, axis=-1, keepdims=True) + jnp.exp(sink - m)
            o = _dot(p.astype(BF16), v_at[h][half]) * (1.0 / den)
            acc = o if acc is None else acc + o
        pairs.append(acc)
    return jnp.concatenate(pairs, axis=-1)


def _ret_block(b, q, k, v, g, cs_ref, dm_ref, qd_ref, kd_ref, cd_ref, nw_ref, o_ref, st_ref):
    dk, dv = RET_QK_DIM, RET_V_DIM
    cos = cs_ref[:, :dk]
    sin = cs_ref[:, dk:]
    outs = []
    for h in range(RET_HEADS):
        qh = q[b, :, h * dk:(h + 1) * dk].astype(F32)
        kh = k[b, :, h * dk:(h + 1) * dk].astype(F32)
        qr = qh * cos + pltpu.roll(qh, dk // 2, 1) * sin
        kr = (kh * cos + pltpu.roll(kh, dk // 2, 1) * sin) * (dk ** -0.5)
        vh = v[b, :, h * dv:(h + 1) * dv]
        st = st_ref[b, h]
        inner = _dot_nt(qr.astype(BF16), kr.astype(BF16)) * dm_ref[h]
        o = _dot(inner.astype(BF16), vh) + _dot((qr * qd_ref[h]).astype(BF16), st.astype(BF16))
        st_ref[b, h] = st * cd_ref[h] + _dot_tn((kr * kd_ref[h]).astype(BF16), vh)
        mu = jnp.mean(o, axis=-1, keepdims=True)
        oc = o - mu
        var = jnp.mean(oc * oc, axis=-1, keepdims=True)
        y = oc * lax.rsqrt(var + EPS) * nw_ref[:, h * dv:(h + 1) * dv]
        gh = g[b, :, h * dv:(h + 1) * dv].astype(F32)
        outs.append(gh * _sigmoid(gh) * y)
    o_ref[b] = jnp.concatenate(outs, axis=-1).astype(o_ref.dtype)


def _memkv_body(m_ref, nw_ref, w_ref, kw_ref, kv_ref):
    m = _rms(m_ref[...], nw_ref[...]).astype(BF16)
    kv = _dot(m, w_ref[...])
    d = MEM_HEAD_DIM
    ks = [_rms(kv[:, h * d:(h + 1) * d], kw_ref[...]) for h in range(MEM_HEADS)]
    kv_ref[...] = jnp.concatenate(ks + [kv[:, MEM_W:]], axis=-1).astype(kv_ref.dtype)


def _memkv(mem2, mem_norm_w, w_kv_bf, k_norm_w, B, M):
    D = mem2.shape[-1]
    return pl.pallas_call(
        _memkv_body,
        grid=(B,),
        in_specs=[pl.BlockSpec((M, D), lambda b: (b, 0)),
                  _const_spec((1, D)),
                  _const_spec(w_kv_bf.shape),
                  _const_spec((1, MEM_HEAD_DIM))],
        out_specs=pl.BlockSpec((M, 2 * MEM_W), lambda b: (b, 0)),
        out_shape=jax.ShapeDtypeStruct((B * M, 2 * MEM_W), BF16),
        compiler_params=_params(("parallel",)),
        name="memkv",
    )(mem2, mem_norm_w.reshape(1, D), w_kv_bf, k_norm_w.reshape(1, -1))


def _mem_block(q, kv_ref, b, qw_ref):
    d = MEM_HEAD_DIM
    outs = []
    for h in range(MEM_HEADS):
        qh = _rms(q[:, h * d:(h + 1) * d].astype(F32), qw_ref[...]).astype(BF16)
        s = _dot_nt(qh, kv_ref[b, :, h * d:(h + 1) * d]) * (d ** -0.5)
        m = jnp.max(s, axis=-1, keepdims=True)
        p = jnp.exp(s - m)
        o = _dot(p.astype(BF16), kv_ref[b, :, MEM_W + h * d:MEM_W + (h + 1) * d])
        outs.append(o * (1.0 / jnp.sum(p, axis=-1, keepdims=True)))
    return jnp.concatenate(outs, axis=-1)


def _mixers_body(sink_ref, cd_ref, x_ref, nw_ref, w_ref, sqw_ref, skw_ref, segq_ref, segk_ref,
                 cs_ref, dm_ref, qd_ref, kd_ref, rnw_ref, kv_ref, mqw_ref,
                 osw_ref, ort_ref, omm_ref, zg_ref, st_ref, kprev_ref, vprev_ref):
    n = pl.program_id(1)
    bs, L, D = x_ref.shape

    @pl.when(n == 0)
    def _():
        st_ref[...] = jnp.zeros_like(st_ref)
        kprev_ref[...] = jnp.zeros_like(kprev_ref)
        vprev_ref[...] = jnp.zeros_like(vprev_ref)

    h = _rms(x_ref[...].reshape(bs * L, D), nw_ref[...]).astype(BF16)
    offs = [sum(IN_SIZES[:i]) for i in range(len(IN_SIZES) + 1)]

    def proj(i):
        z = _dot(h, w_ref[:, offs[i]:offs[i + 1]]).astype(BF16)
        return z.reshape(bs, L, IN_SIZES[i])

    sq, sk, sv = proj(0), proj(1), proj(2)
    qi = lax.broadcasted_iota(I32, (L, 2 * L), 0)
    kj = lax.broadcasted_iota(I32, (L, 2 * L), 1)
    diff = L + qi - kj
    has_prev = jnp.minimum(n, 1) * L
    mask = (diff >= 0) & (diff < WINDOW) & (kj + has_prev >= L)
    for b in range(bs):
        osw_ref[b] = _swa_block(sink_ref, sq[b], sk[b], kprev_ref[b], sv[b], vprev_ref[b],
                                sqw_ref, skw_ref, segq_ref, segk_ref, mask).astype(osw_ref.dtype)
    kprev_ref[...] = sk
    vprev_ref[...] = sv
    rq, rk, rv, rg = proj(3), proj(4), proj(5), proj(6)
    for b in range(bs):
        _ret_block(b, rq, rk, rv, rg, cs_ref, dm_ref, qd_ref, kd_ref, cd_ref, rnw_ref, ort_ref, st_ref)
    mq = proj(7)
    for b in range(bs):
        omm_ref[b] = _mem_block(mq[b], kv_ref, b, mqw_ref).astype(omm_ref.dtype)
    zg_ref[...] = proj(8)


def _mixers(x, mix_norm_w, w_in_bf, swa_q_norm_w, swa_k_norm_w, sinks, ret_norm_w, mkv, mem_q_norm_w):
    B, S, D = x.shape
    M = mkv.shape[0] // B
    L = BAND_BLOCK
    nb = S // L
    bs = BATCH_PER_STEP if B % BATCH_PER_STEP == 0 else 1
    d = SWA_HEAD_DIM
    group = jnp.arange(SWA_Q_W, dtype=I32) // d
    seg_q = (group[:, None] == group[None, :]).astype(BF16)
    seg_k = seg_q[:SWA_KV_W, :SWA_KV_W]
    half = RET_QK_DIM // 2
    inv = 1.0 / (ROPE_BASE ** (jnp.arange(0, half, dtype=F32) / half))
    ang = jnp.arange(S, dtype=F32)[:, None] * inv[None, :]
    cos = jnp.cos(ang)
    sin = jnp.sin(ang)
    cos_sin = jnp.concatenate([cos, cos, -sin, sin], axis=-1)
    log_g = jnp.log1p(-(2.0 ** (-5.0 - jnp.arange(RET_HEADS, dtype=F32))))
    pos = jnp.arange(L, dtype=F32)
    dpos = pos[:, None] - pos[None, :]
    decay_mask = jnp.where(dpos[None] >= 0,
                           jnp.exp(jnp.maximum(dpos, 0.0)[None] * log_g[:, None, None]), 0.0)
    q_decay = jnp.exp((pos[None, :] + 1.0) * log_g[:, None])[..., None]
    k_decay = jnp.exp((L - 1.0 - pos[None, :]) * log_g[:, None])[..., None]
    chunk_decay = jnp.exp(L * log_g)
    qd = jnp.broadcast_to(q_decay, (RET_HEADS, L, RET_QK_DIM))
    kd = jnp.broadcast_to(k_decay, (RET_HEADS, L, RET_QK_DIM))
    blk = lambda w: pl.BlockSpec((bs, L, w), lambda b, n: (b, n, 0))
    smem = pl.BlockSpec(memory_space=pltpu.SMEM)
    outs = pl.pallas_call(
        _mixers_body,
        grid=(B // bs, nb),
        in_specs=[smem, smem,
                  blk(D), _const_spec((1, D)), _const_spec(w_in_bf.shape),
                  _const_spec((1, SWA_Q_W)), _const_spec((1, SWA_KV_W)),
                  _const_spec(seg_q.shape), _const_spec(seg_k.shape),
                  pl.BlockSpec((L, 2 * RET_QK_DIM), lambda b, n: (n, 0)),
                  _const_spec((RET_HEADS, L, L)),
                  _const_spec((RET_HEADS, L, RET_QK_DIM)),
                  _const_spec((RET_HEADS, L, RET_QK_DIM)),
                  _const_spec((1, RET_V_W)),
                  pl.BlockSpec((bs, M, 2 * MEM_W), lambda b, n: (b, 0, 0)),
                  _const_spec((1, MEM_HEAD_DIM))],
        out_specs=[blk(SWA_Q_W), blk(RET_V_W), blk(MEM_W), blk(N_BRANCH * D)],
        out_shape=[jax.ShapeDtypeStruct((B, S, w), BF16) for w in (SWA_Q_W, RET_V_W, MEM_W, N_BRANCH * D)],
        scratch_shapes=[pltpu.VMEM((bs, RET_HEADS, RET_QK_DIM, RET_V_DIM), F32),
                        pltpu.VMEM((bs, L, SWA_KV_W), BF16),
                        pltpu.VMEM((bs, L, SWA_KV_W), BF16)],
        compiler_params=_params(("parallel", "arbitrary"), vmem=VMEM_LIMIT_LARGE),
        name="mixers",
    )(sinks.astype(F32), chunk_decay, x, mix_norm_w.reshape(1, D), w_in_bf,
      jnp.tile(swa_q_norm_w, SWA_Q_HEADS).reshape(1, -1), jnp.tile(swa_k_norm_w, SWA_KV_HEADS).reshape(1, -1),
      seg_q, seg_k, cos_sin, decay_mask, qd, kd, ret_norm_w.reshape(1, -1),
      mkv.reshape(B, M, 2 * MEM_W), mem_q_norm_w.reshape(1, -1))
    return [o.reshape(B * S, o.shape[-1]) for o in outs]


def _merge_body(x_ref, osw_ref, ort_ref, omm_ref, zg_ref, wa_ref, wr_ref, wm_ref, wo_ref, nw_ref,
                wrt_ref, brt_ref, x1_ref, h2_ref, idx_ref, gate_ref, cnt_ref):
    D = D_MODEL
    sg = lambda j: _sigmoid(zg_ref[:, j * D:(j + 1) * D].astype(F32))
    merged = (sg(0) * _dot(osw_ref[...], wa_ref[...])
              + sg(1) * _dot(ort_ref[...], wr_ref[...])
              + sg(2) * _dot(omm_ref[...], wm_ref[...]))
    x1 = x_ref[...] + _dot(merged.astype(BF16), wo_ref[...])
    x1_ref[...] = x1
    h2 = _rms(x1, nw_ref[...])
    h2_ref[...] = h2.astype(h2_ref.dtype)
    h_hi, h_lo = _split_bf16(h2)
    w_hi, w_lo = _split_bf16(wrt_ref[...])
    logits = _dot_nt(w_hi, h_hi) + _dot_nt(w_hi, h_lo) + _dot_nt(w_lo, h_hi) + brt_ref[...]
    n_exp, tm = logits.shape
    sub = lax.broadcasted_iota(I32, logits.shape, 0)
    vals, idxs = [], []
    l = logits
    for _ in range(TOP_K):
        m = jnp.max(l, axis=0, keepdims=True)
        i = jnp.min(jnp.where(l == m, sub, n_exp), axis=0, keepdims=True)
        vals.append(m)
        idxs.append(i)
        l = jnp.where(sub == i, -jnp.inf, l)
    es = [jnp.exp(v - vals[0]) for v in vals]
    den = es[0] + es[1] + es[2] + es[3]
    idx_ref[...] = jnp.concatenate(idxs, axis=0)
    gate_ref[...] = jnp.concatenate([e / den for e in es], axis=0)
    chosen = sum((sub == i).astype(F32) for i in idxs)
    ts = min(TS_SORT, tm)
    lane = lax.broadcasted_iota(I32, cnt_ref.shape, 1)
    cnt_ref[...] = sum(jnp.where(lane == j, jnp.sum(chosen[:, j * ts:(j + 1) * ts], axis=1, keepdims=True), 0.0)
                       for j in range(tm // ts))


def _merge(x2, o_swa, o_ret, o_mem, zg, wa, wr, wm, wo, ffn_norm_w, w_router_t, b_router_col):
    T, D = x2.shape
    E = w_router_t.shape[0]
    tm = min(TM_MERGE, T)
    row = lambda i: (i, 0)
    return pl.pallas_call(
        _merge_body,
        grid=(T // tm,),
        in_specs=[pl.BlockSpec((tm, D), row),
                  pl.BlockSpec((tm, SWA_Q_W), row),
                  pl.BlockSpec((tm, RET_V_W), row),
                  pl.BlockSpec((tm, MEM_W), row),
                  pl.BlockSpec((tm, N_BRANCH * D), row),
                  _const_spec(wa.shape), _const_spec(wr.shape), _const_spec(wm.shape), _const_spec(wo.shape),
                  _const_spec((1, D)),
                  _const_spec(w_router_t.shape), _const_spec((E, 1))],
        out_specs=[pl.BlockSpec((tm, D), row),
                   pl.BlockSpec((tm, D), row),
                   pl.BlockSpec((TOP_K, tm), lambda i: (0, i)),
                   pl.BlockSpec((TOP_K, tm), lambda i: (0, i)),
                   pl.BlockSpec((None, E, LANES), lambda i: (i, 0, 0))],
        out_shape=[jax.ShapeDtypeStruct((T, D), F32),
                   jax.ShapeDtypeStruct((T, D), BF16),
                   jax.ShapeDtypeStruct((TOP_K, T), I32),
                   jax.ShapeDtypeStruct((TOP_K, T), F32),
                   jax.ShapeDtypeStruct((T // tm, E, LANES), F32)],
        compiler_params=_params(("parallel",)),
        name="merge",
    )(x2, o_swa, o_ret, o_mem, zg, wa, wr, wm, wo, ffn_norm_w.reshape(1, D), w_router_t, b_router_col)


def _dma_priority(c):
    return c % 2 if isinstance(c, int) else 0


def _chunk_copy(src_ref, src_chunk, dst_ref, dst_chunk, sem):
    def rows(c):
        start = c * SEG_ALIGN
        return pl.ds(start if isinstance(c, int) else pl.multiple_of(start, SEG_ALIGN), SEG_ALIGN)

    return pltpu.make_async_copy(src_ref.at[rows(src_chunk), :], dst_ref.at[rows(dst_chunk), :], sem)


def _dispatch_body(fill_start_ref, fill_n_ref, nuc_ref, gch_ref, h2_ref, idxt_ref, loff_ref,
                   xs_ref, rowt_ref, buf_ref, zero_ref, sems, zsem):
    s = pl.program_id(0)
    last = pl.num_programs(0) - 1
    slot = s % 2
    ts = h2_ref.shape[0]
    nchunks = gch_ref.shape[-1]
    local_rows = nchunks * SEG_ALIGN
    min_chunks = ts * TOP_K // SEG_ALIGN

    def for_used_chunks(n_used, fn):
        for c in range(min_chunks):
            fn(c)

        def body(c, carry):
            fn(c)
            return carry

        lax.fori_loop(min_chunks, n_used, body, 0)

    def wait_slot(sl, n_used):
        for_used_chunks(n_used, lambda c: _chunk_copy(buf_ref.at[sl], c, xs_ref, c, sems.at[sl]).wait())

    @pl.when(s == 0)
    def _():
        zero_ref[...] = jnp.zeros_like(zero_ref)

        def per_expert(e, total):
            def fill(r, _):
                _chunk_copy(zero_ref, 0, xs_ref, fill_start_ref[e] + r, zsem).start()
                return 0

            lax.fori_loop(0, fill_n_ref[e], fill, 0)
            return total + fill_n_ref[e]

        total = lax.fori_loop(0, N_EXPERTS, per_expert, 0)

        def drain(r, _):
            _chunk_copy(zero_ref, 0, xs_ref, 0, zsem).wait()
            return 0

        lax.fori_loop(0, total, drain, 0)

    @pl.when(s >= 2)
    def _():
        wait_slot(slot, nuc_ref[s - 2])

    idxt = idxt_ref[...]
    sub = lax.broadcasted_iota(I32, (LANES, ts), 0)
    ohs = [(sub == idxt[k:k + 1, :]).astype(F32) for k in range(TOP_K)]
    m = ohs[0] + ohs[1] + ohs[2] + ohs[3]
    tr = lax.broadcasted_iota(I32, (ts, ts), 0)
    tc = lax.broadcasted_iota(I32, (ts, ts), 1)
    earlier = jnp.where(tr < tc, 1.0, 0.0).astype(BF16)
    before = _dot(m.astype(BF16), earlier) + loff_ref[...]
    rows = [jnp.sum(oh * before, axis=0, keepdims=True).astype(I32) for oh in ohs]
    rowt_ref[...] = jnp.concatenate(rows, axis=0)
    ri = lax.broadcasted_iota(I32, (local_rows, ts), 0)
    sel = (ri == rows[0]) | (ri == rows[1]) | (ri == rows[2]) | (ri == rows[3])
    buf_ref[slot] = _pack_halves(_dot(jnp.where(sel, 1.0, 0.0).astype(BF16), h2_ref[...]))
    for_used_chunks(nuc_ref[s],
                    lambda c: _chunk_copy(buf_ref.at[slot], c, xs_ref, gch_ref[0, 0, c],
                                          sems.at[slot]).start(priority=_dma_priority(c)))

    @pl.when(s == last)
    def _():
        wait_slot(slot, nuc_ref[s])

        @pl.when(s >= 1)
        def _():
            wait_slot(1 - slot, nuc_ref[s - 1])


def _dispatch(h2, idxt, loff_col, gchunk, n_used_chunks, fill_start, fill_n, n_rows, ts):
    T, D = h2.shape
    nt = T // ts
    nchunks = gchunk.shape[-1]
    return pl.pallas_call(
        _dispatch_body,
        grid_spec=pltpu.PrefetchScalarGridSpec(
            num_scalar_prefetch=3,
            grid=(nt,),
            in_specs=[pl.BlockSpec((1, 1, nchunks), lambda s, *_: (s, 0, 0), memory_space=pltpu.SMEM),
                      pl.BlockSpec((ts, D), lambda s, *_: (s, 0)),
                      pl.BlockSpec((TOP_K, ts), lambda s, *_: (0, s)),
                      pl.BlockSpec((None, LANES, 1), lambda s, *_: (s, 0, 0))],
            out_specs=[pl.BlockSpec(memory_space=pl.ANY),
                       pl.BlockSpec((TOP_K, ts), lambda s, *_: (0, s))],
            scratch_shapes=[pltpu.VMEM((2, nchunks * SEG_ALIGN, D // 2), U32),
                            pltpu.VMEM((SEG_ALIGN, D // 2), U32),
                            pltpu.SemaphoreType.DMA((2,)),
                            pltpu.SemaphoreType.DMA(())]),
        out_shape=[jax.ShapeDtypeStruct((n_rows, D // 2), U32),
                   jax.ShapeDtypeStruct((TOP_K, T), I32)],
        compiler_params=_params(("arbitrary",)),
        name="dispatch",
    )(fill_start, fill_n, n_used_chunks, gchunk.reshape(nt, 1, nchunks), h2, idxt, loff_col)


def _expert_body(te_ref, nu_ref, tend_ref, x_ref, wu_hbm, bu_ref, wd_hbm, bd_ref, o_ref,
                 wu_f32, wd_f32, wu_bf, wd_bf, slot_ref, wsem):
    i = pl.program_id(0)
    active = i < nu_ref[0]
    e = te_ref[i]
    new_expert = jnp.logical_or(i == 0, e != te_ref[jnp.maximum(i - 1, 0)])
    half = MXU_COLS // 2

    def weight_copies(expert, slot):
        return (pltpu.make_async_copy(wu_hbm.at[expert], wu_f32.at[slot], wsem.at[0, slot]),
                pltpu.make_async_copy(wd_hbm.at[expert], wd_f32.at[slot], wsem.at[1, slot]))

    @pl.when(i == 0)
    def _():
        slot_ref[0] = 1
        for cp in weight_copies(e, 0):
            cp.start()

    @pl.when(jnp.logical_and(active, new_expert))
    def _():
        slot = 1 - slot_ref[0]
        slot_ref[0] = slot
        for cp in weight_copies(e, slot):
            cp.wait()
        first_of_next = tend_ref[e]

        @pl.when(first_of_next < nu_ref[0])
        def _():
            for cp in weight_copies(te_ref[jnp.minimum(first_of_next, te_ref.shape[0] - 1)], 1 - slot):
                cp.start()

        r = lax.broadcasted_iota(I32, (MXU_COLS, MXU_COLS), 0)
        c = lax.broadcasted_iota(I32, (MXU_COLS, MXU_COLS), 1)
        perm = jnp.where(r == jnp.where(c < half, 2 * c, 2 * (c - half) + 1), 1.0, 0.0).astype(BF16)
        for b in range(wu_bf.shape[-1] // MXU_COLS):
            cols = slice(b * MXU_COLS, (b + 1) * MXU_COLS)
            wu_bf[:, cols] = _dot(wu_f32[slot, :, cols].astype(BF16), perm).astype(BF16)
        wd_bf[...] = wd_f32[slot].astype(BF16)

    @pl.when(active)
    def _():
        x = jnp.concatenate(_unpack_halves(x_ref[...]), axis=-1)
        hid = _dot(x, wu_bf[...]) + bu_ref[...]
        acts = []
        for b in range(hid.shape[-1] // MXU_COLS):
            x_glu = jnp.minimum(hid[:, b * MXU_COLS:b * MXU_COLS + half], SWIGLU_LIMIT)
            x_lin = jnp.clip(hid[:, b * MXU_COLS + half:(b + 1) * MXU_COLS], -SWIGLU_LIMIT, SWIGLU_LIMIT)
            acts.append(x_glu * _sigmoid(SWIGLU_ALPHA * x_glu) * (x_lin + 1.0))
        act = jnp.concatenate(acts, axis=-1).astype(BF16)
        out = _dot(act, wd_bf[...]) + bd_ref[...]
        o_ref[...] = _pack_halves(out.astype(BF16).astype(F32))


def _experts(xs, tile_expert, n_used, tile_end, wu, bu, wd, bd, n_tiles):
    tm = TM_EXPERT
    D = wu.shape[1]
    F2 = wu.shape[-1]
    rows = lambda i, te, nu, tend: (jnp.minimum(i, nu[0] - 1), 0)
    bsel = lambda i, te, nu, tend: (te[i], 0, 0)
    return pl.pallas_call(
        _expert_body,
        grid_spec=pltpu.PrefetchScalarGridSpec(
            num_scalar_prefetch=3,
            grid=(n_tiles,),
            in_specs=[pl.BlockSpec((tm, D // 2), rows),
                      pl.BlockSpec(memory_space=pl.ANY),
                      pl.BlockSpec((None, 1, F2), bsel),
                      pl.BlockSpec(memory_space=pl.ANY),
                      pl.BlockSpec((None, 1, D), bsel)],
            out_specs=pl.BlockSpec((tm, D // 2), rows),
            scratch_shapes=[pltpu.VMEM((2, D, F2), F32), pltpu.VMEM((2, F2 // 2, D), F32),
                            pltpu.VMEM((D, F2), BF16), pltpu.VMEM((F2 // 2, D), BF16),
                            pltpu.SMEM((1,), I32), pltpu.SemaphoreType.DMA((2, 2))]),
        out_shape=jax.ShapeDtypeStruct((n_tiles * tm, D // 2), U32),
        compiler_params=_params(("arbitrary",), vmem=VMEM_LIMIT_LARGE),
        name="experts",
    )(tile_expert, n_used, tile_end, xs, wu, bu, wd, bd)


def _combine_body(nuc_ref, gch_ref, ys_ref, x1_ref, rg_ref, o_ref, buf_ref, sems):
    s = pl.program_id(0)
    slot = s % 2
    nchunks = gch_ref.shape[-1]
    min_chunks = x1_ref.shape[0] * TOP_K // SEG_ALIGN

    def for_used_chunks(n_used, fn):
        for c in range(min_chunks):
            fn(c)

        def body(c, carry):
            fn(c)
            return carry

        lax.fori_loop(min_chunks, n_used, body, 0)

    def gather(which, sl, n_used):
        for_used_chunks(n_used, lambda c: _chunk_copy(ys_ref, gch_ref[0, which, c], buf_ref.at[sl], c,
                                                      sems.at[sl]).start(priority=_dma_priority(c)))

    @pl.when(s == 0)
    def _():
        buf_ref[:, min_chunks * SEG_ALIGN:, :] = jnp.zeros_like(buf_ref[:, min_chunks * SEG_ALIGN:, :])
        gather(0, 0, nuc_ref[0])

    @pl.when(s < pl.num_programs(0) - 1)
    def _():
        gather(1, 1 - slot, nuc_ref[s + 1])

    for_used_chunks(nuc_ref[s], lambda c: _chunk_copy(ys_ref, 0, buf_ref.at[slot], c, sems.at[slot]).wait())
    rg = rg_ref[...]
    rows = rg[:, :TOP_K]
    g = lax.bitcast_convert_type(rg[:, TOP_K:], F32)
    ci = lax.broadcasted_iota(I32, (rows.shape[0], nchunks * SEG_ALIGN), 1)
    w = jnp.zeros(ci.shape, F32)
    for k in range(TOP_K):
        w = jnp.where(ci == rows[:, k:k + 1], g[:, k:k + 1], w)
    w = w.astype(BF16)
    half = x1_ref.shape[-1] // 2
    lo, hi = _unpack_halves(buf_ref[slot])
    o_ref[:, :half] = x1_ref[:, :half] + _dot(w, lo)
    o_ref[:, half:] = x1_ref[:, half:] + _dot(w, hi)


def _combine(ys, gchunk, n_used_chunks, x1, rowt, gatest, ts):
    T, D = x1.shape
    nt = T // ts
    nchunks = gchunk.shape[-1]
    gch2 = jnp.stack([gchunk, jnp.concatenate([gchunk[1:], gchunk[-1:]], axis=0)], axis=1)
    rg = jnp.concatenate([rowt, lax.bitcast_convert_type(gatest, I32)], axis=0).T
    return pl.pallas_call(
        _combine_body,
        grid_spec=pltpu.PrefetchScalarGridSpec(
            num_scalar_prefetch=1,
            grid=(nt,),
            in_specs=[pl.BlockSpec((1, 2, nchunks), lambda s, *_: (s, 0, 0), memory_space=pltpu.SMEM),
                      pl.BlockSpec(memory_space=pl.ANY),
                      pl.BlockSpec((ts, D), lambda s, *_: (s, 0)),
                      pl.BlockSpec((ts, 2 * TOP_K), lambda s, *_: (s, 0))],
            out_specs=pl.BlockSpec((ts, D), lambda s, *_: (s, 0)),
            scratch_shapes=[pltpu.VMEM((2, nchunks * SEG_ALIGN, D // 2), U32),
                            pltpu.SemaphoreType.DMA((2,))]),
        out_shape=jax.ShapeDtypeStruct((T, D), F32),
        compiler_params=_params(("arbitrary",)),
        name="combine",
    )(n_used_chunks, gch2, ys, x1, rg)


def _excl_cumsum(a, axis):
    return jnp.cumsum(a, axis=axis) - a


def _routed_experts(h2, x1, idxt, gatest, cnt, w_up, b_up, w_down, b_down):
    T, D = x1.shape
    E = w_up.shape[0]
    ts = min(TS_SORT, T)
    nt = T // ts
    nchunks = (ts * TOP_K + E * SEG_ALIGN) // SEG_ALIGN
    max_rows = T * TOP_K + nt * E * (SEG_ALIGN - 1) + E * (TM_EXPERT - 1)
    n_tiles = -(-max_rows // TM_EXPERT)
    n_rows = n_tiles * TM_EXPERT

    cnt = cnt[:, :, :min(TM_MERGE, T) // ts].transpose(0, 2, 1).reshape(nt, E).astype(I32)
    seg = -(-cnt // SEG_ALIGN)
    loff = _excl_cumsum(seg, 1)
    chunks_e = jnp.sum(seg, axis=0)
    tiles_e = -(-(chunks_e * SEG_ALIGN) // TM_EXPERT)
    tile_end = jnp.cumsum(tiles_e)
    base = (tile_end - tiles_e) * (TM_EXPERT // SEG_ALIGN)
    gstart = base[None, :] + _excl_cumsum(seg, 0)
    c = jnp.arange(nchunks, dtype=I32)
    owner = jnp.sum((loff + seg)[:, None, :] <= c[None, :, None], axis=-1)
    onehot = owner[:, :, None] == jnp.arange(E, dtype=I32)[None, None, :]
    shift = jnp.sum(jnp.where(onehot, (gstart - loff)[:, None, :], 0), axis=-1)
    gchunk = jnp.where(owner < E, shift + c[None, :], 0).astype(I32)
    n_used_chunks = jnp.sum(seg, axis=1).astype(I32)
    fill_start = (base + chunks_e).astype(I32)
    fill_n = (tiles_e * (TM_EXPERT // SEG_ALIGN) - chunks_e).astype(I32)
    n_used = tile_end[-1:].astype(I32)
    tile_ids = jnp.minimum(jnp.arange(n_tiles, dtype=I32), n_used[0] - 1)
    tile_expert = jnp.minimum(jnp.sum(tile_ids[:, None] >= tile_end[None, :], axis=-1), E - 1).astype(I32)
    loff_col = jnp.zeros((nt, LANES, 1), F32).at[:, :E, 0].set((loff * SEG_ALIGN).astype(F32))

    xs, rowt = _dispatch(h2, idxt, loff_col, gchunk, n_used_chunks, fill_start, fill_n, n_rows, ts)
    half = MXU_COLS // 2
    bu = b_up.reshape(E, -1, half, 2).transpose(0, 1, 3, 2).reshape(E, 1, -1)
    ys = _experts(xs, tile_expert, n_used, tile_end.astype(I32), w_up, bu, w_down, b_down[:, None, :], n_tiles)
    return _combine(ys, gchunk, n_used_chunks, x1, rowt, gatest, ts)


def _layer(x, mem, mix_norm_w, mem_norm_w, w_in, swa_q_norm_w, swa_k_norm_w, swa_sinks, ret_norm_w,
           w_mem_kv, mem_q_norm_w, mem_k_norm_w, w_br_swa, w_br_ret, w_br_mem, w_out, ffn_norm_w,
           w_router, b_router, w_up, b_up, w_down, b_down):
    B, S, D = x.shape
    M = mem.shape[1]
    T = B * S

    mkv = _memkv(mem.reshape(B * M, D), mem_norm_w, w_mem_kv.astype(BF16), mem_k_norm_w, B, M)
    o_swa, o_ret, o_mem, zg = _mixers(x, mix_norm_w, w_in.astype(BF16), swa_q_norm_w, swa_k_norm_w, swa_sinks,
                                      ret_norm_w, mkv, mem_q_norm_w)

    E = w_router.shape[-1]
    x1, h2, idxt, gatest, cnt = _merge(
        x.reshape(T, D), o_swa, o_ret, o_mem, zg, w_br_swa.astype(BF16), w_br_ret.astype(BF16),
        w_br_mem.astype(BF16), w_out.astype(BF16), ffn_norm_w, w_router.T, b_router.reshape(E, 1))

    out = _routed_experts(h2, x1, idxt, gatest, cnt, w_up, b_up, w_down, b_down)
    return out.reshape(B, S, D)


def kernel(x, mem, mix_norm_w, mem_norm_w, w_in, swa_q_norm_w, swa_k_norm_w, swa_sinks, ret_norm_w, w_mem_kv, mem_q_norm_w, mem_k_norm_w, w_br_swa, w_br_ret, w_br_mem, w_out, ffn_norm_w, w_router, b_router, w_up, b_up, w_down, b_down):
    args = (x, mem, mix_norm_w, mem_norm_w, w_in, swa_q_norm_w, swa_k_norm_w, swa_sinks, ret_norm_w,
            w_mem_kv, mem_q_norm_w, mem_k_norm_w, w_br_swa, w_br_ret, w_br_mem, w_out, ffn_norm_w,
            w_router, b_router, w_up, b_up, w_down, b_down)
    for l in range(w_in.shape[0]):
        x = _layer(x, mem, *[a[l] for a in args[2:]])
    return x
```

```python
import jax
import jax.numpy as jnp
from jax import lax
from jax.experimental import pallas as pl
from jax.experimental.pallas import tpu as pltpu

F32 = jnp.float32
BF16 = jnp.bfloat16
I32 = jnp.int32
U32 = jnp.uint32

D_MODEL = 1024
SWA_HEAD_DIM = 64
SWA_Q_HEADS = 8
SWA_KV_HEADS = 2
SWA_GROUP = SWA_Q_HEADS // SWA_KV_HEADS
WINDOW = 128
BAND_BLOCK = 128
RET_HEADS = 4
RET_QK_DIM = 128
RET_V_DIM = 256
RET_CHUNK = 128
ROPE_BASE = 10000.0
MEM_HEADS = 4
MEM_HEAD_DIM = 128
N_BRANCH = 3
N_EXPERTS = 32
TOP_K = 4
D_FF = 1024
SWIGLU_LIMIT = 7.0
SWIGLU_ALPHA = 1.702
EPS = 1e-6
NEG_INF = -1e30

SWA_Q_W = SWA_Q_HEADS * SWA_HEAD_DIM
SWA_KV_W = SWA_KV_HEADS * SWA_HEAD_DIM
RET_QK_W = RET_HEADS * RET_QK_DIM
RET_V_W = RET_HEADS * RET_V_DIM
MEM_W = MEM_HEADS * MEM_HEAD_DIM
IN_SIZES = (SWA_Q_W, SWA_KV_W, SWA_KV_W, RET_QK_W, RET_QK_W, RET_V_W, RET_V_W, MEM_W, N_BRANCH * D_MODEL)

SUBLANES = 8
LANES = 128
MXU_COLS = 256
VMEM_LIMIT = 56 * 1024 * 1024
VMEM_LIMIT_LARGE = 62 * 1024 * 1024

TM_MERGE = 1024
TM_EXPERT = 512
BATCH_PER_STEP = 4
TS_SORT = 256
SEG_ALIGN = SUBLANES
LOCAL_ROWS = TS_SORT * TOP_K + N_EXPERTS * SEG_ALIGN
LOCAL_CHUNKS = LOCAL_ROWS // SEG_ALIGN


def _params(sem, vmem=VMEM_LIMIT):
    return pltpu.CompilerParams(dimension_semantics=sem, vmem_limit_bytes=vmem)


def _const_spec(shape):
    nd = len(shape)
    return pl.BlockSpec(shape, lambda *_: (0,) * nd, pipeline_mode=pl.Buffered(1))


def _rms(x, w):
    return x * lax.rsqrt(jnp.mean(x * x, axis=-1, keepdims=True) + EPS) * w


def _sigmoid(x):
    return 0.5 * jnp.tanh(0.5 * x) + 0.5


def _split_bf16(a):
    hi = a.astype(BF16)
    lo = (a - hi.astype(F32)).astype(BF16)
    return hi, lo


def _pack_halves(a):
    n = a.shape[-1] // 2
    lo = lax.bitcast_convert_type(a[:, :n], U32) >> 16
    hi = lax.bitcast_convert_type(a[:, n:], U32) & jnp.uint32(0xFFFF0000)
    return lo | hi


def _unpack_halves(w):
    lo = lax.bitcast_convert_type(w << 16, F32)
    hi = lax.bitcast_convert_type(w & jnp.uint32(0xFFFF0000), F32)
    return lo.astype(BF16), hi.astype(BF16)


def _dot(a, b):
    return jnp.dot(a, b, preferred_element_type=F32)


def _dot_nt(a, b):
    return lax.dot_general(a, b, (((1,), (1,)), ((), ())), preferred_element_type=F32)


def _dot_tn(a, b):
    return lax.dot_general(a, b, (((0,), (0,)), ((), ())), preferred_element_type=F32)


def _head_rms(t, seg_ref, w):
    ss = _dot((t * t).astype(BF16), seg_ref[...])
    return t * lax.rsqrt(ss * (1.0 / SWA_HEAD_DIM) + EPS) * w


def _swa_block(sink_ref, q, kc, kp, vc, vp, qw_ref, kw_ref, segq_ref, segk_ref, mask):
    d = SWA_HEAD_DIM
    q = _head_rms(q.astype(F32), segq_ref, qw_ref[...]) * (d ** -0.5)
    q = q.astype(BF16)
    k = jnp.concatenate([kp, kc], axis=0).astype(F32)
    k = _head_rms(k, segk_ref, kw_ref[...])
    v = jnp.concatenate([vp, vc], axis=0).astype(F32)
    upper = lax.broadcasted_iota(I32, k.shape, 1) >= d
    k_sw = pltpu.roll(k, d, 1)
    v_sw = pltpu.roll(v, d, 1)

    def placed(t, t_sw):
        return [[jnp.where(upper if half == 1 else jnp.logical_not(upper), t if h == half else t_sw,
                           0.0).astype(BF16) for half in range(2)] for h in range(SWA_KV_HEADS)]

    k_at = placed(k, k_sw)
    v_at = placed(v, v_sw)
    pairs = []
    for j in range(SWA_Q_HEADS // 2):
        h = (2 * j) // SWA_GROUP
        qb = q[:, 2 * j * d:2 * (j + 1) * d]
        acc = None
        for half in range(2):
            s = jnp.where(mask, _dot_nt(qb, k_at[h][half]), NEG_INF)
            sink = sink_ref[2 * j + half]
            m = jnp.maximum(jnp.max(s, axis=-1, keepdims=True), sink)
            p = jnp.exp(s - m)
            den = jnp.sum(p, axis=-1, keepdims=True) + jnp.exp(sink - m)
            o = _dot(p.astype(BF16), v_at[h][half]) * (1.0 / den)
            acc = o if acc is None else acc + o
        pairs.append(acc)
    return jnp.concatenate(pairs, axis=-1)


def _ret_block(b, q, k, v, g, cs_ref, dm_ref, qd_ref, kd_ref, cd_ref, nw_ref, o_ref, st_ref):
    dk, dv = RET_QK_DIM, RET_V_DIM
    cos = cs_ref[:, :dk]
    sin = cs_ref[:, dk:]
    outs = []
    for h in range(RET_HEADS):
        qh = q[b, :, h * dk:(h + 1) * dk].astype(F32)
        kh = k[b, :, h * dk:(h + 1) * dk].astype(F32)
        qr = qh * cos + pltpu.roll(qh, dk // 2, 1) * sin
        kr = (kh * cos + pltpu.roll(kh, dk // 2, 1) * sin) * (dk ** -0.5)
        vh = v[b, :, h * dv:(h + 1) * dv]
        st = st_ref[b, h]
        inner = _dot_nt(qr.astype(BF16), kr.astype(BF16)) * dm_ref[h]
        o = _dot(inner.astype(BF16), vh) + _dot((qr * qd_ref[h]).astype(BF16), st.astype(BF16))
        st_ref[b, h] = st * cd_ref[h] + _dot_tn((kr * kd_ref[h]).astype(BF16), vh)
        mu = jnp.mean(o, axis=-1, keepdims=True)
        oc = o - mu
        var = jnp.mean(oc * oc, axis=-1, keepdims=True)
        y = oc * lax.rsqrt(var + EPS) * nw_ref[:, h * dv:(h + 1) * dv]
        gh = g[b, :, h * dv:(h + 1) * dv].astype(F32)
        outs.append(gh * _sigmoid(gh) * y)
    o_ref[b] = jnp.concatenate(outs, axis=-1).astype(o_ref.dtype)


def _memkv_body(m_ref, nw_ref, w_ref, kw_ref, kv_ref):
    m = _rms(m_ref[...], nw_ref[...]).astype(BF16)
    kv = _dot(m, w_ref[...])
    d = MEM_HEAD_DIM
    ks = [_rms(kv[:, h * d:(h + 1) * d], kw_ref[...]) for h in range(MEM_HEADS)]
    kv_ref[...] = jnp.concatenate(ks + [kv[:, MEM_W:]], axis=-1).astype(kv_ref.dtype)


def _memkv(mem2, mem_norm_w, w_kv_bf, k_norm_w, B, M):
    D = mem2.shape[-1]
    return pl.pallas_call(
        _memkv_body,
        grid=(B,),
        in_specs=[pl.BlockSpec((M, D), lambda b: (b, 0)),
                  _const_spec((1, D)),
                  _const_spec(w_kv_bf.shape),
                  _const_spec((1, MEM_HEAD_DIM))],
        out_specs=pl.BlockSpec((M, 2 * MEM_W), lambda b: (b, 0)),
        out_shape=jax.ShapeDtypeStruct((B * M, 2 * MEM_W), BF16),
        compiler_params=_params(("parallel",)),
        name="memkv",
    )(mem2, mem_norm_w.reshape(1, D), w_kv_bf, k_norm_w.reshape(1, -1))


def _mem_block(q, kv_ref, b, qw_ref):
    d = MEM_HEAD_DIM
    outs = []
    for h in range(MEM_HEADS):
        qh = _rms(q[:, h * d:(h + 1) * d].astype(F32), qw_ref[...]).astype(BF16)
        s = _dot_nt(qh, kv_ref[b, :, h * d:(h + 1) * d]) * (d ** -0.5)
        m = jnp.max(s, axis=-1, keepdims=True)
        p = jnp.exp(s - m)
        o = _dot(p.astype(BF16), kv_ref[b, :, MEM_W + h * d:MEM_W + (h + 1) * d])
        outs.append(o * (1.0 / jnp.sum(p, axis=-1, keepdims=True)))
    return jnp.concatenate(outs, axis=-1)


def _mixers_body(sink_ref, cd_ref, x_ref, nw_ref, w_ref, sqw_ref, skw_ref, segq_ref, segk_ref,
                 cs_ref, dm_ref, qd_ref, kd_ref, rnw_ref, kv_ref, mqw_ref,
                 osw_ref, ort_ref, omm_ref, zg_ref, st_ref, kprev_ref, vprev_ref):
    n = pl.program_id(1)
    bs, L, D = x_ref.shape

    @pl.when(n == 0)
    def _():
        st_ref[...] = jnp.zeros_like(st_ref)
        kprev_ref[...] = jnp.zeros_like(kprev_ref)
        vprev_ref[...] = jnp.zeros_like(vprev_ref)

    h = _rms(x_ref[...].reshape(bs * L, D), nw_ref[...]).astype(BF16)
    offs = [sum(IN_SIZES[:i]) for i in range(len(IN_SIZES) + 1)]

    def proj(i):
        z = _dot(h, w_ref[:, offs[i]:offs[i + 1]]).astype(BF16)
        return z.reshape(bs, L, IN_SIZES[i])

    sq, sk, sv = proj(0), proj(1), proj(2)
    qi = lax.broadcasted_iota(I32, (L, 2 * L), 0)
    kj = lax.broadcasted_iota(I32, (L, 2 * L), 1)
    diff = L + qi - kj
    has_prev = jnp.minimum(n, 1) * L
    mask = (diff >= 0) & (diff < WINDOW) & (kj + has_prev >= L)
    for b in range(bs):
        osw_ref[b] = _swa_block(sink_ref, sq[b], sk[b], kprev_ref[b], sv[b], vprev_ref[b],
                                sqw_ref, skw_ref, segq_ref, segk_ref, mask).astype(osw_ref.dtype)
    kprev_ref[...] = sk
    vprev_ref[...] = sv
    rq, rk, rv, rg = proj(3), proj(4), proj(5), proj(6)
    for b in range(bs):
        _ret_block(b, rq, rk, rv, rg, cs_ref, dm_ref, qd_ref, kd_ref, cd_ref, rnw_ref, ort_ref, st_ref)
    mq = proj(7)
    for b in range(bs):
        omm_ref[b] = _mem_block(mq[b], kv_ref, b, mqw_ref).astype(omm_ref.dtype)
    zg_ref[...] = proj(8)


def _mixers(x, mix_norm_w, w_in_bf, swa_q_norm_w, swa_k_norm_w, sinks, ret_norm_w, mkv, mem_q_norm_w):
    B, S, D = x.shape
    M = mkv.shape[0] // B
    L = BAND_BLOCK
    nb = S // L
    bs = BATCH_PER_STEP if B % BATCH_PER_STEP == 0 else 1
    d = SWA_HEAD_DIM
    group = jnp.arange(SWA_Q_W, dtype=I32) // d
    seg_q = (group[:, None] == group[None, :]).astype(BF16)
    seg_k = seg_q[:SWA_KV_W, :SWA_KV_W]
    half = RET_QK_DIM // 2
    inv = 1.0 / (ROPE_BASE ** (jnp.arange(0, half, dtype=F32) / half))
    ang = jnp.arange(S, dtype=F32)[:, None] * inv[None, :]
    cos = jnp.cos(ang)
    sin = jnp.sin(ang)
    cos_sin = jnp.concatenate([cos, cos, -sin, sin], axis=-1)
    log_g = jnp.log1p(-(2.0 ** (-5.0 - jnp.arange(RET_HEADS, dtype=F32))))
    pos = jnp.arange(L, dtype=F32)
    dpos = pos[:, None] - pos[None, :]
    decay_mask = jnp.where(dpos[None] >= 0,
                           jnp.exp(jnp.maximum(dpos, 0.0)[None] * log_g[:, None, None]), 0.0)
    q_decay = jnp.exp((pos[None, :] + 1.0) * log_g[:, None])[..., None]
    k_decay = jnp.exp((L - 1.0 - pos[None, :]) * log_g[:, None])[..., None]
    chunk_decay = jnp.exp(L * log_g)
    qd = jnp.broadcast_to(q_decay, (RET_HEADS, L, RET_QK_DIM))
    kd = jnp.broadcast_to(k_decay, (RET_HEADS, L, RET_QK_DIM))
    blk = lambda w: pl.BlockSpec((bs, L, w), lambda b, n: (b, n, 0))
    smem = pl.BlockSpec(memory_space=pltpu.SMEM)
    outs = pl.pallas_call(
        _mixers_body,
        grid=(B // bs, nb),
        in_specs=[smem, smem,
                  blk(D), _const_spec((1, D)), _const_spec(w_in_bf.shape),
                  _const_spec((1, SWA_Q_W)), _const_spec((1, SWA_KV_W)),
                  _const_spec(seg_q.shape), _const_spec(seg_k.shape),
                  pl.BlockSpec((L, 2 * RET_QK_DIM), lambda b, n: (n, 0)),
                  _const_spec((RET_HEADS, L, L)),
                  _const_spec((RET_HEADS, L, RET_QK_DIM)),
                  _const_spec((RET_HEADS, L, RET_QK_DIM)),
                  _const_spec((1, RET_V_W)),
                  pl.BlockSpec((bs, M, 2 * MEM_W), lambda b, n: (b, 0, 0)),
                  _const_spec((1, MEM_HEAD_DIM))],
        out_specs=[blk(SWA_Q_W), blk(RET_V_W), blk(MEM_W), blk(N_BRANCH * D)],
        out_shape=[jax.ShapeDtypeStruct((B, S, w), BF16) for w in (SWA_Q_W, RET_V_W, MEM_W, N_BRANCH * D)],
        scratch_shapes=[pltpu.VMEM((bs, RET_HEADS, RET_QK_DIM, RET_V_DIM), F32),
                        pltpu.VMEM((bs, L, SWA_KV_W), BF16),
                        pltpu.VMEM((bs, L, SWA_KV_W), BF16)],
        compiler_params=_params(("parallel", "arbitrary"), vmem=VMEM_LIMIT_LARGE),
        name="mixers",
    )(sinks.astype(F32), chunk_decay, x, mix_norm_w.reshape(1, D), w_in_bf,
      jnp.tile(swa_q_norm_w, SWA_Q_HEADS).reshape(1, -1), jnp.tile(swa_k_norm_w, SWA_KV_HEADS).reshape(1, -1),
      seg_q, seg_k, cos_sin, decay_mask, qd, kd, ret_norm_w.reshape(1, -1),
      mkv.reshape(B, M, 2 * MEM_W), mem_q_norm_w.reshape(1, -1))
    return [o.reshape(B * S, o.shape[-1]) for o in outs]


def _merge_body(x_ref, osw_ref, ort_ref, omm_ref, zg_hbm, wa_ref, wr_ref, wm_ref, wo_ref, nw_ref,
                wrt_ref, brt_ref, x1_ref, h2_ref, idx_ref, gate_ref, cnt_ref, zbuf, zsem):
    D = D_MODEL
    i = pl.program_id(0)
    nsteps = pl.num_programs(0)
    tm = x_ref.shape[0]

    def gate_copy(step, slot):
        rows = pl.ds(pl.multiple_of(step * tm, tm), tm)
        return pltpu.make_async_copy(zg_hbm.at[rows, :], zbuf.at[slot], zsem.at[slot])

    @pl.when(i == 0)
    def _():
        gate_copy(0, 0).start()

        @pl.when(nsteps > 1)
        def _():
            gate_copy(1, 1).start()

    @pl.when(i + 2 < nsteps)
    def _():
        gate_copy(i + 2, (i + 2) % 3).start()

    slot = i % 3
    gate_copy(i, slot).wait()
    sg = lambda j: _sigmoid(zbuf[slot, :, j * D:(j + 1) * D].astype(F32))
    merged = (sg(0) * _dot(osw_ref[...], wa_ref[...])
              + sg(1) * _dot(ort_ref[...], wr_ref[...])
              + sg(2) * _dot(omm_ref[...], wm_ref[...]))
    x1 = x_ref[...] + _dot(merged.astype(BF16), wo_ref[...])
    x1_ref[...] = x1
    h2 = _rms(x1, nw_ref[...])
    h2_ref[...] = h2.astype(h2_ref.dtype)
    h_hi, h_lo = _split_bf16(h2)
    w_hi, w_lo = _split_bf16(wrt_ref[...])
    logits = _dot_nt(w_hi, h_hi) + _dot_nt(w_hi, h_lo) + _dot_nt(w_lo, h_hi) + brt_ref[...]
    n_exp, tm = logits.shape
    sub = lax.broadcasted_iota(I32, logits.shape, 0)
    vals, idxs = [], []
    l = logits
    for _ in range(TOP_K):
        m = jnp.max(l, axis=0, keepdims=True)
        i = jnp.min(jnp.where(l == m, sub, n_exp), axis=0, keepdims=True)
        vals.append(m)
        idxs.append(i)
        l = jnp.where(sub == i, -jnp.inf, l)
    es = [jnp.exp(v - vals[0]) for v in vals]
    den = es[0] + es[1] + es[2] + es[3]
    idx_ref[...] = jnp.concatenate(idxs, axis=0)
    gate_ref[...] = jnp.concatenate([e / den for e in es], axis=0)
    chosen = sum((sub == i).astype(F32) for i in idxs)
    ts = min(TS_SORT, tm)
    lane = lax.broadcasted_iota(I32, cnt_ref.shape, 1)
    cnt_ref[...] = sum(jnp.where(lane == j, jnp.sum(chosen[:, j * ts:(j + 1) * ts], axis=1, keepdims=True), 0.0)
                       for j in range(tm // ts))


def _merge(x2, o_swa, o_ret, o_mem, zg, wa, wr, wm, wo, ffn_norm_w, w_router_t, b_router_col):
    T, D = x2.shape
    E = w_router_t.shape[0]
    tm = min(TM_MERGE, T)
    row = lambda i: (i, 0)
    return pl.pallas_call(
        _merge_body,
        grid=(T // tm,),
        in_specs=[pl.BlockSpec((tm, D), row),
                  pl.BlockSpec((tm, SWA_Q_W), row),
                  pl.BlockSpec((tm, RET_V_W), row),
                  pl.BlockSpec((tm, MEM_W), row),
                  pl.BlockSpec(memory_space=pl.ANY),
                  _const_spec(wa.shape), _const_spec(wr.shape), _const_spec(wm.shape), _const_spec(wo.shape),
                  _const_spec((1, D)),
                  _const_spec(w_router_t.shape), _const_spec((E, 1))],
        out_specs=[pl.BlockSpec((tm, D), row),
                   pl.BlockSpec((tm, D), row),
                   pl.BlockSpec((TOP_K, tm), lambda i: (0, i)),
                   pl.BlockSpec((TOP_K, tm), lambda i: (0, i)),
                   pl.BlockSpec((None, E, LANES), lambda i: (i, 0, 0))],
        out_shape=[jax.ShapeDtypeStruct((T, D), F32),
                   jax.ShapeDtypeStruct((T, D), BF16),
                   jax.ShapeDtypeStruct((TOP_K, T), I32),
                   jax.ShapeDtypeStruct((TOP_K, T), F32),
                   jax.ShapeDtypeStruct((T // tm, E, LANES), F32)],
        scratch_shapes=[pltpu.VMEM((3, tm, N_BRANCH * D), BF16), pltpu.SemaphoreType.DMA((3,))],
        compiler_params=_params(("arbitrary",), vmem=VMEM_LIMIT_LARGE),
        name="merge",
    )(x2, o_swa, o_ret, o_mem, zg, wa, wr, wm, wo, ffn_norm_w.reshape(1, D), w_router_t, b_router_col)


def _chunk_copy(src_ref, src_chunk, dst_ref, dst_chunk, sem):
    def rows(c):
        start = c * SEG_ALIGN
        return pl.ds(start if isinstance(c, int) else pl.multiple_of(start, SEG_ALIGN), SEG_ALIGN)

    return pltpu.make_async_copy(src_ref.at[rows(src_chunk), :], dst_ref.at[rows(dst_chunk), :], sem)


def _dispatch_body(fill_start_ref, fill_n_ref, nuc_ref, gch_ref, h2_ref, idxt_ref, loff_ref,
                   xs_ref, rowt_ref, buf_ref, zero_ref, sems, zsem):
    s = pl.program_id(0)
    last = pl.num_programs(0) - 1
    slot = s % 2
    ts = h2_ref.shape[0]
    nchunks = gch_ref.shape[-1]
    local_rows = nchunks * SEG_ALIGN
    min_chunks = ts * TOP_K // SEG_ALIGN

    def for_used_chunks(n_used, fn):
        for c in range(min_chunks):
            fn(c)

        def body(c, carry):
            fn(c)
            return carry

        lax.fori_loop(min_chunks, n_used, body, 0)

    def wait_slot(sl, n_used):
        for_used_chunks(n_used, lambda c: _chunk_copy(buf_ref.at[sl], c, xs_ref, c, sems.at[sl]).wait())

    @pl.when(s == 0)
    def _():
        zero_ref[...] = jnp.zeros_like(zero_ref)

        def per_expert(e, total):
            def fill(r, _):
                _chunk_copy(zero_ref, 0, xs_ref, fill_start_ref[e] + r, zsem).start()
                return 0

            lax.fori_loop(0, fill_n_ref[e], fill, 0)
            return total + fill_n_ref[e]

        total = lax.fori_loop(0, N_EXPERTS, per_expert, 0)

        def drain(r, _):
            _chunk_copy(zero_ref, 0, xs_ref, 0, zsem).wait()
            return 0

        lax.fori_loop(0, total, drain, 0)

    @pl.when(s >= 2)
    def _():
        wait_slot(slot, nuc_ref[s - 2])

    idxt = idxt_ref[...]
    sub = lax.broadcasted_iota(I32, (LANES, ts), 0)
    ohs = [(sub == idxt[k:k + 1, :]).astype(F32) for k in range(TOP_K)]
    m = ohs[0] + ohs[1] + ohs[2] + ohs[3]
    tr = lax.broadcasted_iota(I32, (ts, ts), 0)
    tc = lax.broadcasted_iota(I32, (ts, ts), 1)
    earlier = jnp.where(tr < tc, 1.0, 0.0).astype(BF16)
    before = _dot(m.astype(BF16), earlier) + loff_ref[...]
    rows = [jnp.sum(oh * before, axis=0, keepdims=True).astype(I32) for oh in ohs]
    rowt_ref[...] = jnp.concatenate(rows, axis=0)
    ri = lax.broadcasted_iota(I32, (local_rows, ts), 0)
    sel = (ri == rows[0]) | (ri == rows[1]) | (ri == rows[2]) | (ri == rows[3])
    buf_ref[slot] = _pack_halves(_dot(jnp.where(sel, 1.0, 0.0).astype(BF16), h2_ref[...]))
    for_used_chunks(nuc_ref[s],
                    lambda c: _chunk_copy(buf_ref.at[slot], c, xs_ref, gch_ref[0, 0, c], sems.at[slot]).start())

    @pl.when(s == last)
    def _():
        wait_slot(slot, nuc_ref[s])

        @pl.when(s >= 1)
        def _():
            wait_slot(1 - slot, nuc_ref[s - 1])


def _dispatch(h2, idxt, loff_col, gchunk, n_used_chunks, fill_start, fill_n, n_rows, ts):
    T, D = h2.shape
    nt = T // ts
    nchunks = gchunk.shape[-1]
    return pl.pallas_call(
        _dispatch_body,
        grid_spec=pltpu.PrefetchScalarGridSpec(
            num_scalar_prefetch=3,
            grid=(nt,),
            in_specs=[pl.BlockSpec((1, 1, nchunks), lambda s, *_: (s, 0, 0), memory_space=pltpu.SMEM),
                      pl.BlockSpec((ts, D), lambda s, *_: (s, 0)),
                      pl.BlockSpec((TOP_K, ts), lambda s, *_: (0, s)),
                      pl.BlockSpec((None, LANES, 1), lambda s, *_: (s, 0, 0))],
            out_specs=[pl.BlockSpec(memory_space=pl.ANY),
                       pl.BlockSpec((TOP_K, ts), lambda s, *_: (0, s))],
            scratch_shapes=[pltpu.VMEM((2, nchunks * SEG_ALIGN, D // 2), U32),
                            pltpu.VMEM((SEG_ALIGN, D // 2), U32),
                            pltpu.SemaphoreType.DMA((2,)),
                            pltpu.SemaphoreType.DMA(())]),
        out_shape=[jax.ShapeDtypeStruct((n_rows, D // 2), U32),
                   jax.ShapeDtypeStruct((TOP_K, T), I32)],
        compiler_params=_params(("arbitrary",)),
        name="dispatch",
    )(fill_start, fill_n, n_used_chunks, gchunk.reshape(nt, 1, nchunks), h2, idxt, loff_col)


def _expert_body(te_ref, nu_ref, tend_ref, x_ref, wu_hbm, bu_ref, wd_hbm, bd_ref, o_ref,
                 wu_f32, wd_f32, wu_bf, wd_bf, slot_ref, wsem):
    i = pl.program_id(0)
    active = i < nu_ref[0]
    e = te_ref[i]
    new_expert = jnp.logical_or(i == 0, e != te_ref[jnp.maximum(i - 1, 0)])
    half = MXU_COLS // 2

    def weight_copies(expert, slot):
        return (pltpu.make_async_copy(wu_hbm.at[expert], wu_f32.at[slot], wsem.at[0, slot]),
                pltpu.make_async_copy(wd_hbm.at[expert], wd_f32.at[slot], wsem.at[1, slot]))

    @pl.when(i == 0)
    def _():
        slot_ref[0] = 1
        for cp in weight_copies(e, 0):
            cp.start()

    @pl.when(jnp.logical_and(active, new_expert))
    def _():
        slot = 1 - slot_ref[0]
        slot_ref[0] = slot
        for cp in weight_copies(e, slot):
            cp.wait()
        first_of_next = tend_ref[e]

        @pl.when(first_of_next < nu_ref[0])
        def _():
            for cp in weight_copies(te_ref[jnp.minimum(first_of_next, te_ref.shape[0] - 1)], 1 - slot):
                cp.start()

        r = lax.broadcasted_iota(I32, (MXU_COLS, MXU_COLS), 0)
        c = lax.broadcasted_iota(I32, (MXU_COLS, MXU_COLS), 1)
        perm = jnp.where(r == jnp.where(c < half, 2 * c, 2 * (c - half) + 1), 1.0, 0.0).astype(BF16)
        for b in range(wu_bf.shape[-1] // MXU_COLS):
            cols = slice(b * MXU_COLS, (b + 1) * MXU_COLS)
            wu_bf[:, cols] = _dot(wu_f32[slot, :, cols].astype(BF16), perm).astype(BF16)
        wd_bf[...] = wd_f32[slot].astype(BF16)

    @pl.when(active)
    def _():
        x = jnp.concatenate(_unpack_halves(x_ref[...]), axis=-1)
        hid = _dot(x, wu_bf[...]) + bu_ref[...]
        acts = []
        for b in range(hid.shape[-1] // MXU_COLS):
            x_glu = jnp.minimum(hid[:, b * MXU_COLS:b * MXU_COLS + half], SWIGLU_LIMIT)
            x_lin = jnp.clip(hid[:, b * MXU_COLS + half:(b + 1) * MXU_COLS], -SWIGLU_LIMIT, SWIGLU_LIMIT)
            acts.append(x_glu * _sigmoid(SWIGLU_ALPHA * x_glu) * (x_lin + 1.0))
        act = jnp.concatenate(acts, axis=-1).astype(BF16)
        out = _dot(act, wd_bf[...]) + bd_ref[...]
        o_ref[...] = _pack_halves(out.astype(BF16).astype(F32))


def _experts(xs, tile_expert, n_used, tile_end, wu, bu, wd, bd, n_tiles):
    tm = TM_EXPERT
    D = wu.shape[1]
    F2 = wu.shape[-1]
    rows = lambda i, te, nu, tend: (jnp.minimum(i, nu[0] - 1), 0)
    bsel = lambda i, te, nu, tend: (te[i], 0, 0)
    return pl.pallas_call(
        _expert_body,
        grid_spec=pltpu.PrefetchScalarGridSpec(
            num_scalar_prefetch=3,
            grid=(n_tiles,),
            in_specs=[pl.BlockSpec((tm, D // 2), rows),
                      pl.BlockSpec(memory_space=pl.ANY),
                      pl.BlockSpec((None, 1, F2), bsel),
                      pl.BlockSpec(memory_space=pl.ANY),
                      pl.BlockSpec((None, 1, D), bsel)],
            out_specs=pl.BlockSpec((tm, D // 2), rows),
            scratch_shapes=[pltpu.VMEM((2, D, F2), F32), pltpu.VMEM((2, F2 // 2, D), F32),
                            pltpu.VMEM((D, F2), BF16), pltpu.VMEM((F2 // 2, D), BF16),
                            pltpu.SMEM((1,), I32), pltpu.SemaphoreType.DMA((2, 2))]),
        out_shape=jax.ShapeDtypeStruct((n_tiles * tm, D // 2), U32),
        compiler_params=_params(("arbitrary",), vmem=VMEM_LIMIT_LARGE),
        name="experts",
    )(tile_expert, n_used, tile_end, xs, wu, bu, wd, bd)


def _combine_body(nuc_ref, gch_ref, ys_ref, x1_ref, rg_ref, o_ref, buf_ref, sems):
    s = pl.program_id(0)
    slot = s % 2
    nchunks = gch_ref.shape[-1]
    min_chunks = x1_ref.shape[0] * TOP_K // SEG_ALIGN

    def for_used_chunks(n_used, fn):
        for c in range(min_chunks):
            fn(c)

        def body(c, carry):
            fn(c)
            return carry

        lax.fori_loop(min_chunks, n_used, body, 0)

    def gather(which, sl, n_used):
        for_used_chunks(n_used, lambda c: _chunk_copy(ys_ref, gch_ref[0, which, c], buf_ref.at[sl], c,
                                                      sems.at[sl]).start())

    @pl.when(s == 0)
    def _():
        buf_ref[:, min_chunks * SEG_ALIGN:, :] = jnp.zeros_like(buf_ref[:, min_chunks * SEG_ALIGN:, :])
        gather(0, 0, nuc_ref[0])

    @pl.when(s < pl.num_programs(0) - 1)
    def _():
        gather(1, 1 - slot, nuc_ref[s + 1])

    for_used_chunks(nuc_ref[s], lambda c: _chunk_copy(ys_ref, 0, buf_ref.at[slot], c, sems.at[slot]).wait())
    rg = rg_ref[...]
    rows = rg[:, :TOP_K]
    g = lax.bitcast_convert_type(rg[:, TOP_K:], F32)
    ci = lax.broadcasted_iota(I32, (rows.shape[0], nchunks * SEG_ALIGN), 1)
    w = jnp.zeros(ci.shape, F32)
    for k in range(TOP_K):
        w = jnp.where(ci == rows[:, k:k + 1], g[:, k:k + 1], w)
    w = w.astype(BF16)
    half = x1_ref.shape[-1] // 2
    lo, hi = _unpack_halves(buf_ref[slot])
    o_ref[:, :half] = x1_ref[:, :half] + _dot(w, lo)
    o_ref[:, half:] = x1_ref[:, half:] + _dot(w, hi)


def _combine(ys, gchunk, n_used_chunks, x1, rowt, gatest, ts):
    T, D = x1.shape
    nt = T // ts
    nchunks = gchunk.shape[-1]
    gch2 = jnp.stack([gchunk, jnp.concatenate([gchunk[1:], gchunk[-1:]], axis=0)], axis=1)
    rg = jnp.concatenate([rowt, lax.bitcast_convert_type(gatest, I32)], axis=0).T
    return pl.pallas_call(
        _combine_body,
        grid_spec=pltpu.PrefetchScalarGridSpec(
            num_scalar_prefetch=1,
            grid=(nt,),
            in_specs=[pl.BlockSpec((1, 2, nchunks), lambda s, *_: (s, 0, 0), memory_space=pltpu.SMEM),
                      pl.BlockSpec(memory_space=pl.ANY),
                      pl.BlockSpec((ts, D), lambda s, *_: (s, 0)),
                      pl.BlockSpec((ts, 2 * TOP_K), lambda s, *_: (s, 0))],
            out_specs=pl.BlockSpec((ts, D), lambda s, *_: (s, 0)),
            scratch_shapes=[pltpu.VMEM((2, nchunks * SEG_ALIGN, D // 2), U32),
                            pltpu.SemaphoreType.DMA((2,))]),
        out_shape=jax.ShapeDtypeStruct((T, D), F32),
        compiler_params=_params(("arbitrary",)),
        name="combine",
    )(n_used_chunks, gch2, ys, x1, rg)


def _excl_cumsum(a, axis):
    return jnp.cumsum(a, axis=axis) - a


def _routed_experts(h2, x1, idxt, gatest, cnt, w_up, b_up, w_down, b_down):
    T, D = x1.shape
    E = w_up.shape[0]
    ts = min(TS_SORT, T)
    nt = T // ts
    nchunks = (ts * TOP_K + E * SEG_ALIGN) // SEG_ALIGN
    max_rows = T * TOP_K + nt * E * (SEG_ALIGN - 1) + E * (TM_EXPERT - 1)
    n_tiles = -(-max_rows // TM_EXPERT)
    n_rows = n_tiles * TM_EXPERT

    cnt = cnt[:, :, :min(TM_MERGE, T) // ts].transpose(0, 2, 1).reshape(nt, E).astype(I32)
    seg = -(-cnt // SEG_ALIGN)
    loff = _excl_cumsum(seg, 1)
    chunks_e = jnp.sum(seg, axis=0)
    tiles_e = -(-(chunks_e * SEG_ALIGN) // TM_EXPERT)
    tile_end = jnp.cumsum(tiles_e)
    base = (tile_end - tiles_e) * (TM_EXPERT // SEG_ALIGN)
    gstart = base[None, :] + _excl_cumsum(seg, 0)
    c = jnp.arange(nchunks, dtype=I32)
    owner = jnp.sum((loff + seg)[:, None, :] <= c[None, :, None], axis=-1)
    onehot = owner[:, :, None] == jnp.arange(E, dtype=I32)[None, None, :]
    shift = jnp.sum(jnp.where(onehot, (gstart - loff)[:, None, :], 0), axis=-1)
    gchunk = jnp.where(owner < E, shift + c[None, :], 0).astype(I32)
    n_used_chunks = jnp.sum(seg, axis=1).astype(I32)
    fill_start = (base + chunks_e).astype(I32)
    fill_n = (tiles_e * (TM_EXPERT // SEG_ALIGN) - chunks_e).astype(I32)
    n_used = tile_end[-1:].astype(I32)
    tile_ids = jnp.minimum(jnp.arange(n_tiles, dtype=I32), n_used[0] - 1)
    tile_expert = jnp.minimum(jnp.sum(tile_ids[:, None] >= tile_end[None, :], axis=-1), E - 1).astype(I32)
    loff_col = jnp.zeros((nt, LANES, 1), F32).at[:, :E, 0].set((loff * SEG_ALIGN).astype(F32))

    xs, rowt = _dispatch(h2, idxt, loff_col, gchunk, n_used_chunks, fill_start, fill_n, n_rows, ts)
    half = MXU_COLS // 2
    bu = b_up.reshape(E, -1, half, 2).transpose(0, 1, 3, 2).reshape(E, 1, -1)
    ys = _experts(xs, tile_expert, n_used, tile_end.astype(I32), w_up, bu, w_down, b_down[:, None, :], n_tiles)
    return _combine(ys, gchunk, n_used_chunks, x1, rowt, gatest, ts)


def _layer(x, mem, mix_norm_w, mem_norm_w, w_in, swa_q_norm_w, swa_k_norm_w, swa_sinks, ret_norm_w,
           w_mem_kv, mem_q_norm_w, mem_k_norm_w, w_br_swa, w_br_ret, w_br_mem, w_out, ffn_norm_w,
           w_router, b_router, w_up, b_up, w_down, b_down):
    B, S, D = x.shape
    M = mem.shape[1]
    T = B * S

    mkv = _memkv(mem.reshape(B * M, D), mem_norm_w, w_mem_kv.astype(BF16), mem_k_norm_w, B, M)
    o_swa, o_ret, o_mem, zg = _mixers(x, mix_norm_w, w_in.astype(BF16), swa_q_norm_w, swa_k_norm_w, swa_sinks,
                                      ret_norm_w, mkv, mem_q_norm_w)

    E = w_router.shape[-1]
    x1, h2, idxt, gatest, cnt = _merge(
        x.reshape(T, D), o_swa, o_ret, o_mem, zg, w_br_swa.astype(BF16), w_br_ret.astype(BF16),
        w_br_mem.astype(BF16), w_out.astype(BF16), ffn_norm_w, w_router.T, b_router.reshape(E, 1))

    out = _routed_experts(h2, x1, idxt, gatest, cnt, w_up, b_up, w_down, b_down)
    return out.reshape(B, S, D)


def kernel(x, mem, mix_norm_w, mem_norm_w, w_in, swa_q_norm_w, swa_k_norm_w, swa_sinks, ret_norm_w, w_mem_kv, mem_q_norm_w, mem_k_norm_w, w_br_swa, w_br_ret, w_br_mem, w_out, ffn_norm_w, w_router, b_router, w_up, b_up, w_down, b_down):
    args = (x, mem, mix_norm_w, mem_norm_w, w_in, swa_q_norm_w, swa_k_norm_w, swa_sinks, ret_norm_w,
            w_mem_kv, mem_q_norm_w, mem_k_norm_w, w_br_swa, w_br_ret, w_br_mem, w_out, ffn_norm_w,
            w_router, b_router, w_up, b_up, w_down, b_down)
    for l in range(w_in.shape[0]):
        x = _layer(x, mem, *[a[l] for a in args[2:]])
    return x
```

```python
import jax
import jax.numpy as jnp
from jax import lax
from jax.experimental import pallas as pl
from jax.experimental.pallas import tpu as pltpu

F32 = jnp.float32
BF16 = jnp.bfloat16
I32 = jnp.int32
U32 = jnp.uint32

D_MODEL = 1024
SWA_HEAD_DIM = 64
SWA_Q_HEADS = 8
SWA_KV_HEADS = 2
SWA_GROUP = SWA_Q_HEADS // SWA_KV_HEADS
WINDOW = 128
BAND_BLOCK = 128
RET_HEADS = 4
RET_QK_DIM = 128
RET_V_DIM = 256
RET_CHUNK = 128
ROPE_BASE = 10000.0
MEM_HEADS = 4
MEM_HEAD_DIM = 128
N_BRANCH = 3
N_EXPERTS = 32
TOP_K = 4
D_FF = 1024
SWIGLU_LIMIT = 7.0
SWIGLU_ALPHA = 1.702
EPS = 1e-6
NEG_INF = -1e30

SWA_Q_W = SWA_Q_HEADS * SWA_HEAD_DIM
SWA_KV_W = SWA_KV_HEADS * SWA_HEAD_DIM
RET_QK_W = RET_HEADS * RET_QK_DIM
RET_V_W = RET_HEADS * RET_V_DIM
MEM_W = MEM_HEADS * MEM_HEAD_DIM
IN_SIZES = (SWA_Q_W, SWA_KV_W, SWA_KV_W, RET_QK_W, RET_QK_W, RET_V_W, RET_V_W, MEM_W, N_BRANCH * D_MODEL)

SUBLANES = 8
LANES = 128
MXU_COLS = 256
VMEM_LIMIT = 56 * 1024 * 1024
VMEM_LIMIT_LARGE = 62 * 1024 * 1024

TM_MERGE = 1024
TM_EXPERT = 512
BATCH_PER_STEP = 4
TS_SORT = 256
SEG_ALIGN = SUBLANES
LOCAL_ROWS = TS_SORT * TOP_K + N_EXPERTS * SEG_ALIGN
LOCAL_CHUNKS = LOCAL_ROWS // SEG_ALIGN


def _params(sem, vmem=VMEM_LIMIT):
    return pltpu.CompilerParams(dimension_semantics=sem, vmem_limit_bytes=vmem)


def _const_spec(shape):
    nd = len(shape)
    return pl.BlockSpec(shape, lambda *_: (0,) * nd, pipeline_mode=pl.Buffered(1))


def _rms(x, w):
    return x * lax.rsqrt(jnp.mean(x * x, axis=-1, keepdims=True) + EPS) * w


def _sigmoid(x):
    return 0.5 * jnp.tanh(0.5 * x) + 0.5


def _split_bf16(a):
    hi = a.astype(BF16)
    lo = (a - hi.astype(F32)).astype(BF16)
    return hi, lo


def _pack_halves(a):
    n = a.shape[-1] // 2
    lo = lax.bitcast_convert_type(a[:, :n], U32) >> 16
    hi = lax.bitcast_convert_type(a[:, n:], U32) & jnp.uint32(0xFFFF0000)
    return lo | hi


def _unpack_halves(w):
    lo = lax.bitcast_convert_type(w << 16, F32)
    hi = lax.bitcast_convert_type(w & jnp.uint32(0xFFFF0000), F32)
    return lo.astype(BF16), hi.astype(BF16)


def _dot(a, b):
    return jnp.dot(a, b, preferred_element_type=F32)


def _dot_nt(a, b):
    return lax.dot_general(a, b, (((1,), (1,)), ((), ())), preferred_element_type=F32)


def _dot_tn(a, b):
    return lax.dot_general(a, b, (((0,), (0,)), ((), ())), preferred_element_type=F32)


def _head_rms(t, seg_ref, w):
    ss = _dot((t * t).astype(BF16), seg_ref[...])
    return t * lax.rsqrt(ss * (1.0 / SWA_HEAD_DIM) + EPS) * w


def _swa_block(sink_ref, q, kc, kp, vc, vp, qw_ref, kw_ref, segq_ref, segk_ref, mask):
    d = SWA_HEAD_DIM
    q = _head_rms(q.astype(F32), segq_ref, qw_ref[...]) * (d ** -0.5)
    q = q.astype(BF16)
    k = jnp.concatenate([kp, kc], axis=0).astype(F32)
    k = _head_rms(k, segk_ref, kw_ref[...])
    v = jnp.concatenate([vp, vc], axis=0).astype(F32)
    upper = lax.broadcasted_iota(I32, k.shape, 1) >= d
    k_sw = pltpu.roll(k, d, 1)
    v_sw = pltpu.roll(v, d, 1)

    def placed(t, t_sw):
        return [[jnp.where(upper if half == 1 else jnp.logical_not(upper), t if h == half else t_sw,
                           0.0).astype(BF16) for half in range(2)] for h in range(SWA_KV_HEADS)]

    k_at = placed(k, k_sw)
    v_at = placed(v, v_sw)
    pairs = []
    for j in range(SWA_Q_HEADS // 2):
        h = (2 * j) // SWA_GROUP
        qb = q[:, 2 * j * d:2 * (j + 1) * d]
        acc = None
        for half in range(2):
            s = jnp.where(mask, _dot_nt(qb, k_at[h][half]), NEG_INF)
            sink = sink_ref[2 * j + half]
            m = jnp.maximum(jnp.max(s, axis=-1, keepdims=True), sink)
            p = jnp.exp(s - m)
            den = jnp.sum(p, axis=-1, keepdims=True) + jnp.exp(sink - m)
            o = _dot(p.astype(BF16), v_at[h][half]) * (1.0 / den)
            acc = o if acc is None else acc + o
        pairs.append(acc)
    return jnp.concatenate(pairs, axis=-1)


def _ret_block(b, q, k, v, g, cs_ref, dm_ref, qd_ref, kd_ref, cd_ref, nw_ref, o_ref, st_ref):
    dk, dv = RET_QK_DIM, RET_V_DIM
    cos = cs_ref[:, :dk]
    sin = cs_ref[:, dk:]
    outs = []
    for h in range(RET_HEADS):
        qh = q[b, :, h * dk:(h + 1) * dk].astype(F32)
        kh = k[b, :, h * dk:(h + 1) * dk].astype(F32)
        qr = qh * cos + pltpu.roll(qh, dk // 2, 1) * sin
        kr = (kh * cos + pltpu.roll(kh, dk // 2, 1) * sin) * (dk ** -0.5)
        vh = v[b, :, h * dv:(h + 1) * dv]
        st = st_ref[b, h]
        inner = _dot_nt(qr.astype(BF16), kr.astype(BF16)) * dm_ref[h]
        o = _dot(inner.astype(BF16), vh) + _dot((qr * qd_ref[h]).astype(BF16), st.astype(BF16))
        st_ref[b, h] = st * cd_ref[h] + _dot_tn((kr * kd_ref[h]).astype(BF16), vh)
        mu = jnp.mean(o, axis=-1, keepdims=True)
        oc = o - mu
        var = jnp.mean(oc * oc, axis=-1, keepdims=True)
        y = oc * lax.rsqrt(var + EPS) * nw_ref[:, h * dv:(h + 1) * dv]
        gh = g[b, :, h * dv:(h + 1) * dv].astype(F32)
        outs.append(gh * _sigmoid(gh) * y)
    o_ref[b] = jnp.concatenate(outs, axis=-1).astype(o_ref.dtype)


def _memkv_body(m_ref, nw_ref, w_ref, kw_ref, kv_ref):
    m = _rms(m_ref[...], nw_ref[...]).astype(BF16)
    kv = _dot(m, w_ref[...])
    d = MEM_HEAD_DIM
    ks = [_rms(kv[:, h * d:(h + 1) * d], kw_ref[...]) for h in range(MEM_HEADS)]
    kv_ref[...] = jnp.concatenate(ks + [kv[:, MEM_W:]], axis=-1).astype(kv_ref.dtype)


def _memkv(mem2, mem_norm_w, w_kv_bf, k_norm_w, B, M):
    D = mem2.shape[-1]
    return pl.pallas_call(
        _memkv_body,
        grid=(B,),
        in_specs=[pl.BlockSpec((M, D), lambda b: (b, 0)),
                  _const_spec((1, D)),
                  _const_spec(w_kv_bf.shape),
                  _const_spec((1, MEM_HEAD_DIM))],
        out_specs=pl.BlockSpec((M, 2 * MEM_W), lambda b: (b, 0)),
        out_shape=jax.ShapeDtypeStruct((B * M, 2 * MEM_W), BF16),
        compiler_params=_params(("parallel",)),
        name="memkv",
    )(mem2, mem_norm_w.reshape(1, D), w_kv_bf, k_norm_w.reshape(1, -1))


def _mem_block(q, kv_ref, b, qw_ref):
    d = MEM_HEAD_DIM
    outs = []
    for h in range(MEM_HEADS):
        qh = _rms(q[:, h * d:(h + 1) * d].astype(F32), qw_ref[...]).astype(BF16)
        s = _dot_nt(qh, kv_ref[b, :, h * d:(h + 1) * d]) * (d ** -0.5)
        m = jnp.max(s, axis=-1, keepdims=True)
        p = jnp.exp(s - m)
        o = _dot(p.astype(BF16), kv_ref[b, :, MEM_W + h * d:MEM_W + (h + 1) * d])
        outs.append(o * (1.0 / jnp.sum(p, axis=-1, keepdims=True)))
    return jnp.concatenate(outs, axis=-1)


def _mixers_body(sink_ref, cd_ref, x_ref, nw_ref, w_ref, sqw_ref, skw_ref, segq_ref, segk_ref,
                 cs_ref, dm_ref, qd_ref, kd_ref, rnw_ref, kv_ref, mqw_ref,
                 osw_ref, ort_ref, omm_ref, zg_ref, st_ref, kprev_ref, vprev_ref):
    n = pl.program_id(1)
    bs, L, D = x_ref.shape

    @pl.when(n == 0)
    def _():
        st_ref[...] = jnp.zeros_like(st_ref)
        kprev_ref[...] = jnp.zeros_like(kprev_ref)
        vprev_ref[...] = jnp.zeros_like(vprev_ref)

    h = _rms(x_ref[...].reshape(bs * L, D), nw_ref[...]).astype(BF16)
    offs = [sum(IN_SIZES[:i]) for i in range(len(IN_SIZES) + 1)]

    def proj(i):
        z = _dot(h, w_ref[:, offs[i]:offs[i + 1]]).astype(BF16)
        return z.reshape(bs, L, IN_SIZES[i])

    sq, sk, sv = proj(0), proj(1), proj(2)
    qi = lax.broadcasted_iota(I32, (L, 2 * L), 0)
    kj = lax.broadcasted_iota(I32, (L, 2 * L), 1)
    diff = L + qi - kj
    has_prev = jnp.minimum(n, 1) * L
    mask = (diff >= 0) & (diff < WINDOW) & (kj + has_prev >= L)
    for b in range(bs):
        osw_ref[b] = _swa_block(sink_ref, sq[b], sk[b], kprev_ref[b], sv[b], vprev_ref[b],
                                sqw_ref, skw_ref, segq_ref, segk_ref, mask).astype(osw_ref.dtype)
    kprev_ref[...] = sk
    vprev_ref[...] = sv
    rq, rk, rv, rg = proj(3), proj(4), proj(5), proj(6)
    for b in range(bs):
        _ret_block(b, rq, rk, rv, rg, cs_ref, dm_ref, qd_ref, kd_ref, cd_ref, rnw_ref, ort_ref, st_ref)
    mq = proj(7)
    for b in range(bs):
        omm_ref[b] = _mem_block(mq[b], kv_ref, b, mqw_ref).astype(omm_ref.dtype)
    zg_ref[...] = proj(8)


def _mixers(x, mix_norm_w, w_in_bf, swa_q_norm_w, swa_k_norm_w, sinks, ret_norm_w, mkv, mem_q_norm_w):
    B, S, D = x.shape
    M = mkv.shape[0] // B
    L = BAND_BLOCK
    nb = S // L
    bs = BATCH_PER_STEP if B % BATCH_PER_STEP == 0 else 1
    d = SWA_HEAD_DIM
    group = jnp.arange(SWA_Q_W, dtype=I32) // d
    seg_q = (group[:, None] == group[None, :]).astype(BF16)
    seg_k = seg_q[:SWA_KV_W, :SWA_KV_W]
    half = RET_QK_DIM // 2
    inv = 1.0 / (ROPE_BASE ** (jnp.arange(0, half, dtype=F32) / half))
    ang = jnp.arange(S, dtype=F32)[:, None] * inv[None, :]
    cos = jnp.cos(ang)
    sin = jnp.sin(ang)
    cos_sin = jnp.concatenate([cos, cos, -sin, sin], axis=-1)
    log_g = jnp.log1p(-(2.0 ** (-5.0 - jnp.arange(RET_HEADS, dtype=F32))))
    pos = jnp.arange(L, dtype=F32)
    dpos = pos[:, None] - pos[None, :]
    decay_mask = jnp.where(dpos[None] >= 0,
                           jnp.exp(jnp.maximum(dpos, 0.0)[None] * log_g[:, None, None]), 0.0)
    q_decay = jnp.exp((pos[None, :] + 1.0) * log_g[:, None])[..., None]
    k_decay = jnp.exp((L - 1.0 - pos[None, :]) * log_g[:, None])[..., None]
    chunk_decay = jnp.exp(L * log_g)
    qd = jnp.broadcast_to(q_decay, (RET_HEADS, L, RET_QK_DIM))
    kd = jnp.broadcast_to(k_decay, (RET_HEADS, L, RET_QK_DIM))
    blk = lambda w: pl.BlockSpec((bs, L, w), lambda b, n: (b, n, 0))
    smem = pl.BlockSpec(memory_space=pltpu.SMEM)
    outs = pl.pallas_call(
        _mixers_body,
        grid=(B // bs, nb),
        in_specs=[smem, smem,
                  blk(D), _const_spec((1, D)), _const_spec(w_in_bf.shape),
                  _const_spec((1, SWA_Q_W)), _const_spec((1, SWA_KV_W)),
                  _const_spec(seg_q.shape), _const_spec(seg_k.shape),
                  pl.BlockSpec((L, 2 * RET_QK_DIM), lambda b, n: (n, 0)),
                  _const_spec((RET_HEADS, L, L)),
                  _const_spec((RET_HEADS, L, RET_QK_DIM)),
                  _const_spec((RET_HEADS, L, RET_QK_DIM)),
                  _const_spec((1, RET_V_W)),
                  pl.BlockSpec((bs, M, 2 * MEM_W), lambda b, n: (b, 0, 0)),
                  _const_spec((1, MEM_HEAD_DIM))],
        out_specs=[blk(SWA_Q_W), blk(RET_V_W), blk(MEM_W), blk(N_BRANCH * D)],
        out_shape=[jax.ShapeDtypeStruct((B, S, w), BF16) for w in (SWA_Q_W, RET_V_W, MEM_W, N_BRANCH * D)],
        scratch_shapes=[pltpu.VMEM((bs, RET_HEADS, RET_QK_DIM, RET_V_DIM), F32),
                        pltpu.VMEM((bs, L, SWA_KV_W), BF16),
                        pltpu.VMEM((bs, L, SWA_KV_W), BF16)],
        compiler_params=_params(("parallel", "arbitrary"), vmem=VMEM_LIMIT_LARGE),
        name="mixers",
    )(sinks.astype(F32), chunk_decay, x, mix_norm_w.reshape(1, D), w_in_bf,
      jnp.tile(swa_q_norm_w, SWA_Q_HEADS).reshape(1, -1), jnp.tile(swa_k_norm_w, SWA_KV_HEADS).reshape(1, -1),
      seg_q, seg_k, cos_sin, decay_mask, qd, kd, ret_norm_w.reshape(1, -1),
      mkv.reshape(B, M, 2 * MEM_W), mem_q_norm_w.reshape(1, -1))
    return [o.reshape(B * S, o.shape[-1]) for o in outs]


def _merge_body(x_ref, osw_ref, ort_ref, omm_ref, zg_ref, wa_ref, wr_ref, wm_ref, wo_ref, nw_ref,
                wrt_ref, brt_ref, x1_ref, h2_ref, idx_ref, gate_ref, cnt_ref):
    D = D_MODEL
    sg = lambda j: _sigmoid(zg_ref[:, j * D:(j + 1) * D].astype(F32))
    merged = (sg(0) * _dot(osw_ref[...], wa_ref[...])
              + sg(1) * _dot(ort_ref[...], wr_ref[...])
              + sg(2) * _dot(omm_ref[...], wm_ref[...]))
    x1 = x_ref[...] + _dot(merged.astype(BF16), wo_ref[...])
    x1_ref[...] = x1
    h2 = _rms(x1, nw_ref[...])
    h2_ref[...] = h2.astype(h2_ref.dtype)
    h_hi, h_lo = _split_bf16(h2)
    w_hi, w_lo = _split_bf16(wrt_ref[...])
    logits = _dot_nt(w_hi, h_hi) + _dot_nt(w_hi, h_lo) + _dot_nt(w_lo, h_hi) + brt_ref[...]
    n_exp, tm = logits.shape
    sub = lax.broadcasted_iota(I32, logits.shape, 0)
    vals, idxs = [], []
    l = logits
    for _ in range(TOP_K):
        m = jnp.max(l, axis=0, keepdims=True)
        i = jnp.min(jnp.where(l == m, sub, n_exp), axis=0, keepdims=True)
        vals.append(m)
        idxs.append(i)
        l = jnp.where(sub == i, -jnp.inf, l)
    es = [jnp.exp(v - vals[0]) for v in vals]
    den = es[0] + es[1] + es[2] + es[3]
    idx_ref[...] = jnp.concatenate(idxs, axis=0)
    gate_ref[...] = jnp.concatenate([e / den for e in es], axis=0)
    chosen = sum((sub == i).astype(F32) for i in idxs)
    ts = min(TS_SORT, tm)
    lane = lax.broadcasted_iota(I32, cnt_ref.shape, 1)
    cnt_ref[...] = sum(jnp.where(lane == j, jnp.sum(chosen[:, j * ts:(j + 1) * ts], axis=1, keepdims=True), 0.0)
                       for j in range(tm // ts))


def _merge(x2, o_swa, o_ret, o_mem, zg, wa, wr, wm, wo, ffn_norm_w, w_router_t, b_router_col):
    T, D = x2.shape
    E = w_router_t.shape[0]
    tm = min(TM_MERGE, T)
    row = lambda i: (i, 0)
    return pl.pallas_call(
        _merge_body,
        grid=(T // tm,),
        in_specs=[pl.BlockSpec((tm, D), row),
                  pl.BlockSpec((tm, SWA_Q_W), row),
                  pl.BlockSpec((tm, RET_V_W), row),
                  pl.BlockSpec((tm, MEM_W), row),
                  pl.BlockSpec((tm, N_BRANCH * D), row),
                  _const_spec(wa.shape), _const_spec(wr.shape), _const_spec(wm.shape), _const_spec(wo.shape),
                  _const_spec((1, D)),
                  _const_spec(w_router_t.shape), _const_spec((E, 1))],
        out_specs=[pl.BlockSpec((tm, D), row),
                   pl.BlockSpec((tm, D), row),
                   pl.BlockSpec((TOP_K, tm), lambda i: (0, i)),
                   pl.BlockSpec((TOP_K, tm), lambda i: (0, i)),
                   pl.BlockSpec((None, E, LANES), lambda i: (i, 0, 0))],
        out_shape=[jax.ShapeDtypeStruct((T, D), F32),
                   jax.ShapeDtypeStruct((T, D), BF16),
                   jax.ShapeDtypeStruct((TOP_K, T), I32),
                   jax.ShapeDtypeStruct((TOP_K, T), F32),
                   jax.ShapeDtypeStruct((T // tm, E, LANES), F32)],
        compiler_params=_params(("parallel",)),
        name="merge",
    )(x2, o_swa, o_ret, o_mem, zg, wa, wr, wm, wo, ffn_norm_w.reshape(1, D), w_router_t, b_router_col)


def _chunk_copy(src_ref, src_chunk, dst_ref, dst_chunk, sem):
    def rows(c):
        start = c * SEG_ALIGN
        return pl.ds(start if isinstance(c, int) else pl.multiple_of(start, SEG_ALIGN), SEG_ALIGN)

    return pltpu.make_async_copy(src_ref.at[rows(src_chunk), :], dst_ref.at[rows(dst_chunk), :], sem)


def _dispatch_body(fill_start_ref, fill_n_ref, nuc_ref, gch_ref, h2_ref, idxt_ref, loff_ref,
                   xs_ref, rowt_ref, buf_ref, zero_ref, sems, zsem):
    s = pl.program_id(0)
    last = pl.num_programs(0) - 1
    slot = s % 2
    ts = h2_ref.shape[0]
    nchunks = gch_ref.shape[-1]
    local_rows = nchunks * SEG_ALIGN
    min_chunks = ts * TOP_K // SEG_ALIGN

    def for_used_chunks(n_used, fn):
        for c in range(min_chunks):
            fn(c)

        def body(c, carry):
            fn(c)
            return carry

        lax.fori_loop(min_chunks, n_used, body, 0)

    def wait_slot(sl, n_used):
        for_used_chunks(n_used, lambda c: _chunk_copy(buf_ref.at[sl], c, xs_ref, c, sems.at[sl]).wait())

    @pl.when(s == 0)
    def _():
        zero_ref[...] = jnp.zeros_like(zero_ref)

        def per_expert(e, total):
            def fill(r, _):
                _chunk_copy(zero_ref, 0, xs_ref, fill_start_ref[e] + r, zsem).start()
                return 0

            lax.fori_loop(0, fill_n_ref[e], fill, 0)
            return total + fill_n_ref[e]

        total = lax.fori_loop(0, N_EXPERTS, per_expert, 0)

        def drain(r, _):
            _chunk_copy(zero_ref, 0, xs_ref, 0, zsem).wait()
            return 0

        lax.fori_loop(0, total, drain, 0)

    @pl.when(s >= 2)
    def _():
        wait_slot(slot, nuc_ref[s - 2])

    idxt = idxt_ref[...]
    sub = lax.broadcasted_iota(I32, (LANES, ts), 0)
    ohs = [(sub == idxt[k:k + 1, :]).astype(F32) for k in range(TOP_K)]
    m = ohs[0] + ohs[1] + ohs[2] + ohs[3]
    tr = lax.broadcasted_iota(I32, (ts, ts), 0)
    tc = lax.broadcasted_iota(I32, (ts, ts), 1)
    earlier = jnp.where(tr < tc, 1.0, 0.0).astype(BF16)
    before = _dot(m.astype(BF16), earlier) + loff_ref[...]
    rows = [jnp.sum(oh * before, axis=0, keepdims=True).astype(I32) for oh in ohs]
    rowt_ref[...] = jnp.concatenate(rows, axis=0)
    ri = lax.broadcasted_iota(I32, (local_rows, ts), 0)
    sel = (ri == rows[0]) | (ri == rows[1]) | (ri == rows[2]) | (ri == rows[3])
    buf_ref[slot] = _pack_halves(_dot(jnp.where(sel, 1.0, 0.0).astype(BF16), h2_ref[...]))
    for_used_chunks(nuc_ref[s],
                    lambda c: _chunk_copy(buf_ref.at[slot], c, xs_ref, gch_ref[0, 0, c], sems.at[slot]).start())

    @pl.when(s == last)
    def _():
        wait_slot(slot, nuc_ref[s])

        @pl.when(s >= 1)
        def _():
            wait_slot(1 - slot, nuc_ref[s - 1])


def _dispatch(h2, idxt, loff_col, gchunk, n_used_chunks, fill_start, fill_n, n_rows, ts):
    T, D = h2.shape
    nt = T // ts
    nchunks = gchunk.shape[-1]
    return pl.pallas_call(
        _dispatch_body,
        grid_spec=pltpu.PrefetchScalarGridSpec(
            num_scalar_prefetch=3,
            grid=(nt,),
            in_specs=[pl.BlockSpec((1, 1, nchunks), lambda s, *_: (s, 0, 0), memory_space=pltpu.SMEM),
                      pl.BlockSpec((ts, D), lambda s, *_: (s, 0)),
                      pl.BlockSpec((TOP_K, ts), lambda s, *_: (0, s)),
                      pl.BlockSpec((None, LANES, 1), lambda s, *_: (s, 0, 0))],
            out_specs=[pl.BlockSpec(memory_space=pl.ANY),
                       pl.BlockSpec((TOP_K, ts), lambda s, *_: (0, s))],
            scratch_shapes=[pltpu.VMEM((2, nchunks * SEG_ALIGN, D // 2), U32),
                            pltpu.VMEM((SEG_ALIGN, D // 2), U32),
                            pltpu.SemaphoreType.DMA((2,)),
                            pltpu.SemaphoreType.DMA(())]),
        out_shape=[jax.ShapeDtypeStruct((n_rows, D // 2), U32),
                   jax.ShapeDtypeStruct((TOP_K, T), I32)],
        compiler_params=_params(("arbitrary",)),
        name="dispatch",
    )(fill_start, fill_n, n_used_chunks, gchunk.reshape(nt, 1, nchunks), h2, idxt, loff_col)


def _expert_body(te_ref, nu_ref, tend_ref, x_ref, wu_hbm, bu_ref, wd_hbm, bd_ref, o_ref,
                 wu_f32, wd_f32, wu_bf, wd_bf, slot_ref, wsem):
    i = pl.program_id(0)
    active = i < nu_ref[0]
    e = te_ref[i]
    new_expert = jnp.logical_or(i == 0, e != te_ref[jnp.maximum(i - 1, 0)])
    half = MXU_COLS // 2

    def weight_copies(expert, slot):
        return (pltpu.make_async_copy(wu_hbm.at[expert], wu_f32.at[slot], wsem.at[0, slot]),
                pltpu.make_async_copy(wd_hbm.at[expert], wd_f32.at[slot], wsem.at[1, slot]))

    @pl.when(i == 0)
    def _():
        slot_ref[0] = 1
        for cp in weight_copies(e, 0):
            cp.start()

    @pl.when(jnp.logical_and(active, new_expert))
    def _():
        slot = 1 - slot_ref[0]
        slot_ref[0] = slot
        for cp in weight_copies(e, slot):
            cp.wait()
        first_of_next = tend_ref[e]

        @pl.when(first_of_next < nu_ref[0])
        def _():
            for cp in weight_copies(te_ref[jnp.minimum(first_of_next, te_ref.shape[0] - 1)], 1 - slot):
                cp.start()

        r = lax.broadcasted_iota(I32, (MXU_COLS, MXU_COLS), 0)
        c = lax.broadcasted_iota(I32, (MXU_COLS, MXU_COLS), 1)
        perm = jnp.where(r == jnp.where(c < half, 2 * c, 2 * (c - half) + 1), 1.0, 0.0).astype(BF16)
        for b in range(wu_bf.shape[-1] // MXU_COLS):
            cols = slice(b * MXU_COLS, (b + 1) * MXU_COLS)
            wu_bf[:, cols] = _dot(wu_f32[slot, :, cols].astype(BF16), perm).astype(BF16)
        wd_bf[...] = wd_f32[slot].astype(BF16)

    @pl.when(active)
    def _():
        x = jnp.concatenate(_unpack_halves(x_ref[...]), axis=-1)
        hid = _dot(x, wu_bf[...]) + bu_ref[...]
        acts = []
        for b in range(hid.shape[-1] // MXU_COLS):
            x_glu = jnp.minimum(hid[:, b * MXU_COLS:b * MXU_COLS + half], SWIGLU_LIMIT)
            x_lin = jnp.clip(hid[:, b * MXU_COLS + half:(b + 1) * MXU_COLS], -SWIGLU_LIMIT, SWIGLU_LIMIT)
            acts.append(x_glu * _sigmoid(SWIGLU_ALPHA * x_glu) * (x_lin + 1.0))
        act = jnp.concatenate(acts, axis=-1).astype(BF16)
        out = _dot(act, wd_bf[...]) + bd_ref[...]
        o_ref[...] = _pack_halves(out.astype(BF16).astype(F32))


def _experts(xs, tile_expert, n_used, tile_end, wu, bu, wd, bd, n_tiles):
    tm = TM_EXPERT
    D = wu.shape[1]
    F2 = wu.shape[-1]
    rows = lambda i, te, nu, tend: (jnp.minimum(i, nu[0] - 1), 0)
    bsel = lambda i, te, nu, tend: (te[i], 0, 0)
    return pl.pallas_call(
        _expert_body,
        grid_spec=pltpu.PrefetchScalarGridSpec(
            num_scalar_prefetch=3,
            grid=(n_tiles,),
            in_specs=[pl.BlockSpec((tm, D // 2), rows),
                      pl.BlockSpec(memory_space=pl.ANY),
                      pl.BlockSpec((None, 1, F2), bsel),
                      pl.BlockSpec(memory_space=pl.ANY),
                      pl.BlockSpec((None, 1, D), bsel)],
            out_specs=pl.BlockSpec((tm, D // 2), rows),
            scratch_shapes=[pltpu.VMEM((2, D, F2), F32), pltpu.VMEM((2, F2 // 2, D), F32),
                            pltpu.VMEM((D, F2), BF16), pltpu.VMEM((F2 // 2, D), BF16),
                            pltpu.SMEM((1,), I32), pltpu.SemaphoreType.DMA((2, 2))]),
        out_shape=jax.ShapeDtypeStruct((n_tiles * tm, D // 2), U32),
        compiler_params=_params(("arbitrary",), vmem=VMEM_LIMIT_LARGE),
        name="experts",
    )(tile_expert, n_used, tile_end, xs, wu, bu, wd, bd)


def _combine_body(nuc_ref, gch_ref, ys_ref, x1_ref, rg_ref, o_ref, buf_ref, sems):
    s = pl.program_id(0)
    slot = s % 2
    nchunks = gch_ref.shape[-1]
    min_chunks = x1_ref.shape[0] * TOP_K // SEG_ALIGN

    def for_used_chunks(n_used, fn):
        for c in range(min_chunks):
            fn(c)

        def body(c, carry):
            fn(c)
            return carry

        lax.fori_loop(min_chunks, n_used, body, 0)

    def gather(which, sl, n_used):
        for_used_chunks(n_used, lambda c: _chunk_copy(ys_ref, gch_ref[0, which, c], buf_ref.at[sl], c,
                                                      sems.at[sl]).start())

    @pl.when(s == 0)
    def _():
        buf_ref[:, min_chunks * SEG_ALIGN:, :] = jnp.zeros_like(buf_ref[:, min_chunks * SEG_ALIGN:, :])
        gather(0, 0, nuc_ref[0])

    @pl.when(s < pl.num_programs(0) - 1)
    def _():
        gather(1, 1 - slot, nuc_ref[s + 1])

    rg = rg_ref[...]
    rows = rg[:, :TOP_K]
    g = lax.bitcast_convert_type(rg[:, TOP_K:], F32)
    ci = lax.broadcasted_iota(I32, (rows.shape[0], nchunks * SEG_ALIGN), 1)
    w = jnp.zeros(ci.shape, F32)
    for k in range(TOP_K):
        w = jnp.where(ci == rows[:, k:k + 1], g[:, k:k + 1], w)
    w = w.astype(BF16)
    for_used_chunks(nuc_ref[s], lambda c: _chunk_copy(ys_ref, 0, buf_ref.at[slot], c, sems.at[slot]).wait())
    half = x1_ref.shape[-1] // 2
    lo, hi = _unpack_halves(buf_ref[slot])
    o_ref[:, :half] = x1_ref[:, :half] + _dot(w, lo)
    o_ref[:, half:] = x1_ref[:, half:] + _dot(w, hi)


def _combine(ys, gchunk, n_used_chunks, x1, rowt, gatest, ts):
    T, D = x1.shape
    nt = T // ts
    nchunks = gchunk.shape[-1]
    gch2 = jnp.stack([gchunk, jnp.concatenate([gchunk[1:], gchunk[-1:]], axis=0)], axis=1)
    rg = jnp.concatenate([rowt, lax.bitcast_convert_type(gatest, I32)], axis=0).T
    return pl.pallas_call(
        _combine_body,
        grid_spec=pltpu.PrefetchScalarGridSpec(
            num_scalar_prefetch=1,
            grid=(nt,),
            in_specs=[pl.BlockSpec((1, 2, nchunks), lambda s, *_: (s, 0, 0), memory_space=pltpu.SMEM),
                      pl.BlockSpec(memory_space=pl.ANY),
                      pl.BlockSpec((ts, D), lambda s, *_: (s, 0)),
                      pl.BlockSpec((ts, 2 * TOP_K), lambda s, *_: (s, 0))],
            out_specs=pl.BlockSpec((ts, D), lambda s, *_: (s, 0)),
            scratch_shapes=[pltpu.VMEM((2, nchunks * SEG_ALIGN, D // 2), U32),
                            pltpu.SemaphoreType.DMA((2,))]),
        out_shape=jax.ShapeDtypeStruct((T, D), F32),
        compiler_params=_params(("arbitrary",)),
        name="combine",
    )(n_used_chunks, gch2, ys, x1, rg)


def _excl_cumsum(a, axis):
    return jnp.cumsum(a, axis=axis) - a


def _routed_experts(h2, x1, idxt, gatest, cnt, w_up, b_up, w_down, b_down):
    T, D = x1.shape
    E = w_up.shape[0]
    ts = min(TS_SORT, T)
    nt = T // ts
    nchunks = (ts * TOP_K + E * SEG_ALIGN) // SEG_ALIGN
    max_rows = T * TOP_K + nt * E * (SEG_ALIGN - 1) + E * (TM_EXPERT - 1)
    n_tiles = -(-max_rows // TM_EXPERT)
    n_rows = n_tiles * TM_EXPERT

    cnt = cnt[:, :, :min(TM_MERGE, T) // ts].transpose(0, 2, 1).reshape(nt, E).astype(I32)
    seg = -(-cnt // SEG_ALIGN)
    loff = _excl_cumsum(seg, 1)
    chunks_e = jnp.sum(seg, axis=0)
    tiles_e = -(-(chunks_e * SEG_ALIGN) // TM_EXPERT)
    tile_end = jnp.cumsum(tiles_e)
    base = (tile_end - tiles_e) * (TM_EXPERT // SEG_ALIGN)
    gstart = base[None, :] + _excl_cumsum(seg, 0)
    c = jnp.arange(nchunks, dtype=I32)
    owner = jnp.sum((loff + seg)[:, None, :] <= c[None, :, None], axis=-1)
    onehot = owner[:, :, None] == jnp.arange(E, dtype=I32)[None, None, :]
    shift = jnp.sum(jnp.where(onehot, (gstart - loff)[:, None, :], 0), axis=-1)
    gchunk = jnp.where(owner < E, shift + c[None, :], 0).astype(I32)
    n_used_chunks = jnp.sum(seg, axis=1).astype(I32)
    fill_start = (base + chunks_e).astype(I32)
    fill_n = (tiles_e * (TM_EXPERT // SEG_ALIGN) - chunks_e).astype(I32)
    n_used = tile_end[-1:].astype(I32)
    tile_ids = jnp.minimum(jnp.arange(n_tiles, dtype=I32), n_used[0] - 1)
    tile_expert = jnp.minimum(jnp.sum(tile_ids[:, None] >= tile_end[None, :], axis=-1), E - 1).astype(I32)
    loff_col = jnp.zeros((nt, LANES, 1), F32).at[:, :E, 0].set((loff * SEG_ALIGN).astype(F32))

    xs, rowt = _dispatch(h2, idxt, loff_col, gchunk, n_used_chunks, fill_start, fill_n, n_rows, ts)
    half = MXU_COLS // 2
    bu = b_up.reshape(E, -1, half, 2).transpose(0, 1, 3, 2).reshape(E, 1, -1)
    ys = _experts(xs, tile_expert, n_used, tile_end.astype(I32), w_up, bu, w_down, b_down[:, None, :], n_tiles)
    return _combine(ys, gchunk, n_used_chunks, x1, rowt, gatest, ts)


def _layer(x, mem, mix_norm_w, mem_norm_w, w_in, swa_q_norm_w, swa_k_norm_w, swa_sinks, ret_norm_w,
           w_mem_kv, mem_q_norm_w, mem_k_norm_w, w_br_swa, w_br_ret, w_br_mem, w_out, ffn_norm_w,
           w_router, b_router, w_up, b_up, w_down, b_down):
    B, S, D = x.shape
    M = mem.shape[1]
    T = B * S

    mkv = _memkv(mem.reshape(B * M, D), mem_norm_w, w_mem_kv.astype(BF16), mem_k_norm_w, B, M)
    o_swa, o_ret, o_mem, zg = _mixers(x, mix_norm_w, w_in.astype(BF16), swa_q_norm_w, swa_k_norm_w, swa_sinks,
                                      ret_norm_w, mkv, mem_q_norm_w)

    E = w_router.shape[-1]
    x1, h2, idxt, gatest, cnt = _merge(
        x.reshape(T, D), o_swa, o_ret, o_mem, zg, w_br_swa.astype(BF16), w_br_ret.astype(BF16),
        w_br_mem.astype(BF16), w_out.astype(BF16), ffn_norm_w, w_router.T, b_router.reshape(E, 1))

    out = _routed_experts(h2, x1, idxt, gatest, cnt, w_up, b_up, w_down, b_down)
    return out.reshape(B, S, D)


def kernel(x, mem, mix_norm_w, mem_norm_w, w_in, swa_q_norm_w, swa_k_norm_w, swa_sinks, ret_norm_w, w_mem_kv, mem_q_norm_w, mem_k_norm_w, w_br_swa, w_br_ret, w_br_mem, w_out, ffn_norm_w, w_router, b_router, w_up, b_up, w_down, b_down):
    args = (x, mem, mix_norm_w, mem_norm_w, w_in, swa_q_norm_w, swa_k_norm_w, swa_sinks, ret_norm_w,
            w_mem_kv, mem_q_norm_w, mem_k_norm_w, w_br_swa, w_br_ret, w_br_mem, w_out, ffn_norm_w,
            w_router, b_router, w_up, b_up, w_down, b_down)
    for l in range(w_in.shape[0]):
        x = _layer(x, mem, *[a[l] for a in args[2:]])
    return x
```
